```python
import math
import jax, jax.numpy as jnp
from jax import lax
import numpy as np

D_MODEL = 1024
BATCH = 16
SEQ = 2048
DEPTH = 1

PLE_DIM = 256
ATTN_WIDTH = D_MODEL // 2
DIFF_HEADS = 4
DIFF_HEAD_DIM = ATTN_WIDTH // DIFF_HEADS // 2
CONV_WIDTH = D_MODEL // 2
CONV_KERNEL = 31
N_EXPERTS = 32
TOP_K = 4
EXPERT_FF = D_MODEL
SWIGLU_ALPHA = 1.702
SWIGLU_LIMIT = 7.0
Q_BLOCK = 128
MOE_BLOCK = 128
EPS = 1e-5
IN_COLS = 3 * ATTN_WIDTH + 2 * CONV_WIDTH + 2 * D_MODEL

kernel_name = "hybrid_diffattn_conformer_moe_ple"


def rms_norm(x, g):
    x32 = x.astype(jnp.float32)
    y = x32 * lax.rsqrt(jnp.mean(x32 * x32, axis=-1, keepdims=True) + EPS)
    return (y * g.astype(jnp.float32)).astype(x.dtype)


def layer_norm(x, g, b):
    x32 = x.astype(jnp.float32)
    mu = jnp.mean(x32, axis=-1, keepdims=True)
    xc = x32 - mu
    y = xc * lax.rsqrt(jnp.mean(xc * xc, axis=-1, keepdims=True) + EPS)
    return (y * g.astype(jnp.float32) + b.astype(jnp.float32)).astype(x.dtype)


def diff_attention(q, k, v, lam, lam_init, g_subln):
    B, S, _ = q.shape
    H, d = DIFF_HEADS, DIFF_HEAD_DIM
    q32 = q.reshape(B, S, H, 2, d).astype(jnp.float32)
    k32 = k.reshape(B, S, H, 2, d).astype(jnp.float32)
    v = v.reshape(B, S, H, 2 * d)
    scale = d ** -0.5
    kpos = jnp.arange(S)

    def block(i):
        q0 = i * Q_BLOCK
        qb = lax.dynamic_slice_in_dim(q32, q0, Q_BLOCK, axis=1)
        s = jnp.einsum('bqhcd,bkhcd->bhcqk', qb, k32) * scale
        qpos = q0 + jnp.arange(Q_BLOCK)
        s = jnp.where(qpos[:, None] >= kpos[None, :], s, -jnp.inf)
        pr = jax.nn.softmax(s, axis=-1)
        attn = pr[:, :, 0] - lam * pr[:, :, 1]
        return jnp.einsum('bhqk,bkhe->bqhe', attn.astype(v.dtype), v)

    o = lax.map(block, jnp.arange(S // Q_BLOCK))
    o = jnp.transpose(o, (1, 0, 2, 3, 4)).reshape(B, S, H, 2 * d)
    o = rms_norm(o, g_subln) * (1.0 - lam_init)
    return o.reshape(B, S, H * 2 * d)


def conformer_conv(c, w_dw, b_dw, g_ln, b_ln):
    z = c[..., :CONV_WIDTH] * jax.nn.sigmoid(c[..., CONV_WIDTH:])
    z = lax.conv_general_dilated(
        z, w_dw[:, None, :], window_strides=(1,),
        padding=[(CONV_KERNEL - 1, 0)],
        dimension_numbers=('NWC', 'WIO', 'NWC'),
        feature_group_count=CONV_WIDTH) + b_dw
    z = layer_norm(z, g_ln, b_ln)
    return jax.nn.silu(z)


def moe(u, w_router, b_router, w_gu, b_gu, w_down, b_down):
    B, S, D = u.shape
    N = B * S
    A = N * TOP_K
    xt = u.reshape(N, D)
    logits = (xt @ w_router + b_router).astype(jnp.float32)
    top_val, top_idx = lax.top_k(logits, TOP_K)
    gates = jax.nn.softmax(top_val, axis=-1).astype(xt.dtype)

    e_flat = top_idx.reshape(A)
    order = jnp.argsort(e_flat)
    e_sorted = e_flat[order]
    tok_sorted = order // TOP_K
    g_sorted = gates.reshape(A)[order]
    counts = jnp.bincount(e_flat, length=N_EXPERTS)
    start = jnp.cumsum(counts) - counts
    padded = (counts + MOE_BLOCK - 1) // MOE_BLOCK * MOE_BLOCK
    pend = jnp.cumsum(padded)
    pstart = pend - padded
    pos_sorted = pstart[e_sorted] + (jnp.arange(A) - start[e_sorted])

    n_blocks = -(-A // MOE_BLOCK) + N_EXPERTS
    R = n_blocks * MOE_BLOCK
    row_tok = jnp.zeros((R,), jnp.int32).at[pos_sorted].set(tok_sorted.astype(jnp.int32))
    row_gate = jnp.zeros((R,), xt.dtype).at[pos_sorted].set(g_sorted)
    block_expert = jnp.clip(
        jnp.searchsorted(pend, jnp.arange(n_blocks) * MOE_BLOCK, side='right'),
        0, N_EXPERTS - 1)

    def expert_block(args):
        e, rows = args
        xb = xt[rows]
        gu = xb @ w_gu[e] + b_gu[e]
        gate = jnp.minimum(gu[:, ::2], SWIGLU_LIMIT)
        lin = jnp.clip(gu[:, 1::2], -SWIGLU_LIMIT, SWIGLU_LIMIT)
        glu = gate * jax.nn.sigmoid(gate * SWIGLU_ALPHA)
        return ((lin + 1.0) * glu) @ w_down[e] + b_down[e]

    y = lax.map(expert_block, (block_expert, row_tok.reshape(n_blocks, MOE_BLOCK)))
    y = y.reshape(R, D) * row_gate[:, None]
    out = jnp.zeros((N, D), xt.dtype).at[row_tok].add(y)
    return out.reshape(B, S, D)


def setup_inputs(seed: int = 0) -> dict:
    key = jax.random.key(seed)
    ks = iter(jax.random.split(key, 32))

    def nrm(shape, fan_in):
        return jax.random.normal(next(ks), shape, jnp.float32) * (fan_in ** -0.5)

    def gain(shape):
        return 1.0 + 0.02 * jax.random.normal(next(ks), shape, jnp.float32)

    def small(shape, s=0.02):
        return s * jax.random.normal(next(ks), shape, jnp.float32)

    L, D, E, F = DEPTH, D_MODEL, N_EXPERTS, EXPERT_FF
    return {
        "x": jax.random.normal(next(ks), (BATCH, SEQ, D), jnp.float32),
        "p": jax.random.normal(next(ks), (DEPTH, BATCH, SEQ, PLE_DIM), jnp.float32),
        "g_mix": gain((L, D)),
        "w_in": nrm((L, D, IN_COLS), D),
        "lambda_q1": small((L, DIFF_HEAD_DIM), 0.1),
        "lambda_k1": small((L, DIFF_HEAD_DIM), 0.1),
        "lambda_q2": small((L, DIFF_HEAD_DIM), 0.1),
        "lambda_k2": small((L, DIFF_HEAD_DIM), 0.1),
        "g_subln": gain((L, 2 * DIFF_HEAD_DIM)),
        "w_attn_out": nrm((L, ATTN_WIDTH, D), ATTN_WIDTH),
        "w_dw": nrm((L, CONV_KERNEL, CONV_WIDTH), CONV_KERNEL),
        "b_dw": small((L, CONV_WIDTH)),
        "g_conv_ln": gain((L, CONV_WIDTH)),
        "b_conv_ln": small((L, CONV_WIDTH)),
        "w_conv_out": nrm((L, CONV_WIDTH, D), CONV_WIDTH),
        "w_o": nrm((L, D, D), D),
        "g_ffn": gain((L, D)),
        "w_router": nrm((L, D, E), D),
        "b_router": small((L, E), 0.01),
        "w_gate_up": nrm((L, E, D, 2 * F), D),
        "b_gate_up": small((L, E, 2 * F)),
        "w_down": nrm((L, E, F, D), F),
        "b_down": small((L, E, D)),
        "w_ple": nrm((L, PLE_DIM, D), PLE_DIM),
        "g_ple": gain((L, D)),
        "w_ple_gate": nrm((L, D, D), D),
        "g_final": gain((D,)),
    }


def reference(x, p, g_mix, w_in, lambda_q1, lambda_k1, lambda_q2, lambda_k2, g_subln,
              w_attn_out, w_dw, b_dw, g_conv_ln, b_conv_ln, w_conv_out, w_o, g_ffn,
              w_router, b_router, w_gate_up, b_gate_up, w_down, b_down,
              w_ple, g_ple, w_ple_gate, g_final):
    h = x
    splits = np.cumsum([ATTN_WIDTH, ATTN_WIDTH, ATTN_WIDTH, 2 * CONV_WIDTH, D_MODEL]).tolist()
    for i in range(DEPTH):
        u = rms_norm(h, g_mix[i])
        proj = u @ w_in[i]
        q, k, v, c_in, gate_a, gate_b = jnp.split(proj, splits, axis=-1)

        lam_init = 0.8 - 0.6 * math.exp(-0.3 * i)
        lam = (jnp.exp(jnp.sum(lambda_q1[i] * lambda_k1[i]).astype(jnp.float32))
               - jnp.exp(jnp.sum(lambda_q2[i] * lambda_k2[i]).astype(jnp.float32))
               + lam_init)
        a = diff_attention(q, k, v, lam, lam_init, g_subln[i]) @ w_attn_out[i]
        b = conformer_conv(c_in, w_dw[i], b_dw[i], g_conv_ln[i], b_conv_ln[i]) @ w_conv_out[i]
        m = jax.nn.sigmoid(gate_a) * a + jax.nn.sigmoid(gate_b) * b
        h = h + m @ w_o[i]

        h = h + moe(rms_norm(h, g_ffn[i]), w_router[i], b_router[i],
                    w_gate_up[i], b_gate_up[i], w_down[i], b_down[i])

        h = h + (p[i] @ w_ple[i]) * jax.nn.sigmoid(rms_norm(h, g_ple[i]) @ w_ple_gate[i])
    return rms_norm(h, g_final)
```

```python
import functools

import jax
import jax.numpy as jnp
from jax import lax
from jax.experimental import pallas as pl
from jax.experimental.pallas import tpu as pltpu

F32 = jnp.float32
BF16 = jnp.bfloat16
I32 = jnp.int32

D_MODEL = 1024
ATTN_WIDTH = 512
DIFF_HEADS = 4
HEAD_DIM = 64
HEAD_V = 2 * HEAD_DIM
CONV_WIDTH = 512
CONV_KERNEL = 31
N_EXPERTS = 32
TOP_K = 4
EXPERT_FF = 1024
PLE_DIM = 256
SWIGLU_ALPHA = 1.702
SWIGLU_LIMIT = 7.0
EPS = 1e-5
IN_COLS = 3 * ATTN_WIDTH + 2 * CONV_WIDTH + 2 * D_MODEL
LAMBDA_INIT = 0.2

COL_CONV_A = 3
COL_CONV_B = 4
COL_GATE_A = 5
COL_GATE_B = 7

TM_PROJ = 512
TQ = 256
TM_MIX = 512
T_RANK = 512
T_DISPATCH = 512
EXPERT_BLOCK = 256
T_COMBINE = 256
CONV_ROWS = 64
CONV_PAD = 32
VMEM_LIMIT = 56 * 1024 * 1024


def _params(sem, vmem=VMEM_LIMIT):
    return pltpu.CompilerParams(dimension_semantics=sem, vmem_limit_bytes=vmem)


def _rms(x, g):
    return x * lax.rsqrt(jnp.mean(x * x, axis=-1, keepdims=True) + EPS) * g


def _inproj_kernel(x_ref, g_ref, w_ref, o_ref):
    u = _rms(x_ref[...], g_ref[...]).astype(BF16)
    ch = 512
    for c in range(IN_COLS // ch):
        r = jnp.dot(u, w_ref[:, c * ch:(c + 1) * ch], preferred_element_type=F32)
        if c == 0:
            r = r * (HEAD_DIM ** -0.5)
        o_ref[:, c * ch:(c + 1) * ch] = r.astype(BF16)


def _inproj(xt, g, w):
    n = xt.shape[0]
    return pl.pallas_call(
        _inproj_kernel,
        grid=(n // TM_PROJ,),
        in_specs=[
            pl.BlockSpec((TM_PROJ, D_MODEL), lambda i: (i, 0)),
            pl.BlockSpec((1, D_MODEL), lambda i: (0, 0)),
            pl.BlockSpec((D_MODEL, IN_COLS), lambda i: (0, 0)),
        ],
        out_specs=pl.BlockSpec((TM_PROJ, IN_COLS), lambda i: (i, 0)),
        out_shape=jax.ShapeDtypeStruct((n, IN_COLS), BF16),
        compiler_params=_params(("parallel",)),
        name="inproj",
    )(xt, g, w)


def _attn_kernel(q_ref, k_ref, v_ref, lq1_ref, lk1_ref, lq2_ref, lk2_ref, g_ref,
                 o_ref, m_ref, l_ref, acc_ref):
    i = pl.program_id(2)
    q = q_ref[...]
    lane = lax.broadcasted_iota(I32, (TQ, HEAD_V), 1)
    zero = jnp.zeros_like(q)
    qq = jnp.concatenate([jnp.where(lane < HEAD_DIM, q, zero),
                          jnp.where(lane >= HEAD_DIM, q, zero)], axis=0)

    m_ref[...] = jnp.full((2 * TQ, 1), -1e30, F32)
    l_ref[...] = jnp.zeros((2 * TQ, 1), F32)
    acc_ref[...] = jnp.zeros((2 * TQ, HEAD_V), F32)

    def step(j, masked):
        kb = k_ref[pl.ds(pl.multiple_of(j * TQ, TQ), TQ), :]
        vb = v_ref[pl.ds(pl.multiple_of(j * TQ, TQ), TQ), :]
        s = lax.dot_general(qq, kb, (((1,), (1,)), ((), ())), preferred_element_type=F32)
        if masked:
            row = lax.broadcasted_iota(I32, (2 * TQ, TQ), 0)
            col = lax.broadcasted_iota(I32, (2 * TQ, TQ), 1)
            row = jnp.where(row >= TQ, row - TQ, row)
            s = jnp.where(row >= col, s, -1e30)
        m_old = m_ref[...]
        m_new = jnp.maximum(m_old, jnp.max(s, axis=-1, keepdims=True))
        p = jnp.exp(s - m_new)
        alpha = jnp.exp(m_old - m_new)
        l_ref[...] = alpha * l_ref[...] + jnp.sum(p, axis=-1, keepdims=True)
        acc_ref[...] = alpha * acc_ref[...] + jnp.dot(p.astype(BF16), vb,
                                                      preferred_element_type=F32)
        m_ref[...] = m_new

    def body(j, carry):
        step(j, False)
        return carry

    lax.fori_loop(0, i, body, 0)
    step(i, True)

    o12 = acc_ref[...] / l_ref[...]
    lam = (jnp.exp(jnp.sum(lq1_ref[...] * lk1_ref[...]))
           - jnp.exp(jnp.sum(lq2_ref[...] * lk2_ref[...])) + LAMBDA_INIT)
    o = o12[:TQ] - lam * o12[TQ:]
    o_ref[...] = (_rms(o, g_ref[...]) * (1.0 - LAMBDA_INIT)).astype(o_ref.dtype)


def _attention(proj3, lq1, lk1, lq2, lk2, g_subln):
    b, s, _ = proj3.shape
    vec = pl.BlockSpec((1, HEAD_DIM), lambda bi, h, i: (0, 0))
    return pl.pallas_call(
        _attn_kernel,
        grid=(b, DIFF_HEADS, s // TQ),
        in_specs=[
            pl.BlockSpec((None, TQ, HEAD_V), lambda bi, h, i: (bi, i, h)),
            pl.BlockSpec((None, s, HEAD_V), lambda bi, h, i: (bi, 0, DIFF_HEADS + h)),
            pl.BlockSpec((None, s, HEAD_V), lambda bi, h, i: (bi, 0, 2 * DIFF_HEADS + h)),
            vec, vec, vec, vec,
            pl.BlockSpec((1, HEAD_V), lambda bi, h, i: (0, 0)),
        ],
        out_specs=pl.BlockSpec((None, TQ, HEAD_V), lambda bi, h, i: (bi, i, h)),
        out_shape=jax.ShapeDtypeStruct((b, s, ATTN_WIDTH), BF16),
        scratch_shapes=[
            pltpu.VMEM((2 * TQ, 1), F32),
            pltpu.VMEM((2 * TQ, 1), F32),
            pltpu.VMEM((2 * TQ, HEAD_V), F32),
        ],
        compiler_params=_params(("parallel", "parallel", "parallel")),
        name="attn",
    )(proj3, proj3, proj3, lq1, lk1, lq2, lk2, g_subln)


def _conv_kernel(ca_ref, cb_ref, w_ref, b_ref, g_ref, beta_ref, o_ref, z_ref):
    s = ca_ref.shape[0]
    glu_rows = 256
    z_ref[0:CONV_PAD, :] = jnp.zeros((CONV_PAD, CONV_WIDTH), F32)

    def glu(c, carry):
        r0 = pl.multiple_of(c * glu_rows, glu_rows)
        a = ca_ref[pl.ds(r0, glu_rows), :].astype(F32)
        g = cb_ref[pl.ds(r0, glu_rows), :].astype(F32)
        z_ref[pl.ds(CONV_PAD + r0, glu_rows), :] = a * jax.nn.sigmoid(g)
        return carry

    lax.fori_loop(0, s // glu_rows, glu, 0)

    def conv(c, carry):
        r0 = pl.multiple_of(c * CONV_ROWS, CONV_ROWS)
        acc = jnp.zeros((CONV_ROWS, CONV_WIDTH), F32) + b_ref[...]
        for phase in range(8):
            rows = CONV_ROWS + (8 if phase else 0)
            part = None
            for j in range(CONV_KERNEL):
                off = CONV_PAD - (CONV_KERNEL - 1) + j
                if off % 8 != phase:
                    continue
                term = w_ref[j:j + 1, :] * z_ref[pl.ds(r0 + (off - phase), rows), :]
                part = term if part is None else part + term
            acc = acc + part[phase:phase + CONV_ROWS]
        mu = jnp.mean(acc, axis=-1, keepdims=True)
        xc = acc - mu
        y = xc * lax.rsqrt(jnp.mean(xc * xc, axis=-1, keepdims=True) + EPS)
        y = y * g_ref[...] + beta_ref[...]
        o_ref[pl.ds(r0, CONV_ROWS), :] = (y * jax.nn.sigmoid(y)).astype(o_ref.dtype)
        return carry

    lax.fori_loop(0, s // CONV_ROWS, conv, 0)


def _conv(proj3, w_dw, b_dw, g_ln, b_ln):
    b, s, _ = proj3.shape
    vec = pl.BlockSpec((1, CONV_WIDTH), lambda bi: (0, 0))
    return pl.pallas_call(
        _conv_kernel,
        grid=(b,),
        in_specs=[
            pl.BlockSpec((None, s, CONV_WIDTH), lambda bi: (bi, 0, COL_CONV_A)),
            pl.BlockSpec((None, s, CONV_WIDTH), lambda bi: (bi, 0, COL_CONV_B)),
            pl.BlockSpec((CONV_KERNEL, CONV_WIDTH), lambda bi: (0, 0)),
            vec, vec, vec,
        ],
        out_specs=pl.BlockSpec((None, s, CONV_WIDTH), lambda bi: (bi, 0, 0)),
        out_shape=jax.ShapeDtypeStruct((b, s, CONV_WIDTH), BF16),
        scratch_shapes=[pltpu.VMEM((s + CONV_PAD, CONV_WIDTH), F32)],
        compiler_params=_params(("parallel",)),
        name="conv",
    )(proj3, proj3, w_dw, b_dw, g_ln, b_ln)


def _split_bf16(a):
    hi = a.astype(BF16)
    lo = (a - hi.astype(F32)).astype(BF16)
    return hi, lo


def _mix_kernel(x_ref, o_ref, c_ref, ga0_ref, ga1_ref, gb0_ref, gb1_ref,
                wa_ref, wc_ref, wo_ref, gffn_ref, wr_ref, br_ref,
                h_ref, u_ref, idx_ref, gate_ref):
    o = o_ref[...]
    c = c_ref[...]
    half = D_MODEL // 2
    acc = x_ref[...]
    for hh, (ga_ref, gb_ref) in enumerate(((ga0_ref, gb0_ref), (ga1_ref, gb1_ref))):
        cols = slice(hh * half, (hh + 1) * half)
        a = jnp.dot(o, wa_ref[:, cols], preferred_element_type=F32)
        b = jnp.dot(c, wc_ref[:, cols], preferred_element_type=F32)
        m = (jax.nn.sigmoid(ga_ref[...].astype(F32)) * a
             + jax.nn.sigmoid(gb_ref[...].astype(F32)) * b)
        acc = acc + jnp.dot(m.astype(BF16), wo_ref[cols, :], preferred_element_type=F32)
    h_ref[...] = acc
    u = _rms(acc, gffn_ref[...])
    u_ref[...] = u

    nt = (((1,), (1,)), ((), ()))
    u_hi, u_lo = _split_bf16(u)
    w_hi, w_lo = _split_bf16(wr_ref[...])
    logits = (lax.dot_general(w_hi, u_hi, nt, preferred_element_type=F32)
              + lax.dot_general(w_hi, u_lo, nt, preferred_element_type=F32)
              + lax.dot_general(w_lo, u_hi, nt, preferred_element_type=F32)
              + br_ref[...])
    eidx = lax.broadcasted_iota(I32, logits.shape, 0)
    vals, idxs = [], []
    for _ in range(TOP_K):
        mx = jnp.max(logits, axis=0, keepdims=True)
        sel = jnp.min(jnp.where(logits == mx, eidx, N_EXPERTS), axis=0, keepdims=True)
        vals.append(mx)
        idxs.append(sel)
        logits = jnp.where(eidx == sel, -jnp.inf, logits)
    ex = [jnp.exp(v - vals[0]) for v in vals]
    den = ex[0] + ex[1] + ex[2] + ex[3]
    idx_ref[...] = jnp.concatenate(idxs, axis=0)
    gate_ref[...] = jnp.concatenate([e / den for e in ex], axis=0)


def _mix(xt, o, c, proj, wa, wc, wo, g_ffn, wr_t, br):
    n = xt.shape[0]
    tm = TM_MIX
    half = D_MODEL // 2
    row = lambda w, j: pl.BlockSpec((tm, w), lambda i, j=j: (i, j))
    full = lambda a: pl.BlockSpec(a.shape, lambda i: (0,) * a.ndim)
    return pl.pallas_call(
        _mix_kernel,
        grid=(n // tm,),
        in_specs=[
            row(D_MODEL, 0), row(ATTN_WIDTH, 0), row(CONV_WIDTH, 0),
            row(half, COL_GATE_A), row(half, COL_GATE_A + 1),
            row(half, COL_GATE_B), row(half, COL_GATE_B + 1),
            full(wa), full(wc), full(wo), full(g_ffn), full(wr_t), full(br),
        ],
        out_specs=[
            pl.BlockSpec((tm, D_MODEL), lambda i: (i, 0)),
            pl.BlockSpec((tm, D_MODEL), lambda i: (i, 0)),
            pl.BlockSpec((TOP_K, tm), lambda i: (0, i)),
            pl.BlockSpec((TOP_K, tm), lambda i: (0, i)),
        ],
        out_shape=[
            jax.ShapeDtypeStruct((n, D_MODEL), F32),
            jax.ShapeDtypeStruct((n, D_MODEL), F32),
            jax.ShapeDtypeStruct((TOP_K, n), I32),
            jax.ShapeDtypeStruct((TOP_K, n), F32),
        ],
        compiler_params=_params(("parallel",)),
        name="mix",
    )(xt, o, c, proj, proj, proj, proj, wa, wc, wo, g_ffn, wr_t, br)


def _rank_kernel(idx_ref, rank_ref, cnt_ref, carry_ref):
    t = T_RANK

    @pl.when(pl.program_id(0) == 0)
    def _():
        carry_ref[...] = jnp.zeros_like(carry_ref)

    idx = idx_ref[...]
    eidx = lax.broadcasted_iota(I32, (N_EXPERTS, t), 0)
    onehot = [eidx == idx[k:k + 1, :] for k in range(TOP_K)]
    member = onehot[0] | onehot[1] | onehot[2] | onehot[3]
    r = lax.broadcasted_iota(I32, (t, t), 0)
    c = lax.broadcasted_iota(I32, (t, t), 1)
    tri = jnp.where(r < c, 1.0, 0.0).astype(BF16)
    mem_f = jnp.where(member, 1.0, 0.0)
    prefix = jnp.dot(mem_f.astype(BF16), tri, preferred_element_type=F32) + carry_ref[...]
    ranks = [jnp.sum(jnp.where(onehot[k], prefix, 0.0), axis=0, keepdims=True)
             for k in range(TOP_K)]
    rank_ref[...] = jnp.concatenate(ranks, axis=0).astype(I32)
    carry_ref[...] = carry_ref[...] + jnp.sum(mem_f, axis=1, keepdims=True)
    cnt_ref[...] = carry_ref[...].astype(I32)


def _rank(idx_t):
    n = idx_t.shape[1]
    return pl.pallas_call(
        _rank_kernel,
        grid=(n // T_RANK,),
        in_specs=[pl.BlockSpec((TOP_K, T_RANK), lambda i: (0, i))],
        out_specs=[
            pl.BlockSpec((TOP_K, T_RANK), lambda i: (0, i)),
            pl.BlockSpec((N_EXPERTS, 1), lambda i: (0, 0)),
        ],
        out_shape=[
            jax.ShapeDtypeStruct((TOP_K, n), I32),
            jax.ShapeDtypeStruct((N_EXPERTS, 1), I32),
        ],
        scratch_shapes=[pltpu.VMEM((N_EXPERTS, 1), F32)],
        compiler_params=_params(("arbitrary",)),
        name="rank",
    )(idx_t)


def _pos_kernel(idx_ref, rank_ref, pstart_ref, pos_ref):
    idx = idx_ref[...]
    eidx = lax.broadcasted_iota(I32, (N_EXPERTS, idx.shape[1]), 0)
    rows = [jnp.sum(jnp.where(eidx == idx[k:k + 1, :], pstart_ref[...], 0),
                    axis=0, keepdims=True) for k in range(TOP_K)]
    pos_ref[...] = jnp.concatenate(rows, axis=0) + rank_ref[...]


def _pos(idx_t, rank_t, pstart):
    n = idx_t.shape[1]
    blk = pl.BlockSpec((TOP_K, T_RANK), lambda i: (0, i))
    return pl.pallas_call(
        _pos_kernel,
        grid=(n // T_RANK,),
        in_specs=[blk, blk, pl.BlockSpec((N_EXPERTS, 1), lambda i: (0, 0))],
        out_specs=blk,
        out_shape=jax.ShapeDtypeStruct((TOP_K, n), I32),
        compiler_params=_params(("parallel",)),
        name="pos",
    )(idx_t, rank_t, pstart)


def _dispatch_kernel(pstart_ref, pend_ref, pos_ref, u_ref, xs_ref, zero_ref, sem, zsem):
    t = T_DISPATCH

    def tail_copy(e):
        start = pend_ref[e] - EXPERT_BLOCK
        return pltpu.make_async_copy(
            zero_ref, xs_ref.at[pl.ds(pl.multiple_of(start, EXPERT_BLOCK), EXPERT_BLOCK), :], zsem)

    @pl.when(pl.program_id(0) == 0)
    def _():
        zero_ref[...] = jnp.zeros_like(zero_ref)

        def z(e, carry):
            @pl.when(pend_ref[e] > pstart_ref[e])
            def _():
                cp = tail_copy(e)
                cp.start()
                cp.wait()
            return carry

        lax.fori_loop(0, N_EXPERTS, z, 0)

    def row_copy(tok, k):
        dst = pos_ref[k, tok]
        return pltpu.make_async_copy(u_ref.at[pl.ds(tok, 1), :], xs_ref.at[pl.ds(dst, 1), :], sem)

    def issue(tok, carry):
        for k in range(TOP_K):
            row_copy(tok, k).start()
        return carry

    lax.fori_loop(0, t, issue, 0)

    def drain(tok, carry):
        for k in range(TOP_K):
            row_copy(tok, k).wait()
        return carry

    lax.fori_loop(0, t, drain, 0)


def _dispatch(pstart, pend, pos_t, u, n_rows):
    n = u.shape[0]
    t = T_DISPATCH
    grid_spec = pltpu.PrefetchScalarGridSpec(
        num_scalar_prefetch=2,
        grid=(n // t,),
        in_specs=[
            pl.BlockSpec((TOP_K, t), lambda i, ps, pe: (0, i), memory_space=pltpu.SMEM),
            pl.BlockSpec((t, D_MODEL), lambda i, ps, pe: (i, 0)),
        ],
        out_specs=pl.BlockSpec(memory_space=pl.ANY),
        scratch_shapes=[
            pltpu.VMEM((EXPERT_BLOCK, D_MODEL), F32),
            pltpu.SemaphoreType.DMA(()),
            pltpu.SemaphoreType.DMA(()),
        ],
    )
    return pl.pallas_call(
        _dispatch_kernel,
        grid_spec=grid_spec,
        out_shape=jax.ShapeDtypeStruct((n_rows, D_MODEL), F32),
        compiler_params=_params(("arbitrary",)),
        name="dispatch",
    )(pstart, pend, pos_t, u)


def _expert_kernel(be_ref, nb_ref, x_ref, wg_ref, wl_ref, bg_ref, bl_ref, wd_ref, bd_ref, y_ref):
    i = pl.program_id(0)

    @pl.when(i < nb_ref[0])
    def _():
        xb = x_ref[...].astype(BF16)
        ch = 512
        y = jnp.zeros(y_ref.shape, F32) + bd_ref[...]
        for c in range(EXPERT_FF // ch):
            cols = slice(c * ch, (c + 1) * ch)
            g = jnp.dot(xb, wg_ref[:, cols], preferred_element_type=F32) + bg_ref[:, cols]
            l = jnp.dot(xb, wl_ref[:, cols], preferred_element_type=F32) + bl_ref[:, cols]
            g = jnp.minimum(g, SWIGLU_LIMIT)
            l = jnp.clip(l, -SWIGLU_LIMIT, SWIGLU_LIMIT)
            act = (l + 1.0) * (g * jax.nn.sigmoid(g * SWIGLU_ALPHA))
            y = y + jnp.dot(act.astype(BF16), wd_ref[cols, :], preferred_element_type=F32)
        y_ref[...] = y

    @pl.when(i >= nb_ref[0])
    def _():
        y_ref[...] = jnp.zeros_like(y_ref)


def _experts(block_expert, n_used, xs, wg, wl, bg, bl, wd, bd):
    n_rows = xs.shape[0]
    blk = EXPERT_BLOCK
    wspec = lambda r, c: pl.BlockSpec((None, r, c), lambda i, be, nb: (be[i], 0, 0))
    grid_spec = pltpu.PrefetchScalarGridSpec(
        num_scalar_prefetch=2,
        grid=(n_rows // blk,),
        in_specs=[
            pl.BlockSpec((blk, D_MODEL), lambda i, be, nb: (jnp.minimum(i, nb[0] - 1), 0)),
            wspec(D_MODEL, EXPERT_FF), wspec(D_MODEL, EXPERT_FF),
            wspec(1, EXPERT_FF), wspec(1, EXPERT_FF),
            wspec(EXPERT_FF, D_MODEL), wspec(1, D_MODEL),
        ],
        out_specs=pl.BlockSpec((blk, D_MODEL), lambda i, be, nb: (i, 0)),
    )
    return pl.pallas_call(
        _expert_kernel,
        grid_spec=grid_spec,
        out_shape=jax.ShapeDtypeStruct((n_rows, D_MODEL), F32),
        compiler_params=_params(("arbitrary",)),
        name="expert",
    )(block_expert, n_used, xs, wg, wl, bg, bl, wd, bd)


def _combine_kernel(pos_ref, h_ref, gate_ref, p_ref, y_ref, wple_ref, gple_ref, wpg_ref, gfin_ref,
                    o_ref, ybuf_ref, sem):
    t = T_COMBINE

    def row_copy(tok, k):
        src = pos_ref[k, tok]
        return pltpu.make_async_copy(y_ref.at[pl.ds(src, 1), :],
                                     ybuf_ref.at[k, pl.ds(tok, 1), :], sem)

    def issue(tok, carry):
        for k in range(TOP_K):
            row_copy(tok, k).start()
        return carry

    lax.fori_loop(0, t, issue, 0)

    emb = jnp.dot(p_ref[...].astype(BF16), wple_ref[...], preferred_element_type=F32)

    def drain(tok, carry):
        for k in range(TOP_K):
            row_copy(tok, k).wait()
        return carry

    lax.fori_loop(0, t, drain, 0)

    gates = gate_ref[...]
    h = h_ref[...]
    for k in range(TOP_K):
        h = h + gates[:, k:k + 1] * ybuf_ref[k]
    r = _rms(h, gple_ref[...]).astype(BF16)
    sig = jax.nn.sigmoid(jnp.dot(r, wpg_ref[...], preferred_element_type=F32))
    h = h + emb * sig
    o_ref[...] = _rms(h, gfin_ref[...])


def _combine(pos_t, h, gates, p2, y, wple, gple, wpg, gfin):
    n = h.shape[0]
    t = T_COMBINE
    full = lambda a: pl.BlockSpec(a.shape, lambda i: (0,) * a.ndim)
    return pl.pallas_call(
        _combine_kernel,
        grid=(n // t,),
        in_specs=[
            pl.BlockSpec((TOP_K, t), lambda i: (0, i), memory_space=pltpu.SMEM),
            pl.BlockSpec((t, D_MODEL), lambda i: (i, 0)),
            pl.BlockSpec((t, TOP_K), lambda i: (i, 0)),
            pl.BlockSpec((t, PLE_DIM), lambda i: (i, 0)),
            pl.BlockSpec(memory_space=pl.ANY),
            full(wple), full(gple), full(wpg), full(gfin),
        ],
        out_specs=pl.BlockSpec((t, D_MODEL), lambda i: (i, 0)),
        out_shape=jax.ShapeDtypeStruct((n, D_MODEL), F32),
        scratch_shapes=[
            pltpu.VMEM((TOP_K, t, D_MODEL), F32),
            pltpu.SemaphoreType.DMA(()),
        ],
        compiler_params=_params(("arbitrary",)),
        name="combine",
    )(pos_t, h, gates, p2, y, wple, gple, wpg, gfin)


def kernel(x, p, g_mix, w_in, lambda_q1, lambda_k1, lambda_q2, lambda_k2, g_subln, w_attn_out,
           w_dw, b_dw, g_conv_ln, b_conv_ln, w_conv_out, w_o, g_ffn, w_router, b_router,
           w_gate_up, b_gate_up, w_down, b_down, w_ple, g_ple, w_ple_gate, g_final):
    b, s, d = x.shape
    n = b * s
    xt = x.reshape(n, d)
    vec = lambda a: a.reshape(1, -1)

    proj = _inproj(xt, vec(g_mix[0]), w_in[0].astype(BF16))
    proj3 = proj.reshape(b, s, IN_COLS)
    attn = _attention(proj3, vec(lambda_q1[0]), vec(lambda_k1[0]), vec(lambda_q2[0]),
                      vec(lambda_k2[0]), vec(g_subln[0]))
    conv = _conv(proj3, w_dw[0], vec(b_dw[0]), vec(g_conv_ln[0]), vec(b_conv_ln[0]))
    h1, u2, idx_t, gates_t = _mix(
        xt, attn.reshape(n, ATTN_WIDTH), conv.reshape(n, CONV_WIDTH), proj,
        w_attn_out[0].astype(BF16), w_conv_out[0].astype(BF16), w_o[0].astype(BF16),
        vec(g_ffn[0]), w_router[0].T, b_router[0].reshape(N_EXPERTS, 1))

    rank_t, counts = _rank(idx_t)
    blk = EXPERT_BLOCK
    n_blocks = n * TOP_K // blk + N_EXPERTS
    padded = (counts[:, 0] + blk - 1) // blk * blk
    pend = jnp.cumsum(padded).astype(I32)
    pstart = pend - padded
    n_used = (pend[-1:] // blk).astype(I32)
    block_expert = jnp.minimum(
        jnp.searchsorted(pend, jnp.arange(n_blocks, dtype=I32) * blk, side='right'),
        N_EXPERTS - 1).astype(I32)
    pos_t = _pos(idx_t, rank_t, pstart.reshape(N_EXPERTS, 1))

    xs = _dispatch(pstart, pend, pos_t, u2, n_blocks * blk)
    wgu = w_gate_up[0].astype(BF16)
    y = _experts(block_expert, n_used, xs,
                 wgu[:, :, 0::2], wgu[:, :, 1::2],
                 b_gate_up[0][:, None, 0::2], b_gate_up[0][:, None, 1::2],
                 w_down[0].astype(BF16), b_down[0][:, None, :])

    out = _combine(pos_t, h1, gates_t.T, p[0].reshape(n, PLE_DIM), y,
                   w_ple[0].astype(BF16), vec(g_ple[0]), w_ple_gate[0].astype(BF16), vec(g_final))
    return out.reshape(b, s, d)
```

```python
import jax
import jax.numpy as jnp
from jax import lax
from jax.experimental import pallas as pl
from jax.experimental.pallas import tpu as pltpu

F32 = jnp.float32
BF16 = jnp.bfloat16
I32 = jnp.int32

D_MODEL = 1024
ATTN_WIDTH = 512
DIFF_HEADS = 4
HEAD_DIM = 64
HEAD_V = 2 * HEAD_DIM
CONV_WIDTH = 512
CONV_KERNEL = 31
N_EXPERTS = 32
TOP_K = 4
EXPERT_FF = 1024
PLE_DIM = 256
SWIGLU_ALPHA = 1.702
SWIGLU_LIMIT = 7.0
EPS = 1e-5
IN_COLS = 3 * ATTN_WIDTH + 2 * CONV_WIDTH + 2 * D_MODEL
LAMBDA_INIT = 0.2

MAIN_COLS = IN_COLS - ATTN_WIDTH
COL_CONV_A = 2
COL_CONV_B = 3
COL_GATE_A = 4
COL_GATE_B = 6

LANES = 128
TM_PROJ = 512
TQ = 256
TM_MIX = 512
T_RANK = 512
T_DISPATCH = 512
EXPERT_BLOCK = 256
T_COMBINE = 256
CONV_ROWS = 64
CONV_PAD = 32
DMA_UNROLL = 8
VMEM_LIMIT = 56 * 1024 * 1024

_NT = (((1,), (1,)), ((), ()))


def _params(sem, vmem=VMEM_LIMIT):
    return pltpu.CompilerParams(dimension_semantics=sem, vmem_limit_bytes=vmem)


def _rms(x, g):
    return x * lax.rsqrt(jnp.mean(x * x, axis=-1, keepdims=True) + EPS) * g


def _inproj_kernel(x_ref, g_ref, w_ref, wvt_ref, o_ref, vt_ref):
    u = _rms(x_ref[...], g_ref[...]).astype(BF16)
    ch = 512
    for c in range(MAIN_COLS // ch):
        r = jnp.dot(u, w_ref[:, c * ch:(c + 1) * ch], preferred_element_type=F32)
        if c == 0:
            r = r * (HEAD_DIM ** -0.5)
        o_ref[:, c * ch:(c + 1) * ch] = r.astype(BF16)
    vt = lax.dot_general(wvt_ref[...], u, _NT, preferred_element_type=F32)
    for s in range(TM_PROJ // TQ):
        vt_ref[s] = vt[:, s * TQ:(s + 1) * TQ].astype(BF16)


def _inproj(xt, g, w, wvt):
    n = xt.shape[0]
    return pl.pallas_call(
        _inproj_kernel,
        grid=(n // TM_PROJ,),
        in_specs=[
            pl.BlockSpec((TM_PROJ, D_MODEL), lambda i: (i, 0)),
            pl.BlockSpec((1, D_MODEL), lambda i: (0, 0)),
            pl.BlockSpec((D_MODEL, MAIN_COLS), lambda i: (0, 0)),
            pl.BlockSpec((ATTN_WIDTH, D_MODEL), lambda i: (0, 0)),
        ],
        out_specs=[
            pl.BlockSpec((TM_PROJ, MAIN_COLS), lambda i: (i, 0)),
            pl.BlockSpec((TM_PROJ // TQ, ATTN_WIDTH, TQ), lambda i: (i, 0, 0)),
        ],
        out_shape=[
            jax.ShapeDtypeStruct((n, MAIN_COLS), BF16),
            jax.ShapeDtypeStruct((n // TQ, ATTN_WIDTH, TQ), BF16),
        ],
        compiler_params=_params(("parallel",)),
        name="inproj",
    )(xt, g, w, wvt)


def _attn_kernel(q_ref, k_ref, vt_ref, lq1_ref, lk1_ref, lq2_ref, lk2_ref, g_ref,
                 o_ref, m_ref, l_ref, acc_ref):
    i = pl.program_id(2)
    q = q_ref[...]
    lane = lax.broadcasted_iota(I32, (TQ, HEAD_V), 1)
    zero = jnp.zeros_like(q)
    qq = jnp.concatenate([jnp.where(lane < HEAD_DIM, q, zero),
                          jnp.where(lane >= HEAD_DIM, q, zero)], axis=0)

    m_ref[...] = jnp.full((1, 2 * TQ), -1e30, F32)
    l_ref[...] = jnp.zeros((1, 2 * TQ), F32)
    acc_ref[...] = jnp.zeros((HEAD_V, 2 * TQ), F32)

    def step(j, masked):
        kb = k_ref[pl.ds(pl.multiple_of(j * TQ, TQ), TQ), :]
        s = lax.dot_general(kb, qq, _NT, preferred_element_type=F32)
        if masked:
            key = lax.broadcasted_iota(I32, (TQ, 2 * TQ), 0)
            qry = lax.broadcasted_iota(I32, (TQ, 2 * TQ), 1)
            qry = jnp.where(qry >= TQ, qry - TQ, qry)
            s = jnp.where(key <= qry, s, -1e30)
        m_old = m_ref[...]
        m_new = jnp.maximum(m_old, jnp.max(s, axis=0, keepdims=True))
        p = jnp.exp(s - m_new)
        alpha = jnp.exp(m_old - m_new)
        l_ref[...] = alpha * l_ref[...] + jnp.sum(p, axis=0, keepdims=True)
        acc_ref[...] = alpha * acc_ref[...] + jnp.dot(vt_ref[j], p.astype(BF16),
                                                      preferred_element_type=F32)
        m_ref[...] = m_new

    def body(j, carry):
        step(j, False)
        return carry

    lax.fori_loop(0, i, body, 0)
    step(i, True)

    o12 = acc_ref[...] / l_ref[...]
    lam = (jnp.exp(jnp.sum(lq1_ref[...] * lk1_ref[...]))
           - jnp.exp(jnp.sum(lq2_ref[...] * lk2_ref[...])) + LAMBDA_INIT)
    o = o12[:, :TQ] - lam * o12[:, TQ:]
    o = o * lax.rsqrt(jnp.mean(o * o, axis=0, keepdims=True) + EPS) * g_ref[...]
    o_ref[...] = (o * (1.0 - LAMBDA_INIT)).T.astype(o_ref.dtype)


def _attention(proj3, vt, lq1, lk1, lq2, lk2, g_subln):
    b, s, _ = proj3.shape
    vec = pl.BlockSpec((1, HEAD_DIM), lambda bi, h, i: (0, 0))
    return pl.pallas_call(
        _attn_kernel,
        grid=(b, DIFF_HEADS, s // TQ),
        in_specs=[
            pl.BlockSpec((None, TQ, HEAD_V), lambda bi, h, i: (bi, i, h)),
            pl.BlockSpec((None, s, HEAD_V), lambda bi, h, i: (bi, 0, DIFF_HEADS + h)),
            pl.BlockSpec((s // TQ, HEAD_V, TQ), lambda bi, h, i: (bi, h, 0)),
            vec, vec, vec, vec,
            pl.BlockSpec((HEAD_V, 1), lambda bi, h, i: (0, 0)),
        ],
        out_specs=pl.BlockSpec((None, TQ, HEAD_V), lambda bi, h, i: (bi, i, h)),
        out_shape=jax.ShapeDtypeStruct((b, s, ATTN_WIDTH), BF16),
        scratch_shapes=[
            pltpu.VMEM((1, 2 * TQ), F32),
            pltpu.VMEM((1, 2 * TQ), F32),
            pltpu.VMEM((HEAD_V, 2 * TQ), F32),
        ],
        compiler_params=_params(("parallel", "parallel", "parallel")),
        name="attn",
    )(proj3, proj3, vt, lq1, lk1, lq2, lk2, g_subln)


def _conv_kernel(ca_ref, cb_ref, w_ref, b_ref, g_ref, beta_ref, o_ref, z_ref):
    s = ca_ref.shape[0]
    glu_rows = 256
    z_ref[0:CONV_PAD, :] = jnp.zeros((CONV_PAD, CONV_WIDTH), F32)

    def glu(c, carry):
        r0 = pl.multiple_of(c * glu_rows, glu_rows)
        a = ca_ref[pl.ds(r0, glu_rows), :].astype(F32)
        g = cb_ref[pl.ds(r0, glu_rows), :].astype(F32)
        z_ref[pl.ds(CONV_PAD + r0, glu_rows), :] = a * jax.nn.sigmoid(g)
        return carry

    lax.fori_loop(0, s // glu_rows, glu, 0)

    def conv(c, carry):
        r0 = pl.multiple_of(c * CONV_ROWS, CONV_ROWS)
        acc = jnp.zeros((CONV_ROWS, CONV_WIDTH), F32) + b_ref[...]
        for phase in range(8):
            rows = CONV_ROWS + (8 if phase else 0)
            part = None
            for j in range(CONV_KERNEL):
                off = CONV_PAD - (CONV_KERNEL - 1) + j
                if off % 8 != phase:
                    continue
                term = w_ref[j:j + 1, :] * z_ref[pl.ds(r0 + (off - phase), rows), :]
                part = term if part is None else part + term
            acc = acc + part[phase:phase + CONV_ROWS]
        mu = jnp.mean(acc, axis=-1, keepdims=True)
        xc = acc - mu
        y = xc * lax.rsqrt(jnp.mean(xc * xc, axis=-1, keepdims=True) + EPS)
        y = y * g_ref[...] + beta_ref[...]
        o_ref[pl.ds(r0, CONV_ROWS), :] = (y * jax.nn.sigmoid(y)).astype(o_ref.dtype)
        return carry

    lax.fori_loop(0, s // CONV_ROWS, conv, 0)


def _conv(proj3, w_dw, b_dw, g_ln, b_ln):
    b, s, _ = proj3.shape
    vec = pl.BlockSpec((1, CONV_WIDTH), lambda bi: (0, 0))
    return pl.pallas_call(
        _conv_kernel,
        grid=(b,),
        in_specs=[
            pl.BlockSpec((None, s, CONV_WIDTH), lambda bi: (bi, 0, COL_CONV_A)),
            pl.BlockSpec((None, s, CONV_WIDTH), lambda bi: (bi, 0, COL_CONV_B)),
            pl.BlockSpec((CONV_KERNEL, CONV_WIDTH), lambda bi: (0, 0)),
            vec, vec, vec,
        ],
        out_specs=pl.BlockSpec((None, s, CONV_WIDTH), lambda bi: (bi, 0, 0)),
        out_shape=jax.ShapeDtypeStruct((b, s, CONV_WIDTH), BF16),
        scratch_shapes=[pltpu.VMEM((s + CONV_PAD, CONV_WIDTH), F32)],
        compiler_params=_params(("parallel",)),
        name="conv",
    )(proj3, proj3, w_dw, b_dw, g_ln, b_ln)


def _split_bf16(a):
    hi = a.astype(BF16)
    lo = (a - hi.astype(F32)).astype(BF16)
    return hi, lo


def _mix_kernel(x_ref, o_ref, c_ref, ga0_ref, ga1_ref, gb0_ref, gb1_ref,
                wa_ref, wc_ref, wo_ref, gffn_ref, wr_ref, br_ref,
                h_ref, u_ref, idx_ref, gate_ref):
    o = o_ref[...]
    c = c_ref[...]
    half = D_MODEL // 2
    acc = x_ref[...]
    for hh, (ga_ref, gb_ref) in enumerate(((ga0_ref, gb0_ref), (ga1_ref, gb1_ref))):
        cols = slice(hh * half, (hh + 1) * half)
        a = jnp.dot(o, wa_ref[:, cols], preferred_element_type=F32)
        b = jnp.dot(c, wc_ref[:, cols], preferred_element_type=F32)
        m = (jax.nn.sigmoid(ga_ref[...].astype(F32)) * a
             + jax.nn.sigmoid(gb_ref[...].astype(F32)) * b)
        acc = acc + jnp.dot(m.astype(BF16), wo_ref[cols, :], preferred_element_type=F32)
    h_ref[...] = acc
    u = _rms(acc, gffn_ref[...])
    u_ref[...] = u

    u_hi, u_lo = _split_bf16(u)
    w_hi, w_lo = _split_bf16(wr_ref[...])
    logits = (lax.dot_general(w_hi, u_hi, _NT, preferred_element_type=F32)
              + lax.dot_general(w_hi, u_lo, _NT, preferred_element_type=F32)
              + lax.dot_general(w_lo, u_hi, _NT, preferred_element_type=F32)
              + br_ref[...])
    eidx = lax.broadcasted_iota(I32, logits.shape, 0)
    vals, idxs = [], []
    for _ in range(TOP_K):
        mx = jnp.max(logits, axis=0, keepdims=True)
        sel = jnp.min(jnp.where(logits == mx, eidx, N_EXPERTS), axis=0, keepdims=True)
        vals.append(mx)
        idxs.append(sel)
        logits = jnp.where(eidx == sel, -jnp.inf, logits)
    ex = [jnp.exp(v - vals[0]) for v in vals]
    den = ex[0] + ex[1] + ex[2] + ex[3]
    idx_ref[...] = jnp.concatenate(idxs, axis=0)
    gate_ref[...] = jnp.concatenate([e / den for e in ex], axis=0)


def _mix(xt, o, c, proj, wa, wc, wo, g_ffn, wr_t, br):
    n = xt.shape[0]
    tm = TM_MIX
    half = D_MODEL // 2
    row = lambda w, j: pl.BlockSpec((tm, w), lambda i, j=j: (i, j))
    full = lambda a: pl.BlockSpec(a.shape, lambda i: (0,) * a.ndim)
    return pl.pallas_call(
        _mix_kernel,
        grid=(n // tm,),
        in_specs=[
            row(D_MODEL, 0), row(ATTN_WIDTH, 0), row(CONV_WIDTH, 0),
            row(half, COL_GATE_A), row(half, COL_GATE_A + 1),
            row(half, COL_GATE_B), row(half, COL_GATE_B + 1),
            full(wa), full(wc), full(wo), full(g_ffn), full(wr_t), full(br),
        ],
        out_specs=[
            pl.BlockSpec((tm, D_MODEL), lambda i: (i, 0)),
            pl.BlockSpec((tm, D_MODEL), lambda i: (i, 0)),
            pl.BlockSpec((TOP_K, tm), lambda i: (0, i)),
            pl.BlockSpec((TOP_K, tm), lambda i: (0, i)),
        ],
        out_shape=[
            jax.ShapeDtypeStruct((n, D_MODEL), F32),
            jax.ShapeDtypeStruct((n, D_MODEL), F32),
            jax.ShapeDtypeStruct((TOP_K, n), I32),
            jax.ShapeDtypeStruct((TOP_K, n), F32),
        ],
        compiler_params=_params(("parallel",)),
        name="mix",
    )(xt, o, c, proj, proj, proj, proj, wa, wc, wo, g_ffn, wr_t, br)


def _rank_kernel(idx_ref, rank_ref, cnt_ref, carry_ref):
    t = T_RANK

    @pl.when(pl.program_id(0) == 0)
    def _():
        carry_ref[...] = jnp.zeros_like(carry_ref)

    idx = idx_ref[...]
    eidx = lax.broadcasted_iota(I32, (N_EXPERTS, t), 0)
    onehot = [eidx == idx[k:k + 1, :] for k in range(TOP_K)]
    member = onehot[0] | onehot[1] | onehot[2] | onehot[3]
    r = lax.broadcasted_iota(I32, (t, t), 0)
    c = lax.broadcasted_iota(I32, (t, t), 1)
    tri = jnp.where(r < c, 1.0, 0.0).astype(BF16)
    mem_f = jnp.where(member, 1.0, 0.0)
    prefix = jnp.dot(mem_f.astype(BF16), tri, preferred_element_type=F32) + carry_ref[...]
    ranks = [jnp.sum(jnp.where(onehot[k], prefix, 0.0), axis=0, keepdims=True)
             for k in range(TOP_K)]
    rank_ref[...] = jnp.concatenate(ranks, axis=0).astype(I32)
    carry_ref[...] = carry_ref[...] + jnp.sum(mem_f, axis=1, keepdims=True)
    cnt_ref[...] = carry_ref[...].astype(I32)


def _rank(idx_t):
    n = idx_t.shape[1]
    return pl.pallas_call(
        _rank_kernel,
        grid=(n // T_RANK,),
        in_specs=[pl.BlockSpec((TOP_K, T_RANK), lambda i: (0, i))],
        out_specs=[
            pl.BlockSpec((TOP_K, T_RANK), lambda i: (0, i)),
            pl.BlockSpec((N_EXPERTS, 1), lambda i: (0, 0)),
        ],
        out_shape=[
            jax.ShapeDtypeStruct((TOP_K, n), I32),
            jax.ShapeDtypeStruct((N_EXPERTS, 1), I32),
        ],
        scratch_shapes=[pltpu.VMEM((N_EXPERTS, 1), F32)],
        compiler_params=_params(("arbitrary",)),
        name="rank",
    )(idx_t)


def _pos_kernel(idx_ref, rank_ref, pstart_ref, pos_ref):
    idx = idx_ref[...]
    eidx = lax.broadcasted_iota(I32, (N_EXPERTS, idx.shape[1]), 0)
    rows = [jnp.sum(jnp.where(eidx == idx[k:k + 1, :], pstart_ref[...], 0),
                    axis=0, keepdims=True) for k in range(TOP_K)]
    pos_ref[...] = jnp.concatenate(rows, axis=0) + rank_ref[...]


def _pos(idx_t, rank_t, pstart):
    n = idx_t.shape[1]
    blk = pl.BlockSpec((TOP_K, T_RANK), lambda i: (0, i))
    return pl.pallas_call(
        _pos_kernel,
        grid=(n // T_RANK,),
        in_specs=[blk, blk, pl.BlockSpec((N_EXPERTS, 1), lambda i: (0, 0))],
        out_specs=blk,
        out_shape=jax.ShapeDtypeStruct((TOP_K, n), I32),
        compiler_params=_params(("parallel",)),
        name="pos",
    )(idx_t, rank_t, pstart)


def _dispatch_kernel(pstart_ref, pend_ref, pos_ref, u_ref, xs_ref, zero_ref, sem, zsem):
    t = T_DISPATCH

    def tail_copy(e):
        start = pend_ref[e] - EXPERT_BLOCK
        return pltpu.make_async_copy(
            zero_ref, xs_ref.at[pl.ds(pl.multiple_of(start, EXPERT_BLOCK), EXPERT_BLOCK), :], zsem)

    @pl.when(pl.program_id(0) == 0)
    def _():
        zero_ref[...] = jnp.zeros_like(zero_ref)

        def z(e, carry):
            @pl.when(pend_ref[e] > pstart_ref[e])
            def _():
                cp = tail_copy(e)
                cp.start()
                cp.wait()
            return carry

        lax.fori_loop(0, N_EXPERTS, z, 0)

    def issue(tok, carry):
        for k in range(TOP_K):
            pltpu.make_async_copy(u_ref.at[pl.ds(tok, 1), :],
                                  xs_ref.at[pl.ds(pos_ref[k, tok], 1), :], sem).start()
        return carry

    lax.fori_loop(0, t, issue, 0, unroll=DMA_UNROLL)

    for k in range(TOP_K):
        pltpu.make_async_copy(u_ref, xs_ref.at[pl.ds(0, t), :], sem).wait()


def _dispatch(pstart, pend, pos_t, u, n_rows):
    n = u.shape[0]
    t = T_DISPATCH
    grid_spec = pltpu.PrefetchScalarGridSpec(
        num_scalar_prefetch=2,
        grid=(n // t,),
        in_specs=[
            pl.BlockSpec((TOP_K, t), lambda i, ps, pe: (0, i), memory_space=pltpu.SMEM),
            pl.BlockSpec((t, D_MODEL), lambda i, ps, pe: (i, 0)),
        ],
        out_specs=pl.BlockSpec(memory_space=pl.ANY),
        scratch_shapes=[
            pltpu.VMEM((EXPERT_BLOCK, D_MODEL), F32),
            pltpu.SemaphoreType.DMA(()),
            pltpu.SemaphoreType.DMA(()),
        ],
    )
    return pl.pallas_call(
        _dispatch_kernel,
        grid_spec=grid_spec,
        out_shape=jax.ShapeDtypeStruct((n_rows, D_MODEL), F32),
        compiler_params=_params(("arbitrary",)),
        name="dispatch",
    )(pstart, pend, pos_t, u)


def _expert_kernel(be_ref, nb_ref, x_ref, wgu_ref, bgu_ref, wd_ref, bd_ref, y_ref,
                   wgu_s, wd32_s, wd_s):
    i = pl.program_id(0)
    half = EXPERT_FF // 2
    first = jnp.logical_or(i == 0, be_ref[i] != be_ref[jnp.maximum(i - 1, 0)])

    @pl.when(jnp.logical_and(first, i < nb_ref[0]))
    def _():
        rows = 256
        for r in range(0, D_MODEL, rows):
            wgu_s[r:r + rows, :] = wgu_ref[r:r + rows, :].astype(BF16)
        for cs in range(D_MODEL // LANES):
            lanes = slice(cs * LANES, (cs + 1) * LANES)
            wd32_s[cs, pl.ds(0, half, stride=2), :] = wd_ref[0:half, lanes]
            wd32_s[cs, pl.ds(1, half, stride=2), :] = wd_ref[half:EXPERT_FF, lanes]
            wd_s[:, lanes] = wd32_s[cs].astype(BF16)

    @pl.when(i < nb_ref[0])
    def _():
        xb = x_ref[...].astype(BF16)
        ch = 256
        even = (lax.broadcasted_iota(I32, (EXPERT_BLOCK, LANES), 1) % 2) == 0
        y = jnp.zeros(y_ref.shape, F32) + bd_ref[...]
        for c in range(EXPERT_FF // ch):
            c1 = slice(c * ch, (c + 1) * ch)
            c2 = slice(EXPERT_FF + c * ch, EXPERT_FF + (c + 1) * ch)
            gu1 = jnp.dot(xb, wgu_s[:, c1], preferred_element_type=F32) + bgu_ref[:, c1]
            gu2 = jnp.dot(xb, wgu_s[:, c2], preferred_element_type=F32) + bgu_ref[:, c2]
            acts = []
            for v in range(ch // LANES):
                a = gu1[:, v * LANES:(v + 1) * LANES]
                b = gu2[:, v * LANES:(v + 1) * LANES]
                g = jnp.where(even, a, pltpu.roll(b, 1, axis=1))
                l = jnp.where(even, pltpu.roll(a, LANES - 1, axis=1), b)
                g = jnp.minimum(g, SWIGLU_LIMIT)
                l = jnp.clip(l, -SWIGLU_LIMIT, SWIGLU_LIMIT)
                acts.append(((l + 1.0) * (g * jax.nn.sigmoid(g * SWIGLU_ALPHA))).astype(BF16))
            act = jnp.concatenate(acts, axis=1)
            y = y + jnp.dot(act, wd_s[c1, :], preferred_element_type=F32)
        y_ref[...] = y

    @pl.when(i >= nb_ref[0])
    def _():
        y_ref[...] = jnp.zeros_like(y_ref)


def _experts(block_expert, n_used, xs, wgu, bgu, wd, bd):
    n_rows = xs.shape[0]
    blk = EXPERT_BLOCK
    wspec = lambda r, c: pl.BlockSpec((None, r, c), lambda i, be, nb: (be[i], 0, 0))
    grid_spec = pltpu.PrefetchScalarGridSpec(
        num_scalar_prefetch=2,
        grid=(n_rows // blk,),
        in_specs=[
            pl.BlockSpec((blk, D_MODEL), lambda i, be, nb: (jnp.minimum(i, nb[0] - 1), 0)),
            wspec(D_MODEL, 2 * EXPERT_FF), wspec(1, 2 * EXPERT_FF),
            wspec(EXPERT_FF, D_MODEL), wspec(1, D_MODEL),
        ],
        out_specs=pl.BlockSpec((blk, D_MODEL), lambda i, be, nb: (i, 0)),
        scratch_shapes=[
            pltpu.VMEM((D_MODEL, 2 * EXPERT_FF), BF16),
            pltpu.VMEM((D_MODEL // LANES, EXPERT_FF, LANES), F32),
            pltpu.VMEM((EXPERT_FF, D_MODEL), BF16),
        ],
    )
    return pl.pallas_call(
        _expert_kernel,
        grid_spec=grid_spec,
        out_shape=jax.ShapeDtypeStruct((n_rows, D_MODEL), F32),
        compiler_params=_params(("arbitrary",)),
        name="expert",
    )(block_expert, n_used, xs, wgu, bgu, wd, bd)


def _combine_kernel(pos_ref, h_ref, gate_ref, p_ref, y_ref, wple_ref, gple_ref, wpg_ref, gfin_ref,
                    o_ref, ybuf_ref, sem):
    t = T_COMBINE

    def issue(tok, carry):
        for k in range(TOP_K):
            pltpu.make_async_copy(y_ref.at[pl.ds(pos_ref[k, tok], 1), :],
                                  ybuf_ref.at[k, pl.ds(tok, 1), :], sem).start()
        return carry

    lax.fori_loop(0, t, issue, 0, unroll=DMA_UNROLL)

    emb = jnp.dot(p_ref[...].astype(BF16), wple_ref[...], preferred_element_type=F32)

    for k in range(TOP_K):
        pltpu.make_async_copy(y_ref.at[pl.ds(0, t), :], ybuf_ref.at[k], sem).wait()

    gates = gate_ref[...]
    h = h_ref[...]
    for k in range(TOP_K):
        h = h + gates[:, k:k + 1] * ybuf_ref[k]
    r = _rms(h, gple_ref[...]).astype(BF16)
    sig = jax.nn.sigmoid(jnp.dot(r, wpg_ref[...], preferred_element_type=F32))
    h = h + emb * sig
    o_ref[...] = _rms(h, gfin_ref[...])


def _combine(pos_t, h, gates, p2, y, wple, gple, wpg, gfin):
    n = h.shape[0]
    t = T_COMBINE
    full = lambda a: pl.BlockSpec(a.shape, lambda i: (0,) * a.ndim)
    return pl.pallas_call(
        _combine_kernel,
        grid=(n // t,),
        in_specs=[
            pl.BlockSpec((TOP_K, t), lambda i: (0, i), memory_space=pltpu.SMEM),
            pl.BlockSpec((t, D_MODEL), lambda i: (i, 0)),
            pl.BlockSpec((t, TOP_K), lambda i: (i, 0)),
            pl.BlockSpec((t, PLE_DIM), lambda i: (i, 0)),
            pl.BlockSpec(memory_space=pl.ANY),
            full(wple), full(gple), full(wpg), full(gfin),
        ],
        out_specs=pl.BlockSpec((t, D_MODEL), lambda i: (i, 0)),
        out_shape=jax.ShapeDtypeStruct((n, D_MODEL), F32),
        scratch_shapes=[
            pltpu.VMEM((TOP_K, t, D_MODEL), F32),
            pltpu.SemaphoreType.DMA(()),
        ],
        compiler_params=_params(("arbitrary",)),
        name="combine",
    )(pos_t, h, gates, p2, y, wple, gple, wpg, gfin)


def kernel(x, p, g_mix, w_in, lambda_q1, lambda_k1, lambda_q2, lambda_k2, g_subln, w_attn_out,
           w_dw, b_dw, g_conv_ln, b_conv_ln, w_conv_out, w_o, g_ffn, w_router, b_router,
           w_gate_up, b_gate_up, w_down, b_down, w_ple, g_ple, w_ple_gate, g_final):
    b, s, d = x.shape
    n = b * s
    xt = x.reshape(n, d)
    vec = lambda a: a.reshape(1, -1)

    w_in0 = w_in[0]
    w_main = jnp.concatenate([w_in0[:, :2 * ATTN_WIDTH], w_in0[:, 3 * ATTN_WIDTH:]], axis=1)
    w_vt = w_in0[:, 2 * ATTN_WIDTH:3 * ATTN_WIDTH].T
    proj, vt = _inproj(xt, vec(g_mix[0]), w_main.astype(BF16), w_vt.astype(BF16))
    proj3 = proj.reshape(b, s, MAIN_COLS)
    attn = _attention(proj3, vt, vec(lambda_q1[0]), vec(lambda_k1[0]), vec(lambda_q2[0]),
                      vec(lambda_k2[0]), g_subln[0].reshape(HEAD_V, 1))
    conv = _conv(proj3, w_dw[0], vec(b_dw[0]), vec(g_conv_ln[0]), vec(b_conv_ln[0]))
    h1, u2, idx_t, gates_t = _mix(
        xt, attn.reshape(n, ATTN_WIDTH), conv.reshape(n, CONV_WIDTH), proj,
        w_attn_out[0].astype(BF16), w_conv_out[0].astype(BF16), w_o[0].astype(BF16),
        vec(g_ffn[0]), w_router[0].T, b_router[0].reshape(N_EXPERTS, 1))

    rank_t, counts = _rank(idx_t)
    blk = EXPERT_BLOCK
    n_blocks = n * TOP_K // blk + N_EXPERTS
    padded = (counts[:, 0] + blk - 1) // blk * blk
    pend = jnp.cumsum(padded).astype(I32)
    pstart = pend - padded
    n_used = (pend[-1:] // blk).astype(I32)
    block_start = jnp.arange(n_blocks, dtype=I32) * blk
    block_expert = jnp.minimum(
        jnp.sum((pend[None, :] <= block_start[:, None]).astype(I32), axis=1), N_EXPERTS - 1)
    pos_t = _pos(idx_t, rank_t, pstart.reshape(N_EXPERTS, 1))

    xs = _dispatch(pstart, pend, pos_t, u2, n_blocks * blk)
    y = _experts(block_expert, n_used, xs, w_gate_up[0], b_gate_up[0][:, None, :],
                 w_down[0], b_down[0][:, None, :])

    out = _combine(pos_t, h1, gates_t.T, p[0].reshape(n, PLE_DIM), y,
                   w_ple[0].astype(BF16), vec(g_ple[0]), w_ple_gate[0].astype(BF16), vec(g_final))
    return out.reshape(b, s, d)
```

```python
import jax
import jax.numpy as jnp
from jax import lax
from jax.experimental import pallas as pl
from jax.experimental.pallas import tpu as pltpu

F32 = jnp.float32
BF16 = jnp.bfloat16
I32 = jnp.int32

D_MODEL = 1024
ATTN_WIDTH = 512
DIFF_HEADS = 4
HEAD_DIM = 64
HEAD_V = 2 * HEAD_DIM
CONV_WIDTH = 512
CONV_KERNEL = 31
N_EXPERTS = 32
TOP_K = 4
EXPERT_FF = 1024
PLE_DIM = 256
SWIGLU_ALPHA = 1.702
SWIGLU_LIMIT = 7.0
EPS = 1e-5
IN_COLS = 3 * ATTN_WIDTH + 2 * CONV_WIDTH + 2 * D_MODEL
LAMBDA_INIT = 0.2
LOG2_E = 1.4426950408889634

MAIN_COLS = IN_COLS - ATTN_WIDTH
COL_CONV_A = 2
COL_CONV_B = 3
COL_GATE_A = 4
COL_GATE_B = 6

LANES = 128
TM_PROJ = 512
TQ = 256
TM_MIX = 512
T_RANK = 512
T_DISPATCH = 512
EXPERT_BLOCK = 512
T_COMBINE = 256
CONV_ROWS = 64
CONV_PAD = 32
DMA_UNROLL = 8
VMEM_LIMIT = 56 * 1024 * 1024

_NT = (((1,), (1,)), ((), ()))


def _params(sem, vmem=VMEM_LIMIT):
    return pltpu.CompilerParams(dimension_semantics=sem, vmem_limit_bytes=vmem)


def _rms(x, g):
    return x * lax.rsqrt(jnp.mean(x * x, axis=-1, keepdims=True) + EPS) * g


def _inproj_kernel(x_ref, g_ref, w_ref, wvt_ref, o_ref, vt_ref):
    u = _rms(x_ref[...], g_ref[...]).astype(BF16)
    ch = 512
    for c in range(MAIN_COLS // ch):
        r = jnp.dot(u, w_ref[:, c * ch:(c + 1) * ch], preferred_element_type=F32)
        if c == 0:
            r = r * (HEAD_DIM ** -0.5 * LOG2_E)
        o_ref[:, c * ch:(c + 1) * ch] = r.astype(BF16)
    vt_ref[...] = lax.dot_general(wvt_ref[...], u, _NT,
                                  preferred_element_type=F32).astype(BF16)


def _inproj(xt, g, w, wvt):
    n = xt.shape[0]
    return pl.pallas_call(
        _inproj_kernel,
        grid=(n // TM_PROJ,),
        in_specs=[
            pl.BlockSpec((TM_PROJ, D_MODEL), lambda i: (i, 0)),
            pl.BlockSpec((1, D_MODEL), lambda i: (0, 0)),
            pl.BlockSpec((D_MODEL, MAIN_COLS), lambda i: (0, 0)),
            pl.BlockSpec((ATTN_WIDTH, D_MODEL), lambda i: (0, 0)),
        ],
        out_specs=[
            pl.BlockSpec((TM_PROJ, MAIN_COLS), lambda i: (i, 0)),
            pl.BlockSpec((ATTN_WIDTH, TM_PROJ), lambda i: (0, i)),
        ],
        out_shape=[
            jax.ShapeDtypeStruct((n, MAIN_COLS), BF16),
            jax.ShapeDtypeStruct((ATTN_WIDTH, n), BF16),
        ],
        compiler_params=_params(("parallel",)),
        name="inproj",
    )(xt, g, w, wvt)


def _attn_kernel(q_ref, k_ref, vt_ref, lq1_ref, lk1_ref, lq2_ref, lk2_ref, g_ref, o_ref):
    i = pl.program_id(1)
    s_len = k_ref.shape[0]
    lane = lax.broadcasted_iota(I32, (TQ, HEAD_V), 1)
    key = lax.broadcasted_iota(I32, (TQ, 2 * TQ), 0)
    qry = lax.broadcasted_iota(I32, (TQ, 2 * TQ), 1)
    causal = key <= jnp.where(qry >= TQ, qry - TQ, qry)
    lam = (jnp.exp(jnp.sum(lq1_ref[...] * lk1_ref[...]))
           - jnp.exp(jnp.sum(lq2_ref[...] * lk2_ref[...])) + LAMBDA_INIT)

    def head(c, h):
        cols = slice(h * HEAD_V, (h + 1) * HEAD_V)
        n = c * TQ
        q = q_ref[:, cols]
        zero = jnp.zeros_like(q)
        qq = jnp.concatenate([jnp.where(lane < HEAD_DIM, q, zero),
                              jnp.where(lane >= HEAD_DIM, q, zero)], axis=0)
        s_d = lax.dot_general(k_ref[n:n + TQ, cols], qq, _NT, preferred_element_type=F32)
        s_d = jnp.where(causal, s_d, -1e30)
        m = jnp.max(s_d, axis=0, keepdims=True)
        if c:
            s_f = lax.dot_general(k_ref[0:n, cols], qq, _NT, preferred_element_type=F32)
            m = jnp.maximum(m, jnp.max(s_f, axis=0, keepdims=True))
        p_d = jnp.exp2(s_d - m)
        l = jnp.sum(p_d, axis=0, keepdims=True)
        acc = jnp.dot(vt_ref[cols, n:n + TQ], p_d.astype(BF16), preferred_element_type=F32)
        if c:
            p_f = jnp.exp2(s_f - m)
            l = l + jnp.sum(p_f, axis=0, keepdims=True)
            acc = acc + jnp.dot(vt_ref[cols, 0:n], p_f.astype(BF16), preferred_element_type=F32)
        o12 = acc / l
        o = o12[:, :TQ] - lam * o12[:, TQ:]
        o = o * lax.rsqrt(jnp.mean(o * o, axis=0, keepdims=True) + EPS) * g_ref[...]
        o_ref[:, cols] = (o * (1.0 - LAMBDA_INIT)).T.astype(o_ref.dtype)

    def block(c):
        for h in range(DIFF_HEADS):
            head(c, h)

    for c in range(s_len // TQ):
        pl.when(i == c)(lambda c=c: block(c))


def _attention(proj3, vt, lq1, lk1, lq2, lk2, g_subln):
    b, s, _ = proj3.shape
    vec = pl.BlockSpec((1, HEAD_DIM), lambda bi, i: (0, 0))
    return pl.pallas_call(
        _attn_kernel,
        grid=(b, s // TQ),
        in_specs=[
            pl.BlockSpec((None, TQ, ATTN_WIDTH), lambda bi, i: (bi, i, 0)),
            pl.BlockSpec((None, s, ATTN_WIDTH), lambda bi, i: (bi, 0, 1)),
            pl.BlockSpec((ATTN_WIDTH, s), lambda bi, i: (0, bi)),
            vec, vec, vec, vec,
            pl.BlockSpec((HEAD_V, 1), lambda bi, i: (0, 0)),
        ],
        out_specs=pl.BlockSpec((None, TQ, ATTN_WIDTH), lambda bi, i: (bi, i, 0)),
        out_shape=jax.ShapeDtypeStruct((b, s, ATTN_WIDTH), BF16),
        compiler_params=_params(("parallel", "parallel")),
        name="attn",
    )(proj3, proj3, vt, lq1, lk1, lq2, lk2, g_subln)


def _conv_kernel(ca_ref, cb_ref, w_ref, b_ref, g_ref, beta_ref, o_ref, z_ref):
    s = ca_ref.shape[0]
    glu_rows = 256
    z_ref[0:CONV_PAD, :] = jnp.zeros((CONV_PAD, CONV_WIDTH), F32)

    def glu(c, carry):
        r0 = pl.multiple_of(c * glu_rows, glu_rows)
        a = ca_ref[pl.ds(r0, glu_rows), :].astype(F32)
        g = cb_ref[pl.ds(r0, glu_rows), :].astype(F32)
        z_ref[pl.ds(CONV_PAD + r0, glu_rows), :] = a * jax.nn.sigmoid(g)
        return carry

    lax.fori_loop(0, s // glu_rows, glu, 0)

    def conv(c, carry):
        r0 = pl.multiple_of(c * CONV_ROWS, CONV_ROWS)
        acc = jnp.zeros((CONV_ROWS, CONV_WIDTH), F32) + b_ref[...]
        for phase in range(8):
            rows = CONV_ROWS + (8 if phase else 0)
            part = None
            for j in range(CONV_KERNEL):
                off = CONV_PAD - (CONV_KERNEL - 1) + j
                if off % 8 != phase:
                    continue
                term = w_ref[j:j + 1, :] * z_ref[pl.ds(r0 + (off - phase), rows), :]
                part = term if part is None else part + term
            acc = acc + part[phase:phase + CONV_ROWS]
        mu = jnp.mean(acc, axis=-1, keepdims=True)
        xc = acc - mu
        y = xc * lax.rsqrt(jnp.mean(xc * xc, axis=-1, keepdims=True) + EPS)
        y = y * g_ref[...] + beta_ref[...]
        o_ref[pl.ds(r0, CONV_ROWS), :] = (y * jax.nn.sigmoid(y)).astype(o_ref.dtype)
        return carry

    lax.fori_loop(0, s // CONV_ROWS, conv, 0)


def _conv(proj3, w_dw, b_dw, g_ln, b_ln):
    b, s, _ = proj3.shape
    vec = pl.BlockSpec((1, CONV_WIDTH), lambda bi: (0, 0))
    return pl.pallas_call(
        _conv_kernel,
        grid=(b,),
        in_specs=[
            pl.BlockSpec((None, s, CONV_WIDTH), lambda bi: (bi, 0, COL_CONV_A)),
            pl.BlockSpec((None, s, CONV_WIDTH), lambda bi: (bi, 0, COL_CONV_B)),
            pl.BlockSpec((CONV_KERNEL, CONV_WIDTH), lambda bi: (0, 0)),
            vec, vec, vec,
        ],
        out_specs=pl.BlockSpec((None, s, CONV_WIDTH), lambda bi: (bi, 0, 0)),
        out_shape=jax.ShapeDtypeStruct((b, s, CONV_WIDTH), BF16),
        scratch_shapes=[pltpu.VMEM((s + CONV_PAD, CONV_WIDTH), F32)],
        compiler_params=_params(("parallel",)),
        name="conv",
    )(proj3, proj3, w_dw, b_dw, g_ln, b_ln)


def _split_bf16(a):
    hi = a.astype(BF16)
    lo = (a - hi.astype(F32)).astype(BF16)
    return hi, lo


def _mix_kernel(x_ref, o_ref, c_ref, ga0_ref, ga1_ref, gb0_ref, gb1_ref,
                wa_ref, wc_ref, wo_ref, gffn_ref, wr_ref, br_ref,
                h_ref, u_ref, idx_ref, gate_ref):
    o = o_ref[...]
    c = c_ref[...]
    half = D_MODEL // 2
    acc = x_ref[...]
    for hh, (ga_ref, gb_ref) in enumerate(((ga0_ref, gb0_ref), (ga1_ref, gb1_ref))):
        cols = slice(hh * half, (hh + 1) * half)
        a = jnp.dot(o, wa_ref[:, cols], preferred_element_type=F32)
        b = jnp.dot(c, wc_ref[:, cols], preferred_element_type=F32)
        m = (jax.nn.sigmoid(ga_ref[...].astype(F32)) * a
             + jax.nn.sigmoid(gb_ref[...].astype(F32)) * b)
        acc = acc + jnp.dot(m.astype(BF16), wo_ref[cols, :], preferred_element_type=F32)
    h_ref[...] = acc
    u = _rms(acc, gffn_ref[...])
    u_ref[...] = u

    u_hi, u_lo = _split_bf16(u)
    w_hi, w_lo = _split_bf16(wr_ref[...])
    logits = (lax.dot_general(w_hi, u_hi, _NT, preferred_element_type=F32)
              + lax.dot_general(w_hi, u_lo, _NT, preferred_element_type=F32)
              + lax.dot_general(w_lo, u_hi, _NT, preferred_element_type=F32)
              + br_ref[...])
    eidx = lax.broadcasted_iota(I32, logits.shape, 0)
    vals, idxs = [], []
    for _ in range(TOP_K):
        mx = jnp.max(logits, axis=0, keepdims=True)
        sel = jnp.min(jnp.where(logits == mx, eidx, N_EXPERTS), axis=0, keepdims=True)
        vals.append(mx)
        idxs.append(sel)
        logits = jnp.where(eidx == sel, -jnp.inf, logits)
    ex = [jnp.exp(v - vals[0]) for v in vals]
    den = ex[0] + ex[1] + ex[2] + ex[3]
    idx_ref[...] = jnp.concatenate(idxs, axis=0)
    gate_ref[...] = jnp.concatenate([e / den for e in ex], axis=0)


def _mix(xt, o, c, proj, wa, wc, wo, g_ffn, wr_t, br):
    n = xt.shape[0]
    tm = TM_MIX
    half = D_MODEL // 2
    row = lambda w, j: pl.BlockSpec((tm, w), lambda i, j=j: (i, j))
    full = lambda a: pl.BlockSpec(a.shape, lambda i: (0,) * a.ndim)
    return pl.pallas_call(
        _mix_kernel,
        grid=(n // tm,),
        in_specs=[
            row(D_MODEL, 0), row(ATTN_WIDTH, 0), row(CONV_WIDTH, 0),
            row(half, COL_GATE_A), row(half, COL_GATE_A + 1),
            row(half, COL_GATE_B), row(half, COL_GATE_B + 1),
            full(wa), full(wc), full(wo), full(g_ffn), full(wr_t), full(br),
        ],
        out_specs=[
            pl.BlockSpec((tm, D_MODEL), lambda i: (i, 0)),
            pl.BlockSpec((tm, D_MODEL), lambda i: (i, 0)),
            pl.BlockSpec((TOP_K, tm), lambda i: (0, i)),
            pl.BlockSpec((TOP_K, tm), lambda i: (0, i)),
        ],
        out_shape=[
            jax.ShapeDtypeStruct((n, D_MODEL), F32),
            jax.ShapeDtypeStruct((n, D_MODEL), F32),
            jax.ShapeDtypeStruct((TOP_K, n), I32),
            jax.ShapeDtypeStruct((TOP_K, n), F32),
        ],
        compiler_params=_params(("parallel",)),
        name="mix",
    )(xt, o, c, proj, proj, proj, proj, wa, wc, wo, g_ffn, wr_t, br)


def _rank_kernel(idx_ref, rank_ref, cnt_ref, carry_ref):
    t = T_RANK

    @pl.when(pl.program_id(0) == 0)
    def _():
        carry_ref[...] = jnp.zeros_like(carry_ref)

    idx = idx_ref[...]
    eidx = lax.broadcasted_iota(I32, (N_EXPERTS, t), 0)
    onehot = [eidx == idx[k:k + 1, :] for k in range(TOP_K)]
    member = onehot[0] | onehot[1] | onehot[2] | onehot[3]
    r = lax.broadcasted_iota(I32, (t, t), 0)
    c = lax.broadcasted_iota(I32, (t, t), 1)
    tri = jnp.where(r < c, 1.0, 0.0).astype(BF16)
    mem_f = jnp.where(member, 1.0, 0.0)
    prefix = jnp.dot(mem_f.astype(BF16), tri, preferred_element_type=F32) + carry_ref[...]
    ranks = [jnp.sum(jnp.where(onehot[k], prefix, 0.0), axis=0, keepdims=True)
             for k in range(TOP_K)]
    rank_ref[...] = jnp.concatenate(ranks, axis=0).astype(I32)
    carry_ref[...] = carry_ref[...] + jnp.sum(mem_f, axis=1, keepdims=True)
    cnt_ref[...] = carry_ref[...].astype(I32)


def _rank(idx_t):
    n = idx_t.shape[1]
    return pl.pallas_call(
        _rank_kernel,
        grid=(n // T_RANK,),
        in_specs=[pl.BlockSpec((TOP_K, T_RANK), lambda i: (0, i))],
        out_specs=[
            pl.BlockSpec((TOP_K, T_RANK), lambda i: (0, i)),
            pl.BlockSpec((N_EXPERTS, 1), lambda i: (0, 0)),
        ],
        out_shape=[
            jax.ShapeDtypeStruct((TOP_K, n), I32),
            jax.ShapeDtypeStruct((N_EXPERTS, 1), I32),
        ],
        scratch_shapes=[pltpu.VMEM((N_EXPERTS, 1), F32)],
        compiler_params=_params(("arbitrary",)),
        name="rank",
    )(idx_t)


def _pos_kernel(idx_ref, rank_ref, pstart_ref, pos_ref):
    idx = idx_ref[...]
    eidx = lax.broadcasted_iota(I32, (N_EXPERTS, idx.shape[1]), 0)
    rows = [jnp.sum(jnp.where(eidx == idx[k:k + 1, :], pstart_ref[...], 0),
                    axis=0, keepdims=True) for k in range(TOP_K)]
    pos_ref[...] = jnp.concatenate(rows, axis=0) + rank_ref[...]


def _pos(idx_t, rank_t, pstart):
    n = idx_t.shape[1]
    blk = pl.BlockSpec((TOP_K, T_RANK), lambda i: (0, i))
    return pl.pallas_call(
        _pos_kernel,
        grid=(n // T_RANK,),
        in_specs=[blk, blk, pl.BlockSpec((N_EXPERTS, 1), lambda i: (0, 0))],
        out_specs=blk,
        out_shape=jax.ShapeDtypeStruct((TOP_K, n), I32),
        compiler_params=_params(("parallel",)),
        name="pos",
    )(idx_t, rank_t, pstart)


def _dispatch_kernel(pstart_ref, pend_ref, pos_ref, u_ref, xs_ref, zero_ref, sem, zsem):
    t = T_DISPATCH

    def tail_copy(e):
        start = pend_ref[e] - EXPERT_BLOCK
        return pltpu.make_async_copy(
            zero_ref, xs_ref.at[pl.ds(pl.multiple_of(start, EXPERT_BLOCK), EXPERT_BLOCK), :], zsem)

    @pl.when(pl.program_id(0) == 0)
    def _():
        zero_ref[...] = jnp.zeros_like(zero_ref)

        def z(e, carry):
            @pl.when(pend_ref[e] > pstart_ref[e])
            def _():
                cp = tail_copy(e)
                cp.start()
                cp.wait()
            return carry

        lax.fori_loop(0, N_EXPERTS, z, 0)

    def issue(tok, carry):
        for k in range(TOP_K):
            pltpu.make_async_copy(u_ref.at[pl.ds(tok, 1), :],
                                  xs_ref.at[pl.ds(pos_ref[k, tok], 1), :], sem).start()
        return carry

    lax.fori_loop(0, t, issue, 0, unroll=DMA_UNROLL)

    for k in range(TOP_K):
        pltpu.make_async_copy(u_ref, xs_ref.at[pl.ds(0, t), :], sem).wait()


def _dispatch(pstart, pend, pos_t, u, n_rows):
    n = u.shape[0]
    t = T_DISPATCH
    grid_spec = pltpu.PrefetchScalarGridSpec(
        num_scalar_prefetch=2,
        grid=(n // t,),
        in_specs=[
            pl.BlockSpec((TOP_K, t), lambda i, ps, pe: (0, i), memory_space=pltpu.SMEM),
            pl.BlockSpec((t, D_MODEL), lambda i, ps, pe: (i, 0)),
        ],
        out_specs=pl.BlockSpec(memory_space=pl.ANY),
        scratch_shapes=[
            pltpu.VMEM((EXPERT_BLOCK, D_MODEL), F32),
            pltpu.SemaphoreType.DMA(()),
            pltpu.SemaphoreType.DMA(()),
        ],
    )
    return pl.pallas_call(
        _dispatch_kernel,
        grid_spec=grid_spec,
        out_shape=jax.ShapeDtypeStruct((n_rows, D_MODEL), F32),
        compiler_params=_params(("arbitrary",)),
        name="dispatch",
    )(pstart, pend, pos_t, u)


def _expert_kernel(be_ref, nb_ref, x_ref, wgu_ref, bgu_ref, wd_ref, bd_ref, y_ref,
                   wgu_s, wd32_s, wd_s):
    i = pl.program_id(0)
    half = EXPERT_FF // 2
    first = jnp.logical_or(i == 0, be_ref[i] != be_ref[jnp.maximum(i - 1, 0)])

    @pl.when(jnp.logical_and(first, i < nb_ref[0]))
    def _():
        rows = 256
        for r in range(0, D_MODEL, rows):
            wgu_s[r:r + rows, :] = wgu_ref[r:r + rows, :].astype(BF16)
        for cs in range(D_MODEL // LANES):
            lanes = slice(cs * LANES, (cs + 1) * LANES)
            wd32_s[cs, pl.ds(0, half, stride=2), :] = wd_ref[0:half, lanes]
            wd32_s[cs, pl.ds(1, half, stride=2), :] = wd_ref[half:EXPERT_FF, lanes]
            wd_s[:, lanes] = wd32_s[cs].astype(BF16)

    @pl.when(i < nb_ref[0])
    def _():
        xb = x_ref[...].astype(BF16)
        ch = 256
        even = (lax.broadcasted_iota(I32, (EXPERT_BLOCK, LANES), 1) % 2) == 0
        y = jnp.zeros(y_ref.shape, F32) + bd_ref[...]
        for c in range(EXPERT_FF // ch):
            c1 = slice(c * ch, (c + 1) * ch)
            c2 = slice(EXPERT_FF + c * ch, EXPERT_FF + (c + 1) * ch)
            gu1 = jnp.dot(xb, wgu_s[:, c1], preferred_element_type=F32) + bgu_ref[:, c1]
            gu2 = jnp.dot(xb, wgu_s[:, c2], preferred_element_type=F32) + bgu_ref[:, c2]
            acts = []
            for v in range(ch // LANES):
                a = gu1[:, v * LANES:(v + 1) * LANES]
                b = gu2[:, v * LANES:(v + 1) * LANES]
                g = jnp.where(even, a, pltpu.roll(b, 1, axis=1))
                l = jnp.where(even, pltpu.roll(a, LANES - 1, axis=1), b)
                g = jnp.minimum(g, SWIGLU_LIMIT)
                l = jnp.clip(l, -SWIGLU_LIMIT, SWIGLU_LIMIT)
                acts.append(((l + 1.0) * (g * jax.nn.sigmoid(g * SWIGLU_ALPHA))).astype(BF16))
            act = jnp.concatenate(acts, axis=1)
            y = y + jnp.dot(act, wd_s[c1, :], preferred_element_type=F32)
        y_ref[...] = y

    @pl.when(i >= nb_ref[0])
    def _():
        y_ref[...] = jnp.zeros_like(y_ref)


def _experts(block_expert, n_used, xs, wgu, bgu, wd, bd):
    n_rows = xs.shape[0]
    blk = EXPERT_BLOCK
    wspec = lambda r, c: pl.BlockSpec((None, r, c), lambda i, be, nb: (be[i], 0, 0))
    grid_spec = pltpu.PrefetchScalarGridSpec(
        num_scalar_prefetch=2,
        grid=(n_rows // blk,),
        in_specs=[
            pl.BlockSpec((blk, D_MODEL), lambda i, be, nb: (jnp.minimum(i, nb[0] - 1), 0)),
            wspec(D_MODEL, 2 * EXPERT_FF), wspec(1, 2 * EXPERT_FF),
            wspec(EXPERT_FF, D_MODEL), wspec(1, D_MODEL),
        ],
        out_specs=pl.BlockSpec((blk, D_MODEL), lambda i, be, nb: (i, 0)),
        scratch_shapes=[
            pltpu.VMEM((D_MODEL, 2 * EXPERT_FF), BF16),
            pltpu.VMEM((D_MODEL // LANES, EXPERT_FF, LANES), F32),
            pltpu.VMEM((EXPERT_FF, D_MODEL), BF16),
        ],
    )
    return pl.pallas_call(
        _expert_kernel,
        grid_spec=grid_spec,
        out_shape=jax.ShapeDtypeStruct((n_rows, D_MODEL), F32),
        compiler_params=_params(("arbitrary",)),
        name="expert",
    )(block_expert, n_used, xs, wgu, bgu, wd, bd)


def _combine_kernel(pos_ref, h_ref, gate_ref, p_ref, y_ref, wple_ref, gple_ref, wpg_ref, gfin_ref,
                    o_ref, ybuf_ref, sem):
    t = T_COMBINE

    def issue(tok, carry):
        for k in range(TOP_K):
            pltpu.make_async_copy(y_ref.at[pl.ds(pos_ref[k, tok], 1), :],
                                  ybuf_ref.at[k, pl.ds(tok, 1), :], sem).start()
        return carry

    lax.fori_loop(0, t, issue, 0, unroll=DMA_UNROLL)

    emb = jnp.dot(p_ref[...].astype(BF16), wple_ref[...], preferred_element_type=F32)

    for k in range(TOP_K):
        pltpu.make_async_copy(y_ref.at[pl.ds(0, t), :], ybuf_ref.at[k], sem).wait()

    gates = gate_ref[...]
    h = h_ref[...]
    for k in range(TOP_K):
        h = h + gates[:, k:k + 1] * ybuf_ref[k]
    r = _rms(h, gple_ref[...]).astype(BF16)
    sig = jax.nn.sigmoid(jnp.dot(r, wpg_ref[...], preferred_element_type=F32))
    h = h + emb * sig
    o_ref[...] = _rms(h, gfin_ref[...])


def _combine(pos_t, h, gates, p2, y, wple, gple, wpg, gfin):
    n = h.shape[0]
    t = T_COMBINE
    full = lambda a: pl.BlockSpec(a.shape, lambda i: (0,) * a.ndim)
    return pl.pallas_call(
        _combine_kernel,
        grid=(n // t,),
        in_specs=[
            pl.BlockSpec((TOP_K, t), lambda i: (0, i), memory_space=pltpu.SMEM),
            pl.BlockSpec((t, D_MODEL), lambda i: (i, 0)),
            pl.BlockSpec((t, TOP_K), lambda i: (i, 0)),
            pl.BlockSpec((t, PLE_DIM), lambda i: (i, 0)),
            pl.BlockSpec(memory_space=pl.ANY),
            full(wple), full(gple), full(wpg), full(gfin),
        ],
        out_specs=pl.BlockSpec((t, D_MODEL), lambda i: (i, 0)),
        out_shape=jax.ShapeDtypeStruct((n, D_MODEL), F32),
        scratch_shapes=[
            pltpu.VMEM((TOP_K, t, D_MODEL), F32),
            pltpu.SemaphoreType.DMA(()),
        ],
        compiler_params=_params(("arbitrary",)),
        name="combine",
    )(pos_t, h, gates, p2, y, wple, gple, wpg, gfin)


def kernel(x, p, g_mix, w_in, lambda_q1, lambda_k1, lambda_q2, lambda_k2, g_subln, w_attn_out,
           w_dw, b_dw, g_conv_ln, b_conv_ln, w_conv_out, w_o, g_ffn, w_router, b_router,
           w_gate_up, b_gate_up, w_down, b_down, w_ple, g_ple, w_ple_gate, g_final):
    b, s, d = x.shape
    n = b * s
    xt = x.reshape(n, d)
    vec = lambda a: a.reshape(1, -1)

    w_in0 = w_in[0]
    w_main = jnp.concatenate([w_in0[:, :2 * ATTN_WIDTH], w_in0[:, 3 * ATTN_WIDTH:]], axis=1)
    w_vt = w_in0[:, 2 * ATTN_WIDTH:3 * ATTN_WIDTH].T
    proj, vt = _inproj(xt, vec(g_mix[0]), w_main.astype(BF16), w_vt.astype(BF16))
    proj3 = proj.reshape(b, s, MAIN_COLS)
    attn = _attention(proj3, vt, vec(lambda_q1[0]), vec(lambda_k1[0]), vec(lambda_q2[0]),
                      vec(lambda_k2[0]), g_subln[0].reshape(HEAD_V, 1))
    conv = _conv(proj3, w_dw[0], vec(b_dw[0]), vec(g_conv_ln[0]), vec(b_conv_ln[0]))
    h1, u2, idx_t, gates_t = _mix(
        xt, attn.reshape(n, ATTN_WIDTH), conv.reshape(n, CONV_WIDTH), proj,
        w_attn_out[0].astype(BF16), w_conv_out[0].astype(BF16), w_o[0].astype(BF16),
        vec(g_ffn[0]), w_router[0].T, b_router[0].reshape(N_EXPERTS, 1))

    rank_t, counts = _rank(idx_t)
    blk = EXPERT_BLOCK
    n_blocks = n * TOP_K // blk + N_EXPERTS
    padded = (counts[:, 0] + blk - 1) // blk * blk
    pend = jnp.cumsum(padded).astype(I32)
    pstart = pend - padded
    n_used = (pend[-1:] // blk).astype(I32)
    block_start = jnp.arange(n_blocks, dtype=I32) * blk
    block_expert = jnp.minimum(
        jnp.sum((pend[None, :] <= block_start[:, None]).astype(I32), axis=1), N_EXPERTS - 1)
    pos_t = _pos(idx_t, rank_t, pstart.reshape(N_EXPERTS, 1))

    xs = _dispatch(pstart, pend, pos_t, u2, n_blocks * blk)
    y = _experts(block_expert, n_used, xs, w_gate_up[0], b_gate_up[0][:, None, :],
                 w_down[0], b_down[0][:, None, :])

    out = _combine(pos_t, h1, gates_t.T, p[0].reshape(n, PLE_DIM), y,
                   w_ple[0].astype(BF16), vec(g_ple[0]), w_ple_gate[0].astype(BF16), vec(g_final))
    return out.reshape(b, s, d)
```

```python
import jax
import jax.numpy as jnp
from jax import lax
from jax.experimental import pallas as pl
from jax.experimental.pallas import tpu as pltpu

F32 = jnp.float32
BF16 = jnp.bfloat16
I32 = jnp.int32

D_MODEL = 1024
ATTN_WIDTH = 512
DIFF_HEADS = 4
HEAD_DIM = 64
HEAD_V = 2 * HEAD_DIM
CONV_WIDTH = 512
CONV_KERNEL = 31
N_EXPERTS = 32
TOP_K = 4
EXPERT_FF = 1024
PLE_DIM = 256
SWIGLU_ALPHA = 1.702
SWIGLU_LIMIT = 7.0
EPS = 1e-5
IN_COLS = 3 * ATTN_WIDTH + 2 * CONV_WIDTH + 2 * D_MODEL
LAMBDA_INIT = 0.2
LOG2_E = 1.4426950408889634

MAIN_COLS = IN_COLS - ATTN_WIDTH
COL_CONV_A = 2
COL_CONV_B = 3
COL_GATE_A = 4
COL_GATE_B = 6

LANES = 128
TM_PROJ = 512
TQ = 256
TM_MIX = 512
T_RANK = 512
T_DISPATCH = 512
EXPERT_BLOCK = 512
T_COMBINE = 256
CONV_ROWS = 64
CONV_PAD = 32
DMA_UNROLL = 8
VMEM_LIMIT = 56 * 1024 * 1024

_NT = (((1,), (1,)), ((), ()))


def _params(sem, vmem=VMEM_LIMIT):
    return pltpu.CompilerParams(dimension_semantics=sem, vmem_limit_bytes=vmem)


def _rms(x, g):
    return x * lax.rsqrt(jnp.mean(x * x, axis=-1, keepdims=True) + EPS) * g


ROW_TILE = D_MODEL // LANES


def _store_row_tiles(ref, x, lead=()):
    rows = x.shape[0]
    for g in range(ROW_TILE):
        ref[lead + (pl.ds(g, rows, stride=ROW_TILE), slice(None))] = x[:, g * LANES:(g + 1) * LANES]


def _load_row_tiles(ref, rows, lead=()):
    return jnp.concatenate(
        [ref[lead + (pl.ds(g, rows, stride=ROW_TILE), slice(None))] for g in range(ROW_TILE)], axis=1)


def _inproj_kernel(x_ref, g_ref, w_ref, wvt_ref, o_ref, vt_ref):
    u = _rms(x_ref[...], g_ref[...]).astype(BF16)
    ch = 512
    for c in range(MAIN_COLS // ch):
        r = jnp.dot(u, w_ref[:, c * ch:(c + 1) * ch], preferred_element_type=F32)
        if c == 0:
            r = r * (HEAD_DIM ** -0.5 * LOG2_E)
        o_ref[:, c * ch:(c + 1) * ch] = r.astype(BF16)
    vt_ref[...] = lax.dot_general(wvt_ref[...], u, _NT,
                                  preferred_element_type=F32).astype(BF16)


def _inproj(xt, g, w, wvt):
    n = xt.shape[0]
    return pl.pallas_call(
        _inproj_kernel,
        grid=(n // TM_PROJ,),
        in_specs=[
            pl.BlockSpec((TM_PROJ, D_MODEL), lambda i: (i, 0)),
            pl.BlockSpec((1, D_MODEL), lambda i: (0, 0)),
            pl.BlockSpec((D_MODEL, MAIN_COLS), lambda i: (0, 0)),
            pl.BlockSpec((ATTN_WIDTH, D_MODEL), lambda i: (0, 0)),
        ],
        out_specs=[
            pl.BlockSpec((TM_PROJ, MAIN_COLS), lambda i: (i, 0)),
            pl.BlockSpec((ATTN_WIDTH, TM_PROJ), lambda i: (0, i)),
        ],
        out_shape=[
            jax.ShapeDtypeStruct((n, MAIN_COLS), BF16),
            jax.ShapeDtypeStruct((ATTN_WIDTH, n), BF16),
        ],
        compiler_params=_params(("parallel",)),
        name="inproj",
    )(xt, g, w, wvt)


def _attn_kernel(q_ref, k_ref, vt_ref, lq1_ref, lk1_ref, lq2_ref, lk2_ref, g_ref, o_ref):
    i = pl.program_id(1)
    s_len = k_ref.shape[0]
    lane = lax.broadcasted_iota(I32, (TQ, HEAD_V), 1)
    key = lax.broadcasted_iota(I32, (TQ, 2 * TQ), 0)
    qry = lax.broadcasted_iota(I32, (TQ, 2 * TQ), 1)
    causal = key <= jnp.where(qry >= TQ, qry - TQ, qry)
    lam = (jnp.exp(jnp.sum(lq1_ref[...] * lk1_ref[...]))
           - jnp.exp(jnp.sum(lq2_ref[...] * lk2_ref[...])) + LAMBDA_INIT)

    def head(c, h):
        cols = slice(h * HEAD_V, (h + 1) * HEAD_V)
        n = c * TQ
        q = q_ref[:, cols]
        zero = jnp.zeros_like(q)
        qq = jnp.concatenate([jnp.where(lane < HEAD_DIM, q, zero),
                              jnp.where(lane >= HEAD_DIM, q, zero)], axis=0)
        s_d = lax.dot_general(k_ref[n:n + TQ, cols], qq, _NT, preferred_element_type=F32)
        s_d = jnp.where(causal, s_d, -1e30)
        m = jnp.max(s_d, axis=0, keepdims=True)
        if c:
            s_f = lax.dot_general(k_ref[0:n, cols], qq, _NT, preferred_element_type=F32)
            m = jnp.maximum(m, jnp.max(s_f, axis=0, keepdims=True))
        p_d = jnp.exp2(s_d - m)
        l = jnp.sum(p_d, axis=0, keepdims=True)
        acc = jnp.dot(vt_ref[cols, n:n + TQ], p_d.astype(BF16), preferred_element_type=F32)
        if c:
            p_f = jnp.exp2(s_f - m)
            l = l + jnp.sum(p_f, axis=0, keepdims=True)
            acc = acc + jnp.dot(vt_ref[cols, 0:n], p_f.astype(BF16), preferred_element_type=F32)
        o12 = acc / l
        o = o12[:, :TQ] - lam * o12[:, TQ:]
        o = o * lax.rsqrt(jnp.mean(o * o, axis=0, keepdims=True) + EPS) * g_ref[...]
        o_ref[:, cols] = (o * (1.0 - LAMBDA_INIT)).T.astype(o_ref.dtype)

    def block(c):
        for h in range(DIFF_HEADS):
            head(c, h)

    for c in range(s_len // TQ):
        pl.when(i == c)(lambda c=c: block(c))


def _attention(proj3, vt, lq1, lk1, lq2, lk2, g_subln):
    b, s, _ = proj3.shape
    vec = pl.BlockSpec((1, HEAD_DIM), lambda bi, i: (0, 0))
    return pl.pallas_call(
        _attn_kernel,
        grid=(b, s // TQ),
        in_specs=[
            pl.BlockSpec((None, TQ, ATTN_WIDTH), lambda bi, i: (bi, i, 0)),
            pl.BlockSpec((None, s, ATTN_WIDTH), lambda bi, i: (bi, 0, 1)),
            pl.BlockSpec((ATTN_WIDTH, s), lambda bi, i: (0, bi)),
            vec, vec, vec, vec,
            pl.BlockSpec((HEAD_V, 1), lambda bi, i: (0, 0)),
        ],
        out_specs=pl.BlockSpec((None, TQ, ATTN_WIDTH), lambda bi, i: (bi, i, 0)),
        out_shape=jax.ShapeDtypeStruct((b, s, ATTN_WIDTH), BF16),
        compiler_params=_params(("parallel", "parallel")),
        name="attn",
    )(proj3, proj3, vt, lq1, lk1, lq2, lk2, g_subln)


def _conv_kernel(ca_ref, cb_ref, w_ref, b_ref, g_ref, beta_ref, o_ref, z_ref):
    s = ca_ref.shape[0]
    glu_rows = 256
    z_ref[0:CONV_PAD, :] = jnp.zeros((CONV_PAD, CONV_WIDTH), F32)

    def glu(c, carry):
        r0 = pl.multiple_of(c * glu_rows, glu_rows)
        a = ca_ref[pl.ds(r0, glu_rows), :].astype(F32)
        g = cb_ref[pl.ds(r0, glu_rows), :].astype(F32)
        z_ref[pl.ds(CONV_PAD + r0, glu_rows), :] = a * jax.nn.sigmoid(g)
        return carry

    lax.fori_loop(0, s // glu_rows, glu, 0)

    def conv(c, carry):
        r0 = pl.multiple_of(c * CONV_ROWS, CONV_ROWS)
        acc = jnp.zeros((CONV_ROWS, CONV_WIDTH), F32) + b_ref[...]
        for phase in range(8):
            rows = CONV_ROWS + (8 if phase else 0)
            part = None
            for j in range(CONV_KERNEL):
                off = CONV_PAD - (CONV_KERNEL - 1) + j
                if off % 8 != phase:
                    continue
                term = w_ref[j:j + 1, :] * z_ref[pl.ds(r0 + (off - phase), rows), :]
                part = term if part is None else part + term
            acc = acc + part[phase:phase + CONV_ROWS]
        mu = jnp.mean(acc, axis=-1, keepdims=True)
        xc = acc - mu
        y = xc * lax.rsqrt(jnp.mean(xc * xc, axis=-1, keepdims=True) + EPS)
        y = y * g_ref[...] + beta_ref[...]
        o_ref[pl.ds(r0, CONV_ROWS), :] = (y * jax.nn.sigmoid(y)).astype(o_ref.dtype)
        return carry

    lax.fori_loop(0, s // CONV_ROWS, conv, 0)


def _conv(proj3, w_dw, b_dw, g_ln, b_ln):
    b, s, _ = proj3.shape
    vec = pl.BlockSpec((1, CONV_WIDTH), lambda bi: (0, 0))
    return pl.pallas_call(
        _conv_kernel,
        grid=(b,),
        in_specs=[
            pl.BlockSpec((None, s, CONV_WIDTH), lambda bi: (bi, 0, COL_CONV_A)),
            pl.BlockSpec((None, s, CONV_WIDTH), lambda bi: (bi, 0, COL_CONV_B)),
            pl.BlockSpec((CONV_KERNEL, CONV_WIDTH), lambda bi: (0, 0)),
            vec, vec, vec,
        ],
        out_specs=pl.BlockSpec((None, s, CONV_WIDTH), lambda bi: (bi, 0, 0)),
        out_shape=jax.ShapeDtypeStruct((b, s, CONV_WIDTH), BF16),
        scratch_shapes=[pltpu.VMEM((s + CONV_PAD, CONV_WIDTH), F32)],
        compiler_params=_params(("parallel",)),
        name="conv",
    )(proj3, proj3, w_dw, b_dw, g_ln, b_ln)


def _split_bf16(a):
    hi = a.astype(BF16)
    lo = (a - hi.astype(F32)).astype(BF16)
    return hi, lo


def _mix_kernel(x_ref, o_ref, c_ref, ga0_ref, ga1_ref, gb0_ref, gb1_ref,
                wa_ref, wc_ref, wo_ref, gffn_ref, wr_ref, br_ref,
                h_ref, u_ref, idx_ref, gate_ref):
    o = o_ref[...]
    c = c_ref[...]
    half = D_MODEL // 2
    acc = x_ref[...]
    for hh, (ga_ref, gb_ref) in enumerate(((ga0_ref, gb0_ref), (ga1_ref, gb1_ref))):
        cols = slice(hh * half, (hh + 1) * half)
        a = jnp.dot(o, wa_ref[:, cols], preferred_element_type=F32)
        b = jnp.dot(c, wc_ref[:, cols], preferred_element_type=F32)
        m = (jax.nn.sigmoid(ga_ref[...].astype(F32)) * a
             + jax.nn.sigmoid(gb_ref[...].astype(F32)) * b)
        acc = acc + jnp.dot(m.astype(BF16), wo_ref[cols, :], preferred_element_type=F32)
    h_ref[...] = acc
    u = _rms(acc, gffn_ref[...])
    _store_row_tiles(u_ref, u)

    u_hi, u_lo = _split_bf16(u)
    w_hi, w_lo = _split_bf16(wr_ref[...])
    logits = (lax.dot_general(w_hi, u_hi, _NT, preferred_element_type=F32)
              + lax.dot_general(w_hi, u_lo, _NT, preferred_element_type=F32)
              + lax.dot_general(w_lo, u_hi, _NT, preferred_element_type=F32)
              + br_ref[...])
    eidx = lax.broadcasted_iota(I32, logits.shape, 0)
    vals, idxs = [], []
    for _ in range(TOP_K):
        mx = jnp.max(logits, axis=0, keepdims=True)
        sel = jnp.min(jnp.where(logits == mx, eidx, N_EXPERTS), axis=0, keepdims=True)
        vals.append(mx)
        idxs.append(sel)
        logits = jnp.where(eidx == sel, -jnp.inf, logits)
    ex = [jnp.exp(v - vals[0]) for v in vals]
    den = ex[0] + ex[1] + ex[2] + ex[3]
    idx_ref[...] = jnp.concatenate(idxs, axis=0)
    gate_ref[...] = jnp.concatenate([e / den for e in ex], axis=0)


def _mix(xt, o, c, proj, wa, wc, wo, g_ffn, wr_t, br):
    n = xt.shape[0]
    tm = TM_MIX
    half = D_MODEL // 2
    row = lambda w, j: pl.BlockSpec((tm, w), lambda i, j=j: (i, j))
    full = lambda a: pl.BlockSpec(a.shape, lambda i: (0,) * a.ndim)
    return pl.pallas_call(
        _mix_kernel,
        grid=(n // tm,),
        in_specs=[
            row(D_MODEL, 0), row(ATTN_WIDTH, 0), row(CONV_WIDTH, 0),
            row(half, COL_GATE_A), row(half, COL_GATE_A + 1),
            row(half, COL_GATE_B), row(half, COL_GATE_B + 1),
            full(wa), full(wc), full(wo), full(g_ffn), full(wr_t), full(br),
        ],
        out_specs=[
            pl.BlockSpec((tm, D_MODEL), lambda i: (i, 0)),
            pl.BlockSpec((tm * ROW_TILE, LANES), lambda i: (i, 0)),
            pl.BlockSpec((TOP_K, tm), lambda i: (0, i)),
            pl.BlockSpec((TOP_K, tm), lambda i: (0, i)),
        ],
        out_shape=[
            jax.ShapeDtypeStruct((n, D_MODEL), F32),
            jax.ShapeDtypeStruct((n * ROW_TILE, LANES), F32),
            jax.ShapeDtypeStruct((TOP_K, n), I32),
            jax.ShapeDtypeStruct((TOP_K, n), F32),
        ],
        compiler_params=_params(("parallel",)),
        name="mix",
    )(xt, o, c, proj, proj, proj, proj, wa, wc, wo, g_ffn, wr_t, br)


def _rank_kernel(idx_ref, rank_ref, cnt_ref, carry_ref):
    t = T_RANK

    @pl.when(pl.program_id(0) == 0)
    def _():
        carry_ref[...] = jnp.zeros_like(carry_ref)

    idx = idx_ref[...]
    eidx = lax.broadcasted_iota(I32, (N_EXPERTS, t), 0)
    onehot = [eidx == idx[k:k + 1, :] for k in range(TOP_K)]
    member = onehot[0] | onehot[1] | onehot[2] | onehot[3]
    r = lax.broadcasted_iota(I32, (t, t), 0)
    c = lax.broadcasted_iota(I32, (t, t), 1)
    tri = jnp.where(r < c, 1.0, 0.0).astype(BF16)
    mem_f = jnp.where(member, 1.0, 0.0)
    prefix = jnp.dot(mem_f.astype(BF16), tri, preferred_element_type=F32) + carry_ref[...]
    ranks = [jnp.sum(jnp.where(onehot[k], prefix, 0.0), axis=0, keepdims=True)
             for k in range(TOP_K)]
    rank_ref[...] = jnp.concatenate(ranks, axis=0).astype(I32)
    carry_ref[...] = carry_ref[...] + jnp.sum(mem_f, axis=1, keepdims=True)
    cnt_ref[...] = carry_ref[...].astype(I32)


def _rank(idx_t):
    n = idx_t.shape[1]
    return pl.pallas_call(
        _rank_kernel,
        grid=(n // T_RANK,),
        in_specs=[pl.BlockSpec((TOP_K, T_RANK), lambda i: (0, i))],
        out_specs=[
            pl.BlockSpec((TOP_K, T_RANK), lambda i: (0, i)),
            pl.BlockSpec((N_EXPERTS, 1), lambda i: (0, 0)),
        ],
        out_shape=[
            jax.ShapeDtypeStruct((TOP_K, n), I32),
            jax.ShapeDtypeStruct((N_EXPERTS, 1), I32),
        ],
        scratch_shapes=[pltpu.VMEM((N_EXPERTS, 1), F32)],
        compiler_params=_params(("arbitrary",)),
        name="rank",
    )(idx_t)


def _pos_kernel(idx_ref, rank_ref, pstart_ref, pos_ref):
    idx = idx_ref[...]
    eidx = lax.broadcasted_iota(I32, (N_EXPERTS, idx.shape[1]), 0)
    rows = [jnp.sum(jnp.where(eidx == idx[k:k + 1, :], pstart_ref[...], 0),
                    axis=0, keepdims=True) for k in range(TOP_K)]
    pos_ref[...] = jnp.concatenate(rows, axis=0) + rank_ref[...]


def _pos(idx_t, rank_t, pstart):
    n = idx_t.shape[1]
    blk = pl.BlockSpec((TOP_K, T_RANK), lambda i: (0, i))
    return pl.pallas_call(
        _pos_kernel,
        grid=(n // T_RANK,),
        in_specs=[blk, blk, pl.BlockSpec((N_EXPERTS, 1), lambda i: (0, 0))],
        out_specs=blk,
        out_shape=jax.ShapeDtypeStruct((TOP_K, n), I32),
        compiler_params=_params(("parallel",)),
        name="pos",
    )(idx_t, rank_t, pstart)


def _row(ref, r):
    return ref.at[pl.ds(pl.multiple_of(r * ROW_TILE, ROW_TILE), ROW_TILE), :]


def _dispatch_kernel(pstart_ref, pend_ref, pos_ref, u_ref, xs_ref, zero_ref, sem, zsem):
    t = T_DISPATCH
    blk = EXPERT_BLOCK * ROW_TILE

    def tail_copy(e):
        start = (pend_ref[e] - EXPERT_BLOCK) * ROW_TILE
        return pltpu.make_async_copy(
            zero_ref, xs_ref.at[pl.ds(pl.multiple_of(start, blk), blk), :], zsem)

    @pl.when(pl.program_id(0) == 0)
    def _():
        zero_ref[...] = jnp.zeros_like(zero_ref)

        def z(e, carry):
            @pl.when(pend_ref[e] > pstart_ref[e])
            def _():
                cp = tail_copy(e)
                cp.start()
                cp.wait()
            return carry

        lax.fori_loop(0, N_EXPERTS, z, 0)

    def issue(tok, carry):
        for k in range(TOP_K):
            pltpu.make_async_copy(_row(u_ref, tok), _row(xs_ref, pos_ref[k, tok]), sem).start()
        return carry

    lax.fori_loop(0, t, issue, 0, unroll=DMA_UNROLL)

    for k in range(TOP_K):
        pltpu.make_async_copy(u_ref, xs_ref.at[pl.ds(0, t * ROW_TILE), :], sem).wait()


def _dispatch(pstart, pend, pos_t, u, n_rows):
    n = u.shape[0] // ROW_TILE
    t = T_DISPATCH
    grid_spec = pltpu.PrefetchScalarGridSpec(
        num_scalar_prefetch=2,
        grid=(n // t,),
        in_specs=[
            pl.BlockSpec((TOP_K, t), lambda i, ps, pe: (0, i), memory_space=pltpu.SMEM),
            pl.BlockSpec((t * ROW_TILE, LANES), lambda i, ps, pe: (i, 0)),
        ],
        out_specs=pl.BlockSpec(memory_space=pl.ANY),
        scratch_shapes=[
            pltpu.VMEM((EXPERT_BLOCK * ROW_TILE, LANES), F32),
            pltpu.SemaphoreType.DMA(()),
            pltpu.SemaphoreType.DMA(()),
        ],
    )
    return pl.pallas_call(
        _dispatch_kernel,
        grid_spec=grid_spec,
        out_shape=jax.ShapeDtypeStruct((n_rows * ROW_TILE, LANES), F32),
        compiler_params=_params(("arbitrary",)),
        name="dispatch",
    )(pstart, pend, pos_t, u)


def _expert_kernel(be_ref, nb_ref, x_ref, wgu_ref, bgu_ref, wd_ref, bd_ref, y_ref,
                   wgu_s, wd32_s, wd_s):
    i = pl.program_id(0)
    half = EXPERT_FF // 2
    first = jnp.logical_or(i == 0, be_ref[i] != be_ref[jnp.maximum(i - 1, 0)])

    @pl.when(jnp.logical_and(first, i < nb_ref[0]))
    def _():
        rows = 256
        for r in range(0, D_MODEL, rows):
            wgu_s[r:r + rows, :] = wgu_ref[r:r + rows, :].astype(BF16)
        for cs in range(D_MODEL // LANES):
            lanes = slice(cs * LANES, (cs + 1) * LANES)
            wd32_s[cs, pl.ds(0, half, stride=2), :] = wd_ref[0:half, lanes]
            wd32_s[cs, pl.ds(1, half, stride=2), :] = wd_ref[half:EXPERT_FF, lanes]
            wd_s[:, lanes] = wd32_s[cs].astype(BF16)

    @pl.when(i < nb_ref[0])
    def _():
        xb = _load_row_tiles(x_ref, EXPERT_BLOCK).astype(BF16)
        ch = 256
        even = (lax.broadcasted_iota(I32, (EXPERT_BLOCK, LANES), 1) % 2) == 0
        y = jnp.zeros((EXPERT_BLOCK, D_MODEL), F32) + bd_ref[...]
        for c in range(EXPERT_FF // ch):
            c1 = slice(c * ch, (c + 1) * ch)
            c2 = slice(EXPERT_FF + c * ch, EXPERT_FF + (c + 1) * ch)
            gu1 = jnp.dot(xb, wgu_s[:, c1], preferred_element_type=F32) + bgu_ref[:, c1]
            gu2 = jnp.dot(xb, wgu_s[:, c2], preferred_element_type=F32) + bgu_ref[:, c2]
            acts = []
            for v in range(ch // LANES):
                a = gu1[:, v * LANES:(v + 1) * LANES]
                b = gu2[:, v * LANES:(v + 1) * LANES]
                g = jnp.where(even, a, pltpu.roll(b, 1, axis=1))
                l = jnp.where(even, pltpu.roll(a, LANES - 1, axis=1), b)
                g = jnp.minimum(g, SWIGLU_LIMIT)
                l = jnp.clip(l, -SWIGLU_LIMIT, SWIGLU_LIMIT)
                acts.append(((l + 1.0) * (g * jax.nn.sigmoid(g * SWIGLU_ALPHA))).astype(BF16))
            act = jnp.concatenate(acts, axis=1)
            y = y + jnp.dot(act, wd_s[c1, :], preferred_element_type=F32)
        _store_row_tiles(y_ref, y)

    @pl.when(i >= nb_ref[0])
    def _():
        y_ref[...] = jnp.zeros_like(y_ref)


def _experts(block_expert, n_used, xs, wgu, bgu, wd, bd):
    n_rows = xs.shape[0] // ROW_TILE
    blk = EXPERT_BLOCK
    wspec = lambda r, c: pl.BlockSpec((None, r, c), lambda i, be, nb: (be[i], 0, 0))
    grid_spec = pltpu.PrefetchScalarGridSpec(
        num_scalar_prefetch=2,
        grid=(n_rows // blk,),
        in_specs=[
            pl.BlockSpec((blk * ROW_TILE, LANES), lambda i, be, nb: (jnp.minimum(i, nb[0] - 1), 0)),
            wspec(D_MODEL, 2 * EXPERT_FF), wspec(1, 2 * EXPERT_FF),
            wspec(EXPERT_FF, D_MODEL), wspec(1, D_MODEL),
        ],
        out_specs=pl.BlockSpec((blk * ROW_TILE, LANES), lambda i, be, nb: (i, 0)),
        scratch_shapes=[
            pltpu.VMEM((D_MODEL, 2 * EXPERT_FF), BF16),
            pltpu.VMEM((D_MODEL // LANES, EXPERT_FF, LANES), F32),
            pltpu.VMEM((EXPERT_FF, D_MODEL), BF16),
        ],
    )
    return pl.pallas_call(
        _expert_kernel,
        grid_spec=grid_spec,
        out_shape=jax.ShapeDtypeStruct((n_rows * ROW_TILE, LANES), F32),
        compiler_params=_params(("arbitrary",)),
        name="expert",
    )(block_expert, n_used, xs, wgu, bgu, wd, bd)


def _combine_kernel(pos_ref, posn_ref, h_ref, gate_ref, p_ref, y_ref, wple_ref, gple_ref, wpg_ref,
                    gfin_ref, o_ref, ybuf_ref, sem):
    t = T_COMBINE
    i = pl.program_id(0)
    slot = i % 2

    def gather(idx_ref, to_slot):
        def issue(tok, carry):
            for k in range(TOP_K):
                pltpu.make_async_copy(_row(y_ref, idx_ref[k, tok]),
                                      _row(ybuf_ref.at[to_slot, k], tok), sem.at[to_slot]).start()
            return carry

        lax.fori_loop(0, t, issue, 0, unroll=DMA_UNROLL)

    @pl.when(i == 0)
    def _():
        gather(pos_ref, 0)

    @pl.when(i + 1 < pl.num_programs(0))
    def _():
        gather(posn_ref, 1 - slot)

    emb = jnp.dot(p_ref[...].astype(BF16), wple_ref[...], preferred_element_type=F32)

    for k in range(TOP_K):
        pltpu.make_async_copy(y_ref.at[pl.ds(0, t * ROW_TILE), :], ybuf_ref.at[slot, k],
                              sem.at[slot]).wait()

    gates = gate_ref[...]
    h = h_ref[...]
    for k in range(TOP_K):
        h = h + gates[:, k:k + 1] * _load_row_tiles(ybuf_ref, t, lead=(slot, k))
    r = _rms(h, gple_ref[...]).astype(BF16)
    sig = jax.nn.sigmoid(jnp.dot(r, wpg_ref[...], preferred_element_type=F32))
    h = h + emb * sig
    o_ref[...] = _rms(h, gfin_ref[...])


def _combine(pos_t, h, gates, p2, y, wple, gple, wpg, gfin):
    n = h.shape[0]
    t = T_COMBINE
    last = n // t - 1
    full = lambda a: pl.BlockSpec(a.shape, lambda i: (0,) * a.ndim)
    return pl.pallas_call(
        _combine_kernel,
        grid=(n // t,),
        in_specs=[
            pl.BlockSpec((TOP_K, t), lambda i: (0, i), memory_space=pltpu.SMEM),
            pl.BlockSpec((TOP_K, t), lambda i: (0, jnp.minimum(i + 1, last)),
                         memory_space=pltpu.SMEM),
            pl.BlockSpec((t, D_MODEL), lambda i: (i, 0)),
            pl.BlockSpec((t, TOP_K), lambda i: (i, 0)),
            pl.BlockSpec((t, PLE_DIM), lambda i: (i, 0)),
            pl.BlockSpec(memory_space=pl.ANY),
            full(wple), full(gple), full(wpg), full(gfin),
        ],
        out_specs=pl.BlockSpec((t, D_MODEL), lambda i: (i, 0)),
        out_shape=jax.ShapeDtypeStruct((n, D_MODEL), F32),
        scratch_shapes=[
            pltpu.VMEM((2, TOP_K, t * ROW_TILE, LANES), F32),
            pltpu.SemaphoreType.DMA((2,)),
        ],
        compiler_params=_params(("arbitrary",)),
        name="combine",
    )(pos_t, pos_t, h, gates, p2, y, wple, gple, wpg, gfin)


def kernel(x, p, g_mix, w_in, lambda_q1, lambda_k1, lambda_q2, lambda_k2, g_subln, w_attn_out,
           w_dw, b_dw, g_conv_ln, b_conv_ln, w_conv_out, w_o, g_ffn, w_router, b_router,
           w_gate_up, b_gate_up, w_down, b_down, w_ple, g_ple, w_ple_gate, g_final):
    b, s, d = x.shape
    n = b * s
    xt = x.reshape(n, d)
    vec = lambda a: a.reshape(1, -1)

    w_in0 = w_in[0]
    w_main = jnp.concatenate([w_in0[:, :2 * ATTN_WIDTH], w_in0[:, 3 * ATTN_WIDTH:]], axis=1)
    w_vt = w_in0[:, 2 * ATTN_WIDTH:3 * ATTN_WIDTH].T
    proj, vt = _inproj(xt, vec(g_mix[0]), w_main.astype(BF16), w_vt.astype(BF16))
    proj3 = proj.reshape(b, s, MAIN_COLS)
    attn = _attention(proj3, vt, vec(lambda_q1[0]), vec(lambda_k1[0]), vec(lambda_q2[0]),
                      vec(lambda_k2[0]), g_subln[0].reshape(HEAD_V, 1))
    conv = _conv(proj3, w_dw[0], vec(b_dw[0]), vec(g_conv_ln[0]), vec(b_conv_ln[0]))
    h1, u2, idx_t, gates_t = _mix(
        xt, attn.reshape(n, ATTN_WIDTH), conv.reshape(n, CONV_WIDTH), proj,
        w_attn_out[0].astype(BF16), w_conv_out[0].astype(BF16), w_o[0].astype(BF16),
        vec(g_ffn[0]), w_router[0].T, b_router[0].reshape(N_EXPERTS, 1))

    rank_t, counts = _rank(idx_t)
    blk = EXPERT_BLOCK
    n_blocks = n * TOP_K // blk + N_EXPERTS
    padded = (counts[:, 0] + blk - 1) // blk * blk
    pend = jnp.cumsum(padded).astype(I32)
    pstart = pend - padded
    n_used = (pend[-1:] // blk).astype(I32)
    block_start = jnp.arange(n_blocks, dtype=I32) * blk
    block_expert = jnp.minimum(
        jnp.sum((pend[None, :] <= block_start[:, None]).astype(I32), axis=1), N_EXPERTS - 1)
    pos_t = _pos(idx_t, rank_t, pstart.reshape(N_EXPERTS, 1))

    xs = _dispatch(pstart, pend, pos_t, u2, n_blocks * blk)
    y = _experts(block_expert, n_used, xs, w_gate_up[0], b_gate_up[0][:, None, :],
                 w_down[0], b_down[0][:, None, :])

    out = _combine(pos_t, h1, gates_t.T, p[0].reshape(n, PLE_DIM), y,
                   w_ple[0].astype(BF16), vec(g_ple[0]), w_ple_gate[0].astype(BF16), vec(g_final))
    return out.reshape(b, s, d)
```

```python
import jax
import jax.numpy as jnp
from jax import lax
from jax.experimental import pallas as pl
from jax.experimental.pallas import tpu as pltpu

F32 = jnp.float32
BF16 = jnp.bfloat16
I32 = jnp.int32

D_MODEL = 1024
ATTN_WIDTH = 512
DIFF_HEADS = 4
HEAD_DIM = 64
HEAD_V = 2 * HEAD_DIM
CONV_WIDTH = 512
CONV_KERNEL = 31
N_EXPERTS = 32
TOP_K = 4
EXPERT_FF = 1024
PLE_DIM = 256
SWIGLU_ALPHA = 1.702
SWIGLU_LIMIT = 7.0
EPS = 1e-5
IN_COLS = 3 * ATTN_WIDTH + 2 * CONV_WIDTH + 2 * D_MODEL
LAMBDA_INIT = 0.2
LOG2_E = 1.4426950408889634

MAIN_COLS = IN_COLS - ATTN_WIDTH
COL_CONV_A = 2
COL_CONV_B = 3
COL_GATE_A = 4
COL_GATE_B = 6

LANES = 128
TM_PROJ = 512
TQ = 256
TM_MIX = 512
T_RANK = 512
T_INV = 512
EXPERT_BLOCK = 512
T_COMBINE = 256
CONV_ROWS = 64
CONV_PAD = 32
DMA_UNROLL = 8
VMEM_LIMIT = 56 * 1024 * 1024

_NT = (((1,), (1,)), ((), ()))


def _params(sem, vmem=VMEM_LIMIT):
    return pltpu.CompilerParams(dimension_semantics=sem, vmem_limit_bytes=vmem)


def _rms(x, g):
    return x * lax.rsqrt(jnp.mean(x * x, axis=-1, keepdims=True) + EPS) * g


ROW_TILE = D_MODEL // LANES


def _store_row_tiles(ref, x, lead=()):
    rows = x.shape[0]
    for g in range(ROW_TILE):
        ref[lead + (pl.ds(g, rows, stride=ROW_TILE), slice(None))] = x[:, g * LANES:(g + 1) * LANES]


def _load_row_tiles(ref, rows, lead=()):
    return jnp.concatenate(
        [ref[lead + (pl.ds(g, rows, stride=ROW_TILE), slice(None))] for g in range(ROW_TILE)], axis=1)


def _inproj_kernel(x_ref, g_ref, w_ref, wvt_ref, o_ref, vt_ref):
    u = _rms(x_ref[...], g_ref[...]).astype(BF16)
    ch = 512
    for c in range(MAIN_COLS // ch):
        r = jnp.dot(u, w_ref[:, c * ch:(c + 1) * ch], preferred_element_type=F32)
        if c == 0:
            r = r * (HEAD_DIM ** -0.5 * LOG2_E)
        o_ref[:, c * ch:(c + 1) * ch] = r.astype(BF16)
    vt_ref[...] = lax.dot_general(wvt_ref[...], u, _NT,
                                  preferred_element_type=F32).astype(BF16)


def _inproj(xt, g, w, wvt):
    n = xt.shape[0]
    return pl.pallas_call(
        _inproj_kernel,
        grid=(n // TM_PROJ,),
        in_specs=[
            pl.BlockSpec((TM_PROJ, D_MODEL), lambda i: (i, 0)),
            pl.BlockSpec((1, D_MODEL), lambda i: (0, 0)),
            pl.BlockSpec((D_MODEL, MAIN_COLS), lambda i: (0, 0)),
            pl.BlockSpec((ATTN_WIDTH, D_MODEL), lambda i: (0, 0)),
        ],
        out_specs=[
            pl.BlockSpec((TM_PROJ, MAIN_COLS), lambda i: (i, 0)),
            pl.BlockSpec((ATTN_WIDTH, TM_PROJ), lambda i: (0, i)),
        ],
        out_shape=[
            jax.ShapeDtypeStruct((n, MAIN_COLS), BF16),
            jax.ShapeDtypeStruct((ATTN_WIDTH, n), BF16),
        ],
        compiler_params=_params(("parallel",)),
        name="inproj",
    )(xt, g, w, wvt)


def _attn_kernel(q_ref, k_ref, vt_ref, lq1_ref, lk1_ref, lq2_ref, lk2_ref, g_ref, o_ref):
    i = pl.program_id(1)
    s_len = k_ref.shape[0]
    lane = lax.broadcasted_iota(I32, (TQ, HEAD_V), 1)
    key = lax.broadcasted_iota(I32, (TQ, 2 * TQ), 0)
    qry = lax.broadcasted_iota(I32, (TQ, 2 * TQ), 1)
    causal = key <= jnp.where(qry >= TQ, qry - TQ, qry)
    lam = (jnp.exp(jnp.sum(lq1_ref[...] * lk1_ref[...]))
           - jnp.exp(jnp.sum(lq2_ref[...] * lk2_ref[...])) + LAMBDA_INIT)

    def head(c, h):
        cols = slice(h * HEAD_V, (h + 1) * HEAD_V)
        n = c * TQ
        q = q_ref[:, cols]
        zero = jnp.zeros_like(q)
        qq = jnp.concatenate([jnp.where(lane < HEAD_DIM, q, zero),
                              jnp.where(lane >= HEAD_DIM, q, zero)], axis=0)
        s_d = lax.dot_general(k_ref[n:n + TQ, cols], qq, _NT, preferred_element_type=F32)
        s_d = jnp.where(causal, s_d, -1e30)
        m = jnp.max(s_d, axis=0, keepdims=True)
        if c:
            s_f = lax.dot_general(k_ref[0:n, cols], qq, _NT, preferred_element_type=F32)
            m = jnp.maximum(m, jnp.max(s_f, axis=0, keepdims=True))
        p_d = jnp.exp2(s_d - m)
        l = jnp.sum(p_d, axis=0, keepdims=True)
        acc = jnp.dot(vt_ref[cols, n:n + TQ], p_d.astype(BF16), preferred_element_type=F32)
        if c:
            p_f = jnp.exp2(s_f - m)
            l = l + jnp.sum(p_f, axis=0, keepdims=True)
            acc = acc + jnp.dot(vt_ref[cols, 0:n], p_f.astype(BF16), preferred_element_type=F32)
        o12 = acc / l
        o = o12[:, :TQ] - lam * o12[:, TQ:]
        o = o * lax.rsqrt(jnp.mean(o * o, axis=0, keepdims=True) + EPS) * g_ref[...]
        o_ref[:, cols] = (o * (1.0 - LAMBDA_INIT)).T.astype(o_ref.dtype)

    def block(c):
        for h in range(DIFF_HEADS):
            head(c, h)

    for c in range(s_len // TQ):
        pl.when(i == c)(lambda c=c: block(c))


def _attention(proj3, vt, lq1, lk1, lq2, lk2, g_subln):
    b, s, _ = proj3.shape
    vec = pl.BlockSpec((1, HEAD_DIM), lambda bi, i: (0, 0))
    return pl.pallas_call(
        _attn_kernel,
        grid=(b, s // TQ),
        in_specs=[
            pl.BlockSpec((None, TQ, ATTN_WIDTH), lambda bi, i: (bi, i, 0)),
            pl.BlockSpec((None, s, ATTN_WIDTH), lambda bi, i: (bi, 0, 1)),
            pl.BlockSpec((ATTN_WIDTH, s), lambda bi, i: (0, bi)),
            vec, vec, vec, vec,
            pl.BlockSpec((HEAD_V, 1), lambda bi, i: (0, 0)),
        ],
        out_specs=pl.BlockSpec((None, TQ, ATTN_WIDTH), lambda bi, i: (bi, i, 0)),
        out_shape=jax.ShapeDtypeStruct((b, s, ATTN_WIDTH), BF16),
        compiler_params=_params(("parallel", "parallel")),
        name="attn",
    )(proj3, proj3, vt, lq1, lk1, lq2, lk2, g_subln)


def _conv_kernel(ca_ref, cb_ref, w_ref, b_ref, g_ref, beta_ref, o_ref, z_ref):
    s = ca_ref.shape[0]
    glu_rows = 256
    z_ref[0:CONV_PAD, :] = jnp.zeros((CONV_PAD, CONV_WIDTH), F32)

    def glu(c, carry):
        r0 = pl.multiple_of(c * glu_rows, glu_rows)
        a = ca_ref[pl.ds(r0, glu_rows), :].astype(F32)
        g = cb_ref[pl.ds(r0, glu_rows), :].astype(F32)
        z_ref[pl.ds(CONV_PAD + r0, glu_rows), :] = a * jax.nn.sigmoid(g)
        return carry

    lax.fori_loop(0, s // glu_rows, glu, 0)

    def conv(c, carry):
        r0 = pl.multiple_of(c * CONV_ROWS, CONV_ROWS)
        acc = jnp.zeros((CONV_ROWS, CONV_WIDTH), F32) + b_ref[...]
        for phase in range(8):
            rows = CONV_ROWS + (8 if phase else 0)
            part = None
            for j in range(CONV_KERNEL):
                off = CONV_PAD - (CONV_KERNEL - 1) + j
                if off % 8 != phase:
                    continue
                term = w_ref[j:j + 1, :] * z_ref[pl.ds(r0 + (off - phase), rows), :]
                part = term if part is None else part + term
            acc = acc + part[phase:phase + CONV_ROWS]
        mu = jnp.mean(acc, axis=-1, keepdims=True)
        xc = acc - mu
        y = xc * lax.rsqrt(jnp.mean(xc * xc, axis=-1, keepdims=True) + EPS)
        y = y * g_ref[...] + beta_ref[...]
        o_ref[pl.ds(r0, CONV_ROWS), :] = (y * jax.nn.sigmoid(y)).astype(o_ref.dtype)
        return carry

    lax.fori_loop(0, s // CONV_ROWS, conv, 0)


def _conv(proj3, w_dw, b_dw, g_ln, b_ln):
    b, s, _ = proj3.shape
    vec = pl.BlockSpec((1, CONV_WIDTH), lambda bi: (0, 0))
    return pl.pallas_call(
        _conv_kernel,
        grid=(b,),
        in_specs=[
            pl.BlockSpec((None, s, CONV_WIDTH), lambda bi: (bi, 0, COL_CONV_A)),
            pl.BlockSpec((None, s, CONV_WIDTH), lambda bi: (bi, 0, COL_CONV_B)),
            pl.BlockSpec((CONV_KERNEL, CONV_WIDTH), lambda bi: (0, 0)),
            vec, vec, vec,
        ],
        out_specs=pl.BlockSpec((None, s, CONV_WIDTH), lambda bi: (bi, 0, 0)),
        out_shape=jax.ShapeDtypeStruct((b, s, CONV_WIDTH), BF16),
        scratch_shapes=[pltpu.VMEM((s + CONV_PAD, CONV_WIDTH), F32)],
        compiler_params=_params(("parallel",)),
        name="conv",
    )(proj3, proj3, w_dw, b_dw, g_ln, b_ln)


def _split_bf16(a):
    hi = a.astype(BF16)
    lo = (a - hi.astype(F32)).astype(BF16)
    return hi, lo


def _mix_kernel(x_ref, o_ref, c_ref, ga0_ref, ga1_ref, gb0_ref, gb1_ref,
                wa_ref, wc_ref, wo_ref, gffn_ref, wr_ref, br_ref,
                h_ref, u_ref, idx_ref, gate_ref):
    o = o_ref[...]
    c = c_ref[...]
    half = D_MODEL // 2
    acc = x_ref[...]
    for hh, (ga_ref, gb_ref) in enumerate(((ga0_ref, gb0_ref), (ga1_ref, gb1_ref))):
        cols = slice(hh * half, (hh + 1) * half)
        a = jnp.dot(o, wa_ref[:, cols], preferred_element_type=F32)
        b = jnp.dot(c, wc_ref[:, cols], preferred_element_type=F32)
        m = (jax.nn.sigmoid(ga_ref[...].astype(F32)) * a
             + jax.nn.sigmoid(gb_ref[...].astype(F32)) * b)
        acc = acc + jnp.dot(m.astype(BF16), wo_ref[cols, :], preferred_element_type=F32)
    h_ref[...] = acc
    u = _rms(acc, gffn_ref[...])
    _store_row_tiles(u_ref, u)

    u_hi, u_lo = _split_bf16(u)
    w_hi, w_lo = _split_bf16(wr_ref[...])
    logits = (lax.dot_general(w_hi, u_hi, _NT, preferred_element_type=F32)
              + lax.dot_general(w_hi, u_lo, _NT, preferred_element_type=F32)
              + lax.dot_general(w_lo, u_hi, _NT, preferred_element_type=F32)
              + br_ref[...])
    eidx = lax.broadcasted_iota(I32, logits.shape, 0)
    vals, idxs = [], []
    for _ in range(TOP_K):
        mx = jnp.max(logits, axis=0, keepdims=True)
        sel = jnp.min(jnp.where(logits == mx, eidx, N_EXPERTS), axis=0, keepdims=True)
        vals.append(mx)
        idxs.append(sel)
        logits = jnp.where(eidx == sel, -jnp.inf, logits)
    ex = [jnp.exp(v - vals[0]) for v in vals]
    den = ex[0] + ex[1] + ex[2] + ex[3]
    idx_ref[...] = jnp.concatenate(idxs, axis=0)
    gate_ref[...] = jnp.concatenate([e / den for e in ex], axis=0)


def _mix(xt, o, c, proj, wa, wc, wo, g_ffn, wr_t, br):
    n = xt.shape[0]
    tm = TM_MIX
    half = D_MODEL // 2
    row = lambda w, j: pl.BlockSpec((tm, w), lambda i, j=j: (i, j))
    full = lambda a: pl.BlockSpec(a.shape, lambda i: (0,) * a.ndim)
    return pl.pallas_call(
        _mix_kernel,
        grid=(n // tm,),
        in_specs=[
            row(D_MODEL, 0), row(ATTN_WIDTH, 0), row(CONV_WIDTH, 0),
            row(half, COL_GATE_A), row(half, COL_GATE_A + 1),
            row(half, COL_GATE_B), row(half, COL_GATE_B + 1),
            full(wa), full(wc), full(wo), full(g_ffn), full(wr_t), full(br),
        ],
        out_specs=[
            pl.BlockSpec((tm, D_MODEL), lambda i: (i, 0)),
            pl.BlockSpec((tm * ROW_TILE, LANES), lambda i: (i, 0)),
            pl.BlockSpec((TOP_K, tm), lambda i: (0, i)),
            pl.BlockSpec((TOP_K, tm), lambda i: (0, i)),
        ],
        out_shape=[
            jax.ShapeDtypeStruct((n, D_MODEL), F32),
            jax.ShapeDtypeStruct((n * ROW_TILE, LANES), F32),
            jax.ShapeDtypeStruct((TOP_K, n), I32),
            jax.ShapeDtypeStruct((TOP_K, n), F32),
        ],
        compiler_params=_params(("parallel",)),
        name="mix",
    )(xt, o, c, proj, proj, proj, proj, wa, wc, wo, g_ffn, wr_t, br)


def _rank_kernel(idx_ref, rank_ref, cnt_ref, carry_ref):
    t = T_RANK

    @pl.when(pl.program_id(0) == 0)
    def _():
        carry_ref[...] = jnp.zeros_like(carry_ref)

    idx = idx_ref[...]
    eidx = lax.broadcasted_iota(I32, (N_EXPERTS, t), 0)
    onehot = [eidx == idx[k:k + 1, :] for k in range(TOP_K)]
    member = onehot[0] | onehot[1] | onehot[2] | onehot[3]
    r = lax.broadcasted_iota(I32, (t, t), 0)
    c = lax.broadcasted_iota(I32, (t, t), 1)
    tri = jnp.where(r < c, 1.0, 0.0).astype(BF16)
    mem_f = jnp.where(member, 1.0, 0.0)
    prefix = jnp.dot(mem_f.astype(BF16), tri, preferred_element_type=F32) + carry_ref[...]
    ranks = [jnp.sum(jnp.where(onehot[k], prefix, 0.0), axis=0, keepdims=True)
             for k in range(TOP_K)]
    rank_ref[...] = jnp.concatenate(ranks, axis=0).astype(I32)
    carry_ref[...] = carry_ref[...] + jnp.sum(mem_f, axis=1, keepdims=True)
    cnt_ref[...] = carry_ref[...].astype(I32)


def _rank(idx_t):
    n = idx_t.shape[1]
    return pl.pallas_call(
        _rank_kernel,
        grid=(n // T_RANK,),
        in_specs=[pl.BlockSpec((TOP_K, T_RANK), lambda i: (0, i))],
        out_specs=[
            pl.BlockSpec((TOP_K, T_RANK), lambda i: (0, i)),
            pl.BlockSpec((N_EXPERTS, 1), lambda i: (0, 0)),
        ],
        out_shape=[
            jax.ShapeDtypeStruct((TOP_K, n), I32),
            jax.ShapeDtypeStruct((N_EXPERTS, 1), I32),
        ],
        scratch_shapes=[pltpu.VMEM((N_EXPERTS, 1), F32)],
        compiler_params=_params(("arbitrary",)),
        name="rank",
    )(idx_t)


def _pos_kernel(idx_ref, rank_ref, pstart_ref, pos_ref):
    idx = idx_ref[...]
    eidx = lax.broadcasted_iota(I32, (N_EXPERTS, idx.shape[1]), 0)
    rows = [jnp.sum(jnp.where(eidx == idx[k:k + 1, :], pstart_ref[...], 0),
                    axis=0, keepdims=True) for k in range(TOP_K)]
    pos_ref[...] = jnp.concatenate(rows, axis=0) + rank_ref[...]


def _pos(idx_t, rank_t, pstart):
    n = idx_t.shape[1]
    blk = pl.BlockSpec((TOP_K, T_RANK), lambda i: (0, i))
    return pl.pallas_call(
        _pos_kernel,
        grid=(n // T_RANK,),
        in_specs=[blk, blk, pl.BlockSpec((N_EXPERTS, 1), lambda i: (0, 0))],
        out_specs=blk,
        out_shape=jax.ShapeDtypeStruct((TOP_K, n), I32),
        compiler_params=_params(("parallel",)),
        name="pos",
    )(idx_t, rank_t, pstart)


def _row(ref, r):
    return ref.at[pl.ds(pl.multiple_of(r * ROW_TILE, ROW_TILE), ROW_TILE), :]


def _inv_kernel(cnt_ref, pstart_ref, pend_ref, pos_ref, dst_ref):
    t = T_INV
    i = pl.program_id(0)
    n_real = pl.num_programs(0) * t * TOP_K

    @pl.when(i == 0)
    def _():
        def per_expert(e, carry):
            first = pstart_ref[e] + cnt_ref[e]

            def pad(j, c):
                dst_ref[first + j] = n_real + e * EXPERT_BLOCK + j
                return c

            lax.fori_loop(0, pend_ref[e] - first, pad, 0)
            return carry

        lax.fori_loop(0, N_EXPERTS, per_expert, 0)

        def tail(r, c):
            dst_ref[r] = n_real
            return c

        lax.fori_loop(pend_ref[N_EXPERTS - 1], dst_ref.shape[0], tail, 0)

    def body(tok, carry):
        for k in range(TOP_K):
            dst_ref[pos_ref[k, tok]] = (i * t + tok) * TOP_K + k
        return carry

    lax.fori_loop(0, t, body, 0, unroll=DMA_UNROLL)


def _inv(counts, pstart, pend, pos_t, n_rows):
    n = pos_t.shape[1]
    grid_spec = pltpu.PrefetchScalarGridSpec(
        num_scalar_prefetch=3,
        grid=(n // T_INV,),
        in_specs=[pl.BlockSpec((TOP_K, T_INV), lambda i, c, ps, pe: (0, i),
                               memory_space=pltpu.SMEM)],
        out_specs=pl.BlockSpec(memory_space=pltpu.SMEM),
    )
    return pl.pallas_call(
        _inv_kernel,
        grid_spec=grid_spec,
        out_shape=jax.ShapeDtypeStruct((n_rows,), I32),
        compiler_params=_params(("arbitrary",)),
        name="inv",
    )(counts, pstart, pend, pos_t)


def _expert_kernel(be_ref, nb_ref, dst_ref, u_ref, wgu_ref, bgu_ref, wd_ref, bd_ref, y_ref,
                   wgu_s, wd32_s, wd_s, xb_s, xbuf, ybuf, gsem, ssem):
    i = pl.program_id(0)
    nb = nb_ref[0]
    blk = EXPERT_BLOCK
    n_tok = u_ref.shape[0] // ROW_TILE
    half = EXPERT_FF // 2
    cur = i % 2
    first = jnp.logical_or(i == 0, be_ref[i] != be_ref[jnp.maximum(i - 1, 0)])

    def gather_row(block, slot, r):
        tok = jnp.minimum(dst_ref[block * blk + r] >> 2, n_tok - 1)
        pltpu.make_async_copy(_row(u_ref, tok), _row(xbuf.at[slot], r), gsem.at[slot]).start()

    def scatter_row(block, slot, r, real):
        dst = jnp.where(real, dst_ref[block * blk + r], n_tok * TOP_K + r)
        pltpu.make_async_copy(_row(ybuf.at[slot], r), _row(y_ref, dst), ssem.at[slot]).start()

    def wait_rows(buf, sem, slot):
        pltpu.make_async_copy(buf.at[slot], buf.at[slot], sem.at[slot]).wait()

    def looped(fn):
        def body(r, carry):
            fn(r)
            return carry

        lax.fori_loop(0, blk, body, 0, unroll=DMA_UNROLL)

    @pl.when(i == 0)
    def _():
        ybuf[...] = jnp.zeros_like(ybuf)
        looped(lambda r: gather_row(0, 0, r))

    @pl.when(jnp.logical_and(first, i < nb))
    def _():
        rows = 256
        for r in range(0, D_MODEL, rows):
            wgu_s[r:r + rows, :] = wgu_ref[r:r + rows, :].astype(BF16)
        for cs in range(D_MODEL // LANES):
            lanes = slice(cs * LANES, (cs + 1) * LANES)
            wd32_s[cs, pl.ds(0, half, stride=2), :] = wd_ref[0:half, lanes]
            wd32_s[cs, pl.ds(1, half, stride=2), :] = wd_ref[half:EXPERT_FF, lanes]
            wd_s[:, lanes] = wd32_s[cs].astype(BF16)

    @pl.when(i < nb)
    def _():
        wait_rows(xbuf, gsem, cur)
        xb_s[...] = _load_row_tiles(xbuf, blk, lead=(cur,)).astype(BF16)

        prev = jnp.maximum(i - 1, 0)
        nxt = jnp.minimum(i + 1, nb - 1)
        for r in range(blk):
            scatter_row(prev, 1 - cur, r, i >= 1)
            gather_row(nxt, 1 - cur, r)

        ch = 256
        even = (lax.broadcasted_iota(I32, (blk, LANES), 1) % 2) == 0
        y = jnp.zeros((blk, D_MODEL), F32) + bd_ref[...]
        for c in range(EXPERT_FF // ch):
            c1 = slice(c * ch, (c + 1) * ch)
            c2 = slice(EXPERT_FF + c * ch, EXPERT_FF + (c + 1) * ch)
            gu1 = jnp.dot(xb_s[...], wgu_s[:, c1], preferred_element_type=F32) + bgu_ref[:, c1]
            gu2 = jnp.dot(xb_s[...], wgu_s[:, c2], preferred_element_type=F32) + bgu_ref[:, c2]
            acts = []
            for v in range(ch // LANES):
                a = gu1[:, v * LANES:(v + 1) * LANES]
                b = gu2[:, v * LANES:(v + 1) * LANES]
                g = jnp.where(even, a, pltpu.roll(b, 1, axis=1))
                l = jnp.where(even, pltpu.roll(a, LANES - 1, axis=1), b)
                g = jnp.minimum(g, SWIGLU_LIMIT)
                l = jnp.clip(l, -SWIGLU_LIMIT, SWIGLU_LIMIT)
                acts.append(((l + 1.0) * (g * jax.nn.sigmoid(g * SWIGLU_ALPHA))).astype(BF16))
            act = jnp.concatenate(acts, axis=1)
            y = y + jnp.dot(act, wd_s[c1, :], preferred_element_type=F32)

        @pl.when(i >= 1)
        def _():
            wait_rows(ybuf, ssem, cur)

        _store_row_tiles(ybuf, y, lead=(cur,))

        @pl.when(i == nb - 1)
        def _():
            looped(lambda r: scatter_row(i, cur, r, True))
            wait_rows(ybuf, ssem, cur)
            wait_rows(ybuf, ssem, 1 - cur)
            wait_rows(xbuf, gsem, 1 - cur)


def _experts(block_expert, n_used, row_dst, u, wgu, bgu, wd, bd):
    n_rows = row_dst.shape[0]
    blk = EXPERT_BLOCK
    wspec = lambda r, c: pl.BlockSpec((None, r, c), lambda i, be, nb, rd: (be[i], 0, 0))
    grid_spec = pltpu.PrefetchScalarGridSpec(
        num_scalar_prefetch=3,
        grid=(n_rows // blk,),
        in_specs=[
            pl.BlockSpec(memory_space=pl.ANY),
            wspec(D_MODEL, 2 * EXPERT_FF), wspec(1, 2 * EXPERT_FF),
            wspec(EXPERT_FF, D_MODEL), wspec(1, D_MODEL),
        ],
        out_specs=pl.BlockSpec(memory_space=pl.ANY),
        scratch_shapes=[
            pltpu.VMEM((D_MODEL, 2 * EXPERT_FF), BF16),
            pltpu.VMEM((D_MODEL // LANES, EXPERT_FF, LANES), F32),
            pltpu.VMEM((EXPERT_FF, D_MODEL), BF16),
            pltpu.VMEM((blk, D_MODEL), BF16),
            pltpu.VMEM((2, blk * ROW_TILE, LANES), F32),
            pltpu.VMEM((2, blk * ROW_TILE, LANES), F32),
            pltpu.SemaphoreType.DMA((2,)),
            pltpu.SemaphoreType.DMA((2,)),
        ],
    )
    return pl.pallas_call(
        _expert_kernel,
        grid_spec=grid_spec,
        out_shape=jax.ShapeDtypeStruct((n_rows * ROW_TILE, LANES), F32),
        compiler_params=_params(("arbitrary",)),
        name="expert",
    )(block_expert, n_used, row_dst, u, wgu, bgu, wd, bd)


def _combine_kernel(h_ref, gate_ref, p_ref, y_ref, wple_ref, gple_ref, wpg_ref, gfin_ref, o_ref):
    t = T_COMBINE
    emb = jnp.dot(p_ref[...].astype(BF16), wple_ref[...], preferred_element_type=F32)
    gates = gate_ref[...]
    h = h_ref[...]
    for k in range(TOP_K):
        yk = jnp.concatenate(
            [y_ref[pl.ds(k * ROW_TILE + g, t, stride=TOP_K * ROW_TILE), :] for g in range(ROW_TILE)],
            axis=1)
        h = h + gates[:, k:k + 1] * yk
    r = _rms(h, gple_ref[...]).astype(BF16)
    sig = jax.nn.sigmoid(jnp.dot(r, wpg_ref[...], preferred_element_type=F32))
    h = h + emb * sig
    o_ref[...] = _rms(h, gfin_ref[...])


def _combine(h, gates, p2, y, wple, gple, wpg, gfin):
    n = h.shape[0]
    t = T_COMBINE
    full = lambda a: pl.BlockSpec(a.shape, lambda i: (0,) * a.ndim)
    return pl.pallas_call(
        _combine_kernel,
        grid=(n // t,),
        in_specs=[
            pl.BlockSpec((t, D_MODEL), lambda i: (i, 0)),
            pl.BlockSpec((t, TOP_K), lambda i: (i, 0)),
            pl.BlockSpec((t, PLE_DIM), lambda i: (i, 0)),
            pl.BlockSpec((t * TOP_K * ROW_TILE, LANES), lambda i: (i, 0)),
            full(wple), full(gple), full(wpg), full(gfin),
        ],
        out_specs=pl.BlockSpec((t, D_MODEL), lambda i: (i, 0)),
        out_shape=jax.ShapeDtypeStruct((n, D_MODEL), F32),
        compiler_params=_params(("parallel",)),
        name="combine",
    )(h, gates, p2, y, wple, gple, wpg, gfin)


def kernel(x, p, g_mix, w_in, lambda_q1, lambda_k1, lambda_q2, lambda_k2, g_subln, w_attn_out,
           w_dw, b_dw, g_conv_ln, b_conv_ln, w_conv_out, w_o, g_ffn, w_router, b_router,
           w_gate_up, b_gate_up, w_down, b_down, w_ple, g_ple, w_ple_gate, g_final):
    b, s, d = x.shape
    n = b * s
    xt = x.reshape(n, d)
    vec = lambda a: a.reshape(1, -1)

    w_in0 = w_in[0]
    w_main = jnp.concatenate([w_in0[:, :2 * ATTN_WIDTH], w_in0[:, 3 * ATTN_WIDTH:]], axis=1)
    w_vt = w_in0[:, 2 * ATTN_WIDTH:3 * ATTN_WIDTH].T
    proj, vt = _inproj(xt, vec(g_mix[0]), w_main.astype(BF16), w_vt.astype(BF16))
    proj3 = proj.reshape(b, s, MAIN_COLS)
    attn = _attention(proj3, vt, vec(lambda_q1[0]), vec(lambda_k1[0]), vec(lambda_q2[0]),
                      vec(lambda_k2[0]), g_subln[0].reshape(HEAD_V, 1))
    conv = _conv(proj3, w_dw[0], vec(b_dw[0]), vec(g_conv_ln[0]), vec(b_conv_ln[0]))
    h1, u2, idx_t, gates_t = _mix(
        xt, attn.reshape(n, ATTN_WIDTH), conv.reshape(n, CONV_WIDTH), proj,
        w_attn_out[0].astype(BF16), w_conv_out[0].astype(BF16), w_o[0].astype(BF16),
        vec(g_ffn[0]), w_router[0].T, b_router[0].reshape(N_EXPERTS, 1))

    rank_t, counts = _rank(idx_t)
    blk = EXPERT_BLOCK
    n_blocks = n * TOP_K // blk + N_EXPERTS
    padded = (counts[:, 0] + blk - 1) // blk * blk
    pend = jnp.cumsum(padded).astype(I32)
    pstart = pend - padded
    n_used = (pend[-1:] // blk).astype(I32)
    block_start = jnp.arange(n_blocks, dtype=I32) * blk
    block_expert = jnp.minimum(
        jnp.sum((pend[None, :] <= block_start[:, None]).astype(I32), axis=1), N_EXPERTS - 1)
    pos_t = _pos(idx_t, rank_t, pstart.reshape(N_EXPERTS, 1))

    row_dst = _inv(counts[:, 0], pstart, pend, pos_t, n_blocks * blk)
    y = _experts(block_expert, n_used, row_dst, u2, w_gate_up[0], b_gate_up[0][:, None, :],
                 w_down[0], b_down[0][:, None, :])

    out = _combine(h1, gates_t.T, p[0].reshape(n, PLE_DIM), y,
                   w_ple[0].astype(BF16), vec(g_ple[0]), w_ple_gate[0].astype(BF16), vec(g_final))
    return out.reshape(b, s, d)
```

```python
import jax
import jax.numpy as jnp
from jax import lax
from jax.experimental import pallas as pl
from jax.experimental.pallas import tpu as pltpu

F32 = jnp.float32
BF16 = jnp.bfloat16
I32 = jnp.int32

D_MODEL = 1024
ATTN_WIDTH = 512
DIFF_HEADS = 4
HEAD_DIM = 64
HEAD_V = 2 * HEAD_DIM
CONV_WIDTH = 512
CONV_KERNEL = 31
N_EXPERTS = 32
TOP_K = 4
EXPERT_FF = 1024
PLE_DIM = 256
SWIGLU_ALPHA = 1.702
SWIGLU_LIMIT = 7.0
EPS = 1e-5
IN_COLS = 3 * ATTN_WIDTH + 2 * CONV_WIDTH + 2 * D_MODEL
LAMBDA_INIT = 0.2
LOG2_E = 1.4426950408889634

MAIN_COLS = IN_COLS - ATTN_WIDTH
COL_CONV_A = 2
COL_CONV_B = 3
COL_GATE_A = 4
COL_GATE_B = 6

LANES = 128
TM_PROJ = 512
TQ = 256
TM_MIX = 512
T_ROUTE = 512
EXPERT_BLOCK = 512
CONV_ROWS = 64
CONV_PAD = 32
DMA_UNROLL = 8
VMEM_LIMIT = 56 * 1024 * 1024

_NT = (((1,), (1,)), ((), ()))


def _params(sem, vmem=VMEM_LIMIT):
    return pltpu.CompilerParams(dimension_semantics=sem, vmem_limit_bytes=vmem)


def _rms(x, g):
    return x * lax.rsqrt(jnp.mean(x * x, axis=-1, keepdims=True) + EPS) * g


ROW_TILE = D_MODEL // LANES


def _store_row_tiles(ref, x, lead=()):
    rows = x.shape[0]
    for g in range(ROW_TILE):
        ref[lead + (pl.ds(g, rows, stride=ROW_TILE), slice(None))] = x[:, g * LANES:(g + 1) * LANES]


def _load_row_tiles(ref, rows, lead=()):
    return jnp.concatenate(
        [ref[lead + (pl.ds(g, rows, stride=ROW_TILE), slice(None))] for g in range(ROW_TILE)], axis=1)


def _inproj_kernel(x_ref, g_ref, w_ref, wvt_ref, o_ref, vt_ref):
    u = _rms(x_ref[...], g_ref[...]).astype(BF16)
    ch = 512
    for c in range(MAIN_COLS // ch):
        r = jnp.dot(u, w_ref[:, c * ch:(c + 1) * ch], preferred_element_type=F32)
        if c == 0:
            r = r * (HEAD_DIM ** -0.5 * LOG2_E)
        o_ref[:, c * ch:(c + 1) * ch] = r.astype(BF16)
    vt_ref[...] = lax.dot_general(wvt_ref[...], u, _NT,
                                  preferred_element_type=F32).astype(BF16)


def _inproj(xt, g, w, wvt):
    n = xt.shape[0]
    return pl.pallas_call(
        _inproj_kernel,
        grid=(n // TM_PROJ,),
        in_specs=[
            pl.BlockSpec((TM_PROJ, D_MODEL), lambda i: (i, 0)),
            pl.BlockSpec((1, D_MODEL), lambda i: (0, 0)),
            pl.BlockSpec((D_MODEL, MAIN_COLS), lambda i: (0, 0)),
            pl.BlockSpec((ATTN_WIDTH, D_MODEL), lambda i: (0, 0)),
        ],
        out_specs=[
            pl.BlockSpec((TM_PROJ, MAIN_COLS), lambda i: (i, 0)),
            pl.BlockSpec((ATTN_WIDTH, TM_PROJ), lambda i: (0, i)),
        ],
        out_shape=[
            jax.ShapeDtypeStruct((n, MAIN_COLS), BF16),
            jax.ShapeDtypeStruct((ATTN_WIDTH, n), BF16),
        ],
        compiler_params=_params(("parallel",)),
        name="inproj",
    )(xt, g, w, wvt)


def _attn_kernel(q_ref, k_ref, vt_ref, lq1_ref, lk1_ref, lq2_ref, lk2_ref, g_ref, o_ref):
    i = pl.program_id(1)
    s_len = k_ref.shape[0]
    lane = lax.broadcasted_iota(I32, (TQ, HEAD_V), 1)
    key = lax.broadcasted_iota(I32, (TQ, 2 * TQ), 0)
    qry = lax.broadcasted_iota(I32, (TQ, 2 * TQ), 1)
    causal = key <= jnp.where(qry >= TQ, qry - TQ, qry)
    lam = (jnp.exp(jnp.sum(lq1_ref[...] * lk1_ref[...]))
           - jnp.exp(jnp.sum(lq2_ref[...] * lk2_ref[...])) + LAMBDA_INIT)

    def head(c, h):
        cols = slice(h * HEAD_V, (h + 1) * HEAD_V)
        n = c * TQ
        q = q_ref[:, cols]
        zero = jnp.zeros_like(q)
        qq = jnp.concatenate([jnp.where(lane < HEAD_DIM, q, zero),
                              jnp.where(lane >= HEAD_DIM, q, zero)], axis=0)
        s_d = lax.dot_general(k_ref[n:n + TQ, cols], qq, _NT, preferred_element_type=F32)
        s_d = jnp.where(causal, s_d, -1e30)
        m = jnp.max(s_d, axis=0, keepdims=True)
        if c:
            s_f = lax.dot_general(k_ref[0:n, cols], qq, _NT, preferred_element_type=F32)
            m = jnp.maximum(m, jnp.max(s_f, axis=0, keepdims=True))
        p_d = jnp.exp2(s_d - m)
        l = jnp.sum(p_d, axis=0, keepdims=True)
        acc = jnp.dot(vt_ref[cols, n:n + TQ], p_d.astype(BF16), preferred_element_type=F32)
        if c:
            p_f = jnp.exp2(s_f - m)
            l = l + jnp.sum(p_f, axis=0, keepdims=True)
            acc = acc + jnp.dot(vt_ref[cols, 0:n], p_f.astype(BF16), preferred_element_type=F32)
        o12 = acc / l
        o = o12[:, :TQ] - lam * o12[:, TQ:]
        o = o * lax.rsqrt(jnp.mean(o * o, axis=0, keepdims=True) + EPS) * g_ref[...]
        o_ref[:, cols] = (o * (1.0 - LAMBDA_INIT)).T.astype(o_ref.dtype)

    def block(c):
        for h in range(DIFF_HEADS):
            head(c, h)

    for c in range(s_len // TQ):
        pl.when(i == c)(lambda c=c: block(c))


def _attention(proj3, vt, lq1, lk1, lq2, lk2, g_subln):
    b, s, _ = proj3.shape
    vec = pl.BlockSpec((1, HEAD_DIM), lambda bi, i: (0, 0))
    return pl.pallas_call(
        _attn_kernel,
        grid=(b, s // TQ),
        in_specs=[
            pl.BlockSpec((None, TQ, ATTN_WIDTH), lambda bi, i: (bi, i, 0)),
            pl.BlockSpec((None, s, ATTN_WIDTH), lambda bi, i: (bi, 0, 1)),
            pl.BlockSpec((ATTN_WIDTH, s), lambda bi, i: (0, bi)),
            vec, vec, vec, vec,
            pl.BlockSpec((HEAD_V, 1), lambda bi, i: (0, 0)),
        ],
        out_specs=pl.BlockSpec((None, TQ, ATTN_WIDTH), lambda bi, i: (bi, i, 0)),
        out_shape=jax.ShapeDtypeStruct((b, s, ATTN_WIDTH), BF16),
        compiler_params=_params(("parallel", "parallel")),
        name="attn",
    )(proj3, proj3, vt, lq1, lk1, lq2, lk2, g_subln)


def _conv_kernel(ca_ref, cb_ref, w_ref, b_ref, g_ref, beta_ref, o_ref, z_ref):
    s = ca_ref.shape[0]
    glu_rows = 256
    z_ref[0:CONV_PAD, :] = jnp.zeros((CONV_PAD, CONV_WIDTH), F32)

    def glu(c, carry):
        r0 = pl.multiple_of(c * glu_rows, glu_rows)
        a = ca_ref[pl.ds(r0, glu_rows), :].astype(F32)
        g = cb_ref[pl.ds(r0, glu_rows), :].astype(F32)
        z_ref[pl.ds(CONV_PAD + r0, glu_rows), :] = a * jax.nn.sigmoid(g)
        return carry

    lax.fori_loop(0, s // glu_rows, glu, 0)

    def conv(c, carry):
        r0 = pl.multiple_of(c * CONV_ROWS, CONV_ROWS)
        acc = jnp.zeros((CONV_ROWS, CONV_WIDTH), F32) + b_ref[...]
        for phase in range(8):
            rows = CONV_ROWS + (8 if phase else 0)
            part = None
            for j in range(CONV_KERNEL):
                off = CONV_PAD - (CONV_KERNEL - 1) + j
                if off % 8 != phase:
                    continue
                term = w_ref[j:j + 1, :] * z_ref[pl.ds(r0 + (off - phase), rows), :]
                part = term if part is None else part + term
            acc = acc + part[phase:phase + CONV_ROWS]
        mu = jnp.mean(acc, axis=-1, keepdims=True)
        xc = acc - mu
        y = xc * lax.rsqrt(jnp.mean(xc * xc, axis=-1, keepdims=True) + EPS)
        y = y * g_ref[...] + beta_ref[...]
        o_ref[pl.ds(r0, CONV_ROWS), :] = (y * jax.nn.sigmoid(y)).astype(o_ref.dtype)
        return carry

    lax.fori_loop(0, s // CONV_ROWS, conv, 0)


def _conv(proj3, w_dw, b_dw, g_ln, b_ln):
    b, s, _ = proj3.shape
    vec = pl.BlockSpec((1, CONV_WIDTH), lambda bi: (0, 0))
    return pl.pallas_call(
        _conv_kernel,
        grid=(b,),
        in_specs=[
            pl.BlockSpec((None, s, CONV_WIDTH), lambda bi: (bi, 0, COL_CONV_A)),
            pl.BlockSpec((None, s, CONV_WIDTH), lambda bi: (bi, 0, COL_CONV_B)),
            pl.BlockSpec((CONV_KERNEL, CONV_WIDTH), lambda bi: (0, 0)),
            vec, vec, vec,
        ],
        out_specs=pl.BlockSpec((None, s, CONV_WIDTH), lambda bi: (bi, 0, 0)),
        out_shape=jax.ShapeDtypeStruct((b, s, CONV_WIDTH), BF16),
        scratch_shapes=[pltpu.VMEM((s + CONV_PAD, CONV_WIDTH), F32)],
        compiler_params=_params(("parallel",)),
        name="conv",
    )(proj3, proj3, w_dw, b_dw, g_ln, b_ln)


def _split_bf16(a):
    hi = a.astype(BF16)
    lo = (a - hi.astype(F32)).astype(BF16)
    return hi, lo


def _mix_kernel(x_ref, o_ref, c_ref, ga0_ref, ga1_ref, gb0_ref, gb1_ref,
                wa_ref, wc_ref, wo_ref, gffn_ref, wr_ref, br_ref,
                h_ref, u_ref, idx_ref, gate_ref):
    o = o_ref[...]
    c = c_ref[...]
    half = D_MODEL // 2
    acc = x_ref[...]
    for hh, (ga_ref, gb_ref) in enumerate(((ga0_ref, gb0_ref), (ga1_ref, gb1_ref))):
        cols = slice(hh * half, (hh + 1) * half)
        a = jnp.dot(o, wa_ref[:, cols], preferred_element_type=F32)
        b = jnp.dot(c, wc_ref[:, cols], preferred_element_type=F32)
        m = (jax.nn.sigmoid(ga_ref[...].astype(F32)) * a
             + jax.nn.sigmoid(gb_ref[...].astype(F32)) * b)
        acc = acc + jnp.dot(m.astype(BF16), wo_ref[cols, :], preferred_element_type=F32)
    h_ref[...] = acc
    u = _rms(acc, gffn_ref[...])
    _store_row_tiles(u_ref, u)

    u_hi, u_lo = _split_bf16(u)
    w_hi, w_lo = _split_bf16(wr_ref[...])
    logits = (lax.dot_general(w_hi, u_hi, _NT, preferred_element_type=F32)
              + lax.dot_general(w_hi, u_lo, _NT, preferred_element_type=F32)
              + lax.dot_general(w_lo, u_hi, _NT, preferred_element_type=F32)
              + br_ref[...])
    eidx = lax.broadcasted_iota(I32, logits.shape, 0)
    vals, idxs = [], []
    for _ in range(TOP_K):
        mx = jnp.max(logits, axis=0, keepdims=True)
        sel = jnp.min(jnp.where(logits == mx, eidx, N_EXPERTS), axis=0, keepdims=True)
        vals.append(mx)
        idxs.append(sel)
        logits = jnp.where(eidx == sel, -jnp.inf, logits)
    ex = [jnp.exp(v - vals[0]) for v in vals]
    den = ex[0] + ex[1] + ex[2] + ex[3]
    idx_ref[...] = jnp.concatenate(idxs, axis=0)
    gate_ref[...] = jnp.concatenate([e / den for e in ex], axis=0)


def _mix(xt, o, c, proj, wa, wc, wo, g_ffn, wr_t, br):
    n = xt.shape[0]
    tm = TM_MIX
    half = D_MODEL // 2
    row = lambda w, j: pl.BlockSpec((tm, w), lambda i, j=j: (i, j))
    full = lambda a: pl.BlockSpec(a.shape, lambda i: (0,) * a.ndim)
    return pl.pallas_call(
        _mix_kernel,
        grid=(n // tm,),
        in_specs=[
            row(D_MODEL, 0), row(ATTN_WIDTH, 0), row(CONV_WIDTH, 0),
            row(half, COL_GATE_A), row(half, COL_GATE_A + 1),
            row(half, COL_GATE_B), row(half, COL_GATE_B + 1),
            full(wa), full(wc), full(wo), full(g_ffn), full(wr_t), full(br),
        ],
        out_specs=[
            pl.BlockSpec((tm, D_MODEL), lambda i: (i, 0)),
            pl.BlockSpec((tm * ROW_TILE, LANES), lambda i: (i, 0)),
            pl.BlockSpec((TOP_K, tm), lambda i: (0, i)),
            pl.BlockSpec((TOP_K, tm), lambda i: (0, i)),
        ],
        out_shape=[
            jax.ShapeDtypeStruct((n, D_MODEL), F32),
            jax.ShapeDtypeStruct((n * ROW_TILE, LANES), F32),
            jax.ShapeDtypeStruct((TOP_K, n), I32),
            jax.ShapeDtypeStruct((TOP_K, n), F32),
        ],
        compiler_params=_params(("parallel",)),
        name="mix",
    )(xt, o, c, proj, proj, proj, proj, wa, wc, wo, g_ffn, wr_t, br)


def _rank_kernel(idx_ref, rank_ref, cnt_ref):
    t = T_ROUTE
    idx = idx_ref[...]
    eidx = lax.broadcasted_iota(I32, (N_EXPERTS, t), 0)
    onehot = [eidx == idx[k:k + 1, :] for k in range(TOP_K)]
    member = onehot[0] | onehot[1] | onehot[2] | onehot[3]
    r = lax.broadcasted_iota(I32, (t, t), 0)
    c = lax.broadcasted_iota(I32, (t, t), 1)
    tri = jnp.where(r < c, 1.0, 0.0).astype(BF16)
    mem_f = jnp.where(member, 1.0, 0.0)
    prefix = jnp.dot(mem_f.astype(BF16), tri, preferred_element_type=F32)
    ranks = [jnp.sum(jnp.where(onehot[k], prefix, 0.0), axis=0, keepdims=True)
             for k in range(TOP_K)]
    rank_ref[...] = jnp.concatenate(ranks, axis=0).astype(I32)
    cnt_ref[...] = jnp.sum(mem_f, axis=1, keepdims=True).astype(I32)


def _rank(idx_t):
    n = idx_t.shape[1]
    return pl.pallas_call(
        _rank_kernel,
        grid=(n // T_ROUTE,),
        in_specs=[pl.BlockSpec((TOP_K, T_ROUTE), lambda i: (0, i))],
        out_specs=[
            pl.BlockSpec((TOP_K, T_ROUTE), lambda i: (0, i)),
            pl.BlockSpec((None, N_EXPERTS, 1), lambda i: (i, 0, 0)),
        ],
        out_shape=[
            jax.ShapeDtypeStruct((TOP_K, n), I32),
            jax.ShapeDtypeStruct((n // T_ROUTE, N_EXPERTS, 1), I32),
        ],
        compiler_params=_params(("parallel",)),
        name="rank",
    )(idx_t)


def _pos_kernel(idx_ref, rank_ref, off_ref, pos_ref):
    idx = idx_ref[...]
    eidx = lax.broadcasted_iota(I32, (N_EXPERTS, idx.shape[1]), 0)
    rows = [jnp.sum(jnp.where(eidx == idx[k:k + 1, :], off_ref[...], 0),
                    axis=0, keepdims=True) for k in range(TOP_K)]
    pos_ref[...] = jnp.concatenate(rows, axis=0) + rank_ref[...]


def _pos(idx_t, rank_t, local_off):
    n = idx_t.shape[1]
    blk = pl.BlockSpec((TOP_K, T_ROUTE), lambda i: (0, i))
    return pl.pallas_call(
        _pos_kernel,
        grid=(n // T_ROUTE,),
        in_specs=[blk, blk, pl.BlockSpec((None, N_EXPERTS, 1), lambda i: (i, 0, 0))],
        out_specs=blk,
        out_shape=jax.ShapeDtypeStruct((TOP_K, n), I32),
        compiler_params=_params(("parallel",)),
        name="pos",
    )(idx_t, rank_t, local_off)


def _row(ref, r):
    return ref.at[pl.ds(pl.multiple_of(r * ROW_TILE, ROW_TILE), ROW_TILE), :]


def _run_copies(cnt, src_ref, src_row, dst_ref, dst_row, sem):
    for bit in reversed(range(T_ROUTE.bit_length())):
        size = 1 << bit
        piece = cnt & size

        @pl.when(piece != 0)
        def _(size=size, src_row=src_row, dst_row=dst_row):
            rows = size * ROW_TILE
            pltpu.make_async_copy(
                src_ref.at[pl.ds(pl.multiple_of(src_row * ROW_TILE, ROW_TILE), rows), :],
                dst_ref.at[pl.ds(pl.multiple_of(dst_row * ROW_TILE, ROW_TILE), rows), :],
                sem).start()

        src_row = src_row + piece
        dst_row = dst_row + piece


def _dispatch_kernel(pstart_ref, pend_ref, cnt_ref, run_ref, slot_ref, u_ref, xs_ref,
                     grouped, zero_ref, sem, zsem):
    t = T_ROUTE
    i = pl.program_id(0)
    cur = i % 2
    blk = EXPERT_BLOCK * ROW_TILE
    buf = grouped.at[cur]

    def drain(slot):
        pltpu.make_async_copy(grouped.at[slot], grouped.at[slot], sem.at[slot]).wait()

    @pl.when(i == 0)
    def _():
        zero_ref[...] = jnp.zeros_like(zero_ref)

        def z(e, carry):
            @pl.when(pend_ref[e] > pstart_ref[e])
            def _():
                start = (pend_ref[e] - EXPERT_BLOCK) * ROW_TILE
                cp = pltpu.make_async_copy(
                    zero_ref, xs_ref.at[pl.ds(pl.multiple_of(start, blk), blk), :], zsem)
                cp.start()
                cp.wait()
            return carry

        lax.fori_loop(0, N_EXPERTS, z, 0)

    @pl.when(i >= 2)
    def _():
        drain(cur)

    def group(tok, carry):
        tile = u_ref[pl.ds(pl.multiple_of(tok * ROW_TILE, ROW_TILE), ROW_TILE), :]
        for k in range(TOP_K):
            buf[pl.ds(pl.multiple_of(slot_ref[k, tok] * ROW_TILE, ROW_TILE), ROW_TILE), :] = tile
        return carry

    lax.fori_loop(0, t, group, 0, unroll=DMA_UNROLL)

    def runs(e, off):
        cnt = cnt_ref[i * N_EXPERTS + e]
        _run_copies(cnt, buf, off, xs_ref, run_ref[i * N_EXPERTS + e], sem.at[cur])
        return off + cnt

    lax.fori_loop(0, N_EXPERTS, runs, 0)

    @pl.when(i == pl.num_programs(0) - 1)
    def _():
        drain(cur)

        @pl.when(i >= 1)
        def _():
            drain(1 - cur)


def _dispatch(pstart, pend, block_cnt, run_start, slot_t, u, n_rows):
    n = u.shape[0] // ROW_TILE
    t = T_ROUTE
    grid_spec = pltpu.PrefetchScalarGridSpec(
        num_scalar_prefetch=4,
        grid=(n // t,),
        in_specs=[
            pl.BlockSpec((TOP_K, t), lambda i, *_: (0, i), memory_space=pltpu.SMEM),
            pl.BlockSpec((t * ROW_TILE, LANES), lambda i, *_: (i, 0)),
        ],
        out_specs=pl.BlockSpec(memory_space=pl.ANY),
        scratch_shapes=[
            pltpu.VMEM((2, TOP_K * t * ROW_TILE, LANES), F32),
            pltpu.VMEM((EXPERT_BLOCK * ROW_TILE, LANES), F32),
            pltpu.SemaphoreType.DMA((2,)),
            pltpu.SemaphoreType.DMA(()),
        ],
    )
    return pl.pallas_call(
        _dispatch_kernel,
        grid_spec=grid_spec,
        out_shape=jax.ShapeDtypeStruct((n_rows * ROW_TILE, LANES), F32),
        compiler_params=_params(("arbitrary",)),
        name="dispatch",
    )(pstart, pend, block_cnt, run_start, slot_t, u)


def _expert_kernel(be_ref, nb_ref, x_ref, wgu_ref, bgu_ref, wd_ref, bd_ref, y_ref,
                   wgu_s, wd32_s, wd_s):
    i = pl.program_id(0)
    nb = nb_ref[0]
    blk = EXPERT_BLOCK
    half = EXPERT_FF // 2
    first = jnp.logical_or(i == 0, be_ref[i] != be_ref[jnp.maximum(i - 1, 0)])

    @pl.when(jnp.logical_and(first, i < nb))
    def _():
        rows = 256
        for r in range(0, D_MODEL, rows):
            wgu_s[r:r + rows, :] = wgu_ref[r:r + rows, :].astype(BF16)
        for cs in range(D_MODEL // LANES):
            lanes = slice(cs * LANES, (cs + 1) * LANES)
            wd32_s[cs, pl.ds(0, half, stride=2), :] = wd_ref[0:half, lanes]
            wd32_s[cs, pl.ds(1, half, stride=2), :] = wd_ref[half:EXPERT_FF, lanes]
            wd_s[:, lanes] = wd32_s[cs].astype(BF16)

    @pl.when(i < nb)
    def _():
        xb = _load_row_tiles(x_ref, blk).astype(BF16)
        ch = 256
        even = (lax.broadcasted_iota(I32, (blk, LANES), 1) % 2) == 0
        y = jnp.zeros((blk, D_MODEL), F32) + bd_ref[...]
        for c in range(EXPERT_FF // ch):
            c1 = slice(c * ch, (c + 1) * ch)
            c2 = slice(EXPERT_FF + c * ch, EXPERT_FF + (c + 1) * ch)
            gu1 = jnp.dot(xb, wgu_s[:, c1], preferred_element_type=F32) + bgu_ref[:, c1]
            gu2 = jnp.dot(xb, wgu_s[:, c2], preferred_element_type=F32) + bgu_ref[:, c2]
            acts = []
            for v in range(ch // LANES):
                a = gu1[:, v * LANES:(v + 1) * LANES]
                b = gu2[:, v * LANES:(v + 1) * LANES]
                g = jnp.where(even, a, pltpu.roll(b, 1, axis=1))
                l = jnp.where(even, pltpu.roll(a, LANES - 1, axis=1), b)
                g = jnp.minimum(g, SWIGLU_LIMIT)
                l = jnp.clip(l, -SWIGLU_LIMIT, SWIGLU_LIMIT)
                acts.append(((l + 1.0) * (g * jax.nn.sigmoid(g * SWIGLU_ALPHA))).astype(BF16))
            act = jnp.concatenate(acts, axis=1)
            y = y + jnp.dot(act, wd_s[c1, :], preferred_element_type=F32)
        _store_row_tiles(y_ref, y)

    @pl.when(i >= nb)
    def _():
        y_ref[...] = jnp.zeros_like(y_ref)


def _experts(block_expert, n_used, xs, wgu, bgu, wd, bd):
    n_rows = xs.shape[0] // ROW_TILE
    blk = EXPERT_BLOCK
    wspec = lambda r, c: pl.BlockSpec((None, r, c), lambda i, be, nb: (be[i], 0, 0))
    grid_spec = pltpu.PrefetchScalarGridSpec(
        num_scalar_prefetch=2,
        grid=(n_rows // blk,),
        in_specs=[
            pl.BlockSpec((blk * ROW_TILE, LANES), lambda i, be, nb: (jnp.minimum(i, nb[0] - 1), 0)),
            wspec(D_MODEL, 2 * EXPERT_FF), wspec(1, 2 * EXPERT_FF),
            wspec(EXPERT_FF, D_MODEL), wspec(1, D_MODEL),
        ],
        out_specs=pl.BlockSpec((blk * ROW_TILE, LANES), lambda i, be, nb: (i, 0)),
        scratch_shapes=[
            pltpu.VMEM((D_MODEL, 2 * EXPERT_FF), BF16),
            pltpu.VMEM((D_MODEL // LANES, EXPERT_FF, LANES), F32),
            pltpu.VMEM((EXPERT_FF, D_MODEL), BF16),
        ],
    )
    return pl.pallas_call(
        _expert_kernel,
        grid_spec=grid_spec,
        out_shape=jax.ShapeDtypeStruct((n_rows * ROW_TILE, LANES), F32),
        compiler_params=_params(("arbitrary",)),
        name="expert",
    )(block_expert, n_used, xs, wgu, bgu, wd, bd)


def _combine_kernel(cnt_ref, run_ref, slot_ref, gsm_ref, h_ref, p_ref, y_ref, wple_ref, gple_ref,
                    wpg_ref, gfin_ref, o_ref, staged, moe_s, sem):
    t = T_ROUTE
    i = pl.program_id(0)
    cur = i % 2

    def fetch(block, slot):
        def runs(e, off):
            cnt = cnt_ref[block * N_EXPERTS + e]
            _run_copies(cnt, y_ref, run_ref[block * N_EXPERTS + e], staged.at[slot], off,
                        sem.at[slot])
            return off + cnt

        lax.fori_loop(0, N_EXPERTS, runs, 0)

    @pl.when(i == 0)
    def _():
        fetch(0, 0)

    @pl.when(i + 1 < pl.num_programs(0))
    def _():
        fetch(i + 1, 1 - cur)

    emb = jnp.dot(p_ref[...].astype(BF16), wple_ref[...], preferred_element_type=F32)

    pltpu.make_async_copy(staged.at[cur], staged.at[cur], sem.at[cur]).wait()
    buf = staged.at[cur]

    def pick(tok, carry):
        acc = None
        for k in range(TOP_K):
            row = buf[pl.ds(pl.multiple_of(slot_ref[k, tok] * ROW_TILE, ROW_TILE), ROW_TILE), :]
            term = gsm_ref[k, tok] * row
            acc = term if acc is None else acc + term
        moe_s[pl.ds(pl.multiple_of(tok * ROW_TILE, ROW_TILE), ROW_TILE), :] = acc
        return carry

    lax.fori_loop(0, t, pick, 0, unroll=DMA_UNROLL)

    h = h_ref[...] + _load_row_tiles(moe_s, t)
    r = _rms(h, gple_ref[...]).astype(BF16)
    sig = jax.nn.sigmoid(jnp.dot(r, wpg_ref[...], preferred_element_type=F32))
    h = h + emb * sig
    o_ref[...] = _rms(h, gfin_ref[...])


def _combine(block_cnt, run_start, slot_t, gates_t, h, p2, y, wple, gple, wpg, gfin):
    n = h.shape[0]
    t = T_ROUTE
    full = lambda a: pl.BlockSpec(a.shape, lambda i, *_: (0,) * a.ndim)
    smem = lambda: pl.BlockSpec((TOP_K, t), lambda i, *_: (0, i), memory_space=pltpu.SMEM)
    grid_spec = pltpu.PrefetchScalarGridSpec(
        num_scalar_prefetch=2,
        grid=(n // t,),
        in_specs=[
            smem(), smem(),
            pl.BlockSpec((t, D_MODEL), lambda i, *_: (i, 0)),
            pl.BlockSpec((t, PLE_DIM), lambda i, *_: (i, 0)),
            pl.BlockSpec(memory_space=pl.ANY),
            full(wple), full(gple), full(wpg), full(gfin),
        ],
        out_specs=pl.BlockSpec((t, D_MODEL), lambda i, *_: (i, 0)),
        scratch_shapes=[
            pltpu.VMEM((2, TOP_K * t * ROW_TILE, LANES), F32),
            pltpu.VMEM((t * ROW_TILE, LANES), F32),
            pltpu.SemaphoreType.DMA((2,)),
        ],
    )
    return pl.pallas_call(
        _combine_kernel,
        grid_spec=grid_spec,
        out_shape=jax.ShapeDtypeStruct((n, D_MODEL), F32),
        compiler_params=_params(("arbitrary",)),
        name="combine",
    )(block_cnt, run_start, slot_t, gates_t, h, p2, y, wple, gple, wpg, gfin)


def kernel(x, p, g_mix, w_in, lambda_q1, lambda_k1, lambda_q2, lambda_k2, g_subln, w_attn_out,
           w_dw, b_dw, g_conv_ln, b_conv_ln, w_conv_out, w_o, g_ffn, w_router, b_router,
           w_gate_up, b_gate_up, w_down, b_down, w_ple, g_ple, w_ple_gate, g_final):
    b, s, d = x.shape
    n = b * s
    xt = x.reshape(n, d)
    vec = lambda a: a.reshape(1, -1)

    w_in0 = w_in[0]
    w_main = jnp.concatenate([w_in0[:, :2 * ATTN_WIDTH], w_in0[:, 3 * ATTN_WIDTH:]], axis=1)
    w_vt = w_in0[:, 2 * ATTN_WIDTH:3 * ATTN_WIDTH].T
    proj, vt = _inproj(xt, vec(g_mix[0]), w_main.astype(BF16), w_vt.astype(BF16))
    proj3 = proj.reshape(b, s, MAIN_COLS)
    attn = _attention(proj3, vt, vec(lambda_q1[0]), vec(lambda_k1[0]), vec(lambda_q2[0]),
                      vec(lambda_k2[0]), g_subln[0].reshape(HEAD_V, 1))
    conv = _conv(proj3, w_dw[0], vec(b_dw[0]), vec(g_conv_ln[0]), vec(b_conv_ln[0]))
    h1, u2, idx_t, gates_t = _mix(
        xt, attn.reshape(n, ATTN_WIDTH), conv.reshape(n, CONV_WIDTH), proj,
        w_attn_out[0].astype(BF16), w_conv_out[0].astype(BF16), w_o[0].astype(BF16),
        vec(g_ffn[0]), w_router[0].T, b_router[0].reshape(N_EXPERTS, 1))

    rank_t, block_cnt = _rank(idx_t)
    block_cnt = block_cnt[:, :, 0]
    counts = jnp.sum(block_cnt, axis=0)
    blk = EXPERT_BLOCK
    n_blocks = n * TOP_K // blk + N_EXPERTS
    padded = (counts + blk - 1) // blk * blk
    pend = jnp.cumsum(padded).astype(I32)
    pstart = pend - padded
    n_used = (pend[-1:] // blk).astype(I32)
    block_start = jnp.arange(n_blocks, dtype=I32) * blk
    block_expert = jnp.minimum(
        jnp.sum((pend[None, :] <= block_start[:, None]).astype(I32), axis=1), N_EXPERTS - 1)
    run_start = (pstart[None, :] + jnp.cumsum(block_cnt, axis=0) - block_cnt).astype(I32)
    local_off = (jnp.cumsum(block_cnt, axis=1) - block_cnt).astype(I32)
    slot_t = _pos(idx_t, rank_t, local_off[:, :, None])
    cnt_flat = block_cnt.reshape(-1).astype(I32)
    run_flat = run_start.reshape(-1)

    xs = _dispatch(pstart, pend, cnt_flat, run_flat, slot_t, u2, n_blocks * blk)
    y = _experts(block_expert, n_used, xs, w_gate_up[0], b_gate_up[0][:, None, :],
                 w_down[0], b_down[0][:, None, :])

    out = _combine(cnt_flat, run_flat, slot_t, gates_t, h1, p[0].reshape(n, PLE_DIM), y,
                   w_ple[0].astype(BF16), vec(g_ple[0]), w_ple_gate[0].astype(BF16), vec(g_final))
    return out.reshape(b, s, d)
```

```python
import jax
import jax.numpy as jnp
from jax import lax
from jax.experimental import pallas as pl
from jax.experimental.pallas import tpu as pltpu

F32 = jnp.float32
BF16 = jnp.bfloat16
I32 = jnp.int32

D_MODEL = 1024
ATTN_WIDTH = 512
DIFF_HEADS = 4
HEAD_DIM = 64
HEAD_V = 2 * HEAD_DIM
CONV_WIDTH = 512
CONV_KERNEL = 31
N_EXPERTS = 32
TOP_K = 4
EXPERT_FF = 1024
PLE_DIM = 256
SWIGLU_ALPHA = 1.702
SWIGLU_LIMIT = 7.0
EPS = 1e-5
IN_COLS = 3 * ATTN_WIDTH + 2 * CONV_WIDTH + 2 * D_MODEL
LAMBDA_INIT = 0.2
LOG2_E = 1.4426950408889634

MAIN_COLS = IN_COLS - ATTN_WIDTH
COL_CONV_A = 2
COL_CONV_B = 3
COL_GATE_A = 4
COL_GATE_B = 6

LANES = 128
TM_PROJ = 512
TQ = 256
TM_MIX = 512
T_ROUTE = 512
EXPERT_BLOCK = 512
CONV_ROWS = 64
CONV_PAD = 32
DMA_UNROLL = 8
VMEM_LIMIT = 56 * 1024 * 1024

_NT = (((1,), (1,)), ((), ()))


def _params(sem, vmem=VMEM_LIMIT):
    return pltpu.CompilerParams(dimension_semantics=sem, vmem_limit_bytes=vmem)


def _rms(x, g):
    return x * lax.rsqrt(jnp.mean(x * x, axis=-1, keepdims=True) + EPS) * g


ROW_TILE = D_MODEL // LANES


def _store_row_tiles(ref, x, lead=()):
    rows = x.shape[0]
    for g in range(ROW_TILE):
        ref[lead + (pl.ds(g, rows, stride=ROW_TILE), slice(None))] = x[:, g * LANES:(g + 1) * LANES]


def _load_row_tiles(ref, rows, lead=()):
    return jnp.concatenate(
        [ref[lead + (pl.ds(g, rows, stride=ROW_TILE), slice(None))] for g in range(ROW_TILE)], axis=1)


def _inproj_kernel(x_ref, g_ref, w_ref, wvt_ref, o_ref, vt_ref):
    u = _rms(x_ref[...], g_ref[...]).astype(BF16)
    ch = 512
    for c in range(MAIN_COLS // ch):
        r = jnp.dot(u, w_ref[:, c * ch:(c + 1) * ch], preferred_element_type=F32)
        if c == 0:
            r = r * (HEAD_DIM ** -0.5 * LOG2_E)
        o_ref[:, c * ch:(c + 1) * ch] = r.astype(BF16)
    vt_ref[...] = lax.dot_general(wvt_ref[...], u, _NT,
                                  preferred_element_type=F32).astype(BF16)


def _inproj(xt, g, w, wvt):
    n = xt.shape[0]
    return pl.pallas_call(
        _inproj_kernel,
        grid=(n // TM_PROJ,),
        in_specs=[
            pl.BlockSpec((TM_PROJ, D_MODEL), lambda i: (i, 0)),
            pl.BlockSpec((1, D_MODEL), lambda i: (0, 0)),
            pl.BlockSpec((D_MODEL, MAIN_COLS), lambda i: (0, 0)),
            pl.BlockSpec((ATTN_WIDTH, D_MODEL), lambda i: (0, 0)),
        ],
        out_specs=[
            pl.BlockSpec((TM_PROJ, MAIN_COLS), lambda i: (i, 0)),
            pl.BlockSpec((ATTN_WIDTH, TM_PROJ), lambda i: (0, i)),
        ],
        out_shape=[
            jax.ShapeDtypeStruct((n, MAIN_COLS), BF16),
            jax.ShapeDtypeStruct((ATTN_WIDTH, n), BF16),
        ],
        compiler_params=_params(("parallel",)),
        name="inproj",
    )(xt, g, w, wvt)


def _attn_kernel(q_ref, k_ref, vt_ref, lq1_ref, lk1_ref, lq2_ref, lk2_ref, g_ref, o_ref):
    i = pl.program_id(1)
    s_len = k_ref.shape[0]
    lane = lax.broadcasted_iota(I32, (TQ, HEAD_V), 1)
    key = lax.broadcasted_iota(I32, (TQ, 2 * TQ), 0)
    qry = lax.broadcasted_iota(I32, (TQ, 2 * TQ), 1)
    causal = key <= jnp.where(qry >= TQ, qry - TQ, qry)
    lam = (jnp.exp(jnp.sum(lq1_ref[...] * lk1_ref[...]))
           - jnp.exp(jnp.sum(lq2_ref[...] * lk2_ref[...])) + LAMBDA_INIT)

    def head(c, h):
        cols = slice(h * HEAD_V, (h + 1) * HEAD_V)
        n = c * TQ
        q = q_ref[:, cols]
        zero = jnp.zeros_like(q)
        qq = jnp.concatenate([jnp.where(lane < HEAD_DIM, q, zero),
                              jnp.where(lane >= HEAD_DIM, q, zero)], axis=0)
        s = lax.dot_general(k_ref[0:n + TQ, cols], qq, _NT, preferred_element_type=F32)
        s_d = jnp.where(causal, s[n:], -1e30)
        m = jnp.max(s_d, axis=0, keepdims=True)
        if c:
            m = jnp.maximum(m, jnp.max(s[:n], axis=0, keepdims=True))
        p = jnp.exp2(s_d - m)
        if c:
            p = jnp.concatenate([jnp.exp2(s[:n] - m), p], axis=0)
        l = jnp.sum(p, axis=0, keepdims=True)
        acc = jnp.dot(vt_ref[cols, 0:n + TQ], p.astype(BF16), preferred_element_type=F32)
        o12 = acc / l
        o = o12[:, :TQ] - lam * o12[:, TQ:]
        o = o * lax.rsqrt(jnp.mean(o * o, axis=0, keepdims=True) + EPS) * g_ref[...]
        o_ref[:, cols] = (o * (1.0 - LAMBDA_INIT)).T.astype(o_ref.dtype)

    def block(c):
        for h in range(DIFF_HEADS):
            head(c, h)

    for c in range(s_len // TQ):
        pl.when(i == c)(lambda c=c: block(c))


def _attention(proj3, vt, lq1, lk1, lq2, lk2, g_subln):
    b, s, _ = proj3.shape
    vec = pl.BlockSpec((1, HEAD_DIM), lambda bi, i: (0, 0))
    return pl.pallas_call(
        _attn_kernel,
        grid=(b, s // TQ),
        in_specs=[
            pl.BlockSpec((None, TQ, ATTN_WIDTH), lambda bi, i: (bi, i, 0)),
            pl.BlockSpec((None, s, ATTN_WIDTH), lambda bi, i: (bi, 0, 1)),
            pl.BlockSpec((ATTN_WIDTH, s), lambda bi, i: (0, bi)),
            vec, vec, vec, vec,
            pl.BlockSpec((HEAD_V, 1), lambda bi, i: (0, 0)),
        ],
        out_specs=pl.BlockSpec((None, TQ, ATTN_WIDTH), lambda bi, i: (bi, i, 0)),
        out_shape=jax.ShapeDtypeStruct((b, s, ATTN_WIDTH), BF16),
        compiler_params=_params(("parallel", "parallel")),
        name="attn",
    )(proj3, proj3, vt, lq1, lk1, lq2, lk2, g_subln)


def _conv_kernel(ca_ref, cb_ref, w_ref, b_ref, g_ref, beta_ref, o_ref, z_ref):
    s = ca_ref.shape[0]
    glu_rows = 256
    z_ref[0:CONV_PAD, :] = jnp.zeros((CONV_PAD, CONV_WIDTH), F32)

    def glu(c, carry):
        r0 = pl.multiple_of(c * glu_rows, glu_rows)
        a = ca_ref[pl.ds(r0, glu_rows), :].astype(F32)
        g = cb_ref[pl.ds(r0, glu_rows), :].astype(F32)
        z_ref[pl.ds(CONV_PAD + r0, glu_rows), :] = a * jax.nn.sigmoid(g)
        return carry

    lax.fori_loop(0, s // glu_rows, glu, 0)

    def conv(c, carry):
        r0 = pl.multiple_of(c * CONV_ROWS, CONV_ROWS)
        acc = jnp.zeros((CONV_ROWS, CONV_WIDTH), F32) + b_ref[...]
        for phase in range(8):
            rows = CONV_ROWS + (8 if phase else 0)
            part = None
            for j in range(CONV_KERNEL):
                off = CONV_PAD - (CONV_KERNEL - 1) + j
                if off % 8 != phase:
                    continue
                term = w_ref[j:j + 1, :] * z_ref[pl.ds(r0 + (off - phase), rows), :]
                part = term if part is None else part + term
            acc = acc + part[phase:phase + CONV_ROWS]
        mu = jnp.mean(acc, axis=-1, keepdims=True)
        xc = acc - mu
        y = xc * lax.rsqrt(jnp.mean(xc * xc, axis=-1, keepdims=True) + EPS)
        y = y * g_ref[...] + beta_ref[...]
        o_ref[pl.ds(r0, CONV_ROWS), :] = (y * jax.nn.sigmoid(y)).astype(o_ref.dtype)
        return carry

    lax.fori_loop(0, s // CONV_ROWS, conv, 0)


def _conv(proj3, w_dw, b_dw, g_ln, b_ln):
    b, s, _ = proj3.shape
    vec = pl.BlockSpec((1, CONV_WIDTH), lambda bi: (0, 0))
    return pl.pallas_call(
        _conv_kernel,
        grid=(b,),
        in_specs=[
            pl.BlockSpec((None, s, CONV_WIDTH), lambda bi: (bi, 0, COL_CONV_A)),
            pl.BlockSpec((None, s, CONV_WIDTH), lambda bi: (bi, 0, COL_CONV_B)),
            pl.BlockSpec((CONV_KERNEL, CONV_WIDTH), lambda bi: (0, 0)),
            vec, vec, vec,
        ],
        out_specs=pl.BlockSpec((None, s, CONV_WIDTH), lambda bi: (bi, 0, 0)),
        out_shape=jax.ShapeDtypeStruct((b, s, CONV_WIDTH), BF16),
        scratch_shapes=[pltpu.VMEM((s + CONV_PAD, CONV_WIDTH), F32)],
        compiler_params=_params(("parallel",)),
        name="conv",
    )(proj3, proj3, w_dw, b_dw, g_ln, b_ln)


def _split_bf16(a):
    hi = a.astype(BF16)
    lo = (a - hi.astype(F32)).astype(BF16)
    return hi, lo


def _mix_kernel(x_ref, o_ref, c_ref, ga_ref, gb_ref,
                wa_ref, wc_ref, wo_ref, gffn_ref, wr_ref, br_ref,
                h_ref, u_ref, idx_ref, gate_ref):
    a = jnp.dot(o_ref[...], wa_ref[...], preferred_element_type=F32)
    b = jnp.dot(c_ref[...], wc_ref[...], preferred_element_type=F32)
    m = (jax.nn.sigmoid(ga_ref[...].astype(F32)) * a
         + jax.nn.sigmoid(gb_ref[...].astype(F32)) * b)
    acc = x_ref[...] + jnp.dot(m.astype(BF16), wo_ref[...], preferred_element_type=F32)
    h_ref[...] = acc
    u = _rms(acc, gffn_ref[...])
    _store_row_tiles(u_ref, u)

    u_hi, u_lo = _split_bf16(u)
    w_hi, w_lo = _split_bf16(wr_ref[...])
    logits = (lax.dot_general(w_hi, u_hi, _NT, preferred_element_type=F32)
              + lax.dot_general(w_hi, u_lo, _NT, preferred_element_type=F32)
              + lax.dot_general(w_lo, u_hi, _NT, preferred_element_type=F32)
              + br_ref[...])
    eidx = lax.broadcasted_iota(I32, logits.shape, 0)
    vals, idxs = [], []
    for _ in range(TOP_K):
        mx = jnp.max(logits, axis=0, keepdims=True)
        sel = jnp.min(jnp.where(logits == mx, eidx, N_EXPERTS), axis=0, keepdims=True)
        vals.append(mx)
        idxs.append(sel)
        logits = jnp.where(eidx == sel, -jnp.inf, logits)
    ex = [jnp.exp(v - vals[0]) for v in vals]
    den = ex[0] + ex[1] + ex[2] + ex[3]
    idx_ref[...] = jnp.concatenate(idxs, axis=0)
    gate_ref[...] = jnp.concatenate([e / den for e in ex], axis=0)


def _mix(xt, o, c, proj, wa, wc, wo, g_ffn, wr_t, br):
    n = xt.shape[0]
    tm = TM_MIX
    row = lambda w, j: pl.BlockSpec((tm, w), lambda i, j=j: (i, j))
    full = lambda a: pl.BlockSpec(a.shape, lambda i: (0,) * a.ndim)
    return pl.pallas_call(
        _mix_kernel,
        grid=(n // tm,),
        in_specs=[
            row(D_MODEL, 0), row(ATTN_WIDTH, 0), row(CONV_WIDTH, 0),
            row(D_MODEL, COL_GATE_A // 2), row(D_MODEL, COL_GATE_B // 2),
            full(wa), full(wc), full(wo), full(g_ffn), full(wr_t), full(br),
        ],
        out_specs=[
            pl.BlockSpec((tm, D_MODEL), lambda i: (i, 0)),
            pl.BlockSpec((tm * ROW_TILE, LANES), lambda i: (i, 0)),
            pl.BlockSpec((TOP_K, tm), lambda i: (0, i)),
            pl.BlockSpec((TOP_K, tm), lambda i: (0, i)),
        ],
        out_shape=[
            jax.ShapeDtypeStruct((n, D_MODEL), F32),
            jax.ShapeDtypeStruct((n * ROW_TILE, LANES), F32),
            jax.ShapeDtypeStruct((TOP_K, n), I32),
            jax.ShapeDtypeStruct((TOP_K, n), F32),
        ],
        compiler_params=_params(("parallel",)),
        name="mix",
    )(xt, o, c, proj, proj, wa, wc, wo, g_ffn, wr_t, br)


def _rank_kernel(idx_ref, rank_ref, cnt_ref):
    t = T_ROUTE
    idx = idx_ref[...]
    eidx = lax.broadcasted_iota(I32, (N_EXPERTS, t), 0)
    onehot = [eidx == idx[k:k + 1, :] for k in range(TOP_K)]
    member = onehot[0] | onehot[1] | onehot[2] | onehot[3]
    r = lax.broadcasted_iota(I32, (t, t), 0)
    c = lax.broadcasted_iota(I32, (t, t), 1)
    tri = jnp.where(r < c, 1.0, 0.0).astype(BF16)
    mem_f = jnp.where(member, 1.0, 0.0)
    prefix = jnp.dot(mem_f.astype(BF16), tri, preferred_element_type=F32)
    ranks = [jnp.sum(jnp.where(onehot[k], prefix, 0.0), axis=0, keepdims=True)
             for k in range(TOP_K)]
    rank_ref[...] = jnp.concatenate(ranks, axis=0).astype(I32)
    cnt_ref[...] = jnp.sum(mem_f, axis=1, keepdims=True).astype(I32)


def _rank(idx_t):
    n = idx_t.shape[1]
    return pl.pallas_call(
        _rank_kernel,
        grid=(n // T_ROUTE,),
        in_specs=[pl.BlockSpec((TOP_K, T_ROUTE), lambda i: (0, i))],
        out_specs=[
            pl.BlockSpec((TOP_K, T_ROUTE), lambda i: (0, i)),
            pl.BlockSpec((None, N_EXPERTS, 1), lambda i: (i, 0, 0)),
        ],
        out_shape=[
            jax.ShapeDtypeStruct((TOP_K, n), I32),
            jax.ShapeDtypeStruct((n // T_ROUTE, N_EXPERTS, 1), I32),
        ],
        compiler_params=_params(("parallel",)),
        name="rank",
    )(idx_t)


def _pos_kernel(idx_ref, rank_ref, off_ref, pos_ref):
    idx = idx_ref[...]
    eidx = lax.broadcasted_iota(I32, (N_EXPERTS, idx.shape[1]), 0)
    rows = [jnp.sum(jnp.where(eidx == idx[k:k + 1, :], off_ref[...], 0),
                    axis=0, keepdims=True) for k in range(TOP_K)]
    pos_ref[...] = jnp.concatenate(rows, axis=0) + rank_ref[...]


def _pos(idx_t, rank_t, local_off):
    n = idx_t.shape[1]
    blk = pl.BlockSpec((TOP_K, T_ROUTE), lambda i: (0, i))
    return pl.pallas_call(
        _pos_kernel,
        grid=(n // T_ROUTE,),
        in_specs=[blk, blk, pl.BlockSpec((None, N_EXPERTS, 1), lambda i: (i, 0, 0))],
        out_specs=blk,
        out_shape=jax.ShapeDtypeStruct((TOP_K, n), I32),
        compiler_params=_params(("parallel",)),
        name="pos",
    )(idx_t, rank_t, local_off)


def _row(ref, r):
    return ref.at[pl.ds(pl.multiple_of(r * ROW_TILE, ROW_TILE), ROW_TILE), :]


def _run_copies(cnt, src_ref, src_row, dst_ref, dst_row, sem):
    for bit in reversed(range(T_ROUTE.bit_length())):
        size = 1 << bit
        piece = cnt & size

        @pl.when(piece != 0)
        def _(size=size, src_row=src_row, dst_row=dst_row):
            rows = size * ROW_TILE
            pltpu.make_async_copy(
                src_ref.at[pl.ds(pl.multiple_of(src_row * ROW_TILE, ROW_TILE), rows), :],
                dst_ref.at[pl.ds(pl.multiple_of(dst_row * ROW_TILE, ROW_TILE), rows), :],
                sem).start()

        src_row = src_row + piece
        dst_row = dst_row + piece


def _dispatch_kernel(pstart_ref, pend_ref, cnt_ref, run_ref, slot_ref, u_ref, xs_ref,
                     grouped, zero_ref, sem, zsem):
    t = T_ROUTE
    i = pl.program_id(0)
    cur = i % 2
    blk = EXPERT_BLOCK * ROW_TILE
    buf = grouped.at[cur]

    def drain(slot):
        pltpu.make_async_copy(grouped.at[slot], grouped.at[slot], sem.at[slot]).wait()

    @pl.when(i == 0)
    def _():
        zero_ref[...] = jnp.zeros_like(zero_ref)

        def z(e, carry):
            @pl.when(pend_ref[e] > pstart_ref[e])
            def _():
                start = (pend_ref[e] - EXPERT_BLOCK) * ROW_TILE
                cp = pltpu.make_async_copy(
                    zero_ref, xs_ref.at[pl.ds(pl.multiple_of(start, blk), blk), :], zsem)
                cp.start()
                cp.wait()
            return carry

        lax.fori_loop(0, N_EXPERTS, z, 0)

    @pl.when(i >= 2)
    def _():
        drain(cur)

    def group(tok, carry):
        tile = u_ref[pl.ds(pl.multiple_of(tok * ROW_TILE, ROW_TILE), ROW_TILE), :]
        for k in range(TOP_K):
            buf[pl.ds(pl.multiple_of(slot_ref[k, tok] * ROW_TILE, ROW_TILE), ROW_TILE), :] = tile
        return carry

    lax.fori_loop(0, t, group, 0, unroll=DMA_UNROLL)

    def runs(e, off):
        cnt = cnt_ref[i * N_EXPERTS + e]
        _run_copies(cnt, buf, off, xs_ref, run_ref[i * N_EXPERTS + e], sem.at[cur])
        return off + cnt

    lax.fori_loop(0, N_EXPERTS, runs, 0)

    @pl.when(i == pl.num_programs(0) - 1)
    def _():
        drain(cur)

        @pl.when(i >= 1)
        def _():
            drain(1 - cur)


def _dispatch(pstart, pend, block_cnt, run_start, slot_t, u, n_rows):
    n = u.shape[0] // ROW_TILE
    t = T_ROUTE
    grid_spec = pltpu.PrefetchScalarGridSpec(
        num_scalar_prefetch=4,
        grid=(n // t,),
        in_specs=[
            pl.BlockSpec((TOP_K, t), lambda i, *_: (0, i), memory_space=pltpu.SMEM),
            pl.BlockSpec((t * ROW_TILE, LANES), lambda i, *_: (i, 0)),
        ],
        out_specs=pl.BlockSpec(memory_space=pl.ANY),
        scratch_shapes=[
            pltpu.VMEM((2, TOP_K * t * ROW_TILE, LANES), F32),
            pltpu.VMEM((EXPERT_BLOCK * ROW_TILE, LANES), F32),
            pltpu.SemaphoreType.DMA((2,)),
            pltpu.SemaphoreType.DMA(()),
        ],
    )
    return pl.pallas_call(
        _dispatch_kernel,
        grid_spec=grid_spec,
        out_shape=jax.ShapeDtypeStruct((n_rows * ROW_TILE, LANES), F32),
        compiler_params=_params(("arbitrary",)),
        name="dispatch",
    )(pstart, pend, block_cnt, run_start, slot_t, u)


def _expert_kernel(be_ref, nb_ref, x_ref, wgu_ref, bgu_ref, wd_ref, bd_ref, y_ref,
                   wgu_s, wd32_s, wd_s):
    i = pl.program_id(0)
    nb = nb_ref[0]
    blk = EXPERT_BLOCK
    half = EXPERT_FF // 2
    first = jnp.logical_or(i == 0, be_ref[i] != be_ref[jnp.maximum(i - 1, 0)])

    @pl.when(jnp.logical_and(first, i < nb))
    def _():
        rows = 256
        for r in range(0, D_MODEL, rows):
            wgu_s[r:r + rows, :] = wgu_ref[r:r + rows, :].astype(BF16)
        for cs in range(D_MODEL // LANES):
            lanes = slice(cs * LANES, (cs + 1) * LANES)
            wd32_s[cs, pl.ds(0, half, stride=2), :] = wd_ref[0:half, lanes]
            wd32_s[cs, pl.ds(1, half, stride=2), :] = wd_ref[half:EXPERT_FF, lanes]
            wd_s[:, lanes] = wd32_s[cs].astype(BF16)

    @pl.when(i < nb)
    def _():
        xb = _load_row_tiles(x_ref, blk).astype(BF16)
        ch = 1024
        even =(lax.broadcasted_iota(I32, (blk, LANES), 1) % 2) == 0
        y = jnp.zeros((blk, D_MODEL), F32) + bd_ref[...]
        for c in range(EXPERT_FF // ch):
            c1 = slice(c * ch, (c + 1) * ch)
            c2 = slice(EXPERT_FF + c * ch, EXPERT_FF + (c + 1) * ch)
            gu1 = jnp.dot(xb, wgu_s[:, c1], preferred_element_type=F32) + bgu_ref[:, c1]
            gu2 = jnp.dot(xb, wgu_s[:, c2], preferred_element_type=F32) + bgu_ref[:, c2]
            acts = []
            for v in range(ch // LANES):
                a = gu1[:, v * LANES:(v + 1) * LANES]
                b = gu2[:, v * LANES:(v + 1) * LANES]
                g = jnp.where(even, a, pltpu.roll(b, 1, axis=1))
                l = jnp.where(even, pltpu.roll(a, LANES - 1, axis=1), b)
                g = jnp.minimum(g, SWIGLU_LIMIT)
                l = jnp.clip(l, -SWIGLU_LIMIT, SWIGLU_LIMIT)
                acts.append(((l + 1.0) * (g * jax.nn.sigmoid(g * SWIGLU_ALPHA))).astype(BF16))
            act = jnp.concatenate(acts, axis=1)
            y = y + jnp.dot(act, wd_s[c1, :], preferred_element_type=F32)
        _store_row_tiles(y_ref, y)

    @pl.when(i >= nb)
    def _():
        y_ref[...] = jnp.zeros_like(y_ref)


def _experts(block_expert, n_used, xs, wgu, bgu, wd, bd):
    n_rows = xs.shape[0] // ROW_TILE
    blk = EXPERT_BLOCK
    wspec = lambda r, c: pl.BlockSpec((None, r, c), lambda i, be, nb: (be[i], 0, 0))
    grid_spec = pltpu.PrefetchScalarGridSpec(
        num_scalar_prefetch=2,
        grid=(n_rows // blk,),
        in_specs=[
            pl.BlockSpec((blk * ROW_TILE, LANES), lambda i, be, nb: (jnp.minimum(i, nb[0] - 1), 0)),
            wspec(D_MODEL, 2 * EXPERT_FF), wspec(1, 2 * EXPERT_FF),
            wspec(EXPERT_FF, D_MODEL), wspec(1, D_MODEL),
        ],
        out_specs=pl.BlockSpec((blk * ROW_TILE, LANES), lambda i, be, nb: (i, 0)),
        scratch_shapes=[
            pltpu.VMEM((D_MODEL, 2 * EXPERT_FF), BF16),
            pltpu.VMEM((D_MODEL // LANES, EXPERT_FF, LANES), F32),
            pltpu.VMEM((EXPERT_FF, D_MODEL), BF16),
        ],
    )
    return pl.pallas_call(
        _expert_kernel,
        grid_spec=grid_spec,
        out_shape=jax.ShapeDtypeStruct((n_rows * ROW_TILE, LANES), F32),
        compiler_params=_params(("arbitrary",)),
        name="expert",
    )(block_expert, n_used, xs, wgu, bgu, wd, bd)


def _combine_kernel(cnt_ref, run_ref, slot_ref, gsm_ref, h_ref, p_ref, y_ref, wple_ref, gple_ref,
                    wpg_ref, gfin_ref, o_ref, staged, moe_s, sem):
    t = T_ROUTE
    i = pl.program_id(0)
    cur = i % 2

    def fetch(block, slot):
        def runs(e, off):
            cnt = cnt_ref[block * N_EXPERTS + e]
            _run_copies(cnt, y_ref, run_ref[block * N_EXPERTS + e], staged.at[slot], off,
                        sem.at[slot])
            return off + cnt

        lax.fori_loop(0, N_EXPERTS, runs, 0)

    @pl.when(i == 0)
    def _():
        fetch(0, 0)

    @pl.when(i + 1 < pl.num_programs(0))
    def _():
        fetch(i + 1, 1 - cur)

    emb = jnp.dot(p_ref[...].astype(BF16), wple_ref[...], preferred_element_type=F32)

    pltpu.make_async_copy(staged.at[cur], staged.at[cur], sem.at[cur]).wait()
    buf = staged.at[cur]

    def pick(tok, carry):
        acc = None
        for k in range(TOP_K):
            row = buf[pl.ds(pl.multiple_of(slot_ref[k, tok] * ROW_TILE, ROW_TILE), ROW_TILE), :]
            term = gsm_ref[k, tok] * row
            acc = term if acc is None else acc + term
        moe_s[pl.ds(pl.multiple_of(tok * ROW_TILE, ROW_TILE), ROW_TILE), :] = acc
        return carry

    lax.fori_loop(0, t, pick, 0, unroll=DMA_UNROLL)

    h = h_ref[...] + _load_row_tiles(moe_s, t)
    r = _rms(h, gple_ref[...]).astype(BF16)
    sig = jax.nn.sigmoid(jnp.dot(r, wpg_ref[...], preferred_element_type=F32))
    h = h + emb * sig
    o_ref[...] = _rms(h, gfin_ref[...])


def _combine(block_cnt, run_start, slot_t, gates_t, h, p2, y, wple, gple, wpg, gfin):
    n = h.shape[0]
    t = T_ROUTE
    full = lambda a: pl.BlockSpec(a.shape, lambda i, *_: (0,) * a.ndim)
    smem = lambda: pl.BlockSpec((TOP_K, t), lambda i, *_: (0, i), memory_space=pltpu.SMEM)
    grid_spec = pltpu.PrefetchScalarGridSpec(
        num_scalar_prefetch=2,
        grid=(n // t,),
        in_specs=[
            smem(), smem(),
            pl.BlockSpec((t, D_MODEL), lambda i, *_: (i, 0)),
            pl.BlockSpec((t, PLE_DIM), lambda i, *_: (i, 0)),
            pl.BlockSpec(memory_space=pl.ANY),
            full(wple), full(gple), full(wpg), full(gfin),
        ],
        out_specs=pl.BlockSpec((t, D_MODEL), lambda i, *_: (i, 0)),
        scratch_shapes=[
            pltpu.VMEM((2, TOP_K * t * ROW_TILE, LANES), F32),
            pltpu.VMEM((t * ROW_TILE, LANES), F32),
            pltpu.SemaphoreType.DMA((2,)),
        ],
    )
    return pl.pallas_call(
        _combine_kernel,
        grid_spec=grid_spec,
        out_shape=jax.ShapeDtypeStruct((n, D_MODEL), F32),
        compiler_params=_params(("arbitrary",)),
        name="combine",
    )(block_cnt, run_start, slot_t, gates_t, h, p2, y, wple, gple, wpg, gfin)


def kernel(x, p, g_mix, w_in, lambda_q1, lambda_k1, lambda_q2, lambda_k2, g_subln, w_attn_out,
           w_dw, b_dw, g_conv_ln, b_conv_ln, w_conv_out, w_o, g_ffn, w_router, b_router,
           w_gate_up, b_gate_up, w_down, b_down, w_ple, g_ple, w_ple_gate, g_final):
    b, s, d = x.shape
    n = b * s
    xt = x.reshape(n, d)
    vec = lambda a: a.reshape(1, -1)

    w_in0 = w_in[0]
    w_main = jnp.concatenate([w_in0[:, :2 * ATTN_WIDTH], w_in0[:, 3 * ATTN_WIDTH:]], axis=1)
    w_vt = w_in0[:, 2 * ATTN_WIDTH:3 * ATTN_WIDTH].T
    proj, vt = _inproj(xt, vec(g_mix[0]), w_main.astype(BF16), w_vt.astype(BF16))
    proj3 = proj.reshape(b, s, MAIN_COLS)
    attn = _attention(proj3, vt, vec(lambda_q1[0]), vec(lambda_k1[0]), vec(lambda_q2[0]),
                      vec(lambda_k2[0]), g_subln[0].reshape(HEAD_V, 1))
    conv = _conv(proj3, w_dw[0], vec(b_dw[0]), vec(g_conv_ln[0]), vec(b_conv_ln[0]))
    h1, u2, idx_t, gates_t = _mix(
        xt, attn.reshape(n, ATTN_WIDTH), conv.reshape(n, CONV_WIDTH), proj,
        w_attn_out[0].astype(BF16), w_conv_out[0].astype(BF16), w_o[0].astype(BF16),
        vec(g_ffn[0]), w_router[0].T, b_router[0].reshape(N_EXPERTS, 1))

    rank_t, block_cnt = _rank(idx_t)
    block_cnt = block_cnt[:, :, 0]
    counts = jnp.sum(block_cnt, axis=0)
    blk = EXPERT_BLOCK
    n_blocks = n * TOP_K // blk + N_EXPERTS
    padded = (counts + blk - 1) // blk * blk
    pend = jnp.cumsum(padded).astype(I32)
    pstart = pend - padded
    n_used = (pend[-1:] // blk).astype(I32)
    block_start = jnp.arange(n_blocks, dtype=I32) * blk
    block_expert = jnp.minimum(
        jnp.sum((pend[None, :] <= block_start[:, None]).astype(I32), axis=1), N_EXPERTS - 1)
    run_start = (pstart[None, :] + jnp.cumsum(block_cnt, axis=0) - block_cnt).astype(I32)
    local_off = (jnp.cumsum(block_cnt, axis=1) - block_cnt).astype(I32)
    slot_t = _pos(idx_t, rank_t, local_off[:, :, None])
    cnt_flat = block_cnt.reshape(-1).astype(I32)
    run_flat = run_start.reshape(-1)

    xs = _dispatch(pstart, pend, cnt_flat, run_flat, slot_t, u2, n_blocks * blk)
    y = _experts(block_expert, n_used, xs, w_gate_up[0], b_gate_up[0][:, None, :],
                 w_down[0], b_down[0][:, None, :])

    out = _combine(cnt_flat, run_flat, slot_t, gates_t, h1, p[0].reshape(n, PLE_DIM), y,
                   w_ple[0].astype(BF16), vec(g_ple[0]), w_ple_gate[0].astype(BF16), vec(g_final))
    return out.reshape(b, s, d)
```

```python
import jax
import jax.numpy as jnp
from jax import lax
from jax.experimental import pallas as pl
from jax.experimental.pallas import tpu as pltpu

F32 = jnp.float32
BF16 = jnp.bfloat16
I32 = jnp.int32

D_MODEL = 1024
ATTN_WIDTH = 512
DIFF_HEADS = 4
HEAD_DIM = 64
HEAD_V = 2 * HEAD_DIM
CONV_WIDTH = 512
CONV_KERNEL = 31
N_EXPERTS = 32
TOP_K = 4
EXPERT_FF = 1024
PLE_DIM = 256
SWIGLU_ALPHA = 1.702
SWIGLU_LIMIT = 7.0
EPS = 1e-5
IN_COLS = 3 * ATTN_WIDTH + 2 * CONV_WIDTH + 2 * D_MODEL
LAMBDA_INIT = 0.2
LOG2_E = 1.4426950408889634

MAIN_COLS = IN_COLS - ATTN_WIDTH
COL_CONV_A = 2
COL_CONV_B = 3
COL_GATE_A = 4
COL_GATE_B = 6

LANES = 128
TM_PROJ = 512
TQ = 256
TM_MIX = 512
T_ROUTE = 512
EXPERT_BLOCK = 512
CONV_ROWS = 64
CONV_PAD = 32
DMA_UNROLL = 8
VMEM_LIMIT = 56 * 1024 * 1024

_NT = (((1,), (1,)), ((), ()))


def _params(sem, vmem=VMEM_LIMIT):
    return pltpu.CompilerParams(dimension_semantics=sem, vmem_limit_bytes=vmem)


def _rms(x, g):
    return x * lax.rsqrt(jnp.mean(x * x, axis=-1, keepdims=True) + EPS) * g


ROW_TILE = D_MODEL // LANES


def _store_row_tiles(ref, x, lead=()):
    rows = x.shape[0]
    for g in range(ROW_TILE):
        ref[lead + (pl.ds(g, rows, stride=ROW_TILE), slice(None))] = x[:, g * LANES:(g + 1) * LANES]


def _load_row_tiles(ref, rows, lead=()):
    return jnp.concatenate(
        [ref[lead + (pl.ds(g, rows, stride=ROW_TILE), slice(None))] for g in range(ROW_TILE)], axis=1)


def _inproj_kernel(x_ref, g_ref, w_ref, wvt_ref, o_ref, vt_ref):
    u = _rms(x_ref[...], g_ref[...]).astype(BF16)
    ch = 512
    for c in range(MAIN_COLS // ch):
        r = jnp.dot(u, w_ref[:, c * ch:(c + 1) * ch], preferred_element_type=F32)
        if c == 0:
            r = r * (HEAD_DIM ** -0.5 * LOG2_E)
        o_ref[:, c * ch:(c + 1) * ch] = r.astype(BF16)
    vt_ref[...] = lax.dot_general(wvt_ref[...], u, _NT,
                                  preferred_element_type=F32).astype(BF16)


def _inproj(xt, g, w, wvt):
    n = xt.shape[0]
    return pl.pallas_call(
        _inproj_kernel,
        grid=(n // TM_PROJ,),
        in_specs=[
            pl.BlockSpec((TM_PROJ, D_MODEL), lambda i: (i, 0)),
            pl.BlockSpec((1, D_MODEL), lambda i: (0, 0)),
            pl.BlockSpec((D_MODEL, MAIN_COLS), lambda i: (0, 0)),
            pl.BlockSpec((ATTN_WIDTH, D_MODEL), lambda i: (0, 0)),
        ],
        out_specs=[
            pl.BlockSpec((TM_PROJ, MAIN_COLS), lambda i: (i, 0)),
            pl.BlockSpec((ATTN_WIDTH, TM_PROJ), lambda i: (0, i)),
        ],
        out_shape=[
            jax.ShapeDtypeStruct((n, MAIN_COLS), BF16),
            jax.ShapeDtypeStruct((ATTN_WIDTH, n), BF16),
        ],
        compiler_params=_params(("parallel",)),
        name="inproj",
    )(xt, g, w, wvt)


def _attn_kernel(q_ref, k_ref, vt_ref, lq1_ref, lk1_ref, lq2_ref, lk2_ref, g_ref, o_ref):
    i = pl.program_id(1)
    s_len = k_ref.shape[0]
    lane = lax.broadcasted_iota(I32, (TQ, HEAD_V), 1)
    key = lax.broadcasted_iota(I32, (TQ, 2 * TQ), 0)
    qry = lax.broadcasted_iota(I32, (TQ, 2 * TQ), 1)
    causal = key <= jnp.where(qry >= TQ, qry - TQ, qry)
    lam = (jnp.exp(jnp.sum(lq1_ref[...] * lk1_ref[...]))
           - jnp.exp(jnp.sum(lq2_ref[...] * lk2_ref[...])) + LAMBDA_INIT)

    def block(c):
        heads = range(DIFF_HEADS)
        cols = [slice(h * HEAD_V, (h + 1) * HEAD_V) for h in heads]
        n = c * TQ

        def stacked_q(h):
            q = q_ref[:, cols[h]]
            zero = jnp.zeros_like(q)
            return jnp.concatenate([jnp.where(lane < HEAD_DIM, q, zero),
                                    jnp.where(lane >= HEAD_DIM, q, zero)], axis=0)

        s = [lax.dot_general(k_ref[0:n + TQ, cols[h]], stacked_q(h), _NT,
                             preferred_element_type=F32) for h in heads]
        s_d = [jnp.where(causal, s[h][n:], -1e30) for h in heads]
        m = [jnp.max(s_d[h], axis=0, keepdims=True) for h in heads]
        if c:
            m = [jnp.maximum(m[h], jnp.max(s[h][:n], axis=0, keepdims=True)) for h in heads]
        p = [jnp.exp2(s_d[h] - m[h]) for h in heads]
        if c:
            p = [jnp.concatenate([jnp.exp2(s[h][:n] - m[h]), p[h]], axis=0) for h in heads]
        l = [jnp.sum(p[h], axis=0, keepdims=True) for h in heads]
        acc = [jnp.dot(vt_ref[cols[h], 0:n + TQ], p[h].astype(BF16), preferred_element_type=F32)
               for h in heads]
        for h in heads:
            o12 = acc[h] / l[h]
            o = o12[:, :TQ] - lam * o12[:, TQ:]
            o = o * lax.rsqrt(jnp.mean(o * o, axis=0, keepdims=True) + EPS) * g_ref[...]
            o_ref[:, cols[h]] = (o * (1.0 - LAMBDA_INIT)).T.astype(o_ref.dtype)

    for c in range(s_len // TQ):
        pl.when(i == c)(lambda c=c: block(c))


def _attention(proj3, vt, lq1, lk1, lq2, lk2, g_subln):
    b, s, _ = proj3.shape
    vec = pl.BlockSpec((1, HEAD_DIM), lambda bi, i: (0, 0))
    return pl.pallas_call(
        _attn_kernel,
        grid=(b, s // TQ),
        in_specs=[
            pl.BlockSpec((None, TQ, ATTN_WIDTH), lambda bi, i: (bi, i, 0)),
            pl.BlockSpec((None, s, ATTN_WIDTH), lambda bi, i: (bi, 0, 1)),
            pl.BlockSpec((ATTN_WIDTH, s), lambda bi, i: (0, bi)),
            vec, vec, vec, vec,
            pl.BlockSpec((HEAD_V, 1), lambda bi, i: (0, 0)),
        ],
        out_specs=pl.BlockSpec((None, TQ, ATTN_WIDTH), lambda bi, i: (bi, i, 0)),
        out_shape=jax.ShapeDtypeStruct((b, s, ATTN_WIDTH), BF16),
        compiler_params=_params(("parallel", "parallel")),
        name="attn",
    )(proj3, proj3, vt, lq1, lk1, lq2, lk2, g_subln)


def _conv_kernel(ca_ref, cb_ref, w_ref, b_ref, g_ref, beta_ref, o_ref, z_ref, stage_ref):
    s = ca_ref.shape[0]
    glu_rows = 256
    n_lg = CONV_WIDTH // LANES
    z_ref[:, 0:CONV_PAD, :] = jnp.zeros((n_lg, CONV_PAD, LANES), F32)

    def glu(c, carry):
        r0 = pl.multiple_of(c * glu_rows, glu_rows)
        a = ca_ref[pl.ds(r0, glu_rows), :].astype(F32)
        g = cb_ref[pl.ds(r0, glu_rows), :].astype(F32)
        z = a * jax.nn.sigmoid(g)
        for lg in range(n_lg):
            z_ref[lg, pl.ds(CONV_PAD + r0, glu_rows), :] = z[:, lg * LANES:(lg + 1) * LANES]
        return carry

    lax.fori_loop(0, s // glu_rows, glu, 0)

    def conv(c, carry):
        r0 = pl.multiple_of(c * CONV_ROWS, CONV_ROWS)
        groups = CONV_ROWS // 8
        for lg in range(n_lg):
            lanes = slice(lg * LANES, (lg + 1) * LANES)
            accs = [jnp.zeros((groups, LANES), F32) + b_ref[:, lanes] for _ in range(8)]
            for j in range(CONV_KERNEL):
                off = CONV_PAD - (CONV_KERNEL - 1) + j
                wj = w_ref[j:j + 1, lanes]
                for g in range(8):
                    accs[g] = accs[g] + wj * z_ref[lg, pl.ds(r0 + off + g, groups, stride=8), :]
            for g in range(8):
                stage_ref[lg, pl.ds(g, groups, stride=8), :] = accs[g]
        acc = jnp.concatenate([stage_ref[lg] for lg in range(n_lg)], axis=1)
        mu = jnp.mean(acc, axis=-1, keepdims=True)
        xc = acc - mu
        y = xc * lax.rsqrt(jnp.mean(xc * xc, axis=-1, keepdims=True) + EPS)
        y = y * g_ref[...] + beta_ref[...]
        o_ref[pl.ds(r0, CONV_ROWS), :] = (y * jax.nn.sigmoid(y)).astype(o_ref.dtype)
        return carry

    lax.fori_loop(0, s // CONV_ROWS, conv, 0, unroll=2)


def _conv(proj3, w_dw, b_dw, g_ln, b_ln):
    b, s, _ = proj3.shape
    vec = pl.BlockSpec((1, CONV_WIDTH), lambda bi: (0, 0))
    return pl.pallas_call(
        _conv_kernel,
        grid=(b,),
        in_specs=[
            pl.BlockSpec((None, s, CONV_WIDTH), lambda bi: (bi, 0, COL_CONV_A)),
            pl.BlockSpec((None, s, CONV_WIDTH), lambda bi: (bi, 0, COL_CONV_B)),
            pl.BlockSpec((CONV_KERNEL, CONV_WIDTH), lambda bi: (0, 0)),
            vec, vec, vec,
        ],
        out_specs=pl.BlockSpec((None, s, CONV_WIDTH), lambda bi: (bi, 0, 0)),
        out_shape=jax.ShapeDtypeStruct((b, s, CONV_WIDTH), BF16),
        scratch_shapes=[
            pltpu.VMEM((CONV_WIDTH // LANES, s + CONV_PAD, LANES), F32),
            pltpu.VMEM((CONV_WIDTH // LANES, CONV_ROWS, LANES), F32),
        ],
        compiler_params=_params(("parallel",)),
        name="conv",
    )(proj3, proj3, w_dw, b_dw, g_ln, b_ln)


def _split_bf16(a):
    hi = a.astype(BF16)
    lo = (a - hi.astype(F32)).astype(BF16)
    return hi, lo


def _mix_kernel(x_ref, o_ref, c_ref, ga_ref, gb_ref,
                wa_ref, wc_ref, wo_ref, gffn_ref, wr_ref, br_ref,
                h_ref, u_ref, idx_ref, gate_ref):
    a = jnp.dot(o_ref[...], wa_ref[...], preferred_element_type=F32)
    b = jnp.dot(c_ref[...], wc_ref[...], preferred_element_type=F32)
    m = (jax.nn.sigmoid(ga_ref[...].astype(F32)) * a
         + jax.nn.sigmoid(gb_ref[...].astype(F32)) * b)
    acc = x_ref[...] + jnp.dot(m.astype(BF16), wo_ref[...], preferred_element_type=F32)
    h_ref[...] = acc
    u = _rms(acc, gffn_ref[...])
    _store_row_tiles(u_ref, u)

    u_hi, u_lo = _split_bf16(u)
    w_hi, w_lo = _split_bf16(wr_ref[...])
    logits = (lax.dot_general(w_hi, u_hi, _NT, preferred_element_type=F32)
              + lax.dot_general(w_hi, u_lo, _NT, preferred_element_type=F32)
              + lax.dot_general(w_lo, u_hi, _NT, preferred_element_type=F32)
              + br_ref[...])
    eidx = lax.broadcasted_iota(I32, logits.shape, 0)
    vals, idxs = [], []
    for _ in range(TOP_K):
        mx = jnp.max(logits, axis=0, keepdims=True)
        sel = jnp.min(jnp.where(logits == mx, eidx, N_EXPERTS), axis=0, keepdims=True)
        vals.append(mx)
        idxs.append(sel)
        logits = jnp.where(eidx == sel, -jnp.inf, logits)
    ex = [jnp.exp(v - vals[0]) for v in vals]
    den = ex[0] + ex[1] + ex[2] + ex[3]
    idx_ref[...] = jnp.concatenate(idxs, axis=0)
    gate_ref[...] = jnp.concatenate([e / den for e in ex], axis=0)


def _mix(xt, o, c, proj, wa, wc, wo, g_ffn, wr_t, br):
    n = xt.shape[0]
    tm = TM_MIX
    row = lambda w, j: pl.BlockSpec((tm, w), lambda i, j=j: (i, j))
    full = lambda a: pl.BlockSpec(a.shape, lambda i: (0,) * a.ndim)
    return pl.pallas_call(
        _mix_kernel,
        grid=(n // tm,),
        in_specs=[
            row(D_MODEL, 0), row(ATTN_WIDTH, 0), row(CONV_WIDTH, 0),
            row(D_MODEL, COL_GATE_A // 2), row(D_MODEL, COL_GATE_B // 2),
            full(wa), full(wc), full(wo), full(g_ffn), full(wr_t), full(br),
        ],
        out_specs=[
            pl.BlockSpec((tm, D_MODEL), lambda i: (i, 0)),
            pl.BlockSpec((tm * ROW_TILE, LANES), lambda i: (i, 0)),
            pl.BlockSpec((TOP_K, tm), lambda i: (0, i)),
            pl.BlockSpec((TOP_K, tm), lambda i: (0, i)),
        ],
        out_shape=[
            jax.ShapeDtypeStruct((n, D_MODEL), F32),
            jax.ShapeDtypeStruct((n * ROW_TILE, LANES), F32),
            jax.ShapeDtypeStruct((TOP_K, n), I32),
            jax.ShapeDtypeStruct((TOP_K, n), F32),
        ],
        compiler_params=_params(("parallel",)),
        name="mix",
    )(xt, o, c, proj, proj, wa, wc, wo, g_ffn, wr_t, br)


def _rank_kernel(idx_ref, rank_ref, cnt_ref):
    t = T_ROUTE
    idx = idx_ref[...]
    eidx = lax.broadcasted_iota(I32, (N_EXPERTS, t), 0)
    onehot = [eidx == idx[k:k + 1, :] for k in range(TOP_K)]
    member = onehot[0] | onehot[1] | onehot[2] | onehot[3]
    r = lax.broadcasted_iota(I32, (t, t), 0)
    c = lax.broadcasted_iota(I32, (t, t), 1)
    tri = jnp.where(r < c, 1.0, 0.0).astype(BF16)
    mem_f = jnp.where(member, 1.0, 0.0)
    prefix = jnp.dot(mem_f.astype(BF16), tri, preferred_element_type=F32)
    ranks = [jnp.sum(jnp.where(onehot[k], prefix, 0.0), axis=0, keepdims=True)
             for k in range(TOP_K)]
    rank_ref[...] = jnp.concatenate(ranks, axis=0).astype(I32)
    cnt_ref[...] = jnp.sum(mem_f, axis=1, keepdims=True).astype(I32)


def _rank(idx_t):
    n = idx_t.shape[1]
    return pl.pallas_call(
        _rank_kernel,
        grid=(n // T_ROUTE,),
        in_specs=[pl.BlockSpec((TOP_K, T_ROUTE), lambda i: (0, i))],
        out_specs=[
            pl.BlockSpec((TOP_K, T_ROUTE), lambda i: (0, i)),
            pl.BlockSpec((None, N_EXPERTS, 1), lambda i: (i, 0, 0)),
        ],
        out_shape=[
            jax.ShapeDtypeStruct((TOP_K, n), I32),
            jax.ShapeDtypeStruct((n // T_ROUTE, N_EXPERTS, 1), I32),
        ],
        compiler_params=_params(("parallel",)),
        name="rank",
    )(idx_t)


def _pos_kernel(idx_ref, rank_ref, off_ref, pos_ref):
    idx = idx_ref[...]
    eidx = lax.broadcasted_iota(I32, (N_EXPERTS, idx.shape[1]), 0)
    rows = [jnp.sum(jnp.where(eidx == idx[k:k + 1, :], off_ref[...], 0),
                    axis=0, keepdims=True) for k in range(TOP_K)]
    pos_ref[...] = jnp.concatenate(rows, axis=0) + rank_ref[...]


def _pos(idx_t, rank_t, local_off):
    n = idx_t.shape[1]
    blk = pl.BlockSpec((TOP_K, T_ROUTE), lambda i: (0, i))
    return pl.pallas_call(
        _pos_kernel,
        grid=(n // T_ROUTE,),
        in_specs=[blk, blk, pl.BlockSpec((None, N_EXPERTS, 1), lambda i: (i, 0, 0))],
        out_specs=blk,
        out_shape=jax.ShapeDtypeStruct((TOP_K, n), I32),
        compiler_params=_params(("parallel",)),
        name="pos",
    )(idx_t, rank_t, local_off)


def _row(ref, r):
    return ref.at[pl.ds(pl.multiple_of(r * ROW_TILE, ROW_TILE), ROW_TILE), :]


def _run_copies(cnt, src_ref, src_row, dst_ref, dst_row, sem):
    for bit in reversed(range(T_ROUTE.bit_length())):
        size = 1 << bit
        piece = cnt & size

        @pl.when(piece != 0)
        def _(size=size, src_row=src_row, dst_row=dst_row):
            rows = size * ROW_TILE
            pltpu.make_async_copy(
                src_ref.at[pl.ds(pl.multiple_of(src_row * ROW_TILE, ROW_TILE), rows), :],
                dst_ref.at[pl.ds(pl.multiple_of(dst_row * ROW_TILE, ROW_TILE), rows), :],
                sem).start()

        src_row = src_row + piece
        dst_row = dst_row + piece


def _dispatch_kernel(pstart_ref, pend_ref, cnt_ref, run_ref, slot_ref, u_ref, xs_ref,
                     grouped, zero_ref, sem, zsem):
    t = T_ROUTE
    i = pl.program_id(0)
    cur = i % 2
    blk = EXPERT_BLOCK * ROW_TILE
    buf = grouped.at[cur]

    def drain(slot):
        pltpu.make_async_copy(grouped.at[slot], grouped.at[slot], sem.at[slot]).wait()

    @pl.when(i == 0)
    def _():
        zero_ref[...] = jnp.zeros_like(zero_ref)

        def z(e, carry):
            @pl.when(pend_ref[e] > pstart_ref[e])
            def _():
                start = (pend_ref[e] - EXPERT_BLOCK) * ROW_TILE
                cp = pltpu.make_async_copy(
                    zero_ref, xs_ref.at[pl.ds(pl.multiple_of(start, blk), blk), :], zsem)
                cp.start()
                cp.wait()
            return carry

        lax.fori_loop(0, N_EXPERTS, z, 0)

    @pl.when(i >= 2)
    def _():
        drain(cur)

    def group(tok, carry):
        tile = u_ref[pl.ds(pl.multiple_of(tok * ROW_TILE, ROW_TILE), ROW_TILE), :]
        for k in range(TOP_K):
            buf[pl.ds(pl.multiple_of(slot_ref[k, tok] * ROW_TILE, ROW_TILE), ROW_TILE), :] = tile
        return carry

    lax.fori_loop(0, t, group, 0, unroll=DMA_UNROLL)

    def runs(e, off):
        cnt = cnt_ref[i * N_EXPERTS + e]
        _run_copies(cnt, buf, off, xs_ref, run_ref[i * N_EXPERTS + e], sem.at[cur])
        return off + cnt

    lax.fori_loop(0, N_EXPERTS, runs, 0)

    @pl.when(i == pl.num_programs(0) - 1)
    def _():
        drain(cur)

        @pl.when(i >= 1)
        def _():
            drain(1 - cur)


def _dispatch(pstart, pend, block_cnt, run_start, slot_t, u, n_rows):
    n = u.shape[0] // ROW_TILE
    t = T_ROUTE
    grid_spec = pltpu.PrefetchScalarGridSpec(
        num_scalar_prefetch=4,
        grid=(n // t,),
        in_specs=[
            pl.BlockSpec((TOP_K, t), lambda i, *_: (0, i), memory_space=pltpu.SMEM),
            pl.BlockSpec((t * ROW_TILE, LANES), lambda i, *_: (i, 0)),
        ],
        out_specs=pl.BlockSpec(memory_space=pl.ANY),
        scratch_shapes=[
            pltpu.VMEM((2, TOP_K * t * ROW_TILE, LANES), F32),
            pltpu.VMEM((EXPERT_BLOCK * ROW_TILE, LANES), F32),
            pltpu.SemaphoreType.DMA((2,)),
            pltpu.SemaphoreType.DMA(()),
        ],
    )
    return pl.pallas_call(
        _dispatch_kernel,
        grid_spec=grid_spec,
        out_shape=jax.ShapeDtypeStruct((n_rows * ROW_TILE, LANES), F32),
        compiler_params=_params(("arbitrary",)),
        name="dispatch",
    )(pstart, pend, block_cnt, run_start, slot_t, u)


def _expert_kernel(be_ref, nb_ref, x_ref, wgu_ref, bgu_ref, wd_ref, bd_ref, y_ref,
                   wgu_s, wd32_s, wd_s):
    i = pl.program_id(0)
    nb = nb_ref[0]
    blk = EXPERT_BLOCK
    half = EXPERT_FF // 2
    first = jnp.logical_or(i == 0, be_ref[i] != be_ref[jnp.maximum(i - 1, 0)])

    @pl.when(jnp.logical_and(first, i < nb))
    def _():
        rows = 256
        for r in range(0, D_MODEL, rows):
            wgu_s[r:r + rows, :] = wgu_ref[r:r + rows, :].astype(BF16)
        for cs in range(D_MODEL // LANES):
            lanes = slice(cs * LANES, (cs + 1) * LANES)
            wd32_s[cs, pl.ds(0, half, stride=2), :] = wd_ref[0:half, lanes]
            wd32_s[cs, pl.ds(1, half, stride=2), :] = wd_ref[half:EXPERT_FF, lanes]
            wd_s[:, lanes] = wd32_s[cs].astype(BF16)

    @pl.when(i < nb)
    def _():
        xb = _load_row_tiles(x_ref, blk).astype(BF16)
        ch = 1024
        even =(lax.broadcasted_iota(I32, (blk, LANES), 1) % 2) == 0
        y = jnp.zeros((blk, D_MODEL), F32) + bd_ref[...]
        for c in range(EXPERT_FF // ch):
            c1 = slice(c * ch, (c + 1) * ch)
            c2 = slice(EXPERT_FF + c * ch, EXPERT_FF + (c + 1) * ch)
            gu1 = jnp.dot(xb, wgu_s[:, c1], preferred_element_type=F32) + bgu_ref[:, c1]
            gu2 = jnp.dot(xb, wgu_s[:, c2], preferred_element_type=F32) + bgu_ref[:, c2]
            acts = []
            for v in range(ch // LANES):
                a = gu1[:, v * LANES:(v + 1) * LANES]
                b = gu2[:, v * LANES:(v + 1) * LANES]
                g = jnp.where(even, a, pltpu.roll(b, 1, axis=1))
                l = jnp.where(even, pltpu.roll(a, LANES - 1, axis=1), b)
                g = jnp.minimum(g, SWIGLU_LIMIT)
                l = jnp.clip(l, -SWIGLU_LIMIT, SWIGLU_LIMIT)
                acts.append(((l + 1.0) * (g * jax.nn.sigmoid(g * SWIGLU_ALPHA))).astype(BF16))
            act = jnp.concatenate(acts, axis=1)
            y = y + jnp.dot(act, wd_s[c1, :], preferred_element_type=F32)
        _store_row_tiles(y_ref, y)

    @pl.when(i >= nb)
    def _():
        y_ref[...] = jnp.zeros_like(y_ref)


def _experts(block_expert, n_used, xs, wgu, bgu, wd, bd):
    n_rows = xs.shape[0] // ROW_TILE
    blk = EXPERT_BLOCK
    wspec = lambda r, c: pl.BlockSpec((None, r, c), lambda i, be, nb: (be[i], 0, 0))
    grid_spec = pltpu.PrefetchScalarGridSpec(
        num_scalar_prefetch=2,
        grid=(n_rows // blk,),
        in_specs=[
            pl.BlockSpec((blk * ROW_TILE, LANES), lambda i, be, nb: (jnp.minimum(i, nb[0] - 1), 0)),
            wspec(D_MODEL, 2 * EXPERT_FF), wspec(1, 2 * EXPERT_FF),
            wspec(EXPERT_FF, D_MODEL), wspec(1, D_MODEL),
        ],
        out_specs=pl.BlockSpec((blk * ROW_TILE, LANES), lambda i, be, nb: (i, 0)),
        scratch_shapes=[
            pltpu.VMEM((D_MODEL, 2 * EXPERT_FF), BF16),
            pltpu.VMEM((D_MODEL // LANES, EXPERT_FF, LANES), F32),
            pltpu.VMEM((EXPERT_FF, D_MODEL), BF16),
        ],
    )
    return pl.pallas_call(
        _expert_kernel,
        grid_spec=grid_spec,
        out_shape=jax.ShapeDtypeStruct((n_rows * ROW_TILE, LANES), F32),
        compiler_params=_params(("arbitrary",)),
        name="expert",
    )(block_expert, n_used, xs, wgu, bgu, wd, bd)


def _combine_kernel(cnt_ref, run_ref, slot_ref, gsm_ref, h_ref, p_ref, y_ref, wple_ref, gple_ref,
                    wpg_ref, gfin_ref, o_ref, staged, moe_s, sem):
    t = T_ROUTE
    i = pl.program_id(0)
    cur = i % 2

    def fetch(block, slot):
        def runs(e, off):
            cnt = cnt_ref[block * N_EXPERTS + e]
            _run_copies(cnt, y_ref, run_ref[block * N_EXPERTS + e], staged.at[slot], off,
                        sem.at[slot])
            return off + cnt

        lax.fori_loop(0, N_EXPERTS, runs, 0)

    @pl.when(i == 0)
    def _():
        fetch(0, 0)

    @pl.when(i + 1 < pl.num_programs(0))
    def _():
        fetch(i + 1, 1 - cur)

    emb = jnp.dot(p_ref[...].astype(BF16), wple_ref[...], preferred_element_type=F32)

    pltpu.make_async_copy(staged.at[cur], staged.at[cur], sem.at[cur]).wait()
    buf = staged.at[cur]

    def pick(tok, carry):
        acc = None
        for k in range(TOP_K):
            row = buf[pl.ds(pl.multiple_of(slot_ref[k, tok] * ROW_TILE, ROW_TILE), ROW_TILE), :]
            term = gsm_ref[k, tok] * row
            acc = term if acc is None else acc + term
        moe_s[pl.ds(pl.multiple_of(tok * ROW_TILE, ROW_TILE), ROW_TILE), :] = acc
        return carry

    lax.fori_loop(0, t, pick, 0, unroll=DMA_UNROLL)

    h = h_ref[...] + _load_row_tiles(moe_s, t)
    r = _rms(h, gple_ref[...]).astype(BF16)
    sig = jax.nn.sigmoid(jnp.dot(r, wpg_ref[...], preferred_element_type=F32))
    h = h + emb * sig
    o_ref[...] = _rms(h, gfin_ref[...])


def _combine(block_cnt, run_start, slot_t, gates_t, h, p2, y, wple, gple, wpg, gfin):
    n = h.shape[0]
    t = T_ROUTE
    full = lambda a: pl.BlockSpec(a.shape, lambda i, *_: (0,) * a.ndim)
    smem = lambda: pl.BlockSpec((TOP_K, t), lambda i, *_: (0, i), memory_space=pltpu.SMEM)
    grid_spec = pltpu.PrefetchScalarGridSpec(
        num_scalar_prefetch=2,
        grid=(n // t,),
        in_specs=[
            smem(), smem(),
            pl.BlockSpec((t, D_MODEL), lambda i, *_: (i, 0)),
            pl.BlockSpec((t, PLE_DIM), lambda i, *_: (i, 0)),
            pl.BlockSpec(memory_space=pl.ANY),
            full(wple), full(gple), full(wpg), full(gfin),
        ],
        out_specs=pl.BlockSpec((t, D_MODEL), lambda i, *_: (i, 0)),
        scratch_shapes=[
            pltpu.VMEM((2, TOP_K * t * ROW_TILE, LANES), F32),
            pltpu.VMEM((t * ROW_TILE, LANES), F32),
            pltpu.SemaphoreType.DMA((2,)),
        ],
    )
    return pl.pallas_call(
        _combine_kernel,
        grid_spec=grid_spec,
        out_shape=jax.ShapeDtypeStruct((n, D_MODEL), F32),
        compiler_params=_params(("arbitrary",)),
        name="combine",
    )(block_cnt, run_start, slot_t, gates_t, h, p2, y, wple, gple, wpg, gfin)


def kernel(x, p, g_mix, w_in, lambda_q1, lambda_k1, lambda_q2, lambda_k2, g_subln, w_attn_out,
           w_dw, b_dw, g_conv_ln, b_conv_ln, w_conv_out, w_o, g_ffn, w_router, b_router,
           w_gate_up, b_gate_up, w_down, b_down, w_ple, g_ple, w_ple_gate, g_final):
    b, s, d = x.shape
    n = b * s
    xt = x.reshape(n, d)
    vec = lambda a: a.reshape(1, -1)

    w_in0 = w_in[0]
    w_main = jnp.concatenate([w_in0[:, :2 * ATTN_WIDTH], w_in0[:, 3 * ATTN_WIDTH:]], axis=1)
    w_vt = w_in0[:, 2 * ATTN_WIDTH:3 * ATTN_WIDTH].T
    proj, vt = _inproj(xt, vec(g_mix[0]), w_main.astype(BF16), w_vt.astype(BF16))
    proj3 = proj.reshape(b, s, MAIN_COLS)
    attn = _attention(proj3, vt, vec(lambda_q1[0]), vec(lambda_k1[0]), vec(lambda_q2[0]),
                      vec(lambda_k2[0]), g_subln[0].reshape(HEAD_V, 1))
    conv = _conv(proj3, w_dw[0], vec(b_dw[0]), vec(g_conv_ln[0]), vec(b_conv_ln[0]))
    h1, u2, idx_t, gates_t = _mix(
        xt, attn.reshape(n, ATTN_WIDTH), conv.reshape(n, CONV_WIDTH), proj,
        w_attn_out[0].astype(BF16), w_conv_out[0].astype(BF16), w_o[0].astype(BF16),
        vec(g_ffn[0]), w_router[0].T, b_router[0].reshape(N_EXPERTS, 1))

    rank_t, block_cnt = _rank(idx_t)
    block_cnt = block_cnt[:, :, 0]
    counts = jnp.sum(block_cnt, axis=0)
    blk = EXPERT_BLOCK
    n_blocks = n * TOP_K // blk + N_EXPERTS
    padded = (counts + blk - 1) // blk * blk
    pend = jnp.cumsum(padded).astype(I32)
    pstart = pend - padded
    n_used = (pend[-1:] // blk).astype(I32)
    block_start = jnp.arange(n_blocks, dtype=I32) * blk
    block_expert = jnp.minimum(
        jnp.sum((pend[None, :] <= block_start[:, None]).astype(I32), axis=1), N_EXPERTS - 1)
    run_start = (pstart[None, :] + jnp.cumsum(block_cnt, axis=0) - block_cnt).astype(I32)
    local_off = (jnp.cumsum(block_cnt, axis=1) - block_cnt).astype(I32)
    slot_t = _pos(idx_t, rank_t, local_off[:, :, None])
    cnt_flat = block_cnt.reshape(-1).astype(I32)
    run_flat = run_start.reshape(-1)

    xs = _dispatch(pstart, pend, cnt_flat, run_flat, slot_t, u2, n_blocks * blk)
    y = _experts(block_expert, n_used, xs, w_gate_up[0], b_gate_up[0][:, None, :],
                 w_down[0], b_down[0][:, None, :])

    out = _combine(cnt_flat, run_flat, slot_t, gates_t, h1, p[0].reshape(n, PLE_DIM), y,
                   w_ple[0].astype(BF16), vec(g_ple[0]), w_ple_gate[0].astype(BF16), vec(g_final))
    return out.reshape(b, s, d)
```

```python
import jax
import jax.numpy as jnp
from jax import lax
from jax.experimental import pallas as pl
from jax.experimental.pallas import tpu as pltpu

F32 = jnp.float32
BF16 = jnp.bfloat16
I32 = jnp.int32

D_MODEL = 1024
ATTN_WIDTH = 512
DIFF_HEADS = 4
HEAD_DIM = 64
HEAD_V = 2 * HEAD_DIM
CONV_WIDTH = 512
CONV_KERNEL = 31
N_EXPERTS = 32
TOP_K = 4
EXPERT_FF = 1024
PLE_DIM = 256
SWIGLU_ALPHA = 1.702
SWIGLU_LIMIT = 7.0
EPS = 1e-5
IN_COLS = 3 * ATTN_WIDTH + 2 * CONV_WIDTH + 2 * D_MODEL
LAMBDA_INIT = 0.2
LOG2_E = 1.4426950408889634

MAIN_COLS = IN_COLS - ATTN_WIDTH
COL_CONV_A = 2
COL_CONV_B = 3
COL_GATE_A = 4
COL_GATE_B = 6

LANES = 128
TM_PROJ = 512
TQ = 256
TM_MIX = 512
T_ROUTE = 512
EXPERT_BLOCK = 512
EXPERT_SPLIT = 2
CONV_ROWS = 64
CONV_PAD = 32
DMA_UNROLL = 8
VMEM_LIMIT = 56 * 1024 * 1024

_NT = (((1,), (1,)), ((), ()))


def _params(sem, vmem=VMEM_LIMIT):
    return pltpu.CompilerParams(dimension_semantics=sem, vmem_limit_bytes=vmem)


def _rms(x, g):
    return x * lax.rsqrt(jnp.mean(x * x, axis=-1, keepdims=True) + EPS) * g


ROW_TILE = D_MODEL // LANES


def _store_row_tiles(ref, x, lead=(), first=0):
    rows = x.shape[0]
    for g in range(ROW_TILE):
        ref[lead + (pl.ds(first * ROW_TILE + g, rows, stride=ROW_TILE), slice(None))] = (
            x[:, g * LANES:(g + 1) * LANES])


def _load_row_tiles(ref, rows, lead=(), first=0):
    return jnp.concatenate(
        [ref[lead + (pl.ds(first * ROW_TILE + g, rows, stride=ROW_TILE), slice(None))]
         for g in range(ROW_TILE)], axis=1)


def _inproj_kernel(x_ref, g_ref, w_ref, wvt_ref, o_ref, vt_ref):
    u = _rms(x_ref[...], g_ref[...]).astype(BF16)
    ch = 512
    for c in range(MAIN_COLS // ch):
        r = jnp.dot(u, w_ref[:, c * ch:(c + 1) * ch], preferred_element_type=F32)
        if c == 0:
            r = r * (HEAD_DIM ** -0.5 * LOG2_E)
        o_ref[:, c * ch:(c + 1) * ch] = r.astype(BF16)
    vt_ref[...] = lax.dot_general(wvt_ref[...], u, _NT,
                                  preferred_element_type=F32).astype(BF16)


def _inproj(xt, g, w, wvt):
    n = xt.shape[0]
    return pl.pallas_call(
        _inproj_kernel,
        grid=(n // TM_PROJ,),
        in_specs=[
            pl.BlockSpec((TM_PROJ, D_MODEL), lambda i: (i, 0)),
            pl.BlockSpec((1, D_MODEL), lambda i: (0, 0)),
            pl.BlockSpec((D_MODEL, MAIN_COLS), lambda i: (0, 0)),
            pl.BlockSpec((ATTN_WIDTH, D_MODEL), lambda i: (0, 0)),
        ],
        out_specs=[
            pl.BlockSpec((TM_PROJ, MAIN_COLS), lambda i: (i, 0)),
            pl.BlockSpec((ATTN_WIDTH, TM_PROJ), lambda i: (0, i)),
        ],
        out_shape=[
            jax.ShapeDtypeStruct((n, MAIN_COLS), BF16),
            jax.ShapeDtypeStruct((ATTN_WIDTH, n), BF16),
        ],
        compiler_params=_params(("parallel",)),
        name="inproj",
    )(xt, g, w, wvt)


def _attn_kernel(q_ref, k_ref, vt_ref, lq1_ref, lk1_ref, lq2_ref, lk2_ref, g_ref, o_ref):
    i = pl.program_id(1)
    s_len = k_ref.shape[0]
    lane = lax.broadcasted_iota(I32, (TQ, HEAD_V), 1)
    key = lax.broadcasted_iota(I32, (TQ, 2 * TQ), 0)
    qry = lax.broadcasted_iota(I32, (TQ, 2 * TQ), 1)
    causal = key <= jnp.where(qry >= TQ, qry - TQ, qry)
    lam = (jnp.exp(jnp.sum(lq1_ref[...] * lk1_ref[...]))
           - jnp.exp(jnp.sum(lq2_ref[...] * lk2_ref[...])) + LAMBDA_INIT)

    def block(c):
        heads = range(DIFF_HEADS)
        cols = [slice(h * HEAD_V, (h + 1) * HEAD_V) for h in heads]
        n = c * TQ

        def stacked_q(h):
            q = q_ref[:, cols[h]]
            zero = jnp.zeros_like(q)
            return jnp.concatenate([jnp.where(lane < HEAD_DIM, q, zero),
                                    jnp.where(lane >= HEAD_DIM, q, zero)], axis=0)

        s = [lax.dot_general(k_ref[0:n + TQ, cols[h]], stacked_q(h), _NT,
                             preferred_element_type=F32) for h in heads]
        s_d = [jnp.where(causal, s[h][n:], -1e30) for h in heads]
        m = [jnp.max(s_d[h], axis=0, keepdims=True) for h in heads]
        if c:
            m = [jnp.maximum(m[h], jnp.max(s[h][:n], axis=0, keepdims=True)) for h in heads]
        p = [jnp.exp2(s_d[h] - m[h]) for h in heads]
        if c:
            p = [jnp.concatenate([jnp.exp2(s[h][:n] - m[h]), p[h]], axis=0) for h in heads]
        l = [jnp.sum(p[h], axis=0, keepdims=True) for h in heads]
        acc = [jnp.dot(vt_ref[cols[h], 0:n + TQ], p[h].astype(BF16), preferred_element_type=F32)
               for h in heads]
        for h in heads:
            o12 = acc[h] / l[h]
            o = o12[:, :TQ] - lam * o12[:, TQ:]
            o = o * lax.rsqrt(jnp.mean(o * o, axis=0, keepdims=True) + EPS) * g_ref[...]
            o_ref[:, cols[h]] = (o * (1.0 - LAMBDA_INIT)).T.astype(o_ref.dtype)

    for c in range(s_len // TQ):
        pl.when(i == c)(lambda c=c: block(c))


def _attention(proj3, vt, lq1, lk1, lq2, lk2, g_subln):
    b, s, _ = proj3.shape
    vec = pl.BlockSpec((1, HEAD_DIM), lambda bi, i: (0, 0))
    return pl.pallas_call(
        _attn_kernel,
        grid=(b, s // TQ),
        in_specs=[
            pl.BlockSpec((None, TQ, ATTN_WIDTH), lambda bi, i: (bi, i, 0)),
            pl.BlockSpec((None, s, ATTN_WIDTH), lambda bi, i: (bi, 0, 1)),
            pl.BlockSpec((ATTN_WIDTH, s), lambda bi, i: (0, bi)),
            vec, vec, vec, vec,
            pl.BlockSpec((HEAD_V, 1), lambda bi, i: (0, 0)),
        ],
        out_specs=pl.BlockSpec((None, TQ, ATTN_WIDTH), lambda bi, i: (bi, i, 0)),
        out_shape=jax.ShapeDtypeStruct((b, s, ATTN_WIDTH), BF16),
        compiler_params=_params(("parallel", "parallel")),
        name="attn",
    )(proj3, proj3, vt, lq1, lk1, lq2, lk2, g_subln)


def _conv_kernel(ca_ref, cb_ref, w_ref, b_ref, g_ref, beta_ref, o_ref, z_ref, stage_ref):
    s = ca_ref.shape[0]
    glu_rows = 256
    n_lg = CONV_WIDTH // LANES
    z_ref[:, 0:CONV_PAD, :] = jnp.zeros((n_lg, CONV_PAD, LANES), F32)

    def glu(c, carry):
        r0 = pl.multiple_of(c * glu_rows, glu_rows)
        a = ca_ref[pl.ds(r0, glu_rows), :].astype(F32)
        g = cb_ref[pl.ds(r0, glu_rows), :].astype(F32)
        z = a * jax.nn.sigmoid(g)
        for lg in range(n_lg):
            z_ref[lg, pl.ds(CONV_PAD + r0, glu_rows), :] = z[:, lg * LANES:(lg + 1) * LANES]
        return carry

    lax.fori_loop(0, s // glu_rows, glu, 0)

    def conv(c, carry):
        r0 = pl.multiple_of(c * CONV_ROWS, CONV_ROWS)
        groups = CONV_ROWS // 8
        for lg in range(n_lg):
            lanes = slice(lg * LANES, (lg + 1) * LANES)
            accs = [jnp.zeros((groups, LANES), F32) + b_ref[:, lanes] for _ in range(8)]
            for j in range(CONV_KERNEL):
                off = CONV_PAD - (CONV_KERNEL - 1) + j
                wj = w_ref[j:j + 1, lanes]
                for g in range(8):
                    accs[g] = accs[g] + wj * z_ref[lg, pl.ds(r0 + off + g, groups, stride=8), :]
            for g in range(8):
                stage_ref[lg, pl.ds(g, groups, stride=8), :] = accs[g]
        acc = jnp.concatenate([stage_ref[lg] for lg in range(n_lg)], axis=1)
        mu = jnp.mean(acc, axis=-1, keepdims=True)
        xc = acc - mu
        y = xc * lax.rsqrt(jnp.mean(xc * xc, axis=-1, keepdims=True) + EPS)
        y = y * g_ref[...] + beta_ref[...]
        o_ref[pl.ds(r0, CONV_ROWS), :] = (y * jax.nn.sigmoid(y)).astype(o_ref.dtype)
        return carry

    lax.fori_loop(0, s // CONV_ROWS, conv, 0, unroll=2)


def _conv(proj3, w_dw, b_dw, g_ln, b_ln):
    b, s, _ = proj3.shape
    vec = pl.BlockSpec((1, CONV_WIDTH), lambda bi: (0, 0))
    return pl.pallas_call(
        _conv_kernel,
        grid=(b,),
        in_specs=[
            pl.BlockSpec((None, s, CONV_WIDTH), lambda bi: (bi, 0, COL_CONV_A)),
            pl.BlockSpec((None, s, CONV_WIDTH), lambda bi: (bi, 0, COL_CONV_B)),
            pl.BlockSpec((CONV_KERNEL, CONV_WIDTH), lambda bi: (0, 0)),
            vec, vec, vec,
        ],
        out_specs=pl.BlockSpec((None, s, CONV_WIDTH), lambda bi: (bi, 0, 0)),
        out_shape=jax.ShapeDtypeStruct((b, s, CONV_WIDTH), BF16),
        scratch_shapes=[
            pltpu.VMEM((CONV_WIDTH // LANES, s + CONV_PAD, LANES), F32),
            pltpu.VMEM((CONV_WIDTH // LANES, CONV_ROWS, LANES), F32),
        ],
        compiler_params=_params(("parallel",)),
        name="conv",
    )(proj3, proj3, w_dw, b_dw, g_ln, b_ln)


def _split_bf16(a):
    hi = a.astype(BF16)
    lo = (a - hi.astype(F32)).astype(BF16)
    return hi, lo


def _mix_kernel(x_ref, o_ref, c_ref, ga_ref, gb_ref,
                wa_ref, wc_ref, wo_ref, gffn_ref, wr_ref, br_ref,
                h_ref, u_ref, idx_ref, gate_ref):
    a = jnp.dot(o_ref[...], wa_ref[...], preferred_element_type=F32)
    b = jnp.dot(c_ref[...], wc_ref[...], preferred_element_type=F32)
    m = (jax.nn.sigmoid(ga_ref[...].astype(F32)) * a
         + jax.nn.sigmoid(gb_ref[...].astype(F32)) * b)
    acc = x_ref[...] + jnp.dot(m.astype(BF16), wo_ref[...], preferred_element_type=F32)
    h_ref[...] = acc
    u = _rms(acc, gffn_ref[...])
    _store_row_tiles(u_ref, u)

    u_hi, u_lo = _split_bf16(u)
    w_hi, w_lo = _split_bf16(wr_ref[...])
    logits = (lax.dot_general(w_hi, u_hi, _NT, preferred_element_type=F32)
              + lax.dot_general(w_hi, u_lo, _NT, preferred_element_type=F32)
              + lax.dot_general(w_lo, u_hi, _NT, preferred_element_type=F32)
              + br_ref[...])
    eidx = lax.broadcasted_iota(I32, logits.shape, 0)
    vals, idxs = [], []
    for _ in range(TOP_K):
        mx = jnp.max(logits, axis=0, keepdims=True)
        sel = jnp.min(jnp.where(logits == mx, eidx, N_EXPERTS), axis=0, keepdims=True)
        vals.append(mx)
        idxs.append(sel)
        logits = jnp.where(eidx == sel, -jnp.inf, logits)
    ex = [jnp.exp(v - vals[0]) for v in vals]
    den = ex[0] + ex[1] + ex[2] + ex[3]
    idx_ref[...] = jnp.concatenate(idxs, axis=0)
    gate_ref[...] = jnp.concatenate([e / den for e in ex], axis=0)


def _mix(xt, o, c, proj, wa, wc, wo, g_ffn, wr_t, br):
    n = xt.shape[0]
    tm = TM_MIX
    row = lambda w, j: pl.BlockSpec((tm, w), lambda i, j=j: (i, j))
    full = lambda a: pl.BlockSpec(a.shape, lambda i: (0,) * a.ndim)
    return pl.pallas_call(
        _mix_kernel,
        grid=(n // tm,),
        in_specs=[
            row(D_MODEL, 0), row(ATTN_WIDTH, 0), row(CONV_WIDTH, 0),
            row(D_MODEL, COL_GATE_A // 2), row(D_MODEL, COL_GATE_B // 2),
            full(wa), full(wc), full(wo), full(g_ffn), full(wr_t), full(br),
        ],
        out_specs=[
            pl.BlockSpec((tm, D_MODEL), lambda i: (i, 0)),
            pl.BlockSpec((tm * ROW_TILE, LANES), lambda i: (i, 0)),
            pl.BlockSpec((TOP_K, tm), lambda i: (0, i)),
            pl.BlockSpec((TOP_K, tm), lambda i: (0, i)),
        ],
        out_shape=[
            jax.ShapeDtypeStruct((n, D_MODEL), F32),
            jax.ShapeDtypeStruct((n * ROW_TILE, LANES), F32),
            jax.ShapeDtypeStruct((TOP_K, n), I32),
            jax.ShapeDtypeStruct((TOP_K, n), F32),
        ],
        compiler_params=_params(("parallel",)),
        name="mix",
    )(xt, o, c, proj, proj, wa, wc, wo, g_ffn, wr_t, br)


def _rank_kernel(idx_ref, rank_ref, cnt_ref):
    t = T_ROUTE
    idx = idx_ref[...]
    eidx = lax.broadcasted_iota(I32, (N_EXPERTS, t), 0)
    onehot = [eidx == idx[k:k + 1, :] for k in range(TOP_K)]
    member = onehot[0] | onehot[1] | onehot[2] | onehot[3]
    r = lax.broadcasted_iota(I32, (t, t), 0)
    c = lax.broadcasted_iota(I32, (t, t), 1)
    tri = jnp.where(r < c, 1.0, 0.0).astype(BF16)
    mem_f = jnp.where(member, 1.0, 0.0)
    prefix = jnp.dot(mem_f.astype(BF16), tri, preferred_element_type=F32)
    ranks = [jnp.sum(jnp.where(onehot[k], prefix, 0.0), axis=0, keepdims=True)
             for k in range(TOP_K)]
    rank_ref[...] = jnp.concatenate(ranks, axis=0).astype(I32)
    cnt_ref[...] = jnp.sum(mem_f, axis=1, keepdims=True).astype(I32)


def _rank(idx_t):
    n = idx_t.shape[1]
    return pl.pallas_call(
        _rank_kernel,
        grid=(n // T_ROUTE,),
        in_specs=[pl.BlockSpec((TOP_K, T_ROUTE), lambda i: (0, i))],
        out_specs=[
            pl.BlockSpec((TOP_K, T_ROUTE), lambda i: (0, i)),
            pl.BlockSpec((None, N_EXPERTS, 1), lambda i: (i, 0, 0)),
        ],
        out_shape=[
            jax.ShapeDtypeStruct((TOP_K, n), I32),
            jax.ShapeDtypeStruct((n // T_ROUTE, N_EXPERTS, 1), I32),
        ],
        compiler_params=_params(("parallel",)),
        name="rank",
    )(idx_t)


def _pos_kernel(idx_ref, rank_ref, off_ref, pos_ref):
    idx = idx_ref[...]
    eidx = lax.broadcasted_iota(I32, (N_EXPERTS, idx.shape[1]), 0)
    rows = [jnp.sum(jnp.where(eidx == idx[k:k + 1, :], off_ref[...], 0),
                    axis=0, keepdims=True) for k in range(TOP_K)]
    pos_ref[...] = jnp.concatenate(rows, axis=0) + rank_ref[...]


def _pos(idx_t, rank_t, local_off):
    n = idx_t.shape[1]
    blk = pl.BlockSpec((TOP_K, T_ROUTE), lambda i: (0, i))
    return pl.pallas_call(
        _pos_kernel,
        grid=(n // T_ROUTE,),
        in_specs=[blk, blk, pl.BlockSpec((None, N_EXPERTS, 1), lambda i: (i, 0, 0))],
        out_specs=blk,
        out_shape=jax.ShapeDtypeStruct((TOP_K, n), I32),
        compiler_params=_params(("parallel",)),
        name="pos",
    )(idx_t, rank_t, local_off)


def _row(ref, r):
    return ref.at[pl.ds(pl.multiple_of(r * ROW_TILE, ROW_TILE), ROW_TILE), :]


def _run_copies(cnt, src_ref, src_row, dst_ref, dst_row, sem):
    for bit in reversed(range(T_ROUTE.bit_length())):
        size = 1 << bit
        piece = cnt & size

        @pl.when(piece != 0)
        def _(size=size, src_row=src_row, dst_row=dst_row):
            rows = size * ROW_TILE
            pltpu.make_async_copy(
                src_ref.at[pl.ds(pl.multiple_of(src_row * ROW_TILE, ROW_TILE), rows), :],
                dst_ref.at[pl.ds(pl.multiple_of(dst_row * ROW_TILE, ROW_TILE), rows), :],
                sem).start()

        src_row = src_row + piece
        dst_row = dst_row + piece


def _dispatch_kernel(pstart_ref, pend_ref, cnt_ref, run_ref, slot_ref, u_ref, xs_ref,
                     grouped, zero_ref, sem, zsem):
    t = T_ROUTE
    i = pl.program_id(0)
    cur = i % 2
    blk = EXPERT_BLOCK * ROW_TILE
    buf = grouped.at[cur]

    def drain(slot):
        pltpu.make_async_copy(grouped.at[slot], grouped.at[slot], sem.at[slot]).wait()

    @pl.when(i == 0)
    def _():
        zero_ref[...] = jnp.zeros_like(zero_ref)

        def z(e, carry):
            @pl.when(pend_ref[e] > pstart_ref[e])
            def _():
                start = (pend_ref[e] - EXPERT_BLOCK) * ROW_TILE
                cp = pltpu.make_async_copy(
                    zero_ref, xs_ref.at[pl.ds(pl.multiple_of(start, blk), blk), :], zsem)
                cp.start()
                cp.wait()
            return carry

        lax.fori_loop(0, N_EXPERTS, z, 0)

    @pl.when(i >= 2)
    def _():
        drain(cur)

    def group(tok, carry):
        tile = u_ref[pl.ds(pl.multiple_of(tok * ROW_TILE, ROW_TILE), ROW_TILE), :]
        for k in range(TOP_K):
            buf[pl.ds(pl.multiple_of(slot_ref[k, tok] * ROW_TILE, ROW_TILE), ROW_TILE), :] = tile
        return carry

    lax.fori_loop(0, t, group, 0, unroll=DMA_UNROLL)

    def runs(e, off):
        cnt = cnt_ref[i * N_EXPERTS + e]
        _run_copies(cnt, buf, off, xs_ref, run_ref[i * N_EXPERTS + e], sem.at[cur])
        return off + cnt

    lax.fori_loop(0, N_EXPERTS, runs, 0)

    @pl.when(i == pl.num_programs(0) - 1)
    def _():
        drain(cur)

        @pl.when(i >= 1)
        def _():
            drain(1 - cur)


def _dispatch(pstart, pend, block_cnt, run_start, slot_t, u, n_rows):
    n = u.shape[0] // ROW_TILE
    t = T_ROUTE
    grid_spec = pltpu.PrefetchScalarGridSpec(
        num_scalar_prefetch=4,
        grid=(n // t,),
        in_specs=[
            pl.BlockSpec((TOP_K, t), lambda i, *_: (0, i), memory_space=pltpu.SMEM),
            pl.BlockSpec((t * ROW_TILE, LANES), lambda i, *_: (i, 0)),
        ],
        out_specs=pl.BlockSpec(memory_space=pl.ANY),
        scratch_shapes=[
            pltpu.VMEM((2, TOP_K * t * ROW_TILE, LANES), F32),
            pltpu.VMEM((EXPERT_BLOCK * ROW_TILE, LANES), F32),
            pltpu.SemaphoreType.DMA((2,)),
            pltpu.SemaphoreType.DMA(()),
        ],
    )
    return pl.pallas_call(
        _dispatch_kernel,
        grid_spec=grid_spec,
        out_shape=jax.ShapeDtypeStruct((n_rows * ROW_TILE, LANES), F32),
        compiler_params=_params(("arbitrary",)),
        name="dispatch",
    )(pstart, pend, block_cnt, run_start, slot_t, u)


def _expert_kernel(be_ref, nb_ref, x_ref, wgu_ref, bgu_ref, wd_ref, bd_ref, y_ref,
                   wgu_s, wd32_s, wd_s):
    i = pl.program_id(0)
    nb = nb_ref[0]
    blk = EXPERT_BLOCK
    half = EXPERT_FF // 2
    first = jnp.logical_or(i == 0, be_ref[i] != be_ref[jnp.maximum(i - 1, 0)])

    @pl.when(jnp.logical_and(first, i < nb))
    def _():
        rows = 256
        for r in range(0, D_MODEL, rows):
            wgu_s[r:r + rows, :] = wgu_ref[r:r + rows, :].astype(BF16)
        for cs in range(D_MODEL // LANES):
            lanes = slice(cs * LANES, (cs + 1) * LANES)
            wd32_s[cs, pl.ds(0, half, stride=2), :] = wd_ref[0:half, lanes]
            wd32_s[cs, pl.ds(1, half, stride=2), :] = wd_ref[half:EXPERT_FF, lanes]
            wd_s[:, lanes] = wd32_s[cs].astype(BF16)

    @pl.when(i < nb)
    def _():
        rows = blk // EXPERT_SPLIT
        parts = range(EXPERT_SPLIT)
        even = (lax.broadcasted_iota(I32, (rows, LANES), 1) % 2) == 0
        xb = [_load_row_tiles(x_ref, rows, first=s * rows).astype(BF16) for s in parts]
        gu1 = [jnp.dot(xb[s], wgu_s[:, :EXPERT_FF], preferred_element_type=F32)
               + bgu_ref[:, :EXPERT_FF] for s in parts]
        gu2 = [jnp.dot(xb[s], wgu_s[:, EXPERT_FF:], preferred_element_type=F32)
               + bgu_ref[:, EXPERT_FF:] for s in parts]
        act = []
        for s in parts:
            cols = []
            for v in range(EXPERT_FF // LANES):
                a = gu1[s][:, v * LANES:(v + 1) * LANES]
                b = gu2[s][:, v * LANES:(v + 1) * LANES]
                g = jnp.where(even, a, pltpu.roll(b, 1, axis=1))
                l = jnp.where(even, pltpu.roll(a, LANES - 1, axis=1), b)
                g = jnp.minimum(g, SWIGLU_LIMIT)
                l = jnp.clip(l, -SWIGLU_LIMIT, SWIGLU_LIMIT)
                cols.append(((l + 1.0) * (g * jax.nn.sigmoid(g * SWIGLU_ALPHA))).astype(BF16))
            act.append(jnp.concatenate(cols, axis=1))
        y = [jnp.dot(act[s], wd_s[...], preferred_element_type=F32) + bd_ref[...] for s in parts]
        for s in parts:
            _store_row_tiles(y_ref, y[s], first=s * rows)

    @pl.when(i >= nb)
    def _():
        y_ref[...] = jnp.zeros_like(y_ref)


def _experts(block_expert, n_used, xs, wgu, bgu, wd, bd):
    n_rows = xs.shape[0] // ROW_TILE
    blk = EXPERT_BLOCK
    wspec = lambda r, c: pl.BlockSpec((None, r, c), lambda i, be, nb: (be[i], 0, 0))
    grid_spec = pltpu.PrefetchScalarGridSpec(
        num_scalar_prefetch=2,
        grid=(n_rows // blk,),
        in_specs=[
            pl.BlockSpec((blk * ROW_TILE, LANES), lambda i, be, nb: (jnp.minimum(i, nb[0] - 1), 0)),
            wspec(D_MODEL, 2 * EXPERT_FF), wspec(1, 2 * EXPERT_FF),
            wspec(EXPERT_FF, D_MODEL), wspec(1, D_MODEL),
        ],
        out_specs=pl.BlockSpec((blk * ROW_TILE, LANES), lambda i, be, nb: (i, 0)),
        scratch_shapes=[
            pltpu.VMEM((D_MODEL, 2 * EXPERT_FF), BF16),
            pltpu.VMEM((D_MODEL // LANES, EXPERT_FF, LANES), F32),
            pltpu.VMEM((EXPERT_FF, D_MODEL), BF16),
        ],
    )
    return pl.pallas_call(
        _expert_kernel,
        grid_spec=grid_spec,
        out_shape=jax.ShapeDtypeStruct((n_rows * ROW_TILE, LANES), F32),
        compiler_params=_params(("arbitrary",)),
        name="expert",
    )(block_expert, n_used, xs, wgu, bgu, wd, bd)


def _combine_kernel(cnt_ref, run_ref, slot_ref, gsm_ref, h_ref, p_ref, y_ref, wple_ref, gple_ref,
                    wpg_ref, gfin_ref, o_ref, staged, moe_s, sem):
    t = T_ROUTE
    i = pl.program_id(0)
    cur = i % 2

    def fetch(block, slot):
        def runs(e, off):
            cnt = cnt_ref[block * N_EXPERTS + e]
            _run_copies(cnt, y_ref, run_ref[block * N_EXPERTS + e], staged.at[slot], off,
                        sem.at[slot])
            return off + cnt

        lax.fori_loop(0, N_EXPERTS, runs, 0)

    @pl.when(i == 0)
    def _():
        fetch(0, 0)

    @pl.when(i + 1 < pl.num_programs(0))
    def _():
        fetch(i + 1, 1 - cur)

    emb = jnp.dot(p_ref[...].astype(BF16), wple_ref[...], preferred_element_type=F32)

    pltpu.make_async_copy(staged.at[cur], staged.at[cur], sem.at[cur]).wait()
    buf = staged.at[cur]

    def pick(tok, carry):
        acc = None
        for k in range(TOP_K):
            row = buf[pl.ds(pl.multiple_of(slot_ref[k, tok] * ROW_TILE, ROW_TILE), ROW_TILE), :]
            term = gsm_ref[k, tok] * row
            acc = term if acc is None else acc + term
        moe_s[pl.ds(pl.multiple_of(tok * ROW_TILE, ROW_TILE), ROW_TILE), :] = acc
        return carry

    lax.fori_loop(0, t, pick, 0, unroll=DMA_UNROLL)

    h = h_ref[...] + _load_row_tiles(moe_s, t)
    r = _rms(h, gple_ref[...]).astype(BF16)
    sig = jax.nn.sigmoid(jnp.dot(r, wpg_ref[...], preferred_element_type=F32))
    h = h + emb * sig
    o_ref[...] = _rms(h, gfin_ref[...])


def _combine(block_cnt, run_start, slot_t, gates_t, h, p2, y, wple, gple, wpg, gfin):
    n = h.shape[0]
    t = T_ROUTE
    full = lambda a: pl.BlockSpec(a.shape, lambda i, *_: (0,) * a.ndim)
    smem = lambda: pl.BlockSpec((TOP_K, t), lambda i, *_: (0, i), memory_space=pltpu.SMEM)
    grid_spec = pltpu.PrefetchScalarGridSpec(
        num_scalar_prefetch=2,
        grid=(n // t,),
        in_specs=[
            smem(), smem(),
            pl.BlockSpec((t, D_MODEL), lambda i, *_: (i, 0)),
            pl.BlockSpec((t, PLE_DIM), lambda i, *_: (i, 0)),
            pl.BlockSpec(memory_space=pl.ANY),
            full(wple), full(gple), full(wpg), full(gfin),
        ],
        out_specs=pl.BlockSpec((t, D_MODEL), lambda i, *_: (i, 0)),
        scratch_shapes=[
            pltpu.VMEM((2, TOP_K * t * ROW_TILE, LANES), F32),
            pltpu.VMEM((t * ROW_TILE, LANES), F32),
            pltpu.SemaphoreType.DMA((2,)),
        ],
    )
    return pl.pallas_call(
        _combine_kernel,
        grid_spec=grid_spec,
        out_shape=jax.ShapeDtypeStruct((n, D_MODEL), F32),
        compiler_params=_params(("arbitrary",)),
        name="combine",
    )(block_cnt, run_start, slot_t, gates_t, h, p2, y, wple, gple, wpg, gfin)


def kernel(x, p, g_mix, w_in, lambda_q1, lambda_k1, lambda_q2, lambda_k2, g_subln, w_attn_out,
           w_dw, b_dw, g_conv_ln, b_conv_ln, w_conv_out, w_o, g_ffn, w_router, b_router,
           w_gate_up, b_gate_up, w_down, b_down, w_ple, g_ple, w_ple_gate, g_final):
    b, s, d = x.shape
    n = b * s
    xt = x.reshape(n, d)
    vec = lambda a: a.reshape(1, -1)

    w_in0 = w_in[0]
    w_main = jnp.concatenate([w_in0[:, :2 * ATTN_WIDTH], w_in0[:, 3 * ATTN_WIDTH:]], axis=1)
    w_vt = w_in0[:, 2 * ATTN_WIDTH:3 * ATTN_WIDTH].T
    proj, vt = _inproj(xt, vec(g_mix[0]), w_main.astype(BF16), w_vt.astype(BF16))
    proj3 = proj.reshape(b, s, MAIN_COLS)
    attn = _attention(proj3, vt, vec(lambda_q1[0]), vec(lambda_k1[0]), vec(lambda_q2[0]),
                      vec(lambda_k2[0]), g_subln[0].reshape(HEAD_V, 1))
    conv = _conv(proj3, w_dw[0], vec(b_dw[0]), vec(g_conv_ln[0]), vec(b_conv_ln[0]))
    h1, u2, idx_t, gates_t = _mix(
        xt, attn.reshape(n, ATTN_WIDTH), conv.reshape(n, CONV_WIDTH), proj,
        w_attn_out[0].astype(BF16), w_conv_out[0].astype(BF16), w_o[0].astype(BF16),
        vec(g_ffn[0]), w_router[0].T, b_router[0].reshape(N_EXPERTS, 1))

    rank_t, block_cnt = _rank(idx_t)
    block_cnt = block_cnt[:, :, 0]
    counts = jnp.sum(block_cnt, axis=0)
    blk = EXPERT_BLOCK
    n_blocks = n * TOP_K // blk + N_EXPERTS
    padded = (counts + blk - 1) // blk * blk
    pend = jnp.cumsum(padded).astype(I32)
    pstart = pend - padded
    n_used = (pend[-1:] // blk).astype(I32)
    block_start = jnp.arange(n_blocks, dtype=I32) * blk
    block_expert = jnp.minimum(
        jnp.sum((pend[None, :] <= block_start[:, None]).astype(I32), axis=1), N_EXPERTS - 1)
    run_start = (pstart[None, :] + jnp.cumsum(block_cnt, axis=0) - block_cnt).astype(I32)
    local_off = (jnp.cumsum(block_cnt, axis=1) - block_cnt).astype(I32)
    slot_t = _pos(idx_t, rank_t, local_off[:, :, None])
    cnt_flat = block_cnt.reshape(-1).astype(I32)
    run_flat = run_start.reshape(-1)

    xs = _dispatch(pstart, pend, cnt_flat, run_flat, slot_t, u2, n_blocks * blk)
    y = _experts(block_expert, n_used, xs, w_gate_up[0], b_gate_up[0][:, None, :],
                 w_down[0], b_down[0][:, None, :])

    out = _combine(cnt_flat, run_flat, slot_t, gates_t, h1, p[0].reshape(n, PLE_DIM), y,
                   w_ple[0].astype(BF16), vec(g_ple[0]), w_ple_gate[0].astype(BF16), vec(g_final))
    return out.reshape(b, s, d)
```

```python
import jax
import jax.numpy as jnp
from jax import lax
from jax.experimental import pallas as pl
from jax.experimental.pallas import tpu as pltpu

F32 = jnp.float32
BF16 = jnp.bfloat16
I32 = jnp.int32

D_MODEL = 1024
ATTN_WIDTH = 512
DIFF_HEADS = 4
HEAD_DIM = 64
HEAD_V = 2 * HEAD_DIM
CONV_WIDTH = 512
CONV_KERNEL = 31
N_EXPERTS = 32
TOP_K = 4
EXPERT_FF = 1024
PLE_DIM = 256
SWIGLU_ALPHA = 1.702
SWIGLU_LIMIT = 7.0
EPS = 1e-5
IN_COLS = 3 * ATTN_WIDTH + 2 * CONV_WIDTH + 2 * D_MODEL
LAMBDA_INIT = 0.2
LOG2_E = 1.4426950408889634

MAIN_COLS = IN_COLS - ATTN_WIDTH
COL_CONV_A = 2
COL_CONV_B = 3
COL_GATE_A = 4
COL_GATE_B = 6

LANES = 128
TM_PROJ = 512
TQ = 256
T_ROUTE = 512
EXPERT_BLOCK = 512
EXPERT_SPLIT = 2
CONV_ROWS = 64
CONV_PAD = 32
DMA_UNROLL = 8
VMEM_LIMIT = 56 * 1024 * 1024

_NT = (((1,), (1,)), ((), ()))


def _params(sem, vmem=VMEM_LIMIT):
    return pltpu.CompilerParams(dimension_semantics=sem, vmem_limit_bytes=vmem)


def _rms(x, g):
    return x * lax.rsqrt(jnp.mean(x * x, axis=-1, keepdims=True) + EPS) * g


ROW_TILE = D_MODEL // LANES


def _store_row_tiles(ref, x, lead=(), first=0):
    rows = x.shape[0]
    for g in range(ROW_TILE):
        ref[lead + (pl.ds(first * ROW_TILE + g, rows, stride=ROW_TILE), slice(None))] = (
            x[:, g * LANES:(g + 1) * LANES])


def _load_row_tiles(ref, rows, lead=(), first=0):
    return jnp.concatenate(
        [ref[lead + (pl.ds(first * ROW_TILE + g, rows, stride=ROW_TILE), slice(None))]
         for g in range(ROW_TILE)], axis=1)


def _inproj_kernel(x_ref, g_ref, w_ref, wvt_ref, o_ref, vt_ref):
    u = _rms(x_ref[...], g_ref[...]).astype(BF16)
    ch = 512
    for c in range(MAIN_COLS // ch):
        r = jnp.dot(u, w_ref[:, c * ch:(c + 1) * ch], preferred_element_type=F32)
        if c == 0:
            r = r * (HEAD_DIM ** -0.5 * LOG2_E)
        o_ref[:, c * ch:(c + 1) * ch] = r.astype(BF16)
    vt_ref[...] = lax.dot_general(wvt_ref[...], u, _NT,
                                  preferred_element_type=F32).astype(BF16)


def _inproj(xt, g, w, wvt):
    n = xt.shape[0]
    return pl.pallas_call(
        _inproj_kernel,
        grid=(n // TM_PROJ,),
        in_specs=[
            pl.BlockSpec((TM_PROJ, D_MODEL), lambda i: (i, 0)),
            pl.BlockSpec((1, D_MODEL), lambda i: (0, 0)),
            pl.BlockSpec((D_MODEL, MAIN_COLS), lambda i: (0, 0)),
            pl.BlockSpec((ATTN_WIDTH, D_MODEL), lambda i: (0, 0)),
        ],
        out_specs=[
            pl.BlockSpec((TM_PROJ, MAIN_COLS), lambda i: (i, 0)),
            pl.BlockSpec((ATTN_WIDTH, TM_PROJ), lambda i: (0, i)),
        ],
        out_shape=[
            jax.ShapeDtypeStruct((n, MAIN_COLS), BF16),
            jax.ShapeDtypeStruct((ATTN_WIDTH, n), BF16),
        ],
        compiler_params=_params(("parallel",)),
        name="inproj",
    )(xt, g, w, wvt)


def _attn_kernel(q_ref, k_ref, vt_ref, lq1_ref, lk1_ref, lq2_ref, lk2_ref, g_ref, o_ref):
    i = pl.program_id(1)
    s_len = k_ref.shape[0]
    lane = lax.broadcasted_iota(I32, (TQ, HEAD_V), 1)
    key = lax.broadcasted_iota(I32, (TQ, 2 * TQ), 0)
    qry = lax.broadcasted_iota(I32, (TQ, 2 * TQ), 1)
    causal = key <= jnp.where(qry >= TQ, qry - TQ, qry)
    lam = (jnp.exp(jnp.sum(lq1_ref[...] * lk1_ref[...]))
           - jnp.exp(jnp.sum(lq2_ref[...] * lk2_ref[...])) + LAMBDA_INIT)

    half_blocks = s_len // TQ // 2

    def block(c):
        chains = [(half, h) for half in range(2) for h in range(DIFF_HEADS)]
        cols = [slice(h * HEAD_V, (h + 1) * HEAD_V) for _, h in chains]
        n = [(c + half * half_blocks) * TQ for half, _ in chains]
        ids = range(len(chains))

        def stacked_q(j):
            q = q_ref[chains[j][0], :, cols[j]]
            zero = jnp.zeros_like(q)
            return jnp.concatenate([jnp.where(lane < HEAD_DIM, q, zero),
                                    jnp.where(lane >= HEAD_DIM, q, zero)], axis=0)

        s = [lax.dot_general(k_ref[0:n[j] + TQ, cols[j]], stacked_q(j), _NT,
                             preferred_element_type=F32) for j in ids]
        s_d = [jnp.where(causal, s[j][n[j]:], -1e30) for j in ids]
        m = [jnp.max(s_d[j], axis=0, keepdims=True) for j in ids]
        m = [jnp.maximum(m[j], jnp.max(s[j][:n[j]], axis=0, keepdims=True)) if n[j] else m[j]
             for j in ids]
        p = [jnp.exp2(s_d[j] - m[j]) for j in ids]
        p = [jnp.concatenate([jnp.exp2(s[j][:n[j]] - m[j]), p[j]], axis=0) if n[j] else p[j]
             for j in ids]
        l = [jnp.sum(p[j], axis=0, keepdims=True) for j in ids]
        acc = [jnp.dot(vt_ref[cols[j], 0:n[j] + TQ], p[j].astype(BF16),
                       preferred_element_type=F32) for j in ids]
        for j in ids:
            o12 = acc[j] / l[j]
            o = o12[:, :TQ] - lam * o12[:, TQ:]
            o = o * lax.rsqrt(jnp.mean(o * o, axis=0, keepdims=True) + EPS) * g_ref[...]
            o_ref[chains[j][0], :, cols[j]] = (o * (1.0 - LAMBDA_INIT)).T.astype(o_ref.dtype)

    for c in range(half_blocks):
        pl.when(i == c)(lambda c=c: block(c))


def _attention(proj3, vt, lq1, lk1, lq2, lk2, g_subln):
    b, s, width = proj3.shape
    vec = pl.BlockSpec((1, HEAD_DIM), lambda bi, i: (0, 0))
    halves = pl.BlockSpec((None, 2, TQ, ATTN_WIDTH), lambda bi, i: (bi, 0, i, 0))
    out = pl.pallas_call(
        _attn_kernel,
        grid=(b, s // TQ // 2),
        in_specs=[
            halves,
            pl.BlockSpec((None, s, ATTN_WIDTH), lambda bi, i: (bi, 0, 1)),
            pl.BlockSpec((ATTN_WIDTH, s), lambda bi, i: (0, bi)),
            vec, vec, vec, vec,
            pl.BlockSpec((HEAD_V, 1), lambda bi, i: (0, 0)),
        ],
        out_specs=halves,
        out_shape=jax.ShapeDtypeStruct((b, 2, s // 2, ATTN_WIDTH), BF16),
        compiler_params=_params(("parallel", "parallel")),
        name="attn",
    )(proj3.reshape(b, 2, s // 2, width), proj3, vt, lq1, lk1, lq2, lk2, g_subln)
    return out.reshape(b, s, ATTN_WIDTH)


def _conv_kernel(ca_ref, cb_ref, w_ref, b_ref, g_ref, beta_ref, o_ref, z_ref, stage_ref):
    s = ca_ref.shape[0]
    glu_rows = 256
    n_lg = CONV_WIDTH // LANES
    z_ref[:, 0:CONV_PAD, :] = jnp.zeros((n_lg, CONV_PAD, LANES), F32)

    def glu(c, carry):
        r0 = pl.multiple_of(c * glu_rows, glu_rows)
        a = ca_ref[pl.ds(r0, glu_rows), :].astype(F32)
        g = cb_ref[pl.ds(r0, glu_rows), :].astype(F32)
        z = a * jax.nn.sigmoid(g)
        for lg in range(n_lg):
            z_ref[lg, pl.ds(CONV_PAD + r0, glu_rows), :] = z[:, lg * LANES:(lg + 1) * LANES]
        return carry

    lax.fori_loop(0, s // glu_rows, glu, 0)

    def conv(c, carry):
        r0 = pl.multiple_of(c * CONV_ROWS, CONV_ROWS)
        groups = CONV_ROWS // 8
        for lg in range(n_lg):
            lanes = slice(lg * LANES, (lg + 1) * LANES)
            accs = [jnp.zeros((groups, LANES), F32) + b_ref[:, lanes] for _ in range(8)]
            for j in range(CONV_KERNEL):
                off = CONV_PAD - (CONV_KERNEL - 1) + j
                wj = w_ref[j:j + 1, lanes]
                for g in range(8):
                    accs[g] = accs[g] + wj * z_ref[lg, pl.ds(r0 + off + g, groups, stride=8), :]
            for g in range(8):
                stage_ref[lg, pl.ds(g, groups, stride=8), :] = accs[g]
        acc = jnp.concatenate([stage_ref[lg] for lg in range(n_lg)], axis=1)
        mu = jnp.mean(acc, axis=-1, keepdims=True)
        xc = acc - mu
        y = xc * lax.rsqrt(jnp.mean(xc * xc, axis=-1, keepdims=True) + EPS)
        y = y * g_ref[...] + beta_ref[...]
        o_ref[pl.ds(r0, CONV_ROWS), :] = (y * jax.nn.sigmoid(y)).astype(o_ref.dtype)
        return carry

    lax.fori_loop(0, s // CONV_ROWS, conv, 0, unroll=2)


def _conv(proj3, w_dw, b_dw, g_ln, b_ln):
    b, s, _ = proj3.shape
    vec = pl.BlockSpec((1, CONV_WIDTH), lambda bi: (0, 0))
    return pl.pallas_call(
        _conv_kernel,
        grid=(b,),
        in_specs=[
            pl.BlockSpec((None, s, CONV_WIDTH), lambda bi: (bi, 0, COL_CONV_A)),
            pl.BlockSpec((None, s, CONV_WIDTH), lambda bi: (bi, 0, COL_CONV_B)),
            pl.BlockSpec((CONV_KERNEL, CONV_WIDTH), lambda bi: (0, 0)),
            vec, vec, vec,
        ],
        out_specs=pl.BlockSpec((None, s, CONV_WIDTH), lambda bi: (bi, 0, 0)),
        out_shape=jax.ShapeDtypeStruct((b, s, CONV_WIDTH), BF16),
        scratch_shapes=[
            pltpu.VMEM((CONV_WIDTH // LANES, s + CONV_PAD, LANES), F32),
            pltpu.VMEM((CONV_WIDTH // LANES, CONV_ROWS, LANES), F32),
        ],
        compiler_params=_params(("parallel",)),
        name="conv",
    )(proj3, proj3, w_dw, b_dw, g_ln, b_ln)


def _split_bf16(a):
    hi = a.astype(BF16)
    lo = (a - hi.astype(F32)).astype(BF16)
    return hi, lo


def _mix_kernel(x_ref, o_ref, c_ref, ga_ref, gb_ref,
                wa_ref, wc_ref, wo_ref, gffn_ref, wr_ref, br_ref,
                h_ref, u_ref, gate_ref, slot_ref, cnt_ref):
    a = jnp.dot(o_ref[...], wa_ref[...], preferred_element_type=F32)
    b = jnp.dot(c_ref[...], wc_ref[...], preferred_element_type=F32)
    m = (jax.nn.sigmoid(ga_ref[...].astype(F32)) * a
         + jax.nn.sigmoid(gb_ref[...].astype(F32)) * b)
    acc = x_ref[...] + jnp.dot(m.astype(BF16), wo_ref[...], preferred_element_type=F32)
    h_ref[...] = acc
    u = _rms(acc, gffn_ref[...])
    _store_row_tiles(u_ref, u)

    u_hi, u_lo = _split_bf16(u)
    w_hi, w_lo = _split_bf16(wr_ref[...])
    logits = (lax.dot_general(w_hi, u_hi, _NT, preferred_element_type=F32)
              + lax.dot_general(w_hi, u_lo, _NT, preferred_element_type=F32)
              + lax.dot_general(w_lo, u_hi, _NT, preferred_element_type=F32)
              + br_ref[...])
    eidx = lax.broadcasted_iota(I32, logits.shape, 0)
    vals, idxs = [], []
    for _ in range(TOP_K):
        mx = jnp.max(logits, axis=0, keepdims=True)
        sel = jnp.min(jnp.where(logits == mx, eidx, N_EXPERTS), axis=0, keepdims=True)
        vals.append(mx)
        idxs.append(sel)
        logits = jnp.where(eidx == sel, -jnp.inf, logits)
    ex = [jnp.exp(v - vals[0]) for v in vals]
    den = ex[0] + ex[1] + ex[2] + ex[3]
    gate_ref[...] = jnp.concatenate([e / den for e in ex], axis=0)

    tm = logits.shape[1]
    onehot = [eidx == idxs[k] for k in range(TOP_K)]
    member = jnp.where(onehot[0] | onehot[1] | onehot[2] | onehot[3], 1.0, 0.0).astype(BF16)
    r = lax.broadcasted_iota(I32, (tm, tm), 0)
    c = lax.broadcasted_iota(I32, (tm, tm), 1)
    before = jnp.where(r < c, 1.0, 0.0).astype(BF16)
    rank = jnp.dot(member, before, preferred_element_type=F32)
    cnt_ref[...] = jnp.sum(member.astype(F32), axis=1, keepdims=True).astype(I32)
    cnt_row = lax.dot_general(jnp.ones((8, tm), BF16), member, _NT,
                              preferred_element_type=F32)[0:1]
    e_row = lax.broadcasted_iota(I32, (N_EXPERTS, N_EXPERTS), 0)
    e_col = lax.broadcasted_iota(I32, (N_EXPERTS, N_EXPERTS), 1)
    first = jnp.sum(jnp.where(e_col < e_row, cnt_row, 0.0), axis=1, keepdims=True)
    slot_ref[...] = jnp.concatenate(
        [jnp.sum(jnp.where(onehot[k], rank + first, 0.0), axis=0, keepdims=True)
         for k in range(TOP_K)], axis=0).astype(I32)


def _mix(xt, o, c, proj, wa, wc, wo, g_ffn, wr_t, br):
    n = xt.shape[0]
    tm = T_ROUTE
    row = lambda w, j: pl.BlockSpec((tm, w), lambda i, j=j: (i, j))
    full = lambda a: pl.BlockSpec(a.shape, lambda i: (0,) * a.ndim)
    return pl.pallas_call(
        _mix_kernel,
        grid=(n // tm,),
        in_specs=[
            row(D_MODEL, 0), row(ATTN_WIDTH, 0), row(CONV_WIDTH, 0),
            row(D_MODEL, COL_GATE_A // 2), row(D_MODEL, COL_GATE_B // 2),
            full(wa), full(wc), full(wo), full(g_ffn), full(wr_t), full(br),
        ],
        out_specs=[
            pl.BlockSpec((tm, D_MODEL), lambda i: (i, 0)),
            pl.BlockSpec((tm * ROW_TILE, LANES), lambda i: (i, 0)),
            pl.BlockSpec((TOP_K, tm), lambda i: (0, i)),
            pl.BlockSpec((TOP_K, tm), lambda i: (0, i)),
            pl.BlockSpec((None, N_EXPERTS, 1), lambda i: (i, 0, 0)),
        ],
        out_shape=[
            jax.ShapeDtypeStruct((n, D_MODEL), F32),
            jax.ShapeDtypeStruct((n * ROW_TILE, LANES), F32),
            jax.ShapeDtypeStruct((TOP_K, n), F32),
            jax.ShapeDtypeStruct((TOP_K, n), I32),
            jax.ShapeDtypeStruct((n // tm, N_EXPERTS, 1), I32),
        ],
        compiler_params=_params(("parallel",)),
        name="mix",
    )(xt, o, c, proj, proj, wa, wc, wo, g_ffn, wr_t, br)


def _run_copies(cnt, src_ref, src_row, dst_ref, dst_row, sem):
    for bit in reversed(range(T_ROUTE.bit_length())):
        size = 1 << bit
        piece = cnt & size

        @pl.when(piece != 0)
        def _(size=size, src_row=src_row, dst_row=dst_row):
            rows = size * ROW_TILE
            pltpu.make_async_copy(
                src_ref.at[pl.ds(pl.multiple_of(src_row * ROW_TILE, ROW_TILE), rows), :],
                dst_ref.at[pl.ds(pl.multiple_of(dst_row * ROW_TILE, ROW_TILE), rows), :],
                sem).start()

        src_row = src_row + piece
        dst_row = dst_row + piece


def _dispatch_kernel(pstart_ref, pend_ref, cnt_ref, run_ref, slot_ref, u_ref, xs_ref,
                     grouped, zero_ref, sem, zsem):
    t = T_ROUTE
    i = pl.program_id(0)
    cur = i % 2
    blk = EXPERT_BLOCK * ROW_TILE
    buf = grouped.at[cur]

    def drain(slot):
        pltpu.make_async_copy(grouped.at[slot], grouped.at[slot], sem.at[slot]).wait()

    @pl.when(i == 0)
    def _():
        zero_ref[...] = jnp.zeros_like(zero_ref)

        def z(e, carry):
            @pl.when(pend_ref[e] > pstart_ref[e])
            def _():
                start = (pend_ref[e] - EXPERT_BLOCK) * ROW_TILE
                cp = pltpu.make_async_copy(
                    zero_ref, xs_ref.at[pl.ds(pl.multiple_of(start, blk), blk), :], zsem)
                cp.start()
                cp.wait()
            return carry

        lax.fori_loop(0, N_EXPERTS, z, 0)

    @pl.when(i >= 2)
    def _():
        drain(cur)

    def group(tok, carry):
        tile = u_ref[pl.ds(pl.multiple_of(tok * ROW_TILE, ROW_TILE), ROW_TILE), :]
        for k in range(TOP_K):
            buf[pl.ds(pl.multiple_of(slot_ref[k, tok] * ROW_TILE, ROW_TILE), ROW_TILE), :] = tile
        return carry

    lax.fori_loop(0, t, group, 0, unroll=DMA_UNROLL)

    def runs(e, off):
        cnt = cnt_ref[i * N_EXPERTS + e]
        _run_copies(cnt, buf, off, xs_ref, run_ref[i * N_EXPERTS + e], sem.at[cur])
        return off + cnt

    lax.fori_loop(0, N_EXPERTS, runs, 0)

    @pl.when(i == pl.num_programs(0) - 1)
    def _():
        drain(cur)

        @pl.when(i >= 1)
        def _():
            drain(1 - cur)


def _dispatch(pstart, pend, block_cnt, run_start, slot_t, u, n_rows):
    n = u.shape[0] // ROW_TILE
    t = T_ROUTE
    grid_spec = pltpu.PrefetchScalarGridSpec(
        num_scalar_prefetch=4,
        grid=(n // t,),
        in_specs=[
            pl.BlockSpec((TOP_K, t), lambda i, *_: (0, i), memory_space=pltpu.SMEM),
            pl.BlockSpec((t * ROW_TILE, LANES), lambda i, *_: (i, 0)),
        ],
        out_specs=pl.BlockSpec(memory_space=pl.ANY),
        scratch_shapes=[
            pltpu.VMEM((2, TOP_K * t * ROW_TILE, LANES), F32),
            pltpu.VMEM((EXPERT_BLOCK * ROW_TILE, LANES), F32),
            pltpu.SemaphoreType.DMA((2,)),
            pltpu.SemaphoreType.DMA(()),
        ],
    )
    return pl.pallas_call(
        _dispatch_kernel,
        grid_spec=grid_spec,
        out_shape=jax.ShapeDtypeStruct((n_rows * ROW_TILE, LANES), F32),
        compiler_params=_params(("arbitrary",)),
        name="dispatch",
    )(pstart, pend, block_cnt, run_start, slot_t, u)


def _expert_kernel(be_ref, nb_ref, x_ref, wgu_ref, bgu_ref, wd_ref, bd_ref, y_ref,
                   wgu_s, wd32_s, wd_s):
    i = pl.program_id(0)
    nb = nb_ref[0]
    blk = EXPERT_BLOCK
    half = EXPERT_FF // 2
    first = jnp.logical_or(i == 0, be_ref[i] != be_ref[jnp.maximum(i - 1, 0)])

    @pl.when(jnp.logical_and(first, i < nb))
    def _():
        rows = 256
        for r in range(0, D_MODEL, rows):
            wgu_s[r:r + rows, :] = wgu_ref[r:r + rows, :].astype(BF16)
        for cs in range(D_MODEL // LANES):
            lanes = slice(cs * LANES, (cs + 1) * LANES)
            wd32_s[cs, pl.ds(0, half, stride=2), :] = wd_ref[0:half, lanes]
            wd32_s[cs, pl.ds(1, half, stride=2), :] = wd_ref[half:EXPERT_FF, lanes]
            wd_s[:, lanes] = wd32_s[cs].astype(BF16)

    @pl.when(i < nb)
    def _():
        rows = blk // EXPERT_SPLIT
        parts = range(EXPERT_SPLIT)
        even = (lax.broadcasted_iota(I32, (rows, LANES), 1) % 2) == 0
        xb = [_load_row_tiles(x_ref, rows, first=s * rows).astype(BF16) for s in parts]
        gu1 = [jnp.dot(xb[s], wgu_s[:, :EXPERT_FF], preferred_element_type=F32)
               + bgu_ref[:, :EXPERT_FF] for s in parts]
        gu2 = [jnp.dot(xb[s], wgu_s[:, EXPERT_FF:], preferred_element_type=F32)
               + bgu_ref[:, EXPERT_FF:] for s in parts]
        act = []
        for s in parts:
            cols = []
            for v in range(EXPERT_FF // LANES):
                a = gu1[s][:, v * LANES:(v + 1) * LANES]
                b = gu2[s][:, v * LANES:(v + 1) * LANES]
                g = jnp.where(even, a, pltpu.roll(b, 1, axis=1))
                l = jnp.where(even, pltpu.roll(a, LANES - 1, axis=1), b)
                g = jnp.minimum(g, SWIGLU_LIMIT)
                l = jnp.clip(l, -SWIGLU_LIMIT, SWIGLU_LIMIT)
                cols.append(((l + 1.0) * (g * jax.nn.sigmoid(g * SWIGLU_ALPHA))).astype(BF16))
            act.append(jnp.concatenate(cols, axis=1))
        y = [jnp.dot(act[s], wd_s[...], preferred_element_type=F32) + bd_ref[...] for s in parts]
        for s in parts:
            _store_row_tiles(y_ref, y[s], first=s * rows)

    @pl.when(i >= nb)
    def _():
        y_ref[...] = jnp.zeros_like(y_ref)


def _experts(block_expert, n_used, xs, wgu, bgu, wd, bd):
    n_rows = xs.shape[0] // ROW_TILE
    blk = EXPERT_BLOCK
    wspec = lambda r, c: pl.BlockSpec((None, r, c), lambda i, be, nb: (be[i], 0, 0))
    grid_spec = pltpu.PrefetchScalarGridSpec(
        num_scalar_prefetch=2,
        grid=(n_rows // blk,),
        in_specs=[
            pl.BlockSpec((blk * ROW_TILE, LANES), lambda i, be, nb: (jnp.minimum(i, nb[0] - 1), 0)),
            wspec(D_MODEL, 2 * EXPERT_FF), wspec(1, 2 * EXPERT_FF),
            wspec(EXPERT_FF, D_MODEL), wspec(1, D_MODEL),
        ],
        out_specs=pl.BlockSpec((blk * ROW_TILE, LANES), lambda i, be, nb: (i, 0)),
        scratch_shapes=[
            pltpu.VMEM((D_MODEL, 2 * EXPERT_FF), BF16),
            pltpu.VMEM((D_MODEL // LANES, EXPERT_FF, LANES), F32),
            pltpu.VMEM((EXPERT_FF, D_MODEL), BF16),
        ],
    )
    return pl.pallas_call(
        _expert_kernel,
        grid_spec=grid_spec,
        out_shape=jax.ShapeDtypeStruct((n_rows * ROW_TILE, LANES), F32),
        compiler_params=_params(("arbitrary",)),
        name="expert",
    )(block_expert, n_used, xs, wgu, bgu, wd, bd)


def _combine_kernel(cnt_ref, run_ref, slot_ref, gsm_ref, h_ref, p_ref, y_ref, wple_ref, gple_ref,
                    wpg_ref, gfin_ref, o_ref, staged, moe_s, sem):
    t = T_ROUTE
    i = pl.program_id(0)
    cur = i % 2

    def fetch(block, slot):
        def runs(e, off):
            cnt = cnt_ref[block * N_EXPERTS + e]
            _run_copies(cnt, y_ref, run_ref[block * N_EXPERTS + e], staged.at[slot], off,
                        sem.at[slot])
            return off + cnt

        lax.fori_loop(0, N_EXPERTS, runs, 0)

    @pl.when(i == 0)
    def _():
        fetch(0, 0)

    @pl.when(i + 1 < pl.num_programs(0))
    def _():
        fetch(i + 1, 1 - cur)

    emb = jnp.dot(p_ref[...].astype(BF16), wple_ref[...], preferred_element_type=F32)

    pltpu.make_async_copy(staged.at[cur], staged.at[cur], sem.at[cur]).wait()
    buf = staged.at[cur]

    def pick(tok, carry):
        acc = None
        for k in range(TOP_K):
            row = buf[pl.ds(pl.multiple_of(slot_ref[k, tok] * ROW_TILE, ROW_TILE), ROW_TILE), :]
            term = gsm_ref[k, tok] * row
            acc = term if acc is None else acc + term
        moe_s[pl.ds(pl.multiple_of(tok * ROW_TILE, ROW_TILE), ROW_TILE), :] = acc
        return carry

    lax.fori_loop(0, t, pick, 0, unroll=DMA_UNROLL)

    h = h_ref[...] + _load_row_tiles(moe_s, t)
    r = _rms(h, gple_ref[...]).astype(BF16)
    sig = jax.nn.sigmoid(jnp.dot(r, wpg_ref[...], preferred_element_type=F32))
    h = h + emb * sig
    o_ref[...] = _rms(h, gfin_ref[...])


def _combine(block_cnt, run_start, slot_t, gates_t, h, p2, y, wple, gple, wpg, gfin):
    n = h.shape[0]
    t = T_ROUTE
    full = lambda a: pl.BlockSpec(a.shape, lambda i, *_: (0,) * a.ndim)
    smem = lambda: pl.BlockSpec((TOP_K, t), lambda i, *_: (0, i), memory_space=pltpu.SMEM)
    grid_spec = pltpu.PrefetchScalarGridSpec(
        num_scalar_prefetch=2,
        grid=(n // t,),
        in_specs=[
            smem(), smem(),
            pl.BlockSpec((t, D_MODEL), lambda i, *_: (i, 0)),
            pl.BlockSpec((t, PLE_DIM), lambda i, *_: (i, 0)),
            pl.BlockSpec(memory_space=pl.ANY),
            full(wple), full(gple), full(wpg), full(gfin),
        ],
        out_specs=pl.BlockSpec((t, D_MODEL), lambda i, *_: (i, 0)),
        scratch_shapes=[
            pltpu.VMEM((2, TOP_K * t * ROW_TILE, LANES), F32),
            pltpu.VMEM((t * ROW_TILE, LANES), F32),
            pltpu.SemaphoreType.DMA((2,)),
        ],
    )
    return pl.pallas_call(
        _combine_kernel,
        grid_spec=grid_spec,
        out_shape=jax.ShapeDtypeStruct((n, D_MODEL), F32),
        compiler_params=_params(("arbitrary",)),
        name="combine",
    )(block_cnt, run_start, slot_t, gates_t, h, p2, y, wple, gple, wpg, gfin)


def kernel(x, p, g_mix, w_in, lambda_q1, lambda_k1, lambda_q2, lambda_k2, g_subln, w_attn_out,
           w_dw, b_dw, g_conv_ln, b_conv_ln, w_conv_out, w_o, g_ffn, w_router, b_router,
           w_gate_up, b_gate_up, w_down, b_down, w_ple, g_ple, w_ple_gate, g_final):
    b, s, d = x.shape
    n = b * s
    xt = x.reshape(n, d)
    vec = lambda a: a.reshape(1, -1)

    w_in0 = w_in[0]
    w_main = jnp.concatenate([w_in0[:, :2 * ATTN_WIDTH], w_in0[:, 3 * ATTN_WIDTH:]], axis=1)
    w_vt = w_in0[:, 2 * ATTN_WIDTH:3 * ATTN_WIDTH].T
    proj, vt = _inproj(xt, vec(g_mix[0]), w_main.astype(BF16), w_vt.astype(BF16))
    proj3 = proj.reshape(b, s, MAIN_COLS)
    attn = _attention(proj3, vt, vec(lambda_q1[0]), vec(lambda_k1[0]), vec(lambda_q2[0]),
                      vec(lambda_k2[0]), g_subln[0].reshape(HEAD_V, 1))
    conv = _conv(proj3, w_dw[0], vec(b_dw[0]), vec(g_conv_ln[0]), vec(b_conv_ln[0]))
    h1, u2, gates_t, slot_t, block_cnt = _mix(
        xt, attn.reshape(n, ATTN_WIDTH), conv.reshape(n, CONV_WIDTH), proj,
        w_attn_out[0].astype(BF16), w_conv_out[0].astype(BF16), w_o[0].astype(BF16),
        vec(g_ffn[0]), w_router[0].T, b_router[0].reshape(N_EXPERTS, 1))

    block_cnt = block_cnt[:, :, 0]
    counts = jnp.sum(block_cnt, axis=0)
    blk = EXPERT_BLOCK
    n_blocks = n * TOP_K // blk + N_EXPERTS
    padded = (counts + blk - 1) // blk * blk
    pend = jnp.cumsum(padded).astype(I32)
    pstart = pend - padded
    n_used = (pend[-1:] // blk).astype(I32)
    block_start = jnp.arange(n_blocks, dtype=I32) * blk
    block_expert = jnp.minimum(
        jnp.sum((pend[None, :] <= block_start[:, None]).astype(I32), axis=1), N_EXPERTS - 1)
    run_start = (pstart[None, :] + jnp.cumsum(block_cnt, axis=0) - block_cnt).astype(I32)
    cnt_flat = block_cnt.reshape(-1).astype(I32)
    run_flat = run_start.reshape(-1)

    xs = _dispatch(pstart, pend, cnt_flat, run_flat, slot_t, u2, n_blocks * blk)
    y = _experts(block_expert, n_used, xs, w_gate_up[0], b_gate_up[0][:, None, :],
                 w_down[0], b_down[0][:, None, :])

    out = _combine(cnt_flat, run_flat, slot_t, gates_t, h1, p[0].reshape(n, PLE_DIM), y,
                   w_ple[0].astype(BF16), vec(g_ple[0]), w_ple_gate[0].astype(BF16), vec(g_final))
    return out.reshape(b, s, d)
```

```python
import jax
import jax.numpy as jnp
from jax import lax
from jax.experimental import pallas as pl
from jax.experimental.pallas import tpu as pltpu

F32 = jnp.float32
BF16 = jnp.bfloat16
I32 = jnp.int32

D_MODEL = 1024
ATTN_WIDTH = 512
DIFF_HEADS = 4
HEAD_DIM = 64
HEAD_V = 2 * HEAD_DIM
CONV_WIDTH = 512
CONV_KERNEL = 31
N_EXPERTS = 32
TOP_K = 4
EXPERT_FF = 1024
PLE_DIM = 256
SWIGLU_ALPHA = 1.702
SWIGLU_LIMIT = 7.0
EPS = 1e-5
IN_COLS = 3 * ATTN_WIDTH + 2 * CONV_WIDTH + 2 * D_MODEL
LAMBDA_INIT = 0.2
LOG2_E = 1.4426950408889634

MAIN_COLS = IN_COLS - ATTN_WIDTH
COL_CONV_A = 2
COL_CONV_B = 3
COL_GATE_A = 4
COL_GATE_B = 6

LANES = 128
TM_PROJ = 512
TQ = 256
T_ROUTE = 512
EXPERT_BLOCK = 512
EXPERT_SPLIT = 2
CONV_ROWS = 64
CONV_PAD = 32
RUN_LONG = 128
DMA_UNROLL = 8
VMEM_LIMIT = 56 * 1024 * 1024

_NT = (((1,), (1,)), ((), ()))


def _params(sem, vmem=VMEM_LIMIT):
    return pltpu.CompilerParams(dimension_semantics=sem, vmem_limit_bytes=vmem)


def _rms(x, g):
    return x * lax.rsqrt(jnp.mean(x * x, axis=-1, keepdims=True) + EPS) * g


ROW_TILE = D_MODEL // LANES


def _store_row_tiles(ref, x, lead=(), first=0):
    rows = x.shape[0]
    for g in range(ROW_TILE):
        ref[lead + (pl.ds(first * ROW_TILE + g, rows, stride=ROW_TILE), slice(None))] = (
            x[:, g * LANES:(g + 1) * LANES])


def _load_row_tiles(ref, rows, lead=(), first=0):
    return jnp.concatenate(
        [ref[lead + (pl.ds(first * ROW_TILE + g, rows, stride=ROW_TILE), slice(None))]
         for g in range(ROW_TILE)], axis=1)


def _inproj_kernel(x_ref, g_ref, w_ref, wvt_ref, o_ref, vt_ref):
    u = _rms(x_ref[...], g_ref[...]).astype(BF16)
    ch = 512
    for c in range(MAIN_COLS // ch):
        r = jnp.dot(u, w_ref[:, c * ch:(c + 1) * ch], preferred_element_type=F32)
        if c == 0:
            r = r * (HEAD_DIM ** -0.5 * LOG2_E)
        o_ref[:, c * ch:(c + 1) * ch] = r.astype(BF16)
    vt_ref[...] = lax.dot_general(wvt_ref[...], u, _NT,
                                  preferred_element_type=F32).astype(BF16)


def _inproj(xt, g, w, wvt):
    n = xt.shape[0]
    return pl.pallas_call(
        _inproj_kernel,
        grid=(n // TM_PROJ,),
        in_specs=[
            pl.BlockSpec((TM_PROJ, D_MODEL), lambda i: (i, 0)),
            pl.BlockSpec((1, D_MODEL), lambda i: (0, 0)),
            pl.BlockSpec((D_MODEL, MAIN_COLS), lambda i: (0, 0)),
            pl.BlockSpec((ATTN_WIDTH, D_MODEL), lambda i: (0, 0)),
        ],
        out_specs=[
            pl.BlockSpec((TM_PROJ, MAIN_COLS), lambda i: (i, 0)),
            pl.BlockSpec((ATTN_WIDTH, TM_PROJ), lambda i: (0, i)),
        ],
        out_shape=[
            jax.ShapeDtypeStruct((n, MAIN_COLS), BF16),
            jax.ShapeDtypeStruct((ATTN_WIDTH, n), BF16),
        ],
        compiler_params=_params(("parallel",)),
        name="inproj",
    )(xt, g, w, wvt)


def _attn_kernel(q_ref, k_ref, vt_ref, lq1_ref, lk1_ref, lq2_ref, lk2_ref, g_ref, o_ref):
    i = pl.program_id(1)
    s_len = k_ref.shape[0]
    lane = lax.broadcasted_iota(I32, (TQ, HEAD_V), 1)
    key = lax.broadcasted_iota(I32, (TQ, 2 * TQ), 0)
    qry = lax.broadcasted_iota(I32, (TQ, 2 * TQ), 1)
    causal = key <= jnp.where(qry >= TQ, qry - TQ, qry)
    lam = (jnp.exp(jnp.sum(lq1_ref[...] * lk1_ref[...]))
           - jnp.exp(jnp.sum(lq2_ref[...] * lk2_ref[...])) + LAMBDA_INIT)

    half_blocks = s_len // TQ // 2

    def block(c):
        chains = [(half, h) for half in range(2) for h in range(DIFF_HEADS)]
        cols = [slice(h * HEAD_V, (h + 1) * HEAD_V) for _, h in chains]
        n = [(c + half * half_blocks) * TQ for half, _ in chains]
        ids = range(len(chains))

        def stacked_q(j):
            q = q_ref[chains[j][0], :, cols[j]]
            zero = jnp.zeros_like(q)
            return jnp.concatenate([jnp.where(lane < HEAD_DIM, q, zero),
                                    jnp.where(lane >= HEAD_DIM, q, zero)], axis=0)

        s = [lax.dot_general(k_ref[0:n[j] + TQ, cols[j]], stacked_q(j), _NT,
                             preferred_element_type=F32) for j in ids]
        s_d = [jnp.where(causal, s[j][n[j]:], -1e30) for j in ids]
        m = [jnp.max(s_d[j], axis=0, keepdims=True) for j in ids]
        m = [jnp.maximum(m[j], jnp.max(s[j][:n[j]], axis=0, keepdims=True)) if n[j] else m[j]
             for j in ids]
        p = [jnp.exp2(s_d[j] - m[j]) for j in ids]
        p = [jnp.concatenate([jnp.exp2(s[j][:n[j]] - m[j]), p[j]], axis=0) if n[j] else p[j]
             for j in ids]
        l = [jnp.sum(p[j], axis=0, keepdims=True) for j in ids]
        acc = [jnp.dot(vt_ref[cols[j], 0:n[j] + TQ], p[j].astype(BF16),
                       preferred_element_type=F32) for j in ids]
        for j in ids:
            o12 = acc[j] / l[j]
            o = o12[:, :TQ] - lam * o12[:, TQ:]
            o = o * lax.rsqrt(jnp.mean(o * o, axis=0, keepdims=True) + EPS) * g_ref[...]
            o_ref[chains[j][0], :, cols[j]] = (o * (1.0 - LAMBDA_INIT)).T.astype(o_ref.dtype)

    for c in range(half_blocks):
        pl.when(i == c)(lambda c=c: block(c))


def _attention(proj3, vt, lq1, lk1, lq2, lk2, g_subln):
    b, s, width = proj3.shape
    vec = pl.BlockSpec((1, HEAD_DIM), lambda bi, i: (0, 0))
    halves = pl.BlockSpec((None, 2, TQ, ATTN_WIDTH), lambda bi, i: (bi, 0, i, 0))
    out = pl.pallas_call(
        _attn_kernel,
        grid=(b, s // TQ // 2),
        in_specs=[
            halves,
            pl.BlockSpec((None, s, ATTN_WIDTH), lambda bi, i: (bi, 0, 1)),
            pl.BlockSpec((ATTN_WIDTH, s), lambda bi, i: (0, bi)),
            vec, vec, vec, vec,
            pl.BlockSpec((HEAD_V, 1), lambda bi, i: (0, 0)),
        ],
        out_specs=halves,
        out_shape=jax.ShapeDtypeStruct((b, 2, s // 2, ATTN_WIDTH), BF16),
        compiler_params=_params(("parallel", "parallel")),
        name="attn",
    )(proj3.reshape(b, 2, s // 2, width), proj3, vt, lq1, lk1, lq2, lk2, g_subln)
    return out.reshape(b, s, ATTN_WIDTH)


def _conv_kernel(ca_ref, cb_ref, w_ref, b_ref, g_ref, beta_ref, o_ref, z_ref, stage_ref):
    s = ca_ref.shape[0]
    glu_rows = 256
    n_lg = CONV_WIDTH // LANES
    z_ref[:, 0:CONV_PAD, :] = jnp.zeros((n_lg, CONV_PAD, LANES), F32)

    def glu(c, carry):
        r0 = pl.multiple_of(c * glu_rows, glu_rows)
        a = ca_ref[pl.ds(r0, glu_rows), :].astype(F32)
        g = cb_ref[pl.ds(r0, glu_rows), :].astype(F32)
        z = a * jax.nn.sigmoid(g)
        for lg in range(n_lg):
            z_ref[lg, pl.ds(CONV_PAD + r0, glu_rows), :] = z[:, lg * LANES:(lg + 1) * LANES]
        return carry

    lax.fori_loop(0, s // glu_rows, glu, 0)

    def conv(c, carry):
        r0 = pl.multiple_of(c * CONV_ROWS, CONV_ROWS)
        groups = CONV_ROWS // 8
        for lg in range(n_lg):
            lanes = slice(lg * LANES, (lg + 1) * LANES)
            accs = [jnp.zeros((groups, LANES), F32) + b_ref[:, lanes] for _ in range(8)]
            for j in range(CONV_KERNEL):
                off = CONV_PAD - (CONV_KERNEL - 1) + j
                wj = w_ref[j:j + 1, lanes]
                for g in range(8):
                    accs[g] = accs[g] + wj * z_ref[lg, pl.ds(r0 + off + g, groups, stride=8), :]
            for g in range(8):
                stage_ref[lg, pl.ds(g, groups, stride=8), :] = accs[g]
        acc = jnp.concatenate([stage_ref[lg] for lg in range(n_lg)], axis=1)
        mu = jnp.mean(acc, axis=-1, keepdims=True)
        xc = acc - mu
        y = xc * lax.rsqrt(jnp.mean(xc * xc, axis=-1, keepdims=True) + EPS)
        y = y * g_ref[...] + beta_ref[...]
        o_ref[pl.ds(r0, CONV_ROWS), :] = (y * jax.nn.sigmoid(y)).astype(o_ref.dtype)
        return carry

    lax.fori_loop(0, s // CONV_ROWS, conv, 0, unroll=2)


def _conv(proj3, w_dw, b_dw, g_ln, b_ln):
    b, s, _ = proj3.shape
    vec = pl.BlockSpec((1, CONV_WIDTH), lambda bi: (0, 0))
    return pl.pallas_call(
        _conv_kernel,
        grid=(b,),
        in_specs=[
            pl.BlockSpec((None, s, CONV_WIDTH), lambda bi: (bi, 0, COL_CONV_A)),
            pl.BlockSpec((None, s, CONV_WIDTH), lambda bi: (bi, 0, COL_CONV_B)),
            pl.BlockSpec((CONV_KERNEL, CONV_WIDTH), lambda bi: (0, 0)),
            vec, vec, vec,
        ],
        out_specs=pl.BlockSpec((None, s, CONV_WIDTH), lambda bi: (bi, 0, 0)),
        out_shape=jax.ShapeDtypeStruct((b, s, CONV_WIDTH), BF16),
        scratch_shapes=[
            pltpu.VMEM((CONV_WIDTH // LANES, s + CONV_PAD, LANES), F32),
            pltpu.VMEM((CONV_WIDTH // LANES, CONV_ROWS, LANES), F32),
        ],
        compiler_params=_params(("parallel",)),
        name="conv",
    )(proj3, proj3, w_dw, b_dw, g_ln, b_ln)


def _split_bf16(a):
    hi = a.astype(BF16)
    lo = (a - hi.astype(F32)).astype(BF16)
    return hi, lo


def _mix_kernel(x_ref, o_ref, c_ref, ga_ref, gb_ref,
                wa_ref, wc_ref, wo_ref, gffn_ref, wr_ref, br_ref,
                h_ref, u_ref, gate_ref, slot_ref, cnt_ref):
    a = jnp.dot(o_ref[...], wa_ref[...], preferred_element_type=F32)
    b = jnp.dot(c_ref[...], wc_ref[...], preferred_element_type=F32)
    m = (jax.nn.sigmoid(ga_ref[...].astype(F32)) * a
         + jax.nn.sigmoid(gb_ref[...].astype(F32)) * b)
    acc = x_ref[...] + jnp.dot(m.astype(BF16), wo_ref[...], preferred_element_type=F32)
    h_ref[...] = acc
    u = _rms(acc, gffn_ref[...])
    _store_row_tiles(u_ref, u)

    u_hi, u_lo = _split_bf16(u)
    w_hi, w_lo = _split_bf16(wr_ref[...])
    logits = (lax.dot_general(w_hi, u_hi, _NT, preferred_element_type=F32)
              + lax.dot_general(w_hi, u_lo, _NT, preferred_element_type=F32)
              + lax.dot_general(w_lo, u_hi, _NT, preferred_element_type=F32)
              + br_ref[...])
    eidx = lax.broadcasted_iota(I32, logits.shape, 0)
    vals, idxs = [], []
    for _ in range(TOP_K):
        mx = jnp.max(logits, axis=0, keepdims=True)
        sel = jnp.min(jnp.where(logits == mx, eidx, N_EXPERTS), axis=0, keepdims=True)
        vals.append(mx)
        idxs.append(sel)
        logits = jnp.where(eidx == sel, -jnp.inf, logits)
    ex = [jnp.exp(v - vals[0]) for v in vals]
    den = ex[0] + ex[1] + ex[2] + ex[3]
    gate_ref[...] = jnp.concatenate([e / den for e in ex], axis=0)

    tm = logits.shape[1]
    onehot = [eidx == idxs[k] for k in range(TOP_K)]
    member = jnp.where(onehot[0] | onehot[1] | onehot[2] | onehot[3], 1.0, 0.0).astype(BF16)
    r = lax.broadcasted_iota(I32, (tm, tm), 0)
    c = lax.broadcasted_iota(I32, (tm, tm), 1)
    before = jnp.where(r < c, 1.0, 0.0).astype(BF16)
    rank = jnp.dot(member, before, preferred_element_type=F32)
    cnt_ref[...] = jnp.sum(member.astype(F32), axis=1, keepdims=True).astype(I32)
    cnt_row = lax.dot_general(jnp.ones((8, tm), BF16), member, _NT,
                              preferred_element_type=F32)[0:1]
    e_row = lax.broadcasted_iota(I32, (N_EXPERTS, N_EXPERTS), 0)
    e_col = lax.broadcasted_iota(I32, (N_EXPERTS, N_EXPERTS), 1)
    first = jnp.sum(jnp.where(e_col < e_row, cnt_row, 0.0), axis=1, keepdims=True)
    slot_ref[...] = jnp.concatenate(
        [jnp.sum(jnp.where(onehot[k], rank + first, 0.0), axis=0, keepdims=True)
         for k in range(TOP_K)], axis=0).astype(I32)


def _mix(xt, o, c, proj, wa, wc, wo, g_ffn, wr_t, br):
    n = xt.shape[0]
    tm = T_ROUTE
    row = lambda w, j: pl.BlockSpec((tm, w), lambda i, j=j: (i, j))
    full = lambda a: pl.BlockSpec(a.shape, lambda i: (0,) * a.ndim)
    return pl.pallas_call(
        _mix_kernel,
        grid=(n // tm,),
        in_specs=[
            row(D_MODEL, 0), row(ATTN_WIDTH, 0), row(CONV_WIDTH, 0),
            row(D_MODEL, COL_GATE_A // 2), row(D_MODEL, COL_GATE_B // 2),
            full(wa), full(wc), full(wo), full(g_ffn), full(wr_t), full(br),
        ],
        out_specs=[
            pl.BlockSpec((tm, D_MODEL), lambda i: (i, 0)),
            pl.BlockSpec((tm * ROW_TILE, LANES), lambda i: (i, 0)),
            pl.BlockSpec((TOP_K, tm), lambda i: (0, i)),
            pl.BlockSpec((TOP_K, tm), lambda i: (0, i)),
            pl.BlockSpec((None, N_EXPERTS, 1), lambda i: (i, 0, 0)),
        ],
        out_shape=[
            jax.ShapeDtypeStruct((n, D_MODEL), F32),
            jax.ShapeDtypeStruct((n * ROW_TILE, LANES), F32),
            jax.ShapeDtypeStruct((TOP_K, n), F32),
            jax.ShapeDtypeStruct((TOP_K, n), I32),
            jax.ShapeDtypeStruct((n // tm, N_EXPERTS, 1), I32),
        ],
        compiler_params=_params(("parallel",)),
        name="mix",
    )(xt, o, c, proj, proj, wa, wc, wo, g_ffn, wr_t, br)


def _run_copies(cnt, src_ref, src_row, dst_ref, dst_row, sem):
    def pieces(bits, src_row, dst_row):
        for bit in bits:
            size = 1 << bit
            piece = cnt & size

            @pl.when(piece != 0)
            def _(size=size, src_row=src_row, dst_row=dst_row):
                rows = size * ROW_TILE
                pltpu.make_async_copy(
                    src_ref.at[pl.ds(pl.multiple_of(src_row * ROW_TILE, ROW_TILE), rows), :],
                    dst_ref.at[pl.ds(pl.multiple_of(dst_row * ROW_TILE, ROW_TILE), rows), :],
                    sem).start()

            src_row = src_row + piece
            dst_row = dst_row + piece

    long_bits = RUN_LONG.bit_length() - 1

    @pl.when(cnt >= RUN_LONG)
    def _():
        pieces(reversed(range(long_bits, T_ROUTE.bit_length())), src_row, dst_row)

    head = cnt & -RUN_LONG
    pieces(reversed(range(long_bits)), src_row + head, dst_row + head)


def _dispatch_kernel(pstart_ref, pend_ref, cnt_ref, run_ref, slot_ref, u_ref, xs_ref,
                     grouped, zero_ref, sem, zsem):
    t = T_ROUTE
    i = pl.program_id(0)
    cur = i % 2
    blk = EXPERT_BLOCK * ROW_TILE
    buf = grouped.at[cur]

    def drain(slot):
        pltpu.make_async_copy(grouped.at[slot], grouped.at[slot], sem.at[slot]).wait()

    @pl.when(i == 0)
    def _():
        zero_ref[...] = jnp.zeros_like(zero_ref)

        def zero_block(first_row):
            cp = pltpu.make_async_copy(
                zero_ref, xs_ref.at[pl.ds(pl.multiple_of(first_row * ROW_TILE, blk), blk), :], zsem)
            cp.start()
            cp.wait()

        def z(e, carry):
            @pl.when(pend_ref[e] > pstart_ref[e])
            def _():
                zero_block(pend_ref[e] - EXPERT_BLOCK)
            return carry

        lax.fori_loop(0, N_EXPERTS, z, 0)

        def unused(b, carry):
            zero_block(b * EXPERT_BLOCK)
            return carry

        lax.fori_loop(pend_ref[N_EXPERTS - 1] // EXPERT_BLOCK,
                      xs_ref.shape[0] // blk, unused, 0)

    @pl.when(i >= 2)
    def _():
        drain(cur)

    def group(tok, carry):
        tile = u_ref[pl.ds(pl.multiple_of(tok * ROW_TILE, ROW_TILE), ROW_TILE), :]
        for k in range(TOP_K):
            buf[pl.ds(pl.multiple_of(slot_ref[k, tok] * ROW_TILE, ROW_TILE), ROW_TILE), :] = tile
        return carry

    lax.fori_loop(0, t, group, 0, unroll=DMA_UNROLL)

    def runs(e, off):
        cnt = cnt_ref[i * N_EXPERTS + e]
        _run_copies(cnt, buf, off, xs_ref, run_ref[i * N_EXPERTS + e], sem.at[cur])
        return off + cnt

    lax.fori_loop(0, N_EXPERTS, runs, 0)

    @pl.when(i == pl.num_programs(0) - 1)
    def _():
        drain(cur)

        @pl.when(i >= 1)
        def _():
            drain(1 - cur)


def _dispatch(pstart, pend, block_cnt, run_start, slot_t, u, n_rows):
    n = u.shape[0] // ROW_TILE
    t = T_ROUTE
    grid_spec = pltpu.PrefetchScalarGridSpec(
        num_scalar_prefetch=4,
        grid=(n // t,),
        in_specs=[
            pl.BlockSpec((TOP_K, t), lambda i, *_: (0, i), memory_space=pltpu.SMEM),
            pl.BlockSpec((t * ROW_TILE, LANES), lambda i, *_: (i, 0)),
        ],
        out_specs=pl.BlockSpec(memory_space=pl.ANY),
        scratch_shapes=[
            pltpu.VMEM((2, TOP_K * t * ROW_TILE, LANES), F32),
            pltpu.VMEM((EXPERT_BLOCK * ROW_TILE, LANES), F32),
            pltpu.SemaphoreType.DMA((2,)),
            pltpu.SemaphoreType.DMA(()),
        ],
    )
    return pl.pallas_call(
        _dispatch_kernel,
        grid_spec=grid_spec,
        out_shape=jax.ShapeDtypeStruct((n_rows * ROW_TILE, LANES), F32),
        compiler_params=_params(("arbitrary",)),
        name="dispatch",
    )(pstart, pend, block_cnt, run_start, slot_t, u)


def _expert_kernel(be_ref, nb_ref, x_ref, wgu_ref, bgu_ref, wd_ref, bd_ref, y_ref,
                   wgu_s, wd32_s, wd_s):
    i = pl.program_id(0)
    nb = nb_ref[0]
    blk = EXPERT_BLOCK
    half = EXPERT_FF // 2
    first = jnp.logical_or(i == 0, be_ref[i] != be_ref[jnp.maximum(i - 1, 0)])

    @pl.when(jnp.logical_and(first, i < nb))
    def _():
        rows = 256
        for r in range(0, D_MODEL, rows):
            wgu_s[r:r + rows, :] = wgu_ref[r:r + rows, :].astype(BF16)
        for cs in range(D_MODEL // LANES):
            lanes = slice(cs * LANES, (cs + 1) * LANES)
            wd32_s[cs, pl.ds(0, half, stride=2), :] = wd_ref[0:half, lanes]
            wd32_s[cs, pl.ds(1, half, stride=2), :] = wd_ref[half:EXPERT_FF, lanes]
            wd_s[:, lanes] = wd32_s[cs].astype(BF16)

    @pl.when(i < nb)
    def _():
        rows = blk // EXPERT_SPLIT
        parts = range(EXPERT_SPLIT)
        even = (lax.broadcasted_iota(I32, (rows, LANES), 1) % 2) == 0
        xb = [_load_row_tiles(x_ref, rows, first=s * rows).astype(BF16) for s in parts]
        gu1 = [jnp.dot(xb[s], wgu_s[:, :EXPERT_FF], preferred_element_type=F32)
               + bgu_ref[:, :EXPERT_FF] for s in parts]
        gu2 = [jnp.dot(xb[s], wgu_s[:, EXPERT_FF:], preferred_element_type=F32)
               + bgu_ref[:, EXPERT_FF:] for s in parts]
        act = []
        for s in parts:
            cols = []
            for v in range(EXPERT_FF // LANES):
                a = gu1[s][:, v * LANES:(v + 1) * LANES]
                b = gu2[s][:, v * LANES:(v + 1) * LANES]
                g = jnp.where(even, a, pltpu.roll(b, 1, axis=1))
                l = jnp.where(even, pltpu.roll(a, LANES - 1, axis=1), b)
                g = jnp.minimum(g, SWIGLU_LIMIT)
                l = jnp.clip(l, -SWIGLU_LIMIT, SWIGLU_LIMIT)
                cols.append(((l + 1.0) * (g * jax.nn.sigmoid(g * SWIGLU_ALPHA))).astype(BF16))
            act.append(jnp.concatenate(cols, axis=1))
        y = [jnp.dot(act[s], wd_s[...], preferred_element_type=F32) + bd_ref[...] for s in parts]
        for s in parts:
            _store_row_tiles(y_ref, y[s], first=s * rows)

    @pl.when(i >= nb)
    def _():
        y_ref[...] = jnp.zeros_like(y_ref)


def _experts(block_expert, n_used, xs, wgu, bgu, wd, bd):
    n_rows = xs.shape[0] // ROW_TILE
    blk = EXPERT_BLOCK
    wspec = lambda r, c: pl.BlockSpec((None, r, c), lambda i, be, nb: (be[i], 0, 0))
    grid_spec = pltpu.PrefetchScalarGridSpec(
        num_scalar_prefetch=2,
        grid=(n_rows // blk,),
        in_specs=[
            pl.BlockSpec((blk * ROW_TILE, LANES), lambda i, be, nb: (jnp.minimum(i, nb[0] - 1), 0)),
            wspec(D_MODEL, 2 * EXPERT_FF), wspec(1, 2 * EXPERT_FF),
            wspec(EXPERT_FF, D_MODEL), wspec(1, D_MODEL),
        ],
        out_specs=pl.BlockSpec((blk * ROW_TILE, LANES), lambda i, be, nb: (i, 0)),
        scratch_shapes=[
            pltpu.VMEM((D_MODEL, 2 * EXPERT_FF), BF16),
            pltpu.VMEM((D_MODEL // LANES, EXPERT_FF, LANES), F32),
            pltpu.VMEM((EXPERT_FF, D_MODEL), BF16),
        ],
    )
    return pl.pallas_call(
        _expert_kernel,
        grid_spec=grid_spec,
        out_shape=jax.ShapeDtypeStruct((n_rows * ROW_TILE, LANES), F32),
        compiler_params=_params(("arbitrary",)),
        name="expert",
    )(block_expert, n_used, xs, wgu, bgu, wd, bd)


def _combine_kernel(cnt_ref, run_ref, slot_ref, gsm_ref, h_ref, p_ref, y_ref, wple_ref, gple_ref,
                    wpg_ref, gfin_ref, o_ref, staged, moe_s, sem):
    t = T_ROUTE
    i = pl.program_id(0)
    cur = i % 2

    def fetch(block, slot):
        def runs(e, off):
            cnt = cnt_ref[block * N_EXPERTS + e]
            _run_copies(cnt, y_ref, run_ref[block * N_EXPERTS + e], staged.at[slot], off,
                        sem.at[slot])
            return off + cnt

        lax.fori_loop(0, N_EXPERTS, runs, 0)

    @pl.when(i == 0)
    def _():
        fetch(0, 0)

    @pl.when(i + 1 < pl.num_programs(0))
    def _():
        fetch(i + 1, 1 - cur)

    pltpu.make_async_copy(staged.at[cur], staged.at[cur], sem.at[cur]).wait()
    buf = staged.at[cur]

    def pick(tok, carry):
        acc = None
        for k in range(TOP_K):
            row = buf[pl.ds(pl.multiple_of(slot_ref[k, tok] * ROW_TILE, ROW_TILE), ROW_TILE), :]
            term = gsm_ref[k, tok] * row
            acc = term if acc is None else acc + term
        moe_s[pl.ds(pl.multiple_of(tok * ROW_TILE, ROW_TILE), ROW_TILE), :] = acc
        return carry

    lax.fori_loop(0, t, pick, 0, unroll=DMA_UNROLL)

    rows = t // 2
    parts = [pl.ds(s * rows, rows) for s in range(2)]
    emb = [jnp.dot(p_ref[rs, :].astype(BF16), wple_ref[...], preferred_element_type=F32)
           for rs in parts]
    h = [h_ref[rs, :] + _load_row_tiles(moe_s, rows, first=s * rows) for s, rs in enumerate(parts)]
    r = [_rms(hs, gple_ref[...]).astype(BF16) for hs in h]
    sig = [jax.nn.sigmoid(jnp.dot(rs, wpg_ref[...], preferred_element_type=F32)) for rs in r]
    for s, rs in enumerate(parts):
        o_ref[rs, :] = _rms(h[s] + emb[s] * sig[s], gfin_ref[...])


def _combine(block_cnt, run_start, slot_t, gates_t, h, p2, y, wple, gple, wpg, gfin):
    n = h.shape[0]
    t = T_ROUTE
    full = lambda a: pl.BlockSpec(a.shape, lambda i, *_: (0,) * a.ndim)
    smem = lambda: pl.BlockSpec((TOP_K, t), lambda i, *_: (0, i), memory_space=pltpu.SMEM)
    grid_spec = pltpu.PrefetchScalarGridSpec(
        num_scalar_prefetch=2,
        grid=(n // t,),
        in_specs=[
            smem(), smem(),
            pl.BlockSpec((t, D_MODEL), lambda i, *_: (i, 0)),
            pl.BlockSpec((t, PLE_DIM), lambda i, *_: (i, 0)),
            pl.BlockSpec(memory_space=pl.ANY),
            full(wple), full(gple), full(wpg), full(gfin),
        ],
        out_specs=pl.BlockSpec((t, D_MODEL), lambda i, *_: (i, 0)),
        scratch_shapes=[
            pltpu.VMEM((2, TOP_K * t * ROW_TILE, LANES), F32),
            pltpu.VMEM((t * ROW_TILE, LANES), F32),
            pltpu.SemaphoreType.DMA((2,)),
        ],
    )
    return pl.pallas_call(
        _combine_kernel,
        grid_spec=grid_spec,
        out_shape=jax.ShapeDtypeStruct((n, D_MODEL), F32),
        compiler_params=_params(("arbitrary",)),
        name="combine",
    )(block_cnt, run_start, slot_t, gates_t, h, p2, y, wple, gple, wpg, gfin)


def kernel(x, p, g_mix, w_in, lambda_q1, lambda_k1, lambda_q2, lambda_k2, g_subln, w_attn_out,
           w_dw, b_dw, g_conv_ln, b_conv_ln, w_conv_out, w_o, g_ffn, w_router, b_router,
           w_gate_up, b_gate_up, w_down, b_down, w_ple, g_ple, w_ple_gate, g_final):
    b, s, d = x.shape
    n = b * s
    xt = x.reshape(n, d)
    vec = lambda a: a.reshape(1, -1)

    w_in0 = w_in[0]
    w_main = jnp.concatenate([w_in0[:, :2 * ATTN_WIDTH], w_in0[:, 3 * ATTN_WIDTH:]], axis=1)
    w_vt = w_in0[:, 2 * ATTN_WIDTH:3 * ATTN_WIDTH].T
    proj, vt = _inproj(xt, vec(g_mix[0]), w_main.astype(BF16), w_vt.astype(BF16))
    proj3 = proj.reshape(b, s, MAIN_COLS)
    attn = _attention(proj3, vt, vec(lambda_q1[0]), vec(lambda_k1[0]), vec(lambda_q2[0]),
                      vec(lambda_k2[0]), g_subln[0].reshape(HEAD_V, 1))
    conv = _conv(proj3, w_dw[0], vec(b_dw[0]), vec(g_conv_ln[0]), vec(b_conv_ln[0]))
    h1, u2, gates_t, slot_t, block_cnt = _mix(
        xt, attn.reshape(n, ATTN_WIDTH), conv.reshape(n, CONV_WIDTH), proj,
        w_attn_out[0].astype(BF16), w_conv_out[0].astype(BF16), w_o[0].astype(BF16),
        vec(g_ffn[0]), w_router[0].T, b_router[0].reshape(N_EXPERTS, 1))

    block_cnt = block_cnt[:, :, 0]
    counts = jnp.sum(block_cnt, axis=0)
    blk = EXPERT_BLOCK
    n_blocks = n * TOP_K // blk + N_EXPERTS
    padded = (counts + blk - 1) // blk * blk
    pend = jnp.cumsum(padded).astype(I32)
    pstart = pend - padded
    n_used = (pend[-1:] // blk).astype(I32)
    block_start = jnp.arange(n_blocks, dtype=I32) * blk
    block_expert = jnp.minimum(
        jnp.sum((pend[None, :] <= block_start[:, None]).astype(I32), axis=1), N_EXPERTS - 1)
    run_start = (pstart[None, :] + jnp.cumsum(block_cnt, axis=0) - block_cnt).astype(I32)
    cnt_flat = block_cnt.reshape(-1).astype(I32)
    run_flat = run_start.reshape(-1)

    xs = _dispatch(pstart, pend, cnt_flat, run_flat, slot_t, u2, n_blocks * blk)
    y = _experts(block_expert, n_used, xs, w_gate_up[0], b_gate_up[0][:, None, :],
                 w_down[0], b_down[0][:, None, :])

    out = _combine(cnt_flat, run_flat, slot_t, gates_t, h1, p[0].reshape(n, PLE_DIM), y,
                   w_ple[0].astype(BF16), vec(g_ple[0]), w_ple_gate[0].astype(BF16), vec(g_final))
    return out.reshape(b, s, d)
```

```python
import jax
import jax.numpy as jnp
from jax import lax
from jax.experimental import pallas as pl
from jax.experimental.pallas import tpu as pltpu

F32 = jnp.float32
BF16 = jnp.bfloat16
I32 = jnp.int32

D_MODEL = 1024
ATTN_WIDTH = 512
DIFF_HEADS = 4
HEAD_DIM = 64
HEAD_V = 2 * HEAD_DIM
CONV_WIDTH = 512
CONV_KERNEL = 31
N_EXPERTS = 32
TOP_K = 4
EXPERT_FF = 1024
PLE_DIM = 256
SWIGLU_ALPHA = 1.702
SWIGLU_LIMIT = 7.0
EPS = 1e-5
IN_COLS = 3 * ATTN_WIDTH + 2 * CONV_WIDTH + 2 * D_MODEL
LAMBDA_INIT = 0.2
LOG2_E = 1.4426950408889634

MAIN_COLS = IN_COLS - ATTN_WIDTH
COL_CONV_A = 2
COL_CONV_B = 3
COL_GATE_A = 4
COL_GATE_B = 6

LANES = 128
TM_PROJ = 512
TQ = 256
T_ROUTE = 512
EXPERT_BLOCK = 512
EXPERT_SPLIT = 2
CONV_ROWS = 64
CONV_PAD = 32
RUN_LONG = 128
DMA_UNROLL = 8
VMEM_LIMIT = 56 * 1024 * 1024

_NT = (((1,), (1,)), ((), ()))


def _params(sem, vmem=VMEM_LIMIT):
    return pltpu.CompilerParams(dimension_semantics=sem, vmem_limit_bytes=vmem)


def _rms(x, g):
    return x * lax.rsqrt(jnp.mean(x * x, axis=-1, keepdims=True) + EPS) * g


ROW_TILE = D_MODEL // LANES


def _store_row_tiles(ref, x, lead=(), first=0):
    rows = x.shape[0]
    for g in range(ROW_TILE):
        ref[lead + (pl.ds(first * ROW_TILE + g, rows, stride=ROW_TILE), slice(None))] = (
            x[:, g * LANES:(g + 1) * LANES])


def _load_row_tiles(ref, rows, lead=(), first=0):
    return jnp.concatenate(
        [ref[lead + (pl.ds(first * ROW_TILE + g, rows, stride=ROW_TILE), slice(None))]
         for g in range(ROW_TILE)], axis=1)


def _inproj_kernel(x_ref, g_ref, w_ref, wvt_ref, o_ref, vt_ref):
    u = _rms(x_ref[...], g_ref[...]).astype(BF16)
    ch = 512
    for c in range(MAIN_COLS // ch):
        r = jnp.dot(u, w_ref[:, c * ch:(c + 1) * ch], preferred_element_type=F32)
        if c == 0:
            r = r * (HEAD_DIM ** -0.5 * LOG2_E)
        o_ref[:, c * ch:(c + 1) * ch] = r.astype(BF16)
    vt_ref[...] = lax.dot_general(wvt_ref[...], u, _NT,
                                  preferred_element_type=F32).astype(BF16)


def _inproj(xt, g, w, wvt):
    n = xt.shape[0]
    return pl.pallas_call(
        _inproj_kernel,
        grid=(n // TM_PROJ,),
        in_specs=[
            pl.BlockSpec((TM_PROJ, D_MODEL), lambda i: (i, 0)),
            pl.BlockSpec((1, D_MODEL), lambda i: (0, 0)),
            pl.BlockSpec((D_MODEL, MAIN_COLS), lambda i: (0, 0)),
            pl.BlockSpec((ATTN_WIDTH, D_MODEL), lambda i: (0, 0)),
        ],
        out_specs=[
            pl.BlockSpec((TM_PROJ, MAIN_COLS), lambda i: (i, 0)),
            pl.BlockSpec((ATTN_WIDTH, TM_PROJ), lambda i: (0, i)),
        ],
        out_shape=[
            jax.ShapeDtypeStruct((n, MAIN_COLS), BF16),
            jax.ShapeDtypeStruct((ATTN_WIDTH, n), BF16),
        ],
        compiler_params=_params(("parallel",)),
        name="inproj",
    )(xt, g, w, wvt)


def _attn_kernel(q_ref, k_ref, vt_ref, lq1_ref, lk1_ref, lq2_ref, lk2_ref, g_ref, o_ref):
    i = pl.program_id(1)
    s_len = k_ref.shape[0]
    lane = lax.broadcasted_iota(I32, (TQ, HEAD_V), 1)
    key = lax.broadcasted_iota(I32, (TQ, 2 * TQ), 0)
    qry = lax.broadcasted_iota(I32, (TQ, 2 * TQ), 1)
    causal = key <= jnp.where(qry >= TQ, qry - TQ, qry)
    lam = (jnp.exp(jnp.sum(lq1_ref[...] * lk1_ref[...]))
           - jnp.exp(jnp.sum(lq2_ref[...] * lk2_ref[...])) + LAMBDA_INIT)

    half_blocks = s_len // TQ // 2

    def block(c):
        chains = [(half, h) for half in range(2) for h in range(DIFF_HEADS)]
        cols = [slice(h * HEAD_V, (h + 1) * HEAD_V) for _, h in chains]
        n = [(c + half * half_blocks) * TQ for half, _ in chains]
        ids = range(len(chains))

        def stacked_q(j):
            q = q_ref[chains[j][0], :, cols[j]]
            zero = jnp.zeros_like(q)
            return jnp.concatenate([jnp.where(lane < HEAD_DIM, q, zero),
                                    jnp.where(lane >= HEAD_DIM, q, zero)], axis=0)

        s = [lax.dot_general(k_ref[0:n[j] + TQ, cols[j]], stacked_q(j), _NT,
                             preferred_element_type=F32) for j in ids]
        s_d = [jnp.where(causal, s[j][n[j]:], -1e30) for j in ids]
        m = [jnp.max(s_d[j], axis=0, keepdims=True) for j in ids]
        m = [jnp.maximum(m[j], jnp.max(s[j][:n[j]], axis=0, keepdims=True)) if n[j] else m[j]
             for j in ids]
        p = [jnp.exp2(s_d[j] - m[j]) for j in ids]
        p = [jnp.concatenate([jnp.exp2(s[j][:n[j]] - m[j]), p[j]], axis=0) if n[j] else p[j]
             for j in ids]
        l = [jnp.sum(p[j], axis=0, keepdims=True) for j in ids]
        acc = [jnp.dot(vt_ref[cols[j], 0:n[j] + TQ], p[j].astype(BF16),
                       preferred_element_type=F32) for j in ids]
        for j in ids:
            o12 = acc[j] / l[j]
            o = o12[:, :TQ] - lam * o12[:, TQ:]
            o = o * lax.rsqrt(jnp.mean(o * o, axis=0, keepdims=True) + EPS) * g_ref[...]
            o_ref[chains[j][0], :, cols[j]] = (o * (1.0 - LAMBDA_INIT)).T.astype(o_ref.dtype)

    for c in range(half_blocks):
        pl.when(i == c)(lambda c=c: block(c))


def _attention(proj3, vt, lq1, lk1, lq2, lk2, g_subln):
    b, s, width = proj3.shape
    vec = pl.BlockSpec((1, HEAD_DIM), lambda bi, i: (0, 0))
    halves = pl.BlockSpec((None, 2, TQ, ATTN_WIDTH), lambda bi, i: (bi, 0, i, 0))
    out = pl.pallas_call(
        _attn_kernel,
        grid=(b, s // TQ // 2),
        in_specs=[
            halves,
            pl.BlockSpec((None, s, ATTN_WIDTH), lambda bi, i: (bi, 0, 1)),
            pl.BlockSpec((ATTN_WIDTH, s), lambda bi, i: (0, bi)),
            vec, vec, vec, vec,
            pl.BlockSpec((HEAD_V, 1), lambda bi, i: (0, 0)),
        ],
        out_specs=halves,
        out_shape=jax.ShapeDtypeStruct((b, 2, s // 2, ATTN_WIDTH), BF16),
        compiler_params=_params(("parallel", "parallel")),
        name="attn",
    )(proj3.reshape(b, 2, s // 2, width), proj3, vt, lq1, lk1, lq2, lk2, g_subln)
    return out.reshape(b, s, ATTN_WIDTH)


def _conv_kernel(ca_ref, cb_ref, w_ref, b_ref, g_ref, beta_ref, o_ref, z_ref, stage_ref):
    s = ca_ref.shape[0]
    glu_rows = 256
    n_lg = CONV_WIDTH // LANES
    z_ref[:, 0:CONV_PAD, :] = jnp.zeros((n_lg, CONV_PAD, LANES), F32)

    def glu(c, carry):
        r0 = pl.multiple_of(c * glu_rows, glu_rows)
        a = ca_ref[pl.ds(r0, glu_rows), :].astype(F32)
        g = cb_ref[pl.ds(r0, glu_rows), :].astype(F32)
        z = a * jax.nn.sigmoid(g)
        for lg in range(n_lg):
            z_ref[lg, pl.ds(CONV_PAD + r0, glu_rows), :] = z[:, lg * LANES:(lg + 1) * LANES]
        return carry

    lax.fori_loop(0, s // glu_rows, glu, 0)

    def conv(c, carry):
        r0 = pl.multiple_of(c * CONV_ROWS, CONV_ROWS)
        groups = CONV_ROWS // 8
        for lg in range(n_lg):
            lanes = slice(lg * LANES, (lg + 1) * LANES)
            accs = [jnp.zeros((groups, LANES), F32) + b_ref[:, lanes] for _ in range(8)]
            for j in range(CONV_KERNEL):
                off = CONV_PAD - (CONV_KERNEL - 1) + j
                wj = w_ref[j:j + 1, lanes]
                for g in range(8):
                    accs[g] = accs[g] + wj * z_ref[lg, pl.ds(r0 + off + g, groups, stride=8), :]
            for g in range(8):
                stage_ref[lg, pl.ds(g, groups, stride=8), :] = accs[g]
        acc = jnp.concatenate([stage_ref[lg] for lg in range(n_lg)], axis=1)
        mu = jnp.mean(acc, axis=-1, keepdims=True)
        xc = acc - mu
        y = xc * lax.rsqrt(jnp.mean(xc * xc, axis=-1, keepdims=True) + EPS)
        y = y * g_ref[...] + beta_ref[...]
        o_ref[pl.ds(r0, CONV_ROWS), :] = (y * jax.nn.sigmoid(y)).astype(o_ref.dtype)
        return carry

    lax.fori_loop(0, s // CONV_ROWS, conv, 0, unroll=2)


def _conv(proj3, w_dw, b_dw, g_ln, b_ln):
    b, s, _ = proj3.shape
    vec = pl.BlockSpec((1, CONV_WIDTH), lambda bi: (0, 0))
    return pl.pallas_call(
        _conv_kernel,
        grid=(b,),
        in_specs=[
            pl.BlockSpec((None, s, CONV_WIDTH), lambda bi: (bi, 0, COL_CONV_A)),
            pl.BlockSpec((None, s, CONV_WIDTH), lambda bi: (bi, 0, COL_CONV_B)),
            pl.BlockSpec((CONV_KERNEL, CONV_WIDTH), lambda bi: (0, 0)),
            vec, vec, vec,
        ],
        out_specs=pl.BlockSpec((None, s, CONV_WIDTH), lambda bi: (bi, 0, 0)),
        out_shape=jax.ShapeDtypeStruct((b, s, CONV_WIDTH), BF16),
        scratch_shapes=[
            pltpu.VMEM((CONV_WIDTH // LANES, s + CONV_PAD, LANES), F32),
            pltpu.VMEM((CONV_WIDTH // LANES, CONV_ROWS, LANES), F32),
        ],
        compiler_params=_params(("parallel",)),
        name="conv",
    )(proj3, proj3, w_dw, b_dw, g_ln, b_ln)


def _split_bf16(a):
    hi = a.astype(BF16)
    lo = (a - hi.astype(F32)).astype(BF16)
    return hi, lo


def _mix_kernel(x_ref, o_ref, c_ref, ga_ref, gb_ref,
                wa_ref, wc_ref, wo_ref, gffn_ref, wr_ref, br_ref,
                h_ref, u_ref, gate_ref, slot_ref, cnt_ref):
    a = jnp.dot(o_ref[...], wa_ref[...], preferred_element_type=F32)
    b = jnp.dot(c_ref[...], wc_ref[...], preferred_element_type=F32)
    m = (jax.nn.sigmoid(ga_ref[...].astype(F32)) * a
         + jax.nn.sigmoid(gb_ref[...].astype(F32)) * b)
    acc = x_ref[...] + jnp.dot(m.astype(BF16), wo_ref[...], preferred_element_type=F32)
    h_ref[...] = acc
    u = _rms(acc, gffn_ref[...])
    _store_row_tiles(u_ref, u)

    u_hi, u_lo = _split_bf16(u)
    w_hi, w_lo = _split_bf16(wr_ref[...])
    logits = (lax.dot_general(w_hi, u_hi, _NT, preferred_element_type=F32)
              + lax.dot_general(w_hi, u_lo, _NT, preferred_element_type=F32)
              + lax.dot_general(w_lo, u_hi, _NT, preferred_element_type=F32)
              + br_ref[...])
    eidx = lax.broadcasted_iota(I32, logits.shape, 0)
    vals, idxs = [], []
    for _ in range(TOP_K):
        mx = jnp.max(logits, axis=0, keepdims=True)
        sel = jnp.min(jnp.where(logits == mx, eidx, N_EXPERTS), axis=0, keepdims=True)
        vals.append(mx)
        idxs.append(sel)
        logits = jnp.where(eidx == sel, -jnp.inf, logits)
    ex = [jnp.exp(v - vals[0]) for v in vals]
    den = ex[0] + ex[1] + ex[2] + ex[3]
    gate_ref[...] = jnp.concatenate([e / den for e in ex], axis=0)

    tm = logits.shape[1]
    onehot = [eidx == idxs[k] for k in range(TOP_K)]
    member = jnp.where(onehot[0] | onehot[1] | onehot[2] | onehot[3], 1.0, 0.0).astype(BF16)
    r = lax.broadcasted_iota(I32, (tm, tm), 0)
    c = lax.broadcasted_iota(I32, (tm, tm), 1)
    before = jnp.where(r < c, 1.0, 0.0).astype(BF16)
    rank = jnp.dot(member, before, preferred_element_type=F32)
    cnt_ref[...] = jnp.sum(member.astype(F32), axis=1, keepdims=True).astype(I32)
    cnt_row = lax.dot_general(jnp.ones((8, tm), BF16), member, _NT,
                              preferred_element_type=F32)[0:1]
    e_row = lax.broadcasted_iota(I32, (N_EXPERTS, N_EXPERTS), 0)
    e_col = lax.broadcasted_iota(I32, (N_EXPERTS, N_EXPERTS), 1)
    first = jnp.sum(jnp.where(e_col < e_row, cnt_row, 0.0), axis=1, keepdims=True)
    slot_ref[...] = jnp.concatenate(
        [jnp.sum(jnp.where(onehot[k], rank + first, 0.0), axis=0, keepdims=True)
         for k in range(TOP_K)], axis=0).astype(I32) * ROW_TILE


def _mix(xt, o, c, proj, wa, wc, wo, g_ffn, wr_t, br):
    n = xt.shape[0]
    tm = T_ROUTE
    row = lambda w, j: pl.BlockSpec((tm, w), lambda i, j=j: (i, j))
    full = lambda a: pl.BlockSpec(a.shape, lambda i: (0,) * a.ndim)
    return pl.pallas_call(
        _mix_kernel,
        grid=(n // tm,),
        in_specs=[
            row(D_MODEL, 0), row(ATTN_WIDTH, 0), row(CONV_WIDTH, 0),
            row(D_MODEL, COL_GATE_A // 2), row(D_MODEL, COL_GATE_B // 2),
            full(wa), full(wc), full(wo), full(g_ffn), full(wr_t), full(br),
        ],
        out_specs=[
            pl.BlockSpec((tm, D_MODEL), lambda i: (i, 0)),
            pl.BlockSpec((tm * ROW_TILE, LANES), lambda i: (i, 0)),
            pl.BlockSpec((TOP_K, tm), lambda i: (0, i)),
            pl.BlockSpec((TOP_K, tm), lambda i: (0, i)),
            pl.BlockSpec((None, N_EXPERTS, 1), lambda i: (i, 0, 0)),
        ],
        out_shape=[
            jax.ShapeDtypeStruct((n, D_MODEL), F32),
            jax.ShapeDtypeStruct((n * ROW_TILE, LANES), F32),
            jax.ShapeDtypeStruct((TOP_K, n), F32),
            jax.ShapeDtypeStruct((TOP_K, n), I32),
            jax.ShapeDtypeStruct((n // tm, N_EXPERTS, 1), I32),
        ],
        compiler_params=_params(("parallel",)),
        name="mix",
    )(xt, o, c, proj, proj, wa, wc, wo, g_ffn, wr_t, br)


def _run_copies(cnt, src_ref, src_row, dst_ref, dst_row, sem):
    def pieces(bits, src_row, dst_row):
        for bit in bits:
            size = 1 << bit
            piece = cnt & size

            @pl.when(piece != 0)
            def _(size=size, src_row=src_row, dst_row=dst_row):
                rows = size * ROW_TILE
                pltpu.make_async_copy(
                    src_ref.at[pl.ds(pl.multiple_of(src_row * ROW_TILE, ROW_TILE), rows), :],
                    dst_ref.at[pl.ds(pl.multiple_of(dst_row * ROW_TILE, ROW_TILE), rows), :],
                    sem).start()

            src_row = src_row + piece
            dst_row = dst_row + piece

    long_bits = RUN_LONG.bit_length() - 1

    @pl.when(cnt >= RUN_LONG)
    def _():
        pieces(reversed(range(long_bits, T_ROUTE.bit_length())), src_row, dst_row)

    head = cnt & -RUN_LONG
    pieces(reversed(range(long_bits)), src_row + head, dst_row + head)


def _dispatch_kernel(pstart_ref, pend_ref, cnt_ref, run_ref, slot_ref, u_ref, xs_ref,
                     grouped, zero_ref, sem, zsem):
    t = T_ROUTE
    i = pl.program_id(0)
    cur = i % 2
    blk = EXPERT_BLOCK * ROW_TILE
    buf = grouped.at[cur]

    def drain(slot):
        pltpu.make_async_copy(grouped.at[slot], grouped.at[slot], sem.at[slot]).wait()

    @pl.when(i == 0)
    def _():
        zero_ref[...] = jnp.zeros_like(zero_ref)

        def zero_block(first_row):
            return pltpu.make_async_copy(
                zero_ref, xs_ref.at[pl.ds(pl.multiple_of(first_row * ROW_TILE, blk), blk), :], zsem)

        for act in (lambda cp: cp.start(), lambda cp: cp.wait()):
            def tails(e, carry, act=act):
                @pl.when(pend_ref[e] > pstart_ref[e])
                def _():
                    act(zero_block(pend_ref[e] - EXPERT_BLOCK))
                return carry

            lax.fori_loop(0, N_EXPERTS, tails, 0)

            def unused(b, carry, act=act):
                act(zero_block(b * EXPERT_BLOCK))
                return carry

            lax.fori_loop(pend_ref[N_EXPERTS - 1] // EXPERT_BLOCK,
                          xs_ref.shape[0] // blk, unused, 0)

    @pl.when(i >= 2)
    def _():
        drain(cur)

    def group(tok, carry):
        tile = u_ref[pl.ds(pl.multiple_of(tok * ROW_TILE, ROW_TILE), ROW_TILE), :]
        for k in range(TOP_K):
            slot = pl.multiple_of(slot_ref[tok * TOP_K + k], ROW_TILE)
            buf[pl.ds(slot, ROW_TILE), :] = tile
        return carry

    lax.fori_loop(0, t, group, 0, unroll=DMA_UNROLL)

    def runs(e, off):
        cnt = cnt_ref[i * N_EXPERTS + e]
        _run_copies(cnt, buf, off, xs_ref, run_ref[i * N_EXPERTS + e], sem.at[cur])
        return off + cnt

    lax.fori_loop(0, N_EXPERTS, runs, 0)

    @pl.when(i == pl.num_programs(0) - 1)
    def _():
        drain(cur)

        @pl.when(i >= 1)
        def _():
            drain(1 - cur)


def _dispatch(pstart, pend, block_cnt, run_start, slot_t, u, n_rows):
    n = u.shape[0] // ROW_TILE
    t = T_ROUTE
    grid_spec = pltpu.PrefetchScalarGridSpec(
        num_scalar_prefetch=4,
        grid=(n // t,),
        in_specs=[
            pl.BlockSpec((TOP_K * t,), lambda i, *_: (i,), memory_space=pltpu.SMEM),
            pl.BlockSpec((t * ROW_TILE, LANES), lambda i, *_: (i, 0)),
        ],
        out_specs=pl.BlockSpec(memory_space=pl.ANY),
        scratch_shapes=[
            pltpu.VMEM((2, TOP_K * t * ROW_TILE, LANES), F32),
            pltpu.VMEM((EXPERT_BLOCK * ROW_TILE, LANES), F32),
            pltpu.SemaphoreType.DMA((2,)),
            pltpu.SemaphoreType.DMA(()),
        ],
    )
    return pl.pallas_call(
        _dispatch_kernel,
        grid_spec=grid_spec,
        out_shape=jax.ShapeDtypeStruct((n_rows * ROW_TILE, LANES), F32),
        compiler_params=_params(("arbitrary",)),
        name="dispatch",
    )(pstart, pend, block_cnt, run_start, slot_t, u)


def _expert_kernel(be_ref, nb_ref, x_ref, wgu_ref, bgu_ref, wd_ref, bd_ref, y_ref,
                   wgu_s, wd32_s, wd_s):
    i = pl.program_id(0)
    nb = nb_ref[0]
    blk = EXPERT_BLOCK
    half = EXPERT_FF // 2
    first = jnp.logical_or(i == 0, be_ref[i] != be_ref[jnp.maximum(i - 1, 0)])

    @pl.when(jnp.logical_and(first, i < nb))
    def _():
        rows = 256
        for r in range(0, D_MODEL, rows):
            wgu_s[r:r + rows, :] = wgu_ref[r:r + rows, :].astype(BF16)
        for cs in range(D_MODEL // LANES):
            lanes = slice(cs * LANES, (cs + 1) * LANES)
            wd32_s[cs, pl.ds(0, half, stride=2), :] = wd_ref[0:half, lanes]
            wd32_s[cs, pl.ds(1, half, stride=2), :] = wd_ref[half:EXPERT_FF, lanes]
            wd_s[:, lanes] = wd32_s[cs].astype(BF16)

    @pl.when(i < nb)
    def _():
        rows = blk // EXPERT_SPLIT
        parts = range(EXPERT_SPLIT)
        even = (lax.broadcasted_iota(I32, (rows, LANES), 1) % 2) == 0
        xb = [_load_row_tiles(x_ref, rows, first=s * rows).astype(BF16) for s in parts]
        gu1 = [jnp.dot(xb[s], wgu_s[:, :EXPERT_FF], preferred_element_type=F32)
               + bgu_ref[:, :EXPERT_FF] for s in parts]
        gu2 = [jnp.dot(xb[s], wgu_s[:, EXPERT_FF:], preferred_element_type=F32)
               + bgu_ref[:, EXPERT_FF:] for s in parts]
        act = []
        for s in parts:
            cols = []
            for v in range(EXPERT_FF // LANES):
                a = gu1[s][:, v * LANES:(v + 1) * LANES]
                b = gu2[s][:, v * LANES:(v + 1) * LANES]
                g = jnp.where(even, a, pltpu.roll(b, 1, axis=1))
                l = jnp.where(even, pltpu.roll(a, LANES - 1, axis=1), b)
                g = jnp.minimum(g, SWIGLU_LIMIT)
                l = jnp.clip(l, -SWIGLU_LIMIT, SWIGLU_LIMIT)
                cols.append(((l + 1.0) * (g * jax.nn.sigmoid(g * SWIGLU_ALPHA))).astype(BF16))
            act.append(jnp.concatenate(cols, axis=1))
        y = [jnp.dot(act[s], wd_s[...], preferred_element_type=F32) + bd_ref[...] for s in parts]
        for s in parts:
            _store_row_tiles(y_ref, y[s], first=s * rows)

    @pl.when(i >= nb)
    def _():
        y_ref[...] = jnp.zeros_like(y_ref)


def _experts(block_expert, n_used, xs, wgu, bgu, wd, bd):
    n_rows = xs.shape[0] // ROW_TILE
    blk = EXPERT_BLOCK
    wspec = lambda r, c: pl.BlockSpec((None, r, c), lambda i, be, nb: (be[i], 0, 0))
    grid_spec = pltpu.PrefetchScalarGridSpec(
        num_scalar_prefetch=2,
        grid=(n_rows // blk,),
        in_specs=[
            pl.BlockSpec((blk * ROW_TILE, LANES), lambda i, be, nb: (jnp.minimum(i, nb[0] - 1), 0)),
            wspec(D_MODEL, 2 * EXPERT_FF), wspec(1, 2 * EXPERT_FF),
            wspec(EXPERT_FF, D_MODEL), wspec(1, D_MODEL),
        ],
        out_specs=pl.BlockSpec((blk * ROW_TILE, LANES), lambda i, be, nb: (i, 0)),
        scratch_shapes=[
            pltpu.VMEM((D_MODEL, 2 * EXPERT_FF), BF16),
            pltpu.VMEM((D_MODEL // LANES, EXPERT_FF, LANES), F32),
            pltpu.VMEM((EXPERT_FF, D_MODEL), BF16),
        ],
    )
    return pl.pallas_call(
        _expert_kernel,
        grid_spec=grid_spec,
        out_shape=jax.ShapeDtypeStruct((n_rows * ROW_TILE, LANES), F32),
        compiler_params=_params(("arbitrary",)),
        name="expert",
    )(block_expert, n_used, xs, wgu, bgu, wd, bd)


def _combine_kernel(cnt_ref, run_ref, slot_ref, gsm_ref, h_ref, p_ref, y_ref, wple_ref, gple_ref,
                    wpg_ref, gfin_ref, o_ref, staged, moe_s, sem):
    t = T_ROUTE
    i = pl.program_id(0)
    cur = i % 2

    def fetch(block, slot):
        def runs(e, off):
            cnt = cnt_ref[block * N_EXPERTS + e]
            _run_copies(cnt, y_ref, run_ref[block * N_EXPERTS + e], staged.at[slot], off,
                        sem.at[slot])
            return off + cnt

        lax.fori_loop(0, N_EXPERTS, runs, 0)

    @pl.when(i == 0)
    def _():
        fetch(0, 0)

    @pl.when(i + 1 < pl.num_programs(0))
    def _():
        fetch(i + 1, 1 - cur)

    pltpu.make_async_copy(staged.at[cur], staged.at[cur], sem.at[cur]).wait()
    buf = staged.at[cur]

    def pick(tok, carry):
        acc = None
        for k in range(TOP_K):
            slot = pl.multiple_of(slot_ref[tok * TOP_K + k], ROW_TILE)
            row = buf[pl.ds(slot, ROW_TILE), :]
            term = gsm_ref[tok * TOP_K + k] * row
            acc = term if acc is None else acc + term
        moe_s[pl.ds(pl.multiple_of(tok * ROW_TILE, ROW_TILE), ROW_TILE), :] = acc
        return carry

    lax.fori_loop(0, t, pick, 0, unroll=DMA_UNROLL)

    rows = t // 2
    parts = [pl.ds(s * rows, rows) for s in range(2)]
    emb = [jnp.dot(p_ref[rs, :].astype(BF16), wple_ref[...], preferred_element_type=F32)
           for rs in parts]
    h = [h_ref[rs, :] + _load_row_tiles(moe_s, rows, first=s * rows) for s, rs in enumerate(parts)]
    r = [_rms(hs, gple_ref[...]).astype(BF16) for hs in h]
    sig = [jax.nn.sigmoid(jnp.dot(rs, wpg_ref[...], preferred_element_type=F32)) for rs in r]
    for s, rs in enumerate(parts):
        o_ref[rs, :] = _rms(h[s] + emb[s] * sig[s], gfin_ref[...])


def _combine(block_cnt, run_start, slot_t, gates_t, h, p2, y, wple, gple, wpg, gfin):
    n = h.shape[0]
    t = T_ROUTE
    full = lambda a: pl.BlockSpec(a.shape, lambda i, *_: (0,) * a.ndim)
    smem = lambda: pl.BlockSpec((TOP_K * t,), lambda i, *_: (i,), memory_space=pltpu.SMEM)
    grid_spec = pltpu.PrefetchScalarGridSpec(
        num_scalar_prefetch=2,
        grid=(n // t,),
        in_specs=[
            smem(), smem(),
            pl.BlockSpec((t, D_MODEL), lambda i, *_: (i, 0)),
            pl.BlockSpec((t, PLE_DIM), lambda i, *_: (i, 0)),
            pl.BlockSpec(memory_space=pl.ANY),
            full(wple), full(gple), full(wpg), full(gfin),
        ],
        out_specs=pl.BlockSpec((t, D_MODEL), lambda i, *_: (i, 0)),
        scratch_shapes=[
            pltpu.VMEM((2, TOP_K * t * ROW_TILE, LANES), F32),
            pltpu.VMEM((t * ROW_TILE, LANES), F32),
            pltpu.SemaphoreType.DMA((2,)),
        ],
    )
    return pl.pallas_call(
        _combine_kernel,
        grid_spec=grid_spec,
        out_shape=jax.ShapeDtypeStruct((n, D_MODEL), F32),
        compiler_params=_params(("arbitrary",)),
        name="combine",
    )(block_cnt, run_start, slot_t, gates_t, h, p2, y, wple, gple, wpg, gfin)


def kernel(x, p, g_mix, w_in, lambda_q1, lambda_k1, lambda_q2, lambda_k2, g_subln, w_attn_out,
           w_dw, b_dw, g_conv_ln, b_conv_ln, w_conv_out, w_o, g_ffn, w_router, b_router,
           w_gate_up, b_gate_up, w_down, b_down, w_ple, g_ple, w_ple_gate, g_final):
    b, s, d = x.shape
    n = b * s
    xt = x.reshape(n, d)
    vec = lambda a: a.reshape(1, -1)

    w_in0 = w_in[0]
    w_main = jnp.concatenate([w_in0[:, :2 * ATTN_WIDTH], w_in0[:, 3 * ATTN_WIDTH:]], axis=1)
    w_vt = w_in0[:, 2 * ATTN_WIDTH:3 * ATTN_WIDTH].T
    proj, vt = _inproj(xt, vec(g_mix[0]), w_main.astype(BF16), w_vt.astype(BF16))
    proj3 = proj.reshape(b, s, MAIN_COLS)
    attn = _attention(proj3, vt, vec(lambda_q1[0]), vec(lambda_k1[0]), vec(lambda_q2[0]),
                      vec(lambda_k2[0]), g_subln[0].reshape(HEAD_V, 1))
    conv = _conv(proj3, w_dw[0], vec(b_dw[0]), vec(g_conv_ln[0]), vec(b_conv_ln[0]))
    h1, u2, gates_t, slot_t, block_cnt = _mix(
        xt, attn.reshape(n, ATTN_WIDTH), conv.reshape(n, CONV_WIDTH), proj,
        w_attn_out[0].astype(BF16), w_conv_out[0].astype(BF16), w_o[0].astype(BF16),
        vec(g_ffn[0]), w_router[0].T, b_router[0].reshape(N_EXPERTS, 1))

    block_cnt = block_cnt[:, :, 0]
    counts = jnp.sum(block_cnt, axis=0)
    blk = EXPERT_BLOCK
    n_blocks = n * TOP_K // blk + N_EXPERTS
    padded = (counts + blk - 1) // blk * blk
    pend = jnp.cumsum(padded).astype(I32)
    pstart = pend - padded
    n_used = (pend[-1:] // blk).astype(I32)
    block_start = jnp.arange(n_blocks, dtype=I32) * blk
    block_expert = jnp.minimum(
        jnp.sum((pend[None, :] <= block_start[:, None]).astype(I32), axis=1), N_EXPERTS - 1)
    run_start = (pstart[None, :] + jnp.cumsum(block_cnt, axis=0) - block_cnt).astype(I32)
    cnt_flat = block_cnt.reshape(-1).astype(I32)
    run_flat = run_start.reshape(-1)

    slot_flat = slot_t.T.reshape(-1)
    gate_flat = gates_t.T.reshape(-1)

    xs = _dispatch(pstart, pend, cnt_flat, run_flat, slot_flat, u2, n_blocks * blk)
    y = _experts(block_expert, n_used, xs, w_gate_up[0], b_gate_up[0][:, None, :],
                 w_down[0], b_down[0][:, None, :])

    out = _combine(cnt_flat, run_flat, slot_flat, gate_flat, h1, p[0].reshape(n, PLE_DIM), y,
                   w_ple[0].astype(BF16), vec(g_ple[0]), w_ple_gate[0].astype(BF16), vec(g_final))
    return out.reshape(b, s, d)
```

```python
import jax
import jax.numpy as jnp
from jax import lax
from jax.experimental import pallas as pl
from jax.experimental.pallas import tpu as pltpu

F32 = jnp.float32
BF16 = jnp.bfloat16
I32 = jnp.int32

D_MODEL = 1024
ATTN_WIDTH = 512
DIFF_HEADS = 4
HEAD_DIM = 64
HEAD_V = 2 * HEAD_DIM
CONV_WIDTH = 512
CONV_KERNEL = 31
N_EXPERTS = 32
TOP_K = 4
EXPERT_FF = 1024
PLE_DIM = 256
SWIGLU_ALPHA = 1.702
SWIGLU_LIMIT = 7.0
EPS = 1e-5
IN_COLS = 3 * ATTN_WIDTH + 2 * CONV_WIDTH + 2 * D_MODEL
LAMBDA_INIT = 0.2
LOG2_E = 1.4426950408889634

MAIN_COLS = IN_COLS - ATTN_WIDTH
COL_CONV_A = 2
COL_CONV_B = 3
COL_GATE_A = 4
COL_GATE_B = 6

LANES = 128
TM_PROJ = 512
TQ = 256
T_ROUTE = 512
EXPERT_BLOCK = 512
EXPERT_SPLIT = 2
CONV_ROWS = 64
CONV_PAD = 32
RUN_LONG = 128
DMA_UNROLL = 8
VMEM_LIMIT = 56 * 1024 * 1024

_NT = (((1,), (1,)), ((), ()))


def _params(sem, vmem=VMEM_LIMIT):
    return pltpu.CompilerParams(dimension_semantics=sem, vmem_limit_bytes=vmem)


def _rms(x, g):
    return x * lax.rsqrt(jnp.mean(x * x, axis=-1, keepdims=True) + EPS) * g


ROW_TILE = D_MODEL // LANES


def _store_row_tiles(ref, x, lead=(), first=0):
    rows = x.shape[0]
    for g in range(ROW_TILE):
        ref[lead + (pl.ds(first * ROW_TILE + g, rows, stride=ROW_TILE), slice(None))] = (
            x[:, g * LANES:(g + 1) * LANES])


def _load_row_tiles(ref, rows, lead=(), first=0):
    return jnp.concatenate(
        [ref[lead + (pl.ds(first * ROW_TILE + g, rows, stride=ROW_TILE), slice(None))]
         for g in range(ROW_TILE)], axis=1)


def _inproj_kernel(x_ref, g_ref, w_ref, wvt_ref, o_ref, vt_ref):
    u = _rms(x_ref[...], g_ref[...]).astype(BF16)
    ch = 512
    for c in range(MAIN_COLS // ch):
        r = jnp.dot(u, w_ref[:, c * ch:(c + 1) * ch], preferred_element_type=F32)
        if c == 0:
            r = r * (HEAD_DIM ** -0.5 * LOG2_E)
        o_ref[:, c * ch:(c + 1) * ch] = r.astype(BF16)
    vt_ref[...] = lax.dot_general(wvt_ref[...], u, _NT,
                                  preferred_element_type=F32).astype(BF16)


def _inproj(xt, g, w, wvt):
    n = xt.shape[0]
    return pl.pallas_call(
        _inproj_kernel,
        grid=(n // TM_PROJ,),
        in_specs=[
            pl.BlockSpec((TM_PROJ, D_MODEL), lambda i: (i, 0)),
            pl.BlockSpec((1, D_MODEL), lambda i: (0, 0)),
            pl.BlockSpec((D_MODEL, MAIN_COLS), lambda i: (0, 0)),
            pl.BlockSpec((ATTN_WIDTH, D_MODEL), lambda i: (0, 0)),
        ],
        out_specs=[
            pl.BlockSpec((TM_PROJ, MAIN_COLS), lambda i: (i, 0)),
            pl.BlockSpec((ATTN_WIDTH, TM_PROJ), lambda i: (0, i)),
        ],
        out_shape=[
            jax.ShapeDtypeStruct((n, MAIN_COLS), BF16),
            jax.ShapeDtypeStruct((ATTN_WIDTH, n), BF16),
        ],
        compiler_params=_params(("parallel",)),
        name="inproj",
    )(xt, g, w, wvt)


def _attn_kernel(q_ref, k_ref, vt_ref, lq1_ref, lk1_ref, lq2_ref, lk2_ref, g_ref, o_ref):
    i = pl.program_id(1)
    s_len = k_ref.shape[0]
    lane = lax.broadcasted_iota(I32, (TQ, HEAD_V), 1)
    key = lax.broadcasted_iota(I32, (TQ, 2 * TQ), 0)
    qry = lax.broadcasted_iota(I32, (TQ, 2 * TQ), 1)
    causal = key <= jnp.where(qry >= TQ, qry - TQ, qry)
    lam = (jnp.exp(jnp.sum(lq1_ref[...] * lk1_ref[...]))
           - jnp.exp(jnp.sum(lq2_ref[...] * lk2_ref[...])) + LAMBDA_INIT)

    half_blocks = s_len // TQ // 2

    def block(c):
        chains = [(half, h) for half in range(2) for h in range(DIFF_HEADS)]
        cols = [slice(h * HEAD_V, (h + 1) * HEAD_V) for _, h in chains]
        n = [(c + half * half_blocks) * TQ for half, _ in chains]
        ids = range(len(chains))

        def stacked_q(j):
            q = q_ref[chains[j][0], :, cols[j]]
            zero = jnp.zeros_like(q)
            return jnp.concatenate([jnp.where(lane < HEAD_DIM, q, zero),
                                    jnp.where(lane >= HEAD_DIM, q, zero)], axis=0)

        s = [lax.dot_general(k_ref[0:n[j] + TQ, cols[j]], stacked_q(j), _NT,
                             preferred_element_type=F32) for j in ids]
        s_d = [jnp.where(causal, s[j][n[j]:], -1e30) for j in ids]
        m = [jnp.max(s_d[j], axis=0, keepdims=True) for j in ids]
        m = [jnp.maximum(m[j], jnp.max(s[j][:n[j]], axis=0, keepdims=True)) if n[j] else m[j]
             for j in ids]
        p = [jnp.exp2(s_d[j] - m[j]) for j in ids]
        p = [jnp.concatenate([jnp.exp2(s[j][:n[j]] - m[j]), p[j]], axis=0) if n[j] else p[j]
             for j in ids]
        l = [jnp.sum(p[j], axis=0, keepdims=True) for j in ids]
        acc = [jnp.dot(vt_ref[cols[j], 0:n[j] + TQ], p[j].astype(BF16),
                       preferred_element_type=F32) for j in ids]
        for j in ids:
            o12 = acc[j] / l[j]
            o = o12[:, :TQ] - lam * o12[:, TQ:]
            o = o * lax.rsqrt(jnp.mean(o * o, axis=0, keepdims=True) + EPS) * g_ref[...]
            o_ref[chains[j][0], :, cols[j]] = (o * (1.0 - LAMBDA_INIT)).T.astype(o_ref.dtype)

    for c in range(half_blocks):
        pl.when(i == c)(lambda c=c: block(c))


def _attention(proj3, vt, lq1, lk1, lq2, lk2, g_subln):
    b, s, width = proj3.shape
    vec = pl.BlockSpec((1, HEAD_DIM), lambda bi, i: (0, 0))
    halves = pl.BlockSpec((None, 2, TQ, ATTN_WIDTH), lambda bi, i: (bi, 0, i, 0))
    out = pl.pallas_call(
        _attn_kernel,
        grid=(b, s // TQ // 2),
        in_specs=[
            halves,
            pl.BlockSpec((None, s, ATTN_WIDTH), lambda bi, i: (bi, 0, 1)),
            pl.BlockSpec((ATTN_WIDTH, s), lambda bi, i: (0, bi)),
            vec, vec, vec, vec,
            pl.BlockSpec((HEAD_V, 1), lambda bi, i: (0, 0)),
        ],
        out_specs=halves,
        out_shape=jax.ShapeDtypeStruct((b, 2, s // 2, ATTN_WIDTH), BF16),
        compiler_params=_params(("parallel", "parallel")),
        name="attn",
    )(proj3.reshape(b, 2, s // 2, width), proj3, vt, lq1, lk1, lq2, lk2, g_subln)
    return out.reshape(b, s, ATTN_WIDTH)


def _conv_kernel(ca_ref, cb_ref, w_ref, b_ref, g_ref, beta_ref, o_ref, z_ref, stage_ref):
    s = ca_ref.shape[0]
    glu_rows = 256
    n_lg = CONV_WIDTH // LANES
    z_ref[:, 0:CONV_PAD, :] = jnp.zeros((n_lg, CONV_PAD, LANES), F32)

    def glu(c, carry):
        r0 = pl.multiple_of(c * glu_rows, glu_rows)
        a = ca_ref[pl.ds(r0, glu_rows), :].astype(F32)
        g = cb_ref[pl.ds(r0, glu_rows), :].astype(F32)
        z = a * jax.nn.sigmoid(g)
        for lg in range(n_lg):
            z_ref[lg, pl.ds(CONV_PAD + r0, glu_rows), :] = z[:, lg * LANES:(lg + 1) * LANES]
        return carry

    lax.fori_loop(0, s // glu_rows, glu, 0)

    def conv(c, carry):
        r0 = pl.multiple_of(c * CONV_ROWS, CONV_ROWS)
        groups = CONV_ROWS // 8
        for lg in range(n_lg):
            lanes = slice(lg * LANES, (lg + 1) * LANES)
            accs = [jnp.zeros((groups, LANES), F32) + b_ref[:, lanes] for _ in range(8)]
            for j in range(CONV_KERNEL):
                off = CONV_PAD - (CONV_KERNEL - 1) + j
                wj = w_ref[j:j + 1, lanes]
                for g in range(8):
                    accs[g] = accs[g] + wj * z_ref[lg, pl.ds(r0 + off + g, groups, stride=8), :]
            for g in range(8):
                stage_ref[lg, pl.ds(g, groups, stride=8), :] = accs[g]
        acc = jnp.concatenate([stage_ref[lg] for lg in range(n_lg)], axis=1)
        mu = jnp.mean(acc, axis=-1, keepdims=True)
        xc = acc - mu
        y = xc * lax.rsqrt(jnp.mean(xc * xc, axis=-1, keepdims=True) + EPS)
        y = y * g_ref[...] + beta_ref[...]
        o_ref[pl.ds(r0, CONV_ROWS), :] = (y * jax.nn.sigmoid(y)).astype(o_ref.dtype)
        return carry

    lax.fori_loop(0, s // CONV_ROWS, conv, 0, unroll=2)


def _conv(proj3, w_dw, b_dw, g_ln, b_ln):
    b, s, _ = proj3.shape
    vec = pl.BlockSpec((1, CONV_WIDTH), lambda bi: (0, 0))
    return pl.pallas_call(
        _conv_kernel,
        grid=(b,),
        in_specs=[
            pl.BlockSpec((None, s, CONV_WIDTH), lambda bi: (bi, 0, COL_CONV_A)),
            pl.BlockSpec((None, s, CONV_WIDTH), lambda bi: (bi, 0, COL_CONV_B)),
            pl.BlockSpec((CONV_KERNEL, CONV_WIDTH), lambda bi: (0, 0)),
            vec, vec, vec,
        ],
        out_specs=pl.BlockSpec((None, s, CONV_WIDTH), lambda bi: (bi, 0, 0)),
        out_shape=jax.ShapeDtypeStruct((b, s, CONV_WIDTH), BF16),
        scratch_shapes=[
            pltpu.VMEM((CONV_WIDTH // LANES, s + CONV_PAD, LANES), F32),
            pltpu.VMEM((CONV_WIDTH // LANES, CONV_ROWS, LANES), F32),
        ],
        compiler_params=_params(("parallel",)),
        name="conv",
    )(proj3, proj3, w_dw, b_dw, g_ln, b_ln)


def _split_bf16(a):
    hi = a.astype(BF16)
    lo = (a - hi.astype(F32)).astype(BF16)
    return hi, lo


def _mix_kernel(x_ref, o_ref, c_ref, ga_ref, gb_ref,
                wa_ref, wc_ref, wo_ref, gffn_ref, wr_ref, br_ref,
                h_ref, u_ref, gate_ref, slot_ref, cnt_ref):
    a = jnp.dot(o_ref[...], wa_ref[...], preferred_element_type=F32)
    b = jnp.dot(c_ref[...], wc_ref[...], preferred_element_type=F32)
    m = (jax.nn.sigmoid(ga_ref[...].astype(F32)) * a
         + jax.nn.sigmoid(gb_ref[...].astype(F32)) * b)
    acc = x_ref[...] + jnp.dot(m.astype(BF16), wo_ref[...], preferred_element_type=F32)
    h_ref[...] = acc
    u = _rms(acc, gffn_ref[...])
    _store_row_tiles(u_ref, u)

    u_hi, u_lo = _split_bf16(u)
    w_hi, w_lo = _split_bf16(wr_ref[...])
    logits = (lax.dot_general(w_hi, u_hi, _NT, preferred_element_type=F32)
              + lax.dot_general(w_hi, u_lo, _NT, preferred_element_type=F32)
              + lax.dot_general(w_lo, u_hi, _NT, preferred_element_type=F32)
              + br_ref[...])
    eidx = lax.broadcasted_iota(I32, logits.shape, 0)
    vals, idxs = [], []
    for _ in range(TOP_K):
        mx = jnp.max(logits, axis=0, keepdims=True)
        sel = jnp.min(jnp.where(logits == mx, eidx, N_EXPERTS), axis=0, keepdims=True)
        vals.append(mx)
        idxs.append(sel)
        logits = jnp.where(eidx == sel, -jnp.inf, logits)
    ex = [jnp.exp(v - vals[0]) for v in vals]
    den = ex[0] + ex[1] + ex[2] + ex[3]
    gate_ref[...] = jnp.concatenate([e / den for e in ex], axis=0)

    tm = logits.shape[1]
    onehot = [eidx == idxs[k] for k in range(TOP_K)]
    member = jnp.where(onehot[0] | onehot[1] | onehot[2] | onehot[3], 1.0, 0.0).astype(BF16)
    r = lax.broadcasted_iota(I32, (tm, tm), 0)
    c = lax.broadcasted_iota(I32, (tm, tm), 1)
    before = jnp.where(r < c, 1.0, 0.0).astype(BF16)
    rank = jnp.dot(member, before, preferred_element_type=F32)
    cnt_ref[...] = jnp.sum(member.astype(F32), axis=1, keepdims=True).astype(I32)
    cnt_row = lax.dot_general(jnp.ones((8, tm), BF16), member, _NT,
                              preferred_element_type=F32)[0:1]
    e_row = lax.broadcasted_iota(I32, (N_EXPERTS, N_EXPERTS), 0)
    e_col = lax.broadcasted_iota(I32, (N_EXPERTS, N_EXPERTS), 1)
    first = jnp.sum(jnp.where(e_col < e_row, cnt_row, 0.0), axis=1, keepdims=True)
    slot_ref[...] = jnp.concatenate(
        [jnp.sum(jnp.where(onehot[k], rank + first, 0.0), axis=0, keepdims=True)
         for k in range(TOP_K)], axis=0).astype(I32) * ROW_TILE


def _mix(xt, o, c, proj, wa, wc, wo, g_ffn, wr_t, br):
    n = xt.shape[0]
    tm = T_ROUTE
    row = lambda w, j: pl.BlockSpec((tm, w), lambda i, j=j: (i, j))
    full = lambda a: pl.BlockSpec(a.shape, lambda i: (0,) * a.ndim)
    return pl.pallas_call(
        _mix_kernel,
        grid=(n // tm,),
        in_specs=[
            row(D_MODEL, 0), row(ATTN_WIDTH, 0), row(CONV_WIDTH, 0),
            row(D_MODEL, COL_GATE_A // 2), row(D_MODEL, COL_GATE_B // 2),
            full(wa), full(wc), full(wo), full(g_ffn), full(wr_t), full(br),
        ],
        out_specs=[
            pl.BlockSpec((tm, D_MODEL), lambda i: (i, 0)),
            pl.BlockSpec((tm * ROW_TILE, LANES), lambda i: (i, 0)),
            pl.BlockSpec((None, TOP_K, tm), lambda i: (i, 0, 0)),
            pl.BlockSpec((None, TOP_K, tm), lambda i: (i, 0, 0)),
            pl.BlockSpec((None, N_EXPERTS, 1), lambda i: (i, 0, 0)),
        ],
        out_shape=[
            jax.ShapeDtypeStruct((n, D_MODEL), F32),
            jax.ShapeDtypeStruct((n * ROW_TILE, LANES), F32),
            jax.ShapeDtypeStruct((n // tm, TOP_K, tm), F32),
            jax.ShapeDtypeStruct((n // tm, TOP_K, tm), I32),
            jax.ShapeDtypeStruct((n // tm, N_EXPERTS, 1), I32),
        ],
        compiler_params=_params(("parallel",)),
        name="mix",
    )(xt, o, c, proj, proj, wa, wc, wo, g_ffn, wr_t, br)


def _run_copies(cnt, src_ref, src_row, dst_ref, dst_row, sem):
    def pieces(bits, src_row, dst_row):
        for bit in bits:
            size = 1 << bit
            piece = cnt & size

            @pl.when(piece != 0)
            def _(size=size, src_row=src_row, dst_row=dst_row):
                rows = size * ROW_TILE
                pltpu.make_async_copy(
                    src_ref.at[pl.ds(pl.multiple_of(src_row * ROW_TILE, ROW_TILE), rows), :],
                    dst_ref.at[pl.ds(pl.multiple_of(dst_row * ROW_TILE, ROW_TILE), rows), :],
                    sem).start()

            src_row = src_row + piece
            dst_row = dst_row + piece

    long_bits = RUN_LONG.bit_length() - 1

    @pl.when(cnt >= RUN_LONG)
    def _():
        pieces(reversed(range(long_bits, T_ROUTE.bit_length())), src_row, dst_row)

    head = cnt & -RUN_LONG
    pieces(reversed(range(long_bits)), src_row + head, dst_row + head)


def _dispatch_kernel(pstart_ref, pend_ref, cnt_ref, run_ref, slot_ref, u_ref, xs_ref,
                     grouped, zero_ref, sem, zsem):
    t = T_ROUTE
    i = pl.program_id(0)
    cur = i % 2
    blk = EXPERT_BLOCK * ROW_TILE
    buf = grouped.at[cur]

    def drain(slot):
        pltpu.make_async_copy(grouped.at[slot], grouped.at[slot], sem.at[slot]).wait()

    @pl.when(i == 0)
    def _():
        zero_ref[...] = jnp.zeros_like(zero_ref)

        def zero_block(first_row):
            return pltpu.make_async_copy(
                zero_ref, xs_ref.at[pl.ds(pl.multiple_of(first_row * ROW_TILE, blk), blk), :], zsem)

        for act in (lambda cp: cp.start(), lambda cp: cp.wait()):
            def tails(e, carry, act=act):
                @pl.when(pend_ref[e] > pstart_ref[e])
                def _():
                    act(zero_block(pend_ref[e] - EXPERT_BLOCK))
                return carry

            lax.fori_loop(0, N_EXPERTS, tails, 0)

            def unused(b, carry, act=act):
                act(zero_block(b * EXPERT_BLOCK))
                return carry

            lax.fori_loop(pend_ref[N_EXPERTS - 1] // EXPERT_BLOCK,
                          xs_ref.shape[0] // blk, unused, 0)

    @pl.when(i >= 2)
    def _():
        drain(cur)

    def group(tok, carry):
        tile = u_ref[pl.ds(pl.multiple_of(tok * ROW_TILE, ROW_TILE), ROW_TILE), :]
        for k in range(TOP_K):
            slot = pl.multiple_of(slot_ref[k * t + tok], ROW_TILE)
            buf[pl.ds(slot, ROW_TILE), :] = tile
        return carry

    lax.fori_loop(0, t, group, 0, unroll=DMA_UNROLL)

    def runs(e, off):
        cnt = cnt_ref[i * N_EXPERTS + e]
        _run_copies(cnt, buf, off, xs_ref, run_ref[i * N_EXPERTS + e], sem.at[cur])
        return off + cnt

    lax.fori_loop(0, N_EXPERTS, runs, 0)

    @pl.when(i == pl.num_programs(0) - 1)
    def _():
        drain(cur)

        @pl.when(i >= 1)
        def _():
            drain(1 - cur)


def _dispatch(pstart, pend, block_cnt, run_start, slot_t, u, n_rows):
    n = u.shape[0] // ROW_TILE
    t = T_ROUTE
    grid_spec = pltpu.PrefetchScalarGridSpec(
        num_scalar_prefetch=4,
        grid=(n // t,),
        in_specs=[
            pl.BlockSpec((TOP_K * t,), lambda i, *_: (i,), memory_space=pltpu.SMEM),
            pl.BlockSpec((t * ROW_TILE, LANES), lambda i, *_: (i, 0)),
        ],
        out_specs=pl.BlockSpec(memory_space=pl.ANY),
        scratch_shapes=[
            pltpu.VMEM((2, TOP_K * t * ROW_TILE, LANES), F32),
            pltpu.VMEM((EXPERT_BLOCK * ROW_TILE, LANES), F32),
            pltpu.SemaphoreType.DMA((2,)),
            pltpu.SemaphoreType.DMA(()),
        ],
    )
    return pl.pallas_call(
        _dispatch_kernel,
        grid_spec=grid_spec,
        out_shape=jax.ShapeDtypeStruct((n_rows * ROW_TILE, LANES), F32),
        compiler_params=_params(("arbitrary",)),
        name="dispatch",
    )(pstart, pend, block_cnt, run_start, slot_t, u)


def _expert_kernel(be_ref, nb_ref, x_ref, wgu_ref, bgu_ref, wd_ref, bd_ref, y_ref,
                   wgu_s, wd32_s, wd_s):
    i = pl.program_id(0)
    nb = nb_ref[0]
    blk = EXPERT_BLOCK
    half = EXPERT_FF // 2
    first = jnp.logical_or(i == 0, be_ref[i] != be_ref[jnp.maximum(i - 1, 0)])

    @pl.when(jnp.logical_and(first, i < nb))
    def _():
        rows = 256
        for r in range(0, D_MODEL, rows):
            wgu_s[r:r + rows, :] = wgu_ref[r:r + rows, :].astype(BF16)
        for cs in range(D_MODEL // LANES):
            lanes = slice(cs * LANES, (cs + 1) * LANES)
            wd32_s[cs, pl.ds(0, half, stride=2), :] = wd_ref[0:half, lanes]
            wd32_s[cs, pl.ds(1, half, stride=2), :] = wd_ref[half:EXPERT_FF, lanes]
            wd_s[:, lanes] = wd32_s[cs].astype(BF16)

    @pl.when(i < nb)
    def _():
        rows = blk // EXPERT_SPLIT
        parts = range(EXPERT_SPLIT)
        even = (lax.broadcasted_iota(I32, (rows, LANES), 1) % 2) == 0
        xb = [_load_row_tiles(x_ref, rows, first=s * rows).astype(BF16) for s in parts]
        gu1 = [jnp.dot(xb[s], wgu_s[:, :EXPERT_FF], preferred_element_type=F32)
               + bgu_ref[:, :EXPERT_FF] for s in parts]
        gu2 = [jnp.dot(xb[s], wgu_s[:, EXPERT_FF:], preferred_element_type=F32)
               + bgu_ref[:, EXPERT_FF:] for s in parts]
        act = []
        for s in parts:
            cols = []
            for v in range(EXPERT_FF // LANES):
                a = gu1[s][:, v * LANES:(v + 1) * LANES]
                b = gu2[s][:, v * LANES:(v + 1) * LANES]
                g = jnp.where(even, a, pltpu.roll(b, 1, axis=1))
                l = jnp.where(even, pltpu.roll(a, LANES - 1, axis=1), b)
                g = jnp.minimum(g, SWIGLU_LIMIT)
                l = jnp.clip(l, -SWIGLU_LIMIT, SWIGLU_LIMIT)
                cols.append(((l + 1.0) * (g * jax.nn.sigmoid(g * SWIGLU_ALPHA))).astype(BF16))
            act.append(jnp.concatenate(cols, axis=1))
        y = [jnp.dot(act[s], wd_s[...], preferred_element_type=F32) + bd_ref[...] for s in parts]
        for s in parts:
            _store_row_tiles(y_ref, y[s], first=s * rows)

    @pl.when(i >= nb)
    def _():
        y_ref[...] = jnp.zeros_like(y_ref)


def _experts(block_expert, n_used, xs, wgu, bgu, wd, bd):
    n_rows = xs.shape[0] // ROW_TILE
    blk = EXPERT_BLOCK
    wspec = lambda r, c: pl.BlockSpec((None, r, c), lambda i, be, nb: (be[i], 0, 0))
    grid_spec = pltpu.PrefetchScalarGridSpec(
        num_scalar_prefetch=2,
        grid=(n_rows // blk,),
        in_specs=[
            pl.BlockSpec((blk * ROW_TILE, LANES), lambda i, be, nb: (jnp.minimum(i, nb[0] - 1), 0)),
            wspec(D_MODEL, 2 * EXPERT_FF), wspec(1, 2 * EXPERT_FF),
            wspec(EXPERT_FF, D_MODEL), wspec(1, D_MODEL),
        ],
        out_specs=pl.BlockSpec((blk * ROW_TILE, LANES), lambda i, be, nb: (i, 0)),
        scratch_shapes=[
            pltpu.VMEM((D_MODEL, 2 * EXPERT_FF), BF16),
            pltpu.VMEM((D_MODEL // LANES, EXPERT_FF, LANES), F32),
            pltpu.VMEM((EXPERT_FF, D_MODEL), BF16),
        ],
    )
    return pl.pallas_call(
        _expert_kernel,
        grid_spec=grid_spec,
        out_shape=jax.ShapeDtypeStruct((n_rows * ROW_TILE, LANES), F32),
        compiler_params=_params(("arbitrary",)),
        name="expert",
    )(block_expert, n_used, xs, wgu, bgu, wd, bd)


def _combine_kernel(cnt_ref, run_ref, slot_ref, gsm_ref, h_ref, p_ref, y_ref, wple_ref, gple_ref,
                    wpg_ref, gfin_ref, o_ref, staged, moe_s, sem):
    t = T_ROUTE
    i = pl.program_id(0)
    cur = i % 2

    def fetch(block, slot):
        def runs(e, off):
            cnt = cnt_ref[block * N_EXPERTS + e]
            _run_copies(cnt, y_ref, run_ref[block * N_EXPERTS + e], staged.at[slot], off,
                        sem.at[slot])
            return off + cnt

        lax.fori_loop(0, N_EXPERTS, runs, 0)

    @pl.when(i == 0)
    def _():
        fetch(0, 0)

    @pl.when(i + 1 < pl.num_programs(0))
    def _():
        fetch(i + 1, 1 - cur)

    pltpu.make_async_copy(staged.at[cur], staged.at[cur], sem.at[cur]).wait()
    buf = staged.at[cur]

    def pick(tok, carry):
        acc = None
        for k in range(TOP_K):
            slot = pl.multiple_of(slot_ref[k * t + tok], ROW_TILE)
            row = buf[pl.ds(slot, ROW_TILE), :]
            term = gsm_ref[k * t + tok] * row
            acc = term if acc is None else acc + term
        moe_s[pl.ds(pl.multiple_of(tok * ROW_TILE, ROW_TILE), ROW_TILE), :] = acc
        return carry

    lax.fori_loop(0, t, pick, 0, unroll=DMA_UNROLL)

    rows = t // 2
    parts = [pl.ds(s * rows, rows) for s in range(2)]
    emb = [jnp.dot(p_ref[rs, :].astype(BF16), wple_ref[...], preferred_element_type=F32)
           for rs in parts]
    h = [h_ref[rs, :] + _load_row_tiles(moe_s, rows, first=s * rows) for s, rs in enumerate(parts)]
    r = [_rms(hs, gple_ref[...]).astype(BF16) for hs in h]
    sig = [jax.nn.sigmoid(jnp.dot(rs, wpg_ref[...], preferred_element_type=F32)) for rs in r]
    for s, rs in enumerate(parts):
        o_ref[rs, :] = _rms(h[s] + emb[s] * sig[s], gfin_ref[...])


def _combine(block_cnt, run_start, slot_t, gates_t, h, p2, y, wple, gple, wpg, gfin):
    n = h.shape[0]
    t = T_ROUTE
    full = lambda a: pl.BlockSpec(a.shape, lambda i, *_: (0,) * a.ndim)
    smem = lambda: pl.BlockSpec((TOP_K * t,), lambda i, *_: (i,), memory_space=pltpu.SMEM)
    grid_spec = pltpu.PrefetchScalarGridSpec(
        num_scalar_prefetch=2,
        grid=(n // t,),
        in_specs=[
            smem(), smem(),
            pl.BlockSpec((t, D_MODEL), lambda i, *_: (i, 0)),
            pl.BlockSpec((t, PLE_DIM), lambda i, *_: (i, 0)),
            pl.BlockSpec(memory_space=pl.ANY),
            full(wple), full(gple), full(wpg), full(gfin),
        ],
        out_specs=pl.BlockSpec((t, D_MODEL), lambda i, *_: (i, 0)),
        scratch_shapes=[
            pltpu.VMEM((2, TOP_K * t * ROW_TILE, LANES), F32),
            pltpu.VMEM((t * ROW_TILE, LANES), F32),
            pltpu.SemaphoreType.DMA((2,)),
        ],
    )
    return pl.pallas_call(
        _combine_kernel,
        grid_spec=grid_spec,
        out_shape=jax.ShapeDtypeStruct((n, D_MODEL), F32),
        compiler_params=_params(("arbitrary",)),
        name="combine",
    )(block_cnt, run_start, slot_t, gates_t, h, p2, y, wple, gple, wpg, gfin)


def kernel(x, p, g_mix, w_in, lambda_q1, lambda_k1, lambda_q2, lambda_k2, g_subln, w_attn_out,
           w_dw, b_dw, g_conv_ln, b_conv_ln, w_conv_out, w_o, g_ffn, w_router, b_router,
           w_gate_up, b_gate_up, w_down, b_down, w_ple, g_ple, w_ple_gate, g_final):
    b, s, d = x.shape
    n = b * s
    xt = x.reshape(n, d)
    vec = lambda a: a.reshape(1, -1)

    w_in0 = w_in[0]
    w_main = jnp.concatenate([w_in0[:, :2 * ATTN_WIDTH], w_in0[:, 3 * ATTN_WIDTH:]], axis=1)
    w_vt = w_in0[:, 2 * ATTN_WIDTH:3 * ATTN_WIDTH].T
    proj, vt = _inproj(xt, vec(g_mix[0]), w_main.astype(BF16), w_vt.astype(BF16))
    proj3 = proj.reshape(b, s, MAIN_COLS)
    attn = _attention(proj3, vt, vec(lambda_q1[0]), vec(lambda_k1[0]), vec(lambda_q2[0]),
                      vec(lambda_k2[0]), g_subln[0].reshape(HEAD_V, 1))
    conv = _conv(proj3, w_dw[0], vec(b_dw[0]), vec(g_conv_ln[0]), vec(b_conv_ln[0]))
    h1, u2, gates_t, slot_t, block_cnt = _mix(
        xt, attn.reshape(n, ATTN_WIDTH), conv.reshape(n, CONV_WIDTH), proj,
        w_attn_out[0].astype(BF16), w_conv_out[0].astype(BF16), w_o[0].astype(BF16),
        vec(g_ffn[0]), w_router[0].T, b_router[0].reshape(N_EXPERTS, 1))

    block_cnt = block_cnt[:, :, 0]
    counts = jnp.sum(block_cnt, axis=0)
    blk = EXPERT_BLOCK
    n_blocks = n * TOP_K // blk + N_EXPERTS
    padded = (counts + blk - 1) // blk * blk
    pend = jnp.cumsum(padded).astype(I32)
    pstart = pend - padded
    n_used = (pend[-1:] // blk).astype(I32)
    block_start = jnp.arange(n_blocks, dtype=I32) * blk
    block_expert = jnp.minimum(
        jnp.sum((pend[None, :] <= block_start[:, None]).astype(I32), axis=1), N_EXPERTS - 1)
    run_start = (pstart[None, :] + jnp.cumsum(block_cnt, axis=0) - block_cnt).astype(I32)
    cnt_flat = block_cnt.reshape(-1).astype(I32)
    run_flat = run_start.reshape(-1)

    slot_flat = slot_t.reshape(-1)
    gate_flat = gates_t.reshape(-1)

    xs = _dispatch(pstart, pend, cnt_flat, run_flat, slot_flat, u2, n_blocks * blk)
    y = _experts(block_expert, n_used, xs, w_gate_up[0], b_gate_up[0][:, None, :],
                 w_down[0], b_down[0][:, None, :])

    out = _combine(cnt_flat, run_flat, slot_flat, gate_flat, h1, p[0].reshape(n, PLE_DIM), y,
                   w_ple[0].astype(BF16), vec(g_ple[0]), w_ple_gate[0].astype(BF16), vec(g_final))
    return out.reshape(b, s, d)
```

```python
import jax
import jax.numpy as jnp
from jax import lax
from jax.experimental import pallas as pl
from jax.experimental.pallas import tpu as pltpu

F32 = jnp.float32
BF16 = jnp.bfloat16
I32 = jnp.int32

D_MODEL = 1024
ATTN_WIDTH = 512
DIFF_HEADS = 4
HEAD_DIM = 64
HEAD_V = 2 * HEAD_DIM
CONV_WIDTH = 512
CONV_KERNEL = 31
N_EXPERTS = 32
TOP_K = 4
EXPERT_FF = 1024
PLE_DIM = 256
SWIGLU_ALPHA = 1.702
SWIGLU_LIMIT = 7.0
EPS = 1e-5
IN_COLS = 3 * ATTN_WIDTH + 2 * CONV_WIDTH + 2 * D_MODEL
LAMBDA_INIT = 0.2
LOG2_E = 1.4426950408889634

MAIN_COLS = IN_COLS - ATTN_WIDTH
COL_CONV_A = 2
COL_CONV_B = 3
COL_GATE_A = 4
COL_GATE_B = 6

LANES = 128
TM_PROJ = 512
TQ = 256
T_ROUTE = 512
EXPERT_BLOCK = 512
EXPERT_SPLIT = 2
CONV_ROWS = 64
CONV_PAD = 32
RUN_LONG = 128
DMA_UNROLL = 8
VMEM_LIMIT = 56 * 1024 * 1024

_NT = (((1,), (1,)), ((), ()))


def _params(sem, vmem=VMEM_LIMIT):
    return pltpu.CompilerParams(dimension_semantics=sem, vmem_limit_bytes=vmem)


def _rms(x, g):
    return x * lax.rsqrt(jnp.mean(x * x, axis=-1, keepdims=True) + EPS) * g


ROW_TILE = D_MODEL // LANES


def _store_row_tiles(ref, x, lead=(), first=0):
    rows = x.shape[0]
    for g in range(ROW_TILE):
        ref[lead + (pl.ds(first * ROW_TILE + g, rows, stride=ROW_TILE), slice(None))] = (
            x[:, g * LANES:(g + 1) * LANES])


def _load_row_tiles(ref, rows, lead=(), first=0):
    return jnp.concatenate(
        [ref[lead + (pl.ds(first * ROW_TILE + g, rows, stride=ROW_TILE), slice(None))]
         for g in range(ROW_TILE)], axis=1)


def _inproj_kernel(x_ref, g_ref, w_ref, wvt_ref, o_ref, vt_ref):
    u = _rms(x_ref[...], g_ref[...]).astype(BF16)
    ch = 512
    for c in range(MAIN_COLS // ch):
        src = c * ch if c < 2 * ATTN_WIDTH // ch else c * ch + ATTN_WIDTH
        r = jnp.dot(u, w_ref[:, src:src + ch], preferred_element_type=F32)
        if c == 0:
            r = r * (HEAD_DIM ** -0.5 * LOG2_E)
        o_ref[:, c * ch:(c + 1) * ch] = r.astype(BF16)
    vt_ref[...] = lax.dot_general(wvt_ref[...], u, _NT,
                                  preferred_element_type=F32).astype(BF16)


def _inproj(xt, g, w, wvt):
    n = xt.shape[0]
    return pl.pallas_call(
        _inproj_kernel,
        grid=(n // TM_PROJ,),
        in_specs=[
            pl.BlockSpec((TM_PROJ, D_MODEL), lambda i: (i, 0)),
            pl.BlockSpec((1, D_MODEL), lambda i: (0, 0)),
            pl.BlockSpec((D_MODEL, IN_COLS), lambda i: (0, 0)),
            pl.BlockSpec((ATTN_WIDTH, D_MODEL), lambda i: (0, 0)),
        ],
        out_specs=[
            pl.BlockSpec((TM_PROJ, MAIN_COLS), lambda i: (i, 0)),
            pl.BlockSpec((ATTN_WIDTH, TM_PROJ), lambda i: (0, i)),
        ],
        out_shape=[
            jax.ShapeDtypeStruct((n, MAIN_COLS), BF16),
            jax.ShapeDtypeStruct((ATTN_WIDTH, n), BF16),
        ],
        compiler_params=_params(("parallel",)),
        name="inproj",
    )(xt, g, w, wvt)


def _attn_kernel(q_ref, k_ref, vt_ref, lq1_ref, lk1_ref, lq2_ref, lk2_ref, g_ref, o_ref):
    i = pl.program_id(1)
    s_len = k_ref.shape[0]
    lane = lax.broadcasted_iota(I32, (TQ, HEAD_V), 1)
    key = lax.broadcasted_iota(I32, (TQ, 2 * TQ), 0)
    qry = lax.broadcasted_iota(I32, (TQ, 2 * TQ), 1)
    causal = key <= jnp.where(qry >= TQ, qry - TQ, qry)
    lam = (jnp.exp(jnp.sum(lq1_ref[...] * lk1_ref[...]))
           - jnp.exp(jnp.sum(lq2_ref[...] * lk2_ref[...])) + LAMBDA_INIT)

    half_blocks = s_len // TQ // 2

    def block(c):
        chains = [(half, h) for half in range(2) for h in range(DIFF_HEADS)]
        cols = [slice(h * HEAD_V, (h + 1) * HEAD_V) for _, h in chains]
        n = [(c + half * half_blocks) * TQ for half, _ in chains]
        ids = range(len(chains))

        def stacked_q(j):
            q = q_ref[chains[j][0], :, cols[j]]
            zero = jnp.zeros_like(q)
            return jnp.concatenate([jnp.where(lane < HEAD_DIM, q, zero),
                                    jnp.where(lane >= HEAD_DIM, q, zero)], axis=0)

        s = [lax.dot_general(k_ref[0:n[j] + TQ, cols[j]], stacked_q(j), _NT,
                             preferred_element_type=F32) for j in ids]
        s_d = [jnp.where(causal, s[j][n[j]:], -1e30) for j in ids]
        m = [jnp.max(s_d[j], axis=0, keepdims=True) for j in ids]
        m = [jnp.maximum(m[j], jnp.max(s[j][:n[j]], axis=0, keepdims=True)) if n[j] else m[j]
             for j in ids]
        p = [jnp.exp2(s_d[j] - m[j]) for j in ids]
        p = [jnp.concatenate([jnp.exp2(s[j][:n[j]] - m[j]), p[j]], axis=0) if n[j] else p[j]
             for j in ids]
        l = [jnp.sum(p[j], axis=0, keepdims=True) for j in ids]
        acc = [jnp.dot(vt_ref[cols[j], 0:n[j] + TQ], p[j].astype(BF16),
                       preferred_element_type=F32) for j in ids]
        for j in ids:
            o12 = acc[j] / l[j]
            o = o12[:, :TQ] - lam * o12[:, TQ:]
            o = o * lax.rsqrt(jnp.mean(o * o, axis=0, keepdims=True) + EPS) * g_ref[...]
            o_ref[chains[j][0], :, cols[j]] = (o * (1.0 - LAMBDA_INIT)).T.astype(o_ref.dtype)

    for c in range(half_blocks):
        pl.when(i == c)(lambda c=c: block(c))


def _attention(proj3, vt, lq1, lk1, lq2, lk2, g_subln):
    b, s, width = proj3.shape
    vec = pl.BlockSpec((1, HEAD_DIM), lambda bi, i: (0, 0))
    halves = pl.BlockSpec((None, 2, TQ, ATTN_WIDTH), lambda bi, i: (bi, 0, i, 0))
    out = pl.pallas_call(
        _attn_kernel,
        grid=(b, s // TQ // 2),
        in_specs=[
            halves,
            pl.BlockSpec((None, s, ATTN_WIDTH), lambda bi, i: (bi, 0, 1)),
            pl.BlockSpec((ATTN_WIDTH, s), lambda bi, i: (0, bi)),
            vec, vec, vec, vec,
            pl.BlockSpec((HEAD_V, 1), lambda bi, i: (0, 0)),
        ],
        out_specs=halves,
        out_shape=jax.ShapeDtypeStruct((b, 2, s // 2, ATTN_WIDTH), BF16),
        compiler_params=_params(("parallel", "parallel")),
        name="attn",
    )(proj3.reshape(b, 2, s // 2, width), proj3, vt, lq1, lk1, lq2, lk2, g_subln)
    return out.reshape(b, s, ATTN_WIDTH)


def _conv_kernel(ca_ref, cb_ref, w_ref, b_ref, g_ref, beta_ref, o_ref, z_ref, stage_ref):
    s = ca_ref.shape[0]
    glu_rows = 256
    n_lg = CONV_WIDTH // LANES
    z_ref[:, 0:CONV_PAD, :] = jnp.zeros((n_lg, CONV_PAD, LANES), F32)

    def glu(c, carry):
        r0 = pl.multiple_of(c * glu_rows, glu_rows)
        a = ca_ref[pl.ds(r0, glu_rows), :].astype(F32)
        g = cb_ref[pl.ds(r0, glu_rows), :].astype(F32)
        z = a * jax.nn.sigmoid(g)
        for lg in range(n_lg):
            z_ref[lg, pl.ds(CONV_PAD + r0, glu_rows), :] = z[:, lg * LANES:(lg + 1) * LANES]
        return carry

    lax.fori_loop(0, s // glu_rows, glu, 0)

    def conv(c, carry):
        r0 = pl.multiple_of(c * CONV_ROWS, CONV_ROWS)
        groups = CONV_ROWS // 8
        for lg in range(n_lg):
            lanes = slice(lg * LANES, (lg + 1) * LANES)
            accs = [jnp.zeros((groups, LANES), F32) + b_ref[:, lanes] for _ in range(8)]
            for j in range(CONV_KERNEL):
                off = CONV_PAD - (CONV_KERNEL - 1) + j
                wj = w_ref[j:j + 1, lanes]
                for g in range(8):
                    accs[g] = accs[g] + wj * z_ref[lg, pl.ds(r0 + off + g, groups, stride=8), :]
            for g in range(8):
                stage_ref[lg, pl.ds(g, groups, stride=8), :] = accs[g]
        acc = jnp.concatenate([stage_ref[lg] for lg in range(n_lg)], axis=1)
        mu = jnp.mean(acc, axis=-1, keepdims=True)
        xc = acc - mu
        y = xc * lax.rsqrt(jnp.mean(xc * xc, axis=-1, keepdims=True) + EPS)
        y = y * g_ref[...] + beta_ref[...]
        o_ref[pl.ds(r0, CONV_ROWS), :] = (y * jax.nn.sigmoid(y)).astype(o_ref.dtype)
        return carry

    lax.fori_loop(0, s // CONV_ROWS, conv, 0, unroll=2)


def _conv(proj3, w_dw, b_dw, g_ln, b_ln):
    b, s, _ = proj3.shape
    vec = pl.BlockSpec((1, CONV_WIDTH), lambda bi: (0, 0))
    return pl.pallas_call(
        _conv_kernel,
        grid=(b,),
        in_specs=[
            pl.BlockSpec((None, s, CONV_WIDTH), lambda bi: (bi, 0, COL_CONV_A)),
            pl.BlockSpec((None, s, CONV_WIDTH), lambda bi: (bi, 0, COL_CONV_B)),
            pl.BlockSpec((CONV_KERNEL, CONV_WIDTH), lambda bi: (0, 0)),
            vec, vec, vec,
        ],
        out_specs=pl.BlockSpec((None, s, CONV_WIDTH), lambda bi: (bi, 0, 0)),
        out_shape=jax.ShapeDtypeStruct((b, s, CONV_WIDTH), BF16),
        scratch_shapes=[
            pltpu.VMEM((CONV_WIDTH // LANES, s + CONV_PAD, LANES), F32),
            pltpu.VMEM((CONV_WIDTH // LANES, CONV_ROWS, LANES), F32),
        ],
        compiler_params=_params(("parallel",)),
        name="conv",
    )(proj3, proj3, w_dw, b_dw, g_ln, b_ln)


def _split_bf16(a):
    hi = a.astype(BF16)
    lo = (a - hi.astype(F32)).astype(BF16)
    return hi, lo


def _mix_kernel(x_ref, o_ref, c_ref, ga_ref, gb_ref,
                wa_ref, wc_ref, wo_ref, gffn_ref, wr_ref, br_ref, before_ref,
                h_ref, u_ref, gate_ref, slot_ref, cnt_ref):
    a = jnp.dot(o_ref[...], wa_ref[...], preferred_element_type=F32)
    b = jnp.dot(c_ref[...], wc_ref[...], preferred_element_type=F32)
    m = (jax.nn.sigmoid(ga_ref[...].astype(F32)) * a
         + jax.nn.sigmoid(gb_ref[...].astype(F32)) * b)
    acc = x_ref[...] + jnp.dot(m.astype(BF16), wo_ref[...], preferred_element_type=F32)
    h_ref[...] = acc
    u = _rms(acc, gffn_ref[...])
    _store_row_tiles(u_ref, u)

    u_hi, u_lo = _split_bf16(u)
    w_hi, w_lo = _split_bf16(wr_ref[...])
    logits = (lax.dot_general(w_hi, u_hi, _NT, preferred_element_type=F32)
              + lax.dot_general(w_hi, u_lo, _NT, preferred_element_type=F32)
              + lax.dot_general(w_lo, u_hi, _NT, preferred_element_type=F32)
              + br_ref[...])
    eidx = lax.broadcasted_iota(I32, logits.shape, 0)
    vals, idxs = [], []
    for _ in range(TOP_K):
        mx = jnp.max(logits, axis=0, keepdims=True)
        sel = jnp.min(jnp.where(logits == mx, eidx, N_EXPERTS), axis=0, keepdims=True)
        vals.append(mx)
        idxs.append(sel)
        logits = jnp.where(eidx == sel, -jnp.inf, logits)
    ex = [jnp.exp(v - vals[0]) for v in vals]
    den = ex[0] + ex[1] + ex[2] + ex[3]
    gate_ref[...] = jnp.concatenate([e / den for e in ex], axis=0)

    tm = logits.shape[1]
    onehot = [eidx == idxs[k] for k in range(TOP_K)]
    member = jnp.where(onehot[0] | onehot[1] | onehot[2] | onehot[3], 1.0, 0.0).astype(BF16)
    rank = jnp.dot(member, before_ref[...], preferred_element_type=F32)
    cnt_ref[...] = jnp.sum(member.astype(F32), axis=1, keepdims=True).astype(I32)
    cnt_row = lax.dot_general(jnp.ones((8, tm), BF16), member, _NT,
                              preferred_element_type=F32)[0:1]
    e_row = lax.broadcasted_iota(I32, (N_EXPERTS, N_EXPERTS), 0)
    e_col = lax.broadcasted_iota(I32, (N_EXPERTS, N_EXPERTS), 1)
    first = jnp.sum(jnp.where(e_col < e_row, cnt_row, 0.0), axis=1, keepdims=True)
    slot_ref[...] = jnp.concatenate(
        [jnp.sum(jnp.where(onehot[k], rank + first, 0.0), axis=0, keepdims=True)
         for k in range(TOP_K)], axis=0).astype(I32) * ROW_TILE


def _mix(xt, o, c, proj, wa, wc, wo, g_ffn, wr_t, br):
    n = xt.shape[0]
    tm = T_ROUTE
    pos = jnp.arange(tm, dtype=I32)
    before = (pos[:, None] < pos[None, :]).astype(BF16)
    row = lambda w, j: pl.BlockSpec((tm, w), lambda i, j=j: (i, j))
    full = lambda a: pl.BlockSpec(a.shape, lambda i: (0,) * a.ndim)
    return pl.pallas_call(
        _mix_kernel,
        grid=(n // tm,),
        in_specs=[
            row(D_MODEL, 0), row(ATTN_WIDTH, 0), row(CONV_WIDTH, 0),
            row(D_MODEL, COL_GATE_A // 2), row(D_MODEL, COL_GATE_B // 2),
            full(wa), full(wc), full(wo), full(g_ffn), full(wr_t), full(br), full(before),
        ],
        out_specs=[
            pl.BlockSpec((tm, D_MODEL), lambda i: (i, 0)),
            pl.BlockSpec((tm * ROW_TILE, LANES), lambda i: (i, 0)),
            pl.BlockSpec((None, TOP_K, tm), lambda i: (i, 0, 0)),
            pl.BlockSpec((None, TOP_K, tm), lambda i: (i, 0, 0)),
            pl.BlockSpec((None, N_EXPERTS, 1), lambda i: (i, 0, 0)),
        ],
        out_shape=[
            jax.ShapeDtypeStruct((n, D_MODEL), F32),
            jax.ShapeDtypeStruct((n * ROW_TILE, LANES), F32),
            jax.ShapeDtypeStruct((n // tm, TOP_K, tm), F32),
            jax.ShapeDtypeStruct((n // tm, TOP_K, tm), I32),
            jax.ShapeDtypeStruct((n // tm, N_EXPERTS, 1), I32),
        ],
        compiler_params=_params(("parallel",)),
        name="mix",
    )(xt, o, c, proj, proj, wa, wc, wo, g_ffn, wr_t, br, before)


def _run_copies(cnt, src_ref, src_row, dst_ref, dst_row, sem):
    def pieces(bits, src_row, dst_row):
        for bit in bits:
            size = 1 << bit
            piece = cnt & size

            @pl.when(piece != 0)
            def _(size=size, src_row=src_row, dst_row=dst_row):
                rows = size * ROW_TILE
                pltpu.make_async_copy(
                    src_ref.at[pl.ds(pl.multiple_of(src_row * ROW_TILE, ROW_TILE), rows), :],
                    dst_ref.at[pl.ds(pl.multiple_of(dst_row * ROW_TILE, ROW_TILE), rows), :],
                    sem).start()

            src_row = src_row + piece
            dst_row = dst_row + piece

    long_bits = RUN_LONG.bit_length() - 1

    @pl.when(cnt >= RUN_LONG)
    def _():
        pieces(reversed(range(long_bits, T_ROUTE.bit_length())), src_row, dst_row)

    head = cnt & -RUN_LONG
    pieces(reversed(range(long_bits)), src_row + head, dst_row + head)


def _dispatch_kernel(pstart_ref, pend_ref, cnt_ref, run_ref, slot_ref, u_ref, xs_ref,
                     grouped, zero_ref, sem, zsem):
    t = T_ROUTE
    i = pl.program_id(0)
    cur = i % 2
    blk = EXPERT_BLOCK * ROW_TILE
    buf = grouped.at[cur]

    def drain(slot):
        pltpu.make_async_copy(grouped.at[slot], grouped.at[slot], sem.at[slot]).wait()

    @pl.when(i == 0)
    def _():
        zero_ref[...] = jnp.zeros_like(zero_ref)

        def zero_block(first_row):
            return pltpu.make_async_copy(
                zero_ref, xs_ref.at[pl.ds(pl.multiple_of(first_row * ROW_TILE, blk), blk), :], zsem)

        for act in (lambda cp: cp.start(), lambda cp: cp.wait()):
            def tails(e, carry, act=act):
                @pl.when(pend_ref[e] > pstart_ref[e])
                def _():
                    act(zero_block(pend_ref[e] - EXPERT_BLOCK))
                return carry

            lax.fori_loop(0, N_EXPERTS, tails, 0)

            def unused(b, carry, act=act):
                act(zero_block(b * EXPERT_BLOCK))
                return carry

            lax.fori_loop(pend_ref[N_EXPERTS - 1] // EXPERT_BLOCK,
                          xs_ref.shape[0] // blk, unused, 0)

    @pl.when(i >= 2)
    def _():
        drain(cur)

    def group(tok, carry):
        tile = u_ref[pl.ds(pl.multiple_of(tok * ROW_TILE, ROW_TILE), ROW_TILE), :]
        for k in range(TOP_K):
            slot = pl.multiple_of(slot_ref[k * t + tok], ROW_TILE)
            buf[pl.ds(slot, ROW_TILE), :] = tile
        return carry

    lax.fori_loop(0, t, group, 0, unroll=DMA_UNROLL)

    def runs(e, off):
        cnt = cnt_ref[i * N_EXPERTS + e]
        _run_copies(cnt, buf, off, xs_ref, run_ref[i * N_EXPERTS + e], sem.at[cur])
        return off + cnt

    lax.fori_loop(0, N_EXPERTS, runs, 0)

    @pl.when(i == pl.num_programs(0) - 1)
    def _():
        drain(cur)

        @pl.when(i >= 1)
        def _():
            drain(1 - cur)


def _dispatch(pstart, pend, block_cnt, run_start, slot_t, u, n_rows):
    n = u.shape[0] // ROW_TILE
    t = T_ROUTE
    grid_spec = pltpu.PrefetchScalarGridSpec(
        num_scalar_prefetch=4,
        grid=(n // t,),
        in_specs=[
            pl.BlockSpec((TOP_K * t,), lambda i, *_: (i,), memory_space=pltpu.SMEM),
            pl.BlockSpec((t * ROW_TILE, LANES), lambda i, *_: (i, 0)),
        ],
        out_specs=pl.BlockSpec(memory_space=pl.ANY),
        scratch_shapes=[
            pltpu.VMEM((2, TOP_K * t * ROW_TILE, LANES), F32),
            pltpu.VMEM((EXPERT_BLOCK * ROW_TILE, LANES), F32),
            pltpu.SemaphoreType.DMA((2,)),
            pltpu.SemaphoreType.DMA(()),
        ],
    )
    return pl.pallas_call(
        _dispatch_kernel,
        grid_spec=grid_spec,
        out_shape=jax.ShapeDtypeStruct((n_rows * ROW_TILE, LANES), F32),
        compiler_params=_params(("arbitrary",)),
        name="dispatch",
    )(pstart, pend, block_cnt, run_start, slot_t, u)


def _expert_kernel(be_ref, nb_ref, x_ref, wgu_ref, bgu_ref, wd_ref, bd_ref, y_ref,
                   wgu_s, wd32_s, wd_s):
    i = pl.program_id(0)
    nb = nb_ref[0]
    blk = EXPERT_BLOCK
    half = EXPERT_FF // 2
    first = jnp.logical_or(i == 0, be_ref[i] != be_ref[jnp.maximum(i - 1, 0)])

    @pl.when(jnp.logical_and(first, i < nb))
    def _():
        rows = 256
        for r in range(0, D_MODEL, rows):
            wgu_s[r:r + rows, :] = wgu_ref[r:r + rows, :].astype(BF16)
        for cs in range(D_MODEL // LANES):
            lanes = slice(cs * LANES, (cs + 1) * LANES)
            wd32_s[cs, pl.ds(0, half, stride=2), :] = wd_ref[0:half, lanes]
            wd32_s[cs, pl.ds(1, half, stride=2), :] = wd_ref[half:EXPERT_FF, lanes]
            wd_s[:, lanes] = wd32_s[cs].astype(BF16)

    @pl.when(i < nb)
    def _():
        rows = blk // EXPERT_SPLIT
        parts = range(EXPERT_SPLIT)
        even = (lax.broadcasted_iota(I32, (rows, LANES), 1) % 2) == 0
        xb = [_load_row_tiles(x_ref, rows, first=s * rows).astype(BF16) for s in parts]
        gu1 = [jnp.dot(xb[s], wgu_s[:, :EXPERT_FF], preferred_element_type=F32)
               + bgu_ref[:, :EXPERT_FF] for s in parts]
        gu2 = [jnp.dot(xb[s], wgu_s[:, EXPERT_FF:], preferred_element_type=F32)
               + bgu_ref[:, EXPERT_FF:] for s in parts]
        act = []
        for s in parts:
            cols = []
            for v in range(EXPERT_FF // LANES):
                a = gu1[s][:, v * LANES:(v + 1) * LANES]
                b = gu2[s][:, v * LANES:(v + 1) * LANES]
                g = jnp.where(even, a, pltpu.roll(b, 1, axis=1))
                l = jnp.where(even, pltpu.roll(a, LANES - 1, axis=1), b)
                g = jnp.minimum(g, SWIGLU_LIMIT)
                l = jnp.clip(l, -SWIGLU_LIMIT, SWIGLU_LIMIT)
                cols.append(((l + 1.0) * (g * jax.nn.sigmoid(g * SWIGLU_ALPHA))).astype(BF16))
            act.append(jnp.concatenate(cols, axis=1))
        y = [jnp.dot(act[s], wd_s[...], preferred_element_type=F32) + bd_ref[...] for s in parts]
        for s in parts:
            _store_row_tiles(y_ref, y[s], first=s * rows)

    @pl.when(i >= nb)
    def _():
        y_ref[...] = jnp.zeros_like(y_ref)


def _experts(block_expert, n_used, xs, wgu, bgu, wd, bd):
    n_rows = xs.shape[0] // ROW_TILE
    blk = EXPERT_BLOCK
    wspec = lambda r, c: pl.BlockSpec((None, r, c), lambda i, be, nb: (be[i], 0, 0))
    grid_spec = pltpu.PrefetchScalarGridSpec(
        num_scalar_prefetch=2,
        grid=(n_rows // blk,),
        in_specs=[
            pl.BlockSpec((blk * ROW_TILE, LANES), lambda i, be, nb: (jnp.minimum(i, nb[0] - 1), 0)),
            wspec(D_MODEL, 2 * EXPERT_FF), wspec(1, 2 * EXPERT_FF),
            wspec(EXPERT_FF, D_MODEL), wspec(1, D_MODEL),
        ],
        out_specs=pl.BlockSpec((blk * ROW_TILE, LANES), lambda i, be, nb: (i, 0)),
        scratch_shapes=[
            pltpu.VMEM((D_MODEL, 2 * EXPERT_FF), BF16),
            pltpu.VMEM((D_MODEL // LANES, EXPERT_FF, LANES), F32),
            pltpu.VMEM((EXPERT_FF, D_MODEL), BF16),
        ],
    )
    return pl.pallas_call(
        _expert_kernel,
        grid_spec=grid_spec,
        out_shape=jax.ShapeDtypeStruct((n_rows * ROW_TILE, LANES), F32),
        compiler_params=_params(("arbitrary",)),
        name="expert",
    )(block_expert, n_used, xs, wgu, bgu, wd, bd)


def _combine_kernel(cnt_ref, run_ref, slot_ref, gsm_ref, h_ref, p_ref, y_ref, wple_ref, gple_ref,
                    wpg_ref, gfin_ref, o_ref, staged, moe_s, sem):
    t = T_ROUTE
    i = pl.program_id(0)
    cur = i % 2

    def fetch(block, slot):
        def runs(e, off):
            cnt = cnt_ref[block * N_EXPERTS + e]
            _run_copies(cnt, y_ref, run_ref[block * N_EXPERTS + e], staged.at[slot], off,
                        sem.at[slot])
            return off + cnt

        lax.fori_loop(0, N_EXPERTS, runs, 0)

    @pl.when(i == 0)
    def _():
        fetch(0, 0)

    @pl.when(i + 1 < pl.num_programs(0))
    def _():
        fetch(i + 1, 1 - cur)

    pltpu.make_async_copy(staged.at[cur], staged.at[cur], sem.at[cur]).wait()
    buf = staged.at[cur]

    def pick(tok, carry):
        acc = None
        for k in range(TOP_K):
            slot = pl.multiple_of(slot_ref[k * t + tok], ROW_TILE)
            row = buf[pl.ds(slot, ROW_TILE), :]
            term = gsm_ref[k * t + tok] * row
            acc = term if acc is None else acc + term
        moe_s[pl.ds(pl.multiple_of(tok * ROW_TILE, ROW_TILE), ROW_TILE), :] = acc
        return carry

    lax.fori_loop(0, t, pick, 0, unroll=DMA_UNROLL)

    rows = t // 2
    parts = [pl.ds(s * rows, rows) for s in range(2)]
    emb = [jnp.dot(p_ref[rs, :].astype(BF16), wple_ref[...], preferred_element_type=F32)
           for rs in parts]
    h = [h_ref[rs, :] + _load_row_tiles(moe_s, rows, first=s * rows) for s, rs in enumerate(parts)]
    r = [_rms(hs, gple_ref[...]).astype(BF16) for hs in h]
    sig = [jax.nn.sigmoid(jnp.dot(rs, wpg_ref[...], preferred_element_type=F32)) for rs in r]
    for s, rs in enumerate(parts):
        o_ref[rs, :] = _rms(h[s] + emb[s] * sig[s], gfin_ref[...])


def _combine(block_cnt, run_start, slot_t, gates_t, h, p2, y, wple, gple, wpg, gfin):
    n = h.shape[0]
    t = T_ROUTE
    full = lambda a: pl.BlockSpec(a.shape, lambda i, *_: (0,) * a.ndim)
    smem = lambda: pl.BlockSpec((TOP_K * t,), lambda i, *_: (i,), memory_space=pltpu.SMEM)
    grid_spec = pltpu.PrefetchScalarGridSpec(
        num_scalar_prefetch=2,
        grid=(n // t,),
        in_specs=[
            smem(), smem(),
            pl.BlockSpec((t, D_MODEL), lambda i, *_: (i, 0)),
            pl.BlockSpec((t, PLE_DIM), lambda i, *_: (i, 0)),
            pl.BlockSpec(memory_space=pl.ANY),
            full(wple), full(gple), full(wpg), full(gfin),
        ],
        out_specs=pl.BlockSpec((t, D_MODEL), lambda i, *_: (i, 0)),
        scratch_shapes=[
            pltpu.VMEM((2, TOP_K * t * ROW_TILE, LANES), F32),
            pltpu.VMEM((t * ROW_TILE, LANES), F32),
            pltpu.SemaphoreType.DMA((2,)),
        ],
    )
    return pl.pallas_call(
        _combine_kernel,
        grid_spec=grid_spec,
        out_shape=jax.ShapeDtypeStruct((n, D_MODEL), F32),
        compiler_params=_params(("arbitrary",)),
        name="combine",
    )(block_cnt, run_start, slot_t, gates_t, h, p2, y, wple, gple, wpg, gfin)


def kernel(x, p, g_mix, w_in, lambda_q1, lambda_k1, lambda_q2, lambda_k2, g_subln, w_attn_out,
           w_dw, b_dw, g_conv_ln, b_conv_ln, w_conv_out, w_o, g_ffn, w_router, b_router,
           w_gate_up, b_gate_up, w_down, b_down, w_ple, g_ple, w_ple_gate, g_final):
    b, s, d = x.shape
    n = b * s
    xt = x.reshape(n, d)
    vec = lambda a: a.reshape(1, -1)

    w_in0 = w_in[0].astype(BF16)
    w_vt = w_in0[:, 2 * ATTN_WIDTH:3 * ATTN_WIDTH].T
    proj, vt = _inproj(xt, vec(g_mix[0]), w_in0, w_vt)
    proj3 = proj.reshape(b, s, MAIN_COLS)
    attn = _attention(proj3, vt, vec(lambda_q1[0]), vec(lambda_k1[0]), vec(lambda_q2[0]),
                      vec(lambda_k2[0]), g_subln[0].reshape(HEAD_V, 1))
    conv = _conv(proj3, w_dw[0], vec(b_dw[0]), vec(g_conv_ln[0]), vec(b_conv_ln[0]))
    h1, u2, gates_t, slot_t, block_cnt = _mix(
        xt, attn.reshape(n, ATTN_WIDTH), conv.reshape(n, CONV_WIDTH), proj,
        w_attn_out[0].astype(BF16), w_conv_out[0].astype(BF16), w_o[0].astype(BF16),
        vec(g_ffn[0]), w_router[0].T, b_router[0].reshape(N_EXPERTS, 1))

    block_cnt = block_cnt[:, :, 0]
    counts = jnp.sum(block_cnt, axis=0)
    blk = EXPERT_BLOCK
    n_blocks = n * TOP_K // blk + N_EXPERTS
    padded = (counts + blk - 1) // blk * blk
    pend = jnp.cumsum(padded).astype(I32)
    pstart = pend - padded
    n_used = (pend[-1:] // blk).astype(I32)
    block_start = jnp.arange(n_blocks, dtype=I32) * blk
    block_expert = jnp.minimum(
        jnp.sum((pend[None, :] <= block_start[:, None]).astype(I32), axis=1), N_EXPERTS - 1)
    run_start = (pstart[None, :] + jnp.cumsum(block_cnt, axis=0) - block_cnt).astype(I32)
    cnt_flat = block_cnt.reshape(-1).astype(I32)
    run_flat = run_start.reshape(-1)

    slot_flat = slot_t.reshape(-1)
    gate_flat = gates_t.reshape(-1)

    xs = _dispatch(pstart, pend, cnt_flat, run_flat, slot_flat, u2, n_blocks * blk)
    y = _experts(block_expert, n_used, xs, w_gate_up[0], b_gate_up[0][:, None, :],
                 w_down[0], b_down[0][:, None, :])

    out = _combine(cnt_flat, run_flat, slot_flat, gate_flat, h1, p[0].reshape(n, PLE_DIM), y,
                   w_ple[0].astype(BF16), vec(g_ple[0]), w_ple_gate[0].astype(BF16), vec(g_final))
    return out.reshape(b, s, d)
```

```python
import jax
import jax.numpy as jnp
from jax import lax
from jax.experimental import pallas as pl
from jax.experimental.pallas import tpu as pltpu

F32 = jnp.float32
BF16 = jnp.bfloat16
I32 = jnp.int32

D_MODEL = 1024
ATTN_WIDTH = 512
DIFF_HEADS = 4
HEAD_DIM = 64
HEAD_V = 2 * HEAD_DIM
CONV_WIDTH = 512
CONV_KERNEL = 31
N_EXPERTS = 32
TOP_K = 4
EXPERT_FF = 1024
PLE_DIM = 256
SWIGLU_ALPHA = 1.702
SWIGLU_LIMIT = 7.0
EPS = 1e-5
IN_COLS = 3 * ATTN_WIDTH + 2 * CONV_WIDTH + 2 * D_MODEL
LAMBDA_INIT = 0.2
LOG2_E = 1.4426950408889634

MAIN_COLS = IN_COLS - ATTN_WIDTH
COL_CONV_A = 2
COL_CONV_B = 3
COL_GATE_A = 4
COL_GATE_B = 6

LANES = 128
SUBLANES = 8
TM_PROJ = 512
TQ = 256
T_ROUTE = 512
EXPERT_BLOCK = 512
EXPERT_SPLIT = 2
CONV_ROWS = 64
CONV_PAD = 32
RUN_LONG = 128
DMA_UNROLL = 8
VMEM_V7X = 64 * 1024 * 1024
VMEM_LIMIT = VMEM_V7X - 8 * 1024 * 1024

_NT = (((1,), (1,)), ((), ()))


def _params(sem, vmem=VMEM_LIMIT):
    return pltpu.CompilerParams(dimension_semantics=sem, vmem_limit_bytes=vmem)


def _rms(x, g):
    return x * lax.rsqrt(jnp.mean(x * x, axis=-1, keepdims=True) + EPS) * g


ROW_TILE = D_MODEL // LANES
assert ROW_TILE == SUBLANES


def _store_row_tiles(ref, x, lead=(), first=0):
    rows = x.shape[0]
    for g in range(ROW_TILE):
        ref[lead + (pl.ds(first * ROW_TILE + g, rows, stride=ROW_TILE), slice(None))] = (
            x[:, g * LANES:(g + 1) * LANES])


def _load_row_tiles(ref, rows, lead=(), first=0):
    return jnp.concatenate(
        [ref[lead + (pl.ds(first * ROW_TILE + g, rows, stride=ROW_TILE), slice(None))]
         for g in range(ROW_TILE)], axis=1)


def _inproj_kernel(x_ref, g_ref, w_ref, wvt_ref, o_ref, vt_ref):
    u = _rms(x_ref[...], g_ref[...]).astype(BF16)
    ch = 512
    for c in range(MAIN_COLS // ch):
        src = c * ch if c < 2 * ATTN_WIDTH // ch else c * ch + ATTN_WIDTH
        r = jnp.dot(u, w_ref[:, src:src + ch], preferred_element_type=F32)
        if c == 0:
            r = r * (HEAD_DIM ** -0.5 * LOG2_E)
        o_ref[:, c * ch:(c + 1) * ch] = r.astype(BF16)
    vt_ref[...] = lax.dot_general(wvt_ref[...], u, _NT,
                                  preferred_element_type=F32).astype(BF16)


def _inproj(xt, g, w, wvt):
    n = xt.shape[0]
    return pl.pallas_call(
        _inproj_kernel,
        grid=(n // TM_PROJ,),
        in_specs=[
            pl.BlockSpec((TM_PROJ, D_MODEL), lambda i: (i, 0)),
            pl.BlockSpec((1, D_MODEL), lambda i: (0, 0)),
            pl.BlockSpec((D_MODEL, IN_COLS), lambda i: (0, 0)),
            pl.BlockSpec((ATTN_WIDTH, D_MODEL), lambda i: (0, 0)),
        ],
        out_specs=[
            pl.BlockSpec((TM_PROJ, MAIN_COLS), lambda i: (i, 0)),
            pl.BlockSpec((ATTN_WIDTH, TM_PROJ), lambda i: (0, i)),
        ],
        out_shape=[
            jax.ShapeDtypeStruct((n, MAIN_COLS), BF16),
            jax.ShapeDtypeStruct((ATTN_WIDTH, n), BF16),
        ],
        compiler_params=_params(("parallel",)),
        name="inproj",
    )(xt, g, w, wvt)


def _attn_kernel(q_ref, k_ref, vt_ref, lq1_ref, lk1_ref, lq2_ref, lk2_ref, g_ref, o_ref):
    i = pl.program_id(1)
    s_len = k_ref.shape[0]
    lane = lax.broadcasted_iota(I32, (TQ, HEAD_V), 1)
    key = lax.broadcasted_iota(I32, (TQ, 2 * TQ), 0)
    qry = lax.broadcasted_iota(I32, (TQ, 2 * TQ), 1)
    causal = key <= jnp.where(qry >= TQ, qry - TQ, qry)
    lam = (jnp.exp(jnp.sum(lq1_ref[...] * lk1_ref[...]))
           - jnp.exp(jnp.sum(lq2_ref[...] * lk2_ref[...])) + LAMBDA_INIT)

    half_blocks = s_len // TQ // 2

    def block(c):
        chains = [(half, h) for half in range(2) for h in range(DIFF_HEADS)]
        cols = [slice(h * HEAD_V, (h + 1) * HEAD_V) for _, h in chains]
        n = [(c + half * half_blocks) * TQ for half, _ in chains]
        ids = range(len(chains))

        def stacked_q(j):
            q = q_ref[chains[j][0], :, cols[j]]
            zero = jnp.zeros_like(q)
            return jnp.concatenate([jnp.where(lane < HEAD_DIM, q, zero),
                                    jnp.where(lane >= HEAD_DIM, q, zero)], axis=0)

        s = [lax.dot_general(k_ref[0:n[j] + TQ, cols[j]], stacked_q(j), _NT,
                             preferred_element_type=F32) for j in ids]
        s_d = [jnp.where(causal, s[j][n[j]:], -1e30) for j in ids]
        m = [jnp.max(s_d[j], axis=0, keepdims=True) for j in ids]
        m = [jnp.maximum(m[j], jnp.max(s[j][:n[j]], axis=0, keepdims=True)) if n[j] else m[j]
             for j in ids]
        p = [jnp.exp2(s_d[j] - m[j]) for j in ids]
        p = [jnp.concatenate([jnp.exp2(s[j][:n[j]] - m[j]), p[j]], axis=0) if n[j] else p[j]
             for j in ids]
        l = [jnp.sum(p[j], axis=0, keepdims=True) for j in ids]
        acc = [jnp.dot(vt_ref[cols[j], 0:n[j] + TQ], p[j].astype(BF16),
                       preferred_element_type=F32) for j in ids]
        for j in ids:
            o12 = acc[j] / l[j]
            o = o12[:, :TQ] - lam * o12[:, TQ:]
            o = o * lax.rsqrt(jnp.mean(o * o, axis=0, keepdims=True) + EPS) * g_ref[...]
            o_ref[chains[j][0], :, cols[j]] = (o * (1.0 - LAMBDA_INIT)).T.astype(o_ref.dtype)

    for c in range(half_blocks):
        pl.when(i == c)(lambda c=c: block(c))


def _attention(proj3, vt, lq1, lk1, lq2, lk2, g_subln):
    b, s, width = proj3.shape
    vec = pl.BlockSpec((1, HEAD_DIM), lambda bi, i: (0, 0))
    halves = pl.BlockSpec((None, 2, TQ, ATTN_WIDTH), lambda bi, i: (bi, 0, i, 0))
    out = pl.pallas_call(
        _attn_kernel,
        grid=(b, s // TQ // 2),
        in_specs=[
            halves,
            pl.BlockSpec((None, s, ATTN_WIDTH), lambda bi, i: (bi, 0, 1)),
            pl.BlockSpec((ATTN_WIDTH, s), lambda bi, i: (0, bi)),
            vec, vec, vec, vec,
            pl.BlockSpec((HEAD_V, 1), lambda bi, i: (0, 0)),
        ],
        out_specs=halves,
        out_shape=jax.ShapeDtypeStruct((b, 2, s // 2, ATTN_WIDTH), BF16),
        compiler_params=_params(("parallel", "parallel")),
        name="attn",
    )(proj3.reshape(b, 2, s // 2, width), proj3, vt, lq1, lk1, lq2, lk2, g_subln)
    return out.reshape(b, s, ATTN_WIDTH)


def _conv_kernel(ca_ref, cb_ref, w_ref, b_ref, g_ref, beta_ref, o_ref, z_ref, stage_ref):
    s = ca_ref.shape[0]
    glu_rows = 256
    n_lg = CONV_WIDTH // LANES
    z_ref[:, 0:CONV_PAD, :] = jnp.zeros((n_lg, CONV_PAD, LANES), F32)

    def glu(c, carry):
        r0 = pl.multiple_of(c * glu_rows, glu_rows)
        a = ca_ref[pl.ds(r0, glu_rows), :].astype(F32)
        g = cb_ref[pl.ds(r0, glu_rows), :].astype(F32)
        z = a * jax.nn.sigmoid(g)
        for lg in range(n_lg):
            z_ref[lg, pl.ds(CONV_PAD + r0, glu_rows), :] = z[:, lg * LANES:(lg + 1) * LANES]
        return carry

    lax.fori_loop(0, s // glu_rows, glu, 0)

    def conv(c, carry):
        r0 = pl.multiple_of(c * CONV_ROWS, CONV_ROWS)
        stride = CONV_ROWS // SUBLANES
        for lg in range(n_lg):
            lanes = slice(lg * LANES, (lg + 1) * LANES)
            accs = [jnp.zeros((SUBLANES, LANES), F32) + b_ref[:, lanes] for _ in range(stride)]
            for j in range(CONV_KERNEL):
                off = CONV_PAD - (CONV_KERNEL - 1) + j
                wj = w_ref[j:j + 1, lanes]
                for g in range(stride):
                    accs[g] = accs[g] + wj * z_ref[lg, pl.ds(r0 + off + g, SUBLANES, stride=stride), :]
            for g in range(stride):
                stage_ref[lg, pl.ds(g, SUBLANES, stride=stride), :] = accs[g]
        acc = jnp.concatenate([stage_ref[lg] for lg in range(n_lg)], axis=1)
        mu = jnp.mean(acc, axis=-1, keepdims=True)
        xc = acc - mu
        y = xc * lax.rsqrt(jnp.mean(xc * xc, axis=-1, keepdims=True) + EPS)
        y = y * g_ref[...] + beta_ref[...]
        o_ref[pl.ds(r0, CONV_ROWS), :] = (y * jax.nn.sigmoid(y)).astype(o_ref.dtype)
        return carry

    lax.fori_loop(0, s // CONV_ROWS, conv, 0, unroll=2)


def _conv(proj3, w_dw, b_dw, g_ln, b_ln):
    b, s, _ = proj3.shape
    vec = pl.BlockSpec((1, CONV_WIDTH), lambda bi: (0, 0))
    return pl.pallas_call(
        _conv_kernel,
        grid=(b,),
        in_specs=[
            pl.BlockSpec((None, s, CONV_WIDTH), lambda bi: (bi, 0, COL_CONV_A)),
            pl.BlockSpec((None, s, CONV_WIDTH), lambda bi: (bi, 0, COL_CONV_B)),
            pl.BlockSpec((CONV_KERNEL, CONV_WIDTH), lambda bi: (0, 0)),
            vec, vec, vec,
        ],
        out_specs=pl.BlockSpec((None, s, CONV_WIDTH), lambda bi: (bi, 0, 0)),
        out_shape=jax.ShapeDtypeStruct((b, s, CONV_WIDTH), BF16),
        scratch_shapes=[
            pltpu.VMEM((CONV_WIDTH // LANES, s + CONV_PAD, LANES), F32),
            pltpu.VMEM((CONV_WIDTH // LANES, CONV_ROWS, LANES), F32),
        ],
        compiler_params=_params(("parallel",)),
        name="conv",
    )(proj3, proj3, w_dw, b_dw, g_ln, b_ln)


def _split_bf16(a):
    hi = a.astype(BF16)
    lo = (a - hi.astype(F32)).astype(BF16)
    return hi, lo


def _mix_kernel(x_ref, o_ref, c_ref, ga_ref, gb_ref,
                wa_ref, wc_ref, wo_ref, gffn_ref, wr_ref, br_ref, before_ref,
                h_ref, u_ref, gate_ref, slot_ref, cnt_ref):
    a = jnp.dot(o_ref[...], wa_ref[...], preferred_element_type=F32)
    b = jnp.dot(c_ref[...], wc_ref[...], preferred_element_type=F32)
    m = (jax.nn.sigmoid(ga_ref[...].astype(F32)) * a
         + jax.nn.sigmoid(gb_ref[...].astype(F32)) * b)
    acc = x_ref[...] + jnp.dot(m.astype(BF16), wo_ref[...], preferred_element_type=F32)
    h_ref[...] = acc
    u = _rms(acc, gffn_ref[...])
    _store_row_tiles(u_ref, u)

    u_hi, u_lo = _split_bf16(u)
    w_hi, w_lo = _split_bf16(wr_ref[...])
    logits = (lax.dot_general(w_hi, u_hi, _NT, preferred_element_type=F32)
              + lax.dot_general(w_hi, u_lo, _NT, preferred_element_type=F32)
              + lax.dot_general(w_lo, u_hi, _NT, preferred_element_type=F32)
              + br_ref[...])
    eidx = lax.broadcasted_iota(I32, logits.shape, 0)
    vals, idxs = [], []
    for _ in range(TOP_K):
        mx = jnp.max(logits, axis=0, keepdims=True)
        sel = jnp.min(jnp.where(logits == mx, eidx, N_EXPERTS), axis=0, keepdims=True)
        vals.append(mx)
        idxs.append(sel)
        logits = jnp.where(eidx == sel, -jnp.inf, logits)
    ex = [jnp.exp(v - vals[0]) for v in vals]
    den = sum(ex[1:], ex[0])
    gate_ref[...] = jnp.concatenate([e / den for e in ex], axis=0)

    tm = logits.shape[1]
    onehot = [eidx == idxs[k] for k in range(TOP_K)]
    routed = onehot[0]
    for k in range(1, TOP_K):
        routed = routed | onehot[k]
    member = jnp.where(routed, 1.0, 0.0).astype(BF16)
    rank = jnp.dot(member, before_ref[...], preferred_element_type=F32)
    cnt_ref[...] = jnp.sum(member.astype(F32), axis=1, keepdims=True).astype(I32)
    cnt_row = lax.dot_general(jnp.ones((SUBLANES, tm), BF16), member, _NT,
                              preferred_element_type=F32)[0:1]
    e_row = lax.broadcasted_iota(I32, (N_EXPERTS, N_EXPERTS), 0)
    e_col = lax.broadcasted_iota(I32, (N_EXPERTS, N_EXPERTS), 1)
    first = jnp.sum(jnp.where(e_col < e_row, cnt_row, 0.0), axis=1, keepdims=True)
    slot_ref[...] = jnp.concatenate(
        [jnp.sum(jnp.where(onehot[k], rank + first, 0.0), axis=0, keepdims=True)
         for k in range(TOP_K)], axis=0).astype(I32) * ROW_TILE


def _mix(xt, o, c, proj, wa, wc, wo, g_ffn, wr_t, br):
    n = xt.shape[0]
    tm = T_ROUTE
    pos = jnp.arange(tm, dtype=I32)
    before = (pos[:, None] < pos[None, :]).astype(BF16)
    row = lambda w, j: pl.BlockSpec((tm, w), lambda i, j=j: (i, j))
    full = lambda a: pl.BlockSpec(a.shape, lambda i: (0,) * a.ndim)
    return pl.pallas_call(
        _mix_kernel,
        grid=(n // tm,),
        in_specs=[
            row(D_MODEL, 0), row(ATTN_WIDTH, 0), row(CONV_WIDTH, 0),
            row(D_MODEL, COL_GATE_A // 2), row(D_MODEL, COL_GATE_B // 2),
            full(wa), full(wc), full(wo), full(g_ffn), full(wr_t), full(br), full(before),
        ],
        out_specs=[
            pl.BlockSpec((tm, D_MODEL), lambda i: (i, 0)),
            pl.BlockSpec((tm * ROW_TILE, LANES), lambda i: (i, 0)),
            pl.BlockSpec((None, TOP_K, tm), lambda i: (i, 0, 0)),
            pl.BlockSpec((None, TOP_K, tm), lambda i: (i, 0, 0)),
            pl.BlockSpec((None, N_EXPERTS, 1), lambda i: (i, 0, 0)),
        ],
        out_shape=[
            jax.ShapeDtypeStruct((n, D_MODEL), F32),
            jax.ShapeDtypeStruct((n * ROW_TILE, LANES), F32),
            jax.ShapeDtypeStruct((n // tm, TOP_K, tm), F32),
            jax.ShapeDtypeStruct((n // tm, TOP_K, tm), I32),
            jax.ShapeDtypeStruct((n // tm, N_EXPERTS, 1), I32),
        ],
        compiler_params=_params(("parallel",)),
        name="mix",
    )(xt, o, c, proj, proj, wa, wc, wo, g_ffn, wr_t, br, before)


def _run_copies(cnt, src_ref, src_row, dst_ref, dst_row, sem):
    def pieces(bits, src_row, dst_row):
        for bit in bits:
            size = 1 << bit
            piece = cnt & size

            @pl.when(piece != 0)
            def _(size=size, src_row=src_row, dst_row=dst_row):
                rows = size * ROW_TILE
                pltpu.make_async_copy(
                    src_ref.at[pl.ds(pl.multiple_of(src_row * ROW_TILE, ROW_TILE), rows), :],
                    dst_ref.at[pl.ds(pl.multiple_of(dst_row * ROW_TILE, ROW_TILE), rows), :],
                    sem).start()

            src_row = src_row + piece
            dst_row = dst_row + piece

    long_bits = RUN_LONG.bit_length() - 1

    @pl.when(cnt >= RUN_LONG)
    def _():
        pieces(reversed(range(long_bits, T_ROUTE.bit_length())), src_row, dst_row)

    head = cnt & -RUN_LONG
    pieces(reversed(range(long_bits)), src_row + head, dst_row + head)


def _dispatch_kernel(pstart_ref, pend_ref, cnt_ref, run_ref, slot_ref, u_ref, xs_ref,
                     grouped, zero_ref, sem, zsem):
    t = T_ROUTE
    i = pl.program_id(0)
    cur = i % 2
    blk = EXPERT_BLOCK * ROW_TILE
    buf = grouped.at[cur]

    def drain(slot):
        pltpu.make_async_copy(grouped.at[slot], grouped.at[slot], sem.at[slot]).wait()

    @pl.when(i == 0)
    def _():
        zero_ref[...] = jnp.zeros_like(zero_ref)

        def zero_block(first_row):
            return pltpu.make_async_copy(
                zero_ref, xs_ref.at[pl.ds(pl.multiple_of(first_row * ROW_TILE, blk), blk), :], zsem)

        for act in (lambda cp: cp.start(), lambda cp: cp.wait()):
            def tails(e, carry, act=act):
                @pl.when(pend_ref[e] > pstart_ref[e])
                def _():
                    act(zero_block(pend_ref[e] - EXPERT_BLOCK))
                return carry

            lax.fori_loop(0, N_EXPERTS, tails, 0)

            def unused(b, carry, act=act):
                act(zero_block(b * EXPERT_BLOCK))
                return carry

            lax.fori_loop(pend_ref[N_EXPERTS - 1] // EXPERT_BLOCK,
                          xs_ref.shape[0] // blk, unused, 0)

    @pl.when(i >= 2)
    def _():
        drain(cur)

    def group(tok, carry):
        tile = u_ref[pl.ds(pl.multiple_of(tok * ROW_TILE, ROW_TILE), ROW_TILE), :]
        for k in range(TOP_K):
            slot = pl.multiple_of(slot_ref[k * t + tok], ROW_TILE)
            buf[pl.ds(slot, ROW_TILE), :] = tile
        return carry

    lax.fori_loop(0, t, group, 0, unroll=DMA_UNROLL)

    def runs(e, off):
        cnt = cnt_ref[i * N_EXPERTS + e]
        _run_copies(cnt, buf, off, xs_ref, run_ref[i * N_EXPERTS + e], sem.at[cur])
        return off + cnt

    lax.fori_loop(0, N_EXPERTS, runs, 0)

    @pl.when(i == pl.num_programs(0) - 1)
    def _():
        drain(cur)

        @pl.when(i >= 1)
        def _():
            drain(1 - cur)


def _dispatch(pstart, pend, block_cnt, run_start, slot_t, u, n_rows):
    n = u.shape[0] // ROW_TILE
    t = T_ROUTE
    grid_spec = pltpu.PrefetchScalarGridSpec(
        num_scalar_prefetch=4,
        grid=(n // t,),
        in_specs=[
            pl.BlockSpec((TOP_K * t,), lambda i, *_: (i,), memory_space=pltpu.SMEM),
            pl.BlockSpec((t * ROW_TILE, LANES), lambda i, *_: (i, 0)),
        ],
        out_specs=pl.BlockSpec(memory_space=pl.ANY),
        scratch_shapes=[
            pltpu.VMEM((2, TOP_K * t * ROW_TILE, LANES), F32),
            pltpu.VMEM((EXPERT_BLOCK * ROW_TILE, LANES), F32),
            pltpu.SemaphoreType.DMA((2,)),
            pltpu.SemaphoreType.DMA(()),
        ],
    )
    return pl.pallas_call(
        _dispatch_kernel,
        grid_spec=grid_spec,
        out_shape=jax.ShapeDtypeStruct((n_rows * ROW_TILE, LANES), F32),
        compiler_params=_params(("arbitrary",)),
        name="dispatch",
    )(pstart, pend, block_cnt, run_start, slot_t, u)


def _expert_kernel(be_ref, nb_ref, x_ref, wgu_ref, bgu_ref, wd_ref, bd_ref, y_ref,
                   wgu_s, wd32_s, wd_s):
    i = pl.program_id(0)
    nb = nb_ref[0]
    blk = EXPERT_BLOCK
    half = EXPERT_FF // 2
    first = jnp.logical_or(i == 0, be_ref[i] != be_ref[jnp.maximum(i - 1, 0)])

    @pl.when(jnp.logical_and(first, i < nb))
    def _():
        rows = 256
        for r in range(0, D_MODEL, rows):
            wgu_s[r:r + rows, :] = wgu_ref[r:r + rows, :].astype(BF16)
        for cs in range(D_MODEL // LANES):
            lanes = slice(cs * LANES, (cs + 1) * LANES)
            wd32_s[cs, pl.ds(0, half, stride=2), :] = wd_ref[0:half, lanes]
            wd32_s[cs, pl.ds(1, half, stride=2), :] = wd_ref[half:EXPERT_FF, lanes]
            wd_s[:, lanes] = wd32_s[cs].astype(BF16)

    @pl.when(i < nb)
    def _():
        rows = blk // EXPERT_SPLIT
        parts = range(EXPERT_SPLIT)
        even = (lax.broadcasted_iota(I32, (rows, LANES), 1) % 2) == 0
        xb = [_load_row_tiles(x_ref, rows, first=s * rows).astype(BF16) for s in parts]
        gu1 = [jnp.dot(xb[s], wgu_s[:, :EXPERT_FF], preferred_element_type=F32)
               + bgu_ref[:, :EXPERT_FF] for s in parts]
        gu2 = [jnp.dot(xb[s], wgu_s[:, EXPERT_FF:], preferred_element_type=F32)
               + bgu_ref[:, EXPERT_FF:] for s in parts]
        act = []
        for s in parts:
            cols = []
            for v in range(EXPERT_FF // LANES):
                a = gu1[s][:, v * LANES:(v + 1) * LANES]
                b = gu2[s][:, v * LANES:(v + 1) * LANES]
                g = jnp.where(even, a, pltpu.roll(b, 1, axis=1))
                l = jnp.where(even, pltpu.roll(a, LANES - 1, axis=1), b)
                g = jnp.minimum(g, SWIGLU_LIMIT)
                l = jnp.clip(l, -SWIGLU_LIMIT, SWIGLU_LIMIT)
                cols.append(((l + 1.0) * (g * jax.nn.sigmoid(g * SWIGLU_ALPHA))).astype(BF16))
            act.append(jnp.concatenate(cols, axis=1))
        y = [jnp.dot(act[s], wd_s[...], preferred_element_type=F32) + bd_ref[...] for s in parts]
        for s in parts:
            _store_row_tiles(y_ref, y[s], first=s * rows)

    @pl.when(i >= nb)
    def _():
        y_ref[...] = jnp.zeros_like(y_ref)


def _experts(block_expert, n_used, xs, wgu, bgu, wd, bd):
    n_rows = xs.shape[0] // ROW_TILE
    blk = EXPERT_BLOCK
    wspec = lambda r, c: pl.BlockSpec((None, r, c), lambda i, be, nb: (be[i], 0, 0))
    grid_spec = pltpu.PrefetchScalarGridSpec(
        num_scalar_prefetch=2,
        grid=(n_rows // blk,),
        in_specs=[
            pl.BlockSpec((blk * ROW_TILE, LANES), lambda i, be, nb: (jnp.minimum(i, nb[0] - 1), 0)),
            wspec(D_MODEL, 2 * EXPERT_FF), wspec(1, 2 * EXPERT_FF),
            wspec(EXPERT_FF, D_MODEL), wspec(1, D_MODEL),
        ],
        out_specs=pl.BlockSpec((blk * ROW_TILE, LANES), lambda i, be, nb: (i, 0)),
        scratch_shapes=[
            pltpu.VMEM((D_MODEL, 2 * EXPERT_FF), BF16),
            pltpu.VMEM((D_MODEL // LANES, EXPERT_FF, LANES), F32),
            pltpu.VMEM((EXPERT_FF, D_MODEL), BF16),
        ],
    )
    return pl.pallas_call(
        _expert_kernel,
        grid_spec=grid_spec,
        out_shape=jax.ShapeDtypeStruct((n_rows * ROW_TILE, LANES), F32),
        compiler_params=_params(("arbitrary",)),
        name="expert",
    )(block_expert, n_used, xs, wgu, bgu, wd, bd)


def _combine_kernel(cnt_ref, run_ref, slot_ref, gsm_ref, h_ref, p_ref, y_ref, wple_ref, gple_ref,
                    wpg_ref, gfin_ref, o_ref, staged, moe_s, sem):
    t = T_ROUTE
    i = pl.program_id(0)
    cur = i % 2

    def fetch(block, slot):
        def runs(e, off):
            cnt = cnt_ref[block * N_EXPERTS + e]
            _run_copies(cnt, y_ref, run_ref[block * N_EXPERTS + e], staged.at[slot], off,
                        sem.at[slot])
            return off + cnt

        lax.fori_loop(0, N_EXPERTS, runs, 0)

    @pl.when(i == 0)
    def _():
        fetch(0, 0)

    @pl.when(i + 1 < pl.num_programs(0))
    def _():
        fetch(i + 1, 1 - cur)

    pltpu.make_async_copy(staged.at[cur], staged.at[cur], sem.at[cur]).wait()
    buf = staged.at[cur]

    def pick(tok, carry):
        acc = None
        for k in range(TOP_K):
            slot = pl.multiple_of(slot_ref[k * t + tok], ROW_TILE)
            row = buf[pl.ds(slot, ROW_TILE), :]
            term = gsm_ref[k * t + tok] * row
            acc = term if acc is None else acc + term
        moe_s[pl.ds(pl.multiple_of(tok * ROW_TILE, ROW_TILE), ROW_TILE), :] = acc
        return carry

    lax.fori_loop(0, t, pick, 0, unroll=DMA_UNROLL)

    rows = t // 2
    parts = [pl.ds(s * rows, rows) for s in range(2)]
    emb = [jnp.dot(p_ref[rs, :].astype(BF16), wple_ref[...], preferred_element_type=F32)
           for rs in parts]
    h = [h_ref[rs, :] + _load_row_tiles(moe_s, rows, first=s * rows) for s, rs in enumerate(parts)]
    r = [_rms(hs, gple_ref[...]).astype(BF16) for hs in h]
    sig = [jax.nn.sigmoid(jnp.dot(rs, wpg_ref[...], preferred_element_type=F32)) for rs in r]
    for s, rs in enumerate(parts):
        o_ref[rs, :] = _rms(h[s] + emb[s] * sig[s], gfin_ref[...])


def _combine(block_cnt, run_start, slot_t, gates_t, h, p2, y, wple, gple, wpg, gfin):
    n = h.shape[0]
    t = T_ROUTE
    full = lambda a: pl.BlockSpec(a.shape, lambda i, *_: (0,) * a.ndim)
    smem = lambda: pl.BlockSpec((TOP_K * t,), lambda i, *_: (i,), memory_space=pltpu.SMEM)
    grid_spec = pltpu.PrefetchScalarGridSpec(
        num_scalar_prefetch=2,
        grid=(n // t,),
        in_specs=[
            smem(), smem(),
            pl.BlockSpec((t, D_MODEL), lambda i, *_: (i, 0)),
            pl.BlockSpec((t, PLE_DIM), lambda i, *_: (i, 0)),
            pl.BlockSpec(memory_space=pl.ANY),
            full(wple), full(gple), full(wpg), full(gfin),
        ],
        out_specs=pl.BlockSpec((t, D_MODEL), lambda i, *_: (i, 0)),
        scratch_shapes=[
            pltpu.VMEM((2, TOP_K * t * ROW_TILE, LANES), F32),
            pltpu.VMEM((t * ROW_TILE, LANES), F32),
            pltpu.SemaphoreType.DMA((2,)),
        ],
    )
    return pl.pallas_call(
        _combine_kernel,
        grid_spec=grid_spec,
        out_shape=jax.ShapeDtypeStruct((n, D_MODEL), F32),
        compiler_params=_params(("arbitrary",)),
        name="combine",
    )(block_cnt, run_start, slot_t, gates_t, h, p2, y, wple, gple, wpg, gfin)


def kernel(x, p, g_mix, w_in, lambda_q1, lambda_k1, lambda_q2, lambda_k2, g_subln, w_attn_out,
           w_dw, b_dw, g_conv_ln, b_conv_ln, w_conv_out, w_o, g_ffn, w_router, b_router,
           w_gate_up, b_gate_up, w_down, b_down, w_ple, g_ple, w_ple_gate, g_final):
    b, s, d = x.shape
    n = b * s
    xt = x.reshape(n, d)
    vec = lambda a: a.reshape(1, -1)

    w_in0 = w_in[0].astype(BF16)
    w_vt = w_in0[:, 2 * ATTN_WIDTH:3 * ATTN_WIDTH].T
    proj, vt = _inproj(xt, vec(g_mix[0]), w_in0, w_vt)
    proj3 = proj.reshape(b, s, MAIN_COLS)
    attn = _attention(proj3, vt, vec(lambda_q1[0]), vec(lambda_k1[0]), vec(lambda_q2[0]),
                      vec(lambda_k2[0]), g_subln[0].reshape(HEAD_V, 1))
    conv = _conv(proj3, w_dw[0], vec(b_dw[0]), vec(g_conv_ln[0]), vec(b_conv_ln[0]))
    h1, u2, gates_t, slot_t, block_cnt = _mix(
        xt, attn.reshape(n, ATTN_WIDTH), conv.reshape(n, CONV_WIDTH), proj,
        w_attn_out[0].astype(BF16), w_conv_out[0].astype(BF16), w_o[0].astype(BF16),
        vec(g_ffn[0]), w_router[0].T, b_router[0].reshape(N_EXPERTS, 1))

    block_cnt = block_cnt[:, :, 0]
    counts = jnp.sum(block_cnt, axis=0)
    blk = EXPERT_BLOCK
    n_blocks = n * TOP_K // blk + N_EXPERTS
    padded = (counts + blk - 1) // blk * blk
    pend = jnp.cumsum(padded).astype(I32)
    pstart = pend - padded
    n_used = (pend[-1:] // blk).astype(I32)
    block_start = jnp.arange(n_blocks, dtype=I32) * blk
    block_expert = jnp.minimum(
        jnp.sum((pend[None, :] <= block_start[:, None]).astype(I32), axis=1), N_EXPERTS - 1)
    run_start = (pstart[None, :] + jnp.cumsum(block_cnt, axis=0) - block_cnt).astype(I32)
    cnt_flat = block_cnt.reshape(-1).astype(I32)
    run_flat = run_start.reshape(-1)

    slot_flat = slot_t.reshape(-1)
    gate_flat = gates_t.reshape(-1)

    xs = _dispatch(pstart, pend, cnt_flat, run_flat, slot_flat, u2, n_blocks * blk)
    y = _experts(block_expert, n_used, xs, w_gate_up[0], b_gate_up[0][:, None, :],
                 w_down[0], b_down[0][:, None, :])

    out = _combine(cnt_flat, run_flat, slot_flat, gate_flat, h1, p[0].reshape(n, PLE_DIM), y,
                   w_ple[0].astype(BF16), vec(g_ple[0]), w_ple_gate[0].astype(BF16), vec(g_final))
    return out.reshape(b, s, d)
```

```python
import jax
import jax.numpy as jnp
from jax import lax
from jax.experimental import pallas as pl
from jax.experimental.pallas import tpu as pltpu

F32 = jnp.float32
BF16 = jnp.bfloat16
I32 = jnp.int32

D_MODEL = 1024
ATTN_WIDTH = 512
DIFF_HEADS = 4
HEAD_DIM = 64
HEAD_V = 2 * HEAD_DIM
CONV_WIDTH = 512
CONV_KERNEL = 31
N_EXPERTS = 32
TOP_K = 4
EXPERT_FF = 1024
PLE_DIM = 256
SWIGLU_ALPHA = 1.702
SWIGLU_LIMIT = 7.0
EPS = 1e-5
IN_COLS = 3 * ATTN_WIDTH + 2 * CONV_WIDTH + 2 * D_MODEL
LAMBDA_INIT = 0.2
LOG2_E = 1.4426950408889634

MAIN_COLS = IN_COLS - ATTN_WIDTH
COL_CONV_A = 2
COL_CONV_B = 3
COL_GATE_A = 4
COL_GATE_B = 6

LANES = 128
SUBLANES = 8
TM_PROJ = 512
TQ = 256
T_ROUTE = 512
EXPERT_BLOCK = 512
EXPERT_SPLIT = 2
CONV_ROWS = 64
CONV_PAD = 32
RUN_LONG = 128
DMA_UNROLL = 8
VMEM_V7X = 64 * 1024 * 1024
VMEM_LIMIT = VMEM_V7X - 8 * 1024 * 1024

_NT = (((1,), (1,)), ((), ()))


def _params(sem, vmem=VMEM_LIMIT):
    return pltpu.CompilerParams(dimension_semantics=sem, vmem_limit_bytes=vmem)


def _rms(x, g):
    return x * lax.rsqrt(jnp.mean(x * x, axis=-1, keepdims=True) + EPS) * g


ROW_TILE = D_MODEL // LANES
assert ROW_TILE == SUBLANES


def _store_row_tiles(ref, x, lead=(), first=0):
    rows = x.shape[0]
    for g in range(ROW_TILE):
        ref[lead + (pl.ds(first * ROW_TILE + g, rows, stride=ROW_TILE), slice(None))] = (
            x[:, g * LANES:(g + 1) * LANES])


def _load_row_tiles(ref, rows, lead=(), first=0):
    return jnp.concatenate(
        [ref[lead + (pl.ds(first * ROW_TILE + g, rows, stride=ROW_TILE), slice(None))]
         for g in range(ROW_TILE)], axis=1)


def _inproj_kernel(x_ref, g_ref, w_ref, wvt_ref, o_ref, vt_ref):
    u = _rms(x_ref[...], g_ref[...]).astype(BF16)
    ch = 512
    for c in range(MAIN_COLS // ch):
        src = c * ch if c < 2 * ATTN_WIDTH // ch else c * ch + ATTN_WIDTH
        r = jnp.dot(u, w_ref[:, src:src + ch], preferred_element_type=F32)
        if c == 0:
            r = r * (HEAD_DIM ** -0.5 * LOG2_E)
        o_ref[:, c * ch:(c + 1) * ch] = r.astype(BF16)
    vt_ref[...] = lax.dot_general(wvt_ref[...], u, _NT,
                                  preferred_element_type=F32).astype(BF16)


def _inproj(xt, g, w, wvt):
    n = xt.shape[0]
    return pl.pallas_call(
        _inproj_kernel,
        grid=(n // TM_PROJ,),
        in_specs=[
            pl.BlockSpec((TM_PROJ, D_MODEL), lambda i: (i, 0)),
            pl.BlockSpec((1, D_MODEL), lambda i: (0, 0)),
            pl.BlockSpec((D_MODEL, IN_COLS), lambda i: (0, 0)),
            pl.BlockSpec((ATTN_WIDTH, D_MODEL), lambda i: (0, 0)),
        ],
        out_specs=[
            pl.BlockSpec((TM_PROJ, MAIN_COLS), lambda i: (i, 0)),
            pl.BlockSpec((ATTN_WIDTH, TM_PROJ), lambda i: (0, i)),
        ],
        out_shape=[
            jax.ShapeDtypeStruct((n, MAIN_COLS), BF16),
            jax.ShapeDtypeStruct((ATTN_WIDTH, n), BF16),
        ],
        compiler_params=_params(("parallel",)),
        name="inproj",
    )(xt, g, w, wvt)


def _attn_kernel(q_ref, k_ref, vt_ref, lq1_ref, lk1_ref, lq2_ref, lk2_ref, g_ref, o_ref):
    i = pl.program_id(1)
    s_len = k_ref.shape[0]
    lane = lax.broadcasted_iota(I32, (TQ, HEAD_V), 1)
    key = lax.broadcasted_iota(I32, (TQ, 2 * TQ), 0)
    qry = lax.broadcasted_iota(I32, (TQ, 2 * TQ), 1)
    causal = key <= jnp.where(qry >= TQ, qry - TQ, qry)
    lam = (jnp.exp(jnp.sum(lq1_ref[...] * lk1_ref[...]))
           - jnp.exp(jnp.sum(lq2_ref[...] * lk2_ref[...])) + LAMBDA_INIT)

    half_blocks = s_len // TQ // 2

    def block(c):
        chains = [(half, h) for half in range(2) for h in range(DIFF_HEADS)]
        cols = [slice(h * HEAD_V, (h + 1) * HEAD_V) for _, h in chains]
        n = [(c + half * half_blocks) * TQ for half, _ in chains]
        ids = range(len(chains))

        def stacked_q(j):
            q = q_ref[chains[j][0], :, cols[j]]
            zero = jnp.zeros_like(q)
            return jnp.concatenate([jnp.where(lane < HEAD_DIM, q, zero),
                                    jnp.where(lane >= HEAD_DIM, q, zero)], axis=0)

        s = [lax.dot_general(k_ref[0:n[j] + TQ, cols[j]], stacked_q(j), _NT,
                             preferred_element_type=F32) for j in ids]
        s_d = [jnp.where(causal, s[j][n[j]:], -1e30) for j in ids]
        m = [jnp.max(s_d[j], axis=0, keepdims=True) for j in ids]
        m = [jnp.maximum(m[j], jnp.max(s[j][:n[j]], axis=0, keepdims=True)) if n[j] else m[j]
             for j in ids]
        p = [jnp.exp2(s_d[j] - m[j]) for j in ids]
        p = [jnp.concatenate([jnp.exp2(s[j][:n[j]] - m[j]), p[j]], axis=0) if n[j] else p[j]
             for j in ids]
        l = [jnp.sum(p[j], axis=0, keepdims=True) for j in ids]
        acc = [jnp.dot(vt_ref[cols[j], 0:n[j] + TQ], p[j].astype(BF16),
                       preferred_element_type=F32) for j in ids]
        for j in ids:
            o12 = acc[j] / l[j]
            o = o12[:, :TQ] - lam * o12[:, TQ:]
            o = o * lax.rsqrt(jnp.mean(o * o, axis=0, keepdims=True) + EPS) * g_ref[...]
            o_ref[chains[j][0], :, cols[j]] = (o * (1.0 - LAMBDA_INIT)).T.astype(o_ref.dtype)

    for c in range(half_blocks):
        pl.when(i == c)(lambda c=c: block(c))


def _attention(proj3, vt, lq1, lk1, lq2, lk2, g_subln):
    b, s, width = proj3.shape
    vec = pl.BlockSpec((1, HEAD_DIM), lambda bi, i: (0, 0))
    halves = pl.BlockSpec((None, 2, TQ, ATTN_WIDTH), lambda bi, i: (bi, 0, i, 0))
    out = pl.pallas_call(
        _attn_kernel,
        grid=(b, s // TQ // 2),
        in_specs=[
            halves,
            pl.BlockSpec((None, s, ATTN_WIDTH), lambda bi, i: (bi, 0, 1)),
            pl.BlockSpec((ATTN_WIDTH, s), lambda bi, i: (0, bi)),
            vec, vec, vec, vec,
            pl.BlockSpec((HEAD_V, 1), lambda bi, i: (0, 0)),
        ],
        out_specs=halves,
        out_shape=jax.ShapeDtypeStruct((b, 2, s // 2, ATTN_WIDTH), BF16),
        compiler_params=_params(("parallel", "parallel")),
        name="attn",
    )(proj3.reshape(b, 2, s // 2, width), proj3, vt, lq1, lk1, lq2, lk2, g_subln)
    return out.reshape(b, s, ATTN_WIDTH)


def _conv_kernel(ca_ref, cb_ref, w_ref, b_ref, g_ref, beta_ref, o_ref, z_ref, stage_ref):
    s = ca_ref.shape[0]
    glu_rows = 256
    n_lg = CONV_WIDTH // LANES
    z_ref[:, 0:CONV_PAD, :] = jnp.zeros((n_lg, CONV_PAD, LANES), F32)

    def glu(c, carry):
        r0 = pl.multiple_of(c * glu_rows, glu_rows)
        a = ca_ref[pl.ds(r0, glu_rows), :].astype(F32)
        g = cb_ref[pl.ds(r0, glu_rows), :].astype(F32)
        z = a * jax.nn.sigmoid(g)
        for lg in range(n_lg):
            z_ref[lg, pl.ds(CONV_PAD + r0, glu_rows), :] = z[:, lg * LANES:(lg + 1) * LANES]
        return carry

    lax.fori_loop(0, s // glu_rows, glu, 0)

    def conv(c, carry):
        r0 = pl.multiple_of(c * CONV_ROWS, CONV_ROWS)
        stride = CONV_ROWS // SUBLANES
        for lg in range(n_lg):
            lanes = slice(lg * LANES, (lg + 1) * LANES)
            accs = [jnp.zeros((SUBLANES, LANES), F32) + b_ref[:, lanes] for _ in range(stride)]
            for j in range(CONV_KERNEL):
                off = CONV_PAD - (CONV_KERNEL - 1) + j
                wj = w_ref[j:j + 1, lanes]
                for g in range(stride):
                    accs[g] = accs[g] + wj * z_ref[lg, pl.ds(r0 + off + g, SUBLANES, stride=stride), :]
            for g in range(stride):
                stage_ref[lg, pl.ds(g, SUBLANES, stride=stride), :] = accs[g]
        acc = jnp.concatenate([stage_ref[lg] for lg in range(n_lg)], axis=1)
        mu = jnp.mean(acc, axis=-1, keepdims=True)
        xc = acc - mu
        y = xc * lax.rsqrt(jnp.mean(xc * xc, axis=-1, keepdims=True) + EPS)
        y = y * g_ref[...] + beta_ref[...]
        o_ref[pl.ds(r0, CONV_ROWS), :] = (y * jax.nn.sigmoid(y)).astype(o_ref.dtype)
        return carry

    lax.fori_loop(0, s // CONV_ROWS, conv, 0, unroll=2)


def _conv(proj3, w_dw, b_dw, g_ln, b_ln):
    b, s, _ = proj3.shape
    vec = pl.BlockSpec((1, CONV_WIDTH), lambda bi: (0, 0))
    return pl.pallas_call(
        _conv_kernel,
        grid=(b,),
        in_specs=[
            pl.BlockSpec((None, s, CONV_WIDTH), lambda bi: (bi, 0, COL_CONV_A)),
            pl.BlockSpec((None, s, CONV_WIDTH), lambda bi: (bi, 0, COL_CONV_B)),
            pl.BlockSpec((CONV_KERNEL, CONV_WIDTH), lambda bi: (0, 0)),
            vec, vec, vec,
        ],
        out_specs=pl.BlockSpec((None, s, CONV_WIDTH), lambda bi: (bi, 0, 0)),
        out_shape=jax.ShapeDtypeStruct((b, s, CONV_WIDTH), BF16),
        scratch_shapes=[
            pltpu.VMEM((CONV_WIDTH // LANES, s + CONV_PAD, LANES), F32),
            pltpu.VMEM((CONV_WIDTH // LANES, CONV_ROWS, LANES), F32),
        ],
        compiler_params=_params(("parallel",)),
        name="conv",
    )(proj3, proj3, w_dw, b_dw, g_ln, b_ln)


def _split_bf16(a):
    hi = a.astype(BF16)
    lo = (a - hi.astype(F32)).astype(BF16)
    return hi, lo


def _mix_kernel(x_ref, o_ref, c_ref, ga_ref, gb_ref,
                wa_ref, wc_ref, wo_ref, gffn_ref, wr_ref, br_ref, before_ref,
                h_ref, u_ref, gate_ref, slot_ref, cnt_ref):
    a = jnp.dot(o_ref[...], wa_ref[...], preferred_element_type=F32)
    b = jnp.dot(c_ref[...], wc_ref[...], preferred_element_type=F32)
    m = (jax.nn.sigmoid(ga_ref[...].astype(F32)) * a
         + jax.nn.sigmoid(gb_ref[...].astype(F32)) * b)
    acc = x_ref[...] + jnp.dot(m.astype(BF16), wo_ref[...], preferred_element_type=F32)
    h_ref[...] = acc
    u = _rms(acc, gffn_ref[...])
    _store_row_tiles(u_ref, u)

    u_hi, u_lo = _split_bf16(u)
    w_hi, w_lo = _split_bf16(wr_ref[...])
    logits = (lax.dot_general(w_hi, u_hi, _NT, preferred_element_type=F32)
              + lax.dot_general(w_hi, u_lo, _NT, preferred_element_type=F32)
              + lax.dot_general(w_lo, u_hi, _NT, preferred_element_type=F32)
              + br_ref[...])
    eidx = lax.broadcasted_iota(I32, logits.shape, 0)
    vals, idxs = [], []
    for _ in range(TOP_K):
        mx = jnp.max(logits, axis=0, keepdims=True)
        sel = jnp.min(jnp.where(logits == mx, eidx, N_EXPERTS), axis=0, keepdims=True)
        vals.append(mx)
        idxs.append(sel)
        logits = jnp.where(eidx == sel, -jnp.inf, logits)
    ex = [jnp.exp(v - vals[0]) for v in vals]
    den = sum(ex[1:], ex[0])
    gate_ref[...] = jnp.concatenate([e / den for e in ex], axis=0)

    tm = logits.shape[1]
    onehot = [eidx == idxs[k] for k in range(TOP_K)]
    routed = onehot[0]
    for k in range(1, TOP_K):
        routed = routed | onehot[k]
    member = jnp.where(routed, 1.0, 0.0).astype(BF16)
    rank = jnp.dot(member, before_ref[...], preferred_element_type=F32)
    cnt_ref[...] = jnp.sum(member.astype(F32), axis=1, keepdims=True).astype(I32)
    cnt_row = lax.dot_general(jnp.ones((SUBLANES, tm), BF16), member, _NT,
                              preferred_element_type=F32)[0:1]
    e_row = lax.broadcasted_iota(I32, (N_EXPERTS, N_EXPERTS), 0)
    e_col = lax.broadcasted_iota(I32, (N_EXPERTS, N_EXPERTS), 1)
    first = jnp.sum(jnp.where(e_col < e_row, cnt_row, 0.0), axis=1, keepdims=True)
    slot_ref[...] = jnp.concatenate(
        [jnp.sum(jnp.where(onehot[k], rank + first, 0.0), axis=0, keepdims=True)
         for k in range(TOP_K)], axis=0).astype(I32) * ROW_TILE


def _mix(xt, o, c, proj, wa, wc, wo, g_ffn, wr_t, br):
    n = xt.shape[0]
    tm = T_ROUTE
    pos = jnp.arange(tm, dtype=I32)
    before = (pos[:, None] < pos[None, :]).astype(BF16)
    row = lambda w, j: pl.BlockSpec((tm, w), lambda i, j=j: (i, j))
    full = lambda a: pl.BlockSpec(a.shape, lambda i: (0,) * a.ndim)
    return pl.pallas_call(
        _mix_kernel,
        grid=(n // tm,),
        in_specs=[
            row(D_MODEL, 0), row(ATTN_WIDTH, 0), row(CONV_WIDTH, 0),
            row(D_MODEL, COL_GATE_A // 2), row(D_MODEL, COL_GATE_B // 2),
            full(wa), full(wc), full(wo), full(g_ffn), full(wr_t), full(br), full(before),
        ],
        out_specs=[
            pl.BlockSpec((tm, D_MODEL), lambda i: (i, 0)),
            pl.BlockSpec((tm * ROW_TILE, LANES), lambda i: (i, 0)),
            pl.BlockSpec((None, TOP_K, tm), lambda i: (i, 0, 0)),
            pl.BlockSpec((None, TOP_K, tm), lambda i: (i, 0, 0)),
            pl.BlockSpec((None, N_EXPERTS, 1), lambda i: (i, 0, 0)),
        ],
        out_shape=[
            jax.ShapeDtypeStruct((n, D_MODEL), F32),
            jax.ShapeDtypeStruct((n * ROW_TILE, LANES), F32),
            jax.ShapeDtypeStruct((n // tm, TOP_K, tm), F32),
            jax.ShapeDtypeStruct((n // tm, TOP_K, tm), I32),
            jax.ShapeDtypeStruct((n // tm, N_EXPERTS, 1), I32),
        ],
        compiler_params=_params(("parallel",)),
        name="mix",
    )(xt, o, c, proj, proj, wa, wc, wo, g_ffn, wr_t, br, before)


def _run_copies(cnt, src_ref, src_row, dst_ref, dst_row, sem):
    def pieces(bits, src_row, dst_row):
        for bit in bits:
            size = 1 << bit
            piece = cnt & size

            @pl.when(piece != 0)
            def _(size=size, src_row=src_row, dst_row=dst_row):
                rows = size * ROW_TILE
                pltpu.make_async_copy(
                    src_ref.at[pl.ds(pl.multiple_of(src_row * ROW_TILE, ROW_TILE), rows), :],
                    dst_ref.at[pl.ds(pl.multiple_of(dst_row * ROW_TILE, ROW_TILE), rows), :],
                    sem).start()

            src_row = src_row + piece
            dst_row = dst_row + piece

    long_bits = RUN_LONG.bit_length() - 1

    @pl.when(cnt >= RUN_LONG)
    def _():
        pieces(reversed(range(long_bits, T_ROUTE.bit_length())), src_row, dst_row)

    head = cnt & -RUN_LONG
    pieces(reversed(range(long_bits)), src_row + head, dst_row + head)


def _dispatch_kernel(pstart_ref, pend_ref, cnt_ref, run_ref, slot_ref, u_ref, xs_ref,
                     grouped, zero_ref, sem, zsem):
    t = T_ROUTE
    i = pl.program_id(0)
    cur = i % 2
    blk = EXPERT_BLOCK * ROW_TILE
    buf = grouped.at[cur]

    def drain(slot):
        pltpu.make_async_copy(grouped.at[slot], grouped.at[slot], sem.at[slot]).wait()

    @pl.when(i == 0)
    def _():
        zero_ref[...] = jnp.zeros_like(zero_ref)

        def zero_block(first_row):
            return pltpu.make_async_copy(
                zero_ref, xs_ref.at[pl.ds(pl.multiple_of(first_row * ROW_TILE, blk), blk), :], zsem)

        for act in (lambda cp: cp.start(), lambda cp: cp.wait()):
            def tails(e, carry, act=act):
                @pl.when(pend_ref[e] > pstart_ref[e])
                def _():
                    act(zero_block(pend_ref[e] - EXPERT_BLOCK))
                return carry

            lax.fori_loop(0, N_EXPERTS, tails, 0)

            def unused(b, carry, act=act):
                act(zero_block(b * EXPERT_BLOCK))
                return carry

            lax.fori_loop(pend_ref[N_EXPERTS - 1] // EXPERT_BLOCK,
                          xs_ref.shape[0] // blk, unused, 0)

    @pl.when(i >= 2)
    def _():
        drain(cur)

    def group(tok, carry):
        tile = u_ref[pl.ds(pl.multiple_of(tok * ROW_TILE, ROW_TILE), ROW_TILE), :]
        for k in range(TOP_K):
            slot = pl.multiple_of(slot_ref[k * t + tok], ROW_TILE)
            buf[pl.ds(slot, ROW_TILE), :] = tile
        return carry

    lax.fori_loop(0, t, group, 0, unroll=DMA_UNROLL)

    def runs(e, off):
        cnt = cnt_ref[i * N_EXPERTS + e]
        _run_copies(cnt, buf, off, xs_ref, run_ref[i * N_EXPERTS + e], sem.at[cur])
        return off + cnt

    lax.fori_loop(0, N_EXPERTS, runs, 0)

    @pl.when(i == pl.num_programs(0) - 1)
    def _():
        drain(cur)

        @pl.when(i >= 1)
        def _():
            drain(1 - cur)


def _dispatch(pstart, pend, block_cnt, run_start, slot_t, u, n_rows):
    n = u.shape[0] // ROW_TILE
    t = T_ROUTE
    grid_spec = pltpu.PrefetchScalarGridSpec(
        num_scalar_prefetch=4,
        grid=(n // t,),
        in_specs=[
            pl.BlockSpec((TOP_K * t,), lambda i, *_: (i,), memory_space=pltpu.SMEM),
            pl.BlockSpec((t * ROW_TILE, LANES), lambda i, *_: (i, 0)),
        ],
        out_specs=pl.BlockSpec(memory_space=pl.ANY),
        scratch_shapes=[
            pltpu.VMEM((2, TOP_K * t * ROW_TILE, LANES), F32),
            pltpu.VMEM((EXPERT_BLOCK * ROW_TILE, LANES), F32),
            pltpu.SemaphoreType.DMA((2,)),
            pltpu.SemaphoreType.DMA(()),
        ],
    )
    return pl.pallas_call(
        _dispatch_kernel,
        grid_spec=grid_spec,
        out_shape=jax.ShapeDtypeStruct((n_rows * ROW_TILE, LANES), F32),
        compiler_params=_params(("arbitrary",)),
        name="dispatch",
    )(pstart, pend, block_cnt, run_start, slot_t, u)


def _expert_kernel(be_ref, nb_ref, x_ref, wgu_ref, bgu_ref, wd_ref, bd_ref, y_ref,
                   wgu_s, wd32_s, wd_s):
    i = pl.program_id(0)
    nb = nb_ref[0]
    blk = EXPERT_BLOCK
    half = EXPERT_FF // 2
    first = jnp.logical_or(i == 0, be_ref[i] != be_ref[jnp.maximum(i - 1, 0)])

    @pl.when(jnp.logical_and(first, i < nb))
    def _():
        rows = 256
        for r in range(0, D_MODEL, rows):
            wgu_s[r:r + rows, :] = wgu_ref[r:r + rows, :].astype(BF16)
        for cs in range(D_MODEL // LANES):
            lanes = slice(cs * LANES, (cs + 1) * LANES)
            wd32_s[cs, pl.ds(0, half, stride=2), :] = wd_ref[0:half, lanes]
            wd32_s[cs, pl.ds(1, half, stride=2), :] = wd_ref[half:EXPERT_FF, lanes]
            wd_s[:, lanes] = wd32_s[cs].astype(BF16)

    @pl.when(i < nb)
    def _():
        rows = blk // EXPERT_SPLIT
        parts = range(EXPERT_SPLIT)
        even = (lax.broadcasted_iota(I32, (rows, LANES), 1) % 2) == 0
        xb = [_load_row_tiles(x_ref, rows, first=s * rows).astype(BF16) for s in parts]
        gu1 = [jnp.dot(xb[s], wgu_s[:, :EXPERT_FF], preferred_element_type=F32)
               + bgu_ref[:, :EXPERT_FF] for s in parts]
        gu2 = [jnp.dot(xb[s], wgu_s[:, EXPERT_FF:], preferred_element_type=F32)
               + bgu_ref[:, EXPERT_FF:] for s in parts]
        act = []
        for s in parts:
            cols = []
            for v in range(EXPERT_FF // LANES):
                a = gu1[s][:, v * LANES:(v + 1) * LANES]
                b = gu2[s][:, v * LANES:(v + 1) * LANES]
                g = jnp.where(even, a, pltpu.roll(b, 1, axis=1))
                l = jnp.where(even, pltpu.roll(a, LANES - 1, axis=1), b)
                g = jnp.minimum(g, SWIGLU_LIMIT)
                l = jnp.clip(l, -SWIGLU_LIMIT, SWIGLU_LIMIT)
                cols.append(((l + 1.0) * (g * jax.nn.sigmoid(g * SWIGLU_ALPHA))).astype(BF16))
            act.append(jnp.concatenate(cols, axis=1))
        y = [jnp.dot(act[s], wd_s[...], preferred_element_type=F32) + bd_ref[...] for s in parts]
        for s in parts:
            _store_row_tiles(y_ref, y[s], first=s * rows)

    @pl.when(i >= nb)
    def _():
        y_ref[...] = jnp.zeros_like(y_ref)


def _experts(block_expert, n_used, xs, wgu, bgu, wd, bd):
    n_rows = xs.shape[0] // ROW_TILE
    blk = EXPERT_BLOCK
    wspec = lambda r, c: pl.BlockSpec((None, r, c), lambda i, be, nb: (be[i], 0, 0))
    grid_spec = pltpu.PrefetchScalarGridSpec(
        num_scalar_prefetch=2,
        grid=(n_rows // blk,),
        in_specs=[
            pl.BlockSpec((blk * ROW_TILE, LANES), lambda i, be, nb: (jnp.minimum(i, nb[0] - 1), 0)),
            wspec(D_MODEL, 2 * EXPERT_FF), wspec(1, 2 * EXPERT_FF),
            wspec(EXPERT_FF, D_MODEL), wspec(1, D_MODEL),
        ],
        out_specs=pl.BlockSpec((blk * ROW_TILE, LANES), lambda i, be, nb: (i, 0)),
        scratch_shapes=[
            pltpu.VMEM((D_MODEL, 2 * EXPERT_FF), BF16),
            pltpu.VMEM((D_MODEL // LANES, EXPERT_FF, LANES), F32),
            pltpu.VMEM((EXPERT_FF, D_MODEL), BF16),
        ],
    )
    return pl.pallas_call(
        _expert_kernel,
        grid_spec=grid_spec,
        out_shape=jax.ShapeDtypeStruct((n_rows * ROW_TILE, LANES), F32),
        compiler_params=_params(("arbitrary",)),
        name="expert",
    )(block_expert, n_used, xs, wgu, bgu, wd, bd)


def _combine_kernel(cnt_ref, run_ref, slot0_ref, gate0_ref, slot_ref, gate_ref, h_ref, p_ref, y_ref,
                    wple_ref, gple_ref, wpg_ref, gfin_ref, o_ref, staged, moe_s, sem):
    t = T_ROUTE
    i = pl.program_id(0)
    n_steps = pl.num_programs(0)
    cur = i % 2

    def fetch(block, slot):
        def runs(e, off):
            cnt = cnt_ref[block * N_EXPERTS + e]
            _run_copies(cnt, y_ref, run_ref[block * N_EXPERTS + e], staged.at[slot], off,
                        sem.at[slot])
            return off + cnt

        lax.fori_loop(0, N_EXPERTS, runs, 0)

    def arrived(slot):
        pltpu.make_async_copy(staged.at[slot], staged.at[slot], sem.at[slot]).wait()

    def pick(slots, gates, slot, tok):
        acc = None
        for k in range(TOP_K):
            at = pl.multiple_of(slots[k * t + tok], ROW_TILE)
            term = gates[k * t + tok] * staged[slot, pl.ds(at, ROW_TILE), :]
            acc = term if acc is None else acc + term
        moe_s[slot, pl.ds(pl.multiple_of(tok * ROW_TILE, ROW_TILE), ROW_TILE), :] = acc

    def compute(slot):
        rows = t // 2
        parts = [pl.ds(s * rows, rows) for s in range(2)]
        emb = [jnp.dot(p_ref[rs, :].astype(BF16), wple_ref[...], preferred_element_type=F32)
               for rs in parts]
        h = [h_ref[rs, :] + _load_row_tiles(moe_s, rows, lead=(slot,), first=s * rows)
             for s, rs in enumerate(parts)]
        r = [_rms(hs, gple_ref[...]).astype(BF16) for hs in h]
        sig = [jax.nn.sigmoid(jnp.dot(rs, wpg_ref[...], preferred_element_type=F32)) for rs in r]
        for s, rs in enumerate(parts):
            o_ref[rs, :] = _rms(h[s] + emb[s] * sig[s], gfin_ref[...])

    @pl.when(i == 0)
    def _():
        fetch(0, 0)

        @pl.when(n_steps > 1)
        def _():
            fetch(1, 1)

        arrived(0)

        def first_block(tok, carry):
            pick(slot0_ref, gate0_ref, 0, tok)
            return carry

        lax.fori_loop(0, t, first_block, 0, unroll=DMA_UNROLL)

    @pl.when(i + 2 < n_steps)
    def _():
        fetch(i + 2, cur)

    for parity in range(2):
        @pl.when(jnp.logical_and(cur == parity, i + 1 < n_steps))
        def _(parity=parity):
            arrived(1 - parity)
            for tok in range(t):
                pick(slot_ref, gate_ref, 1 - parity, tok)
            compute(parity)

        @pl.when(jnp.logical_and(cur == parity, i + 1 == n_steps))
        def _(parity=parity):
            compute(parity)


def _combine(block_cnt, run_start, slot_t, gates_t, h, p2, y, wple, gple, wpg, gfin):
    n = h.shape[0]
    t = T_ROUTE
    last = n // t - 1
    full = lambda a: pl.BlockSpec(a.shape, lambda i, *_: (0,) * a.ndim)
    table = lambda ahead: pl.BlockSpec((TOP_K * t,), lambda i, *_: (jnp.minimum(i + ahead, last),),
                                       memory_space=pltpu.SMEM)
    grid_spec = pltpu.PrefetchScalarGridSpec(
        num_scalar_prefetch=2,
        grid=(n // t,),
        in_specs=[
            table(0), table(0), table(1), table(1),
            pl.BlockSpec((t, D_MODEL), lambda i, *_: (i, 0)),
            pl.BlockSpec((t, PLE_DIM), lambda i, *_: (i, 0)),
            pl.BlockSpec(memory_space=pl.ANY),
            full(wple), full(gple), full(wpg), full(gfin),
        ],
        out_specs=pl.BlockSpec((t, D_MODEL), lambda i, *_: (i, 0)),
        scratch_shapes=[
            pltpu.VMEM((2, TOP_K * t * ROW_TILE, LANES), F32),
            pltpu.VMEM((2, t * ROW_TILE, LANES), F32),
            pltpu.SemaphoreType.DMA((2,)),
        ],
    )
    return pl.pallas_call(
        _combine_kernel,
        grid_spec=grid_spec,
        out_shape=jax.ShapeDtypeStruct((n, D_MODEL), F32),
        compiler_params=_params(("arbitrary",)),
        name="combine",
    )(block_cnt, run_start, slot_t, gates_t, slot_t, gates_t, h, p2, y, wple, gple, wpg, gfin)


def kernel(x, p, g_mix, w_in, lambda_q1, lambda_k1, lambda_q2, lambda_k2, g_subln, w_attn_out,
           w_dw, b_dw, g_conv_ln, b_conv_ln, w_conv_out, w_o, g_ffn, w_router, b_router,
           w_gate_up, b_gate_up, w_down, b_down, w_ple, g_ple, w_ple_gate, g_final):
    b, s, d = x.shape
    n = b * s
    xt = x.reshape(n, d)
    vec = lambda a: a.reshape(1, -1)

    w_in0 = w_in[0].astype(BF16)
    w_vt = w_in0[:, 2 * ATTN_WIDTH:3 * ATTN_WIDTH].T
    proj, vt = _inproj(xt, vec(g_mix[0]), w_in0, w_vt)
    proj3 = proj.reshape(b, s, MAIN_COLS)
    attn = _attention(proj3, vt, vec(lambda_q1[0]), vec(lambda_k1[0]), vec(lambda_q2[0]),
                      vec(lambda_k2[0]), g_subln[0].reshape(HEAD_V, 1))
    conv = _conv(proj3, w_dw[0], vec(b_dw[0]), vec(g_conv_ln[0]), vec(b_conv_ln[0]))
    h1, u2, gates_t, slot_t, block_cnt = _mix(
        xt, attn.reshape(n, ATTN_WIDTH), conv.reshape(n, CONV_WIDTH), proj,
        w_attn_out[0].astype(BF16), w_conv_out[0].astype(BF16), w_o[0].astype(BF16),
        vec(g_ffn[0]), w_router[0].T, b_router[0].reshape(N_EXPERTS, 1))

    block_cnt = block_cnt[:, :, 0]
    counts = jnp.sum(block_cnt, axis=0)
    blk = EXPERT_BLOCK
    n_blocks = n * TOP_K // blk + N_EXPERTS
    padded = (counts + blk - 1) // blk * blk
    pend = jnp.cumsum(padded).astype(I32)
    pstart = pend - padded
    n_used = (pend[-1:] // blk).astype(I32)
    block_start = jnp.arange(n_blocks, dtype=I32) * blk
    block_expert = jnp.minimum(
        jnp.sum((pend[None, :] <= block_start[:, None]).astype(I32), axis=1), N_EXPERTS - 1)
    run_start = (pstart[None, :] + jnp.cumsum(block_cnt, axis=0) - block_cnt).astype(I32)
    cnt_flat = block_cnt.reshape(-1).astype(I32)
    run_flat = run_start.reshape(-1)

    slot_flat = slot_t.reshape(-1)
    gate_flat = gates_t.reshape(-1)

    xs = _dispatch(pstart, pend, cnt_flat, run_flat, slot_flat, u2, n_blocks * blk)
    y = _experts(block_expert, n_used, xs, w_gate_up[0], b_gate_up[0][:, None, :],
                 w_down[0], b_down[0][:, None, :])

    out = _combine(cnt_flat, run_flat, slot_flat, gate_flat, h1, p[0].reshape(n, PLE_DIM), y,
                   w_ple[0].astype(BF16), vec(g_ple[0]), w_ple_gate[0].astype(BF16), vec(g_final))
    return out.reshape(b, s, d)
```

```python
import jax
import jax.numpy as jnp
from jax import lax
from jax.experimental import pallas as pl
from jax.experimental.pallas import tpu as pltpu

F32 = jnp.float32
BF16 = jnp.bfloat16
I32 = jnp.int32

D_MODEL = 1024
ATTN_WIDTH = 512
DIFF_HEADS = 4
HEAD_DIM = 64
HEAD_V = 2 * HEAD_DIM
CONV_WIDTH = 512
CONV_KERNEL = 31
N_EXPERTS = 32
TOP_K = 4
EXPERT_FF = 1024
PLE_DIM = 256
SWIGLU_ALPHA = 1.702
SWIGLU_LIMIT = 7.0
EPS = 1e-5
IN_COLS = 3 * ATTN_WIDTH + 2 * CONV_WIDTH + 2 * D_MODEL
LAMBDA_INIT = 0.2
LOG2_E = 1.4426950408889634

MAIN_COLS = IN_COLS - ATTN_WIDTH
COL_CONV_A = 2
COL_CONV_B = 3
COL_GATE_A = 4
COL_GATE_B = 6

LANES = 128
SUBLANES = 8
TM_PROJ = 512
TQ = 256
T_ROUTE = 512
EXPERT_BLOCK = 512
EXPERT_SPLIT = 2
CONV_ROWS = 64
CONV_PAD = 32
RUN_LONG = 128
DMA_UNROLL = 16
VMEM_V7X = 64 * 1024 * 1024
VMEM_LIMIT = VMEM_V7X - 8 * 1024 * 1024

_NT = (((1,), (1,)), ((), ()))


def _params(sem, vmem=VMEM_LIMIT):
    return pltpu.CompilerParams(dimension_semantics=sem, vmem_limit_bytes=vmem)


def _rms(x, g):
    return x * lax.rsqrt(jnp.mean(x * x, axis=-1, keepdims=True) + EPS) * g


ROW_TILE = D_MODEL // LANES
assert ROW_TILE == SUBLANES


def _store_row_tiles(ref, x, lead=(), first=0):
    rows = x.shape[0]
    for g in range(ROW_TILE):
        ref[lead + (pl.ds(first * ROW_TILE + g, rows, stride=ROW_TILE), slice(None))] = (
            x[:, g * LANES:(g + 1) * LANES])


def _load_row_tiles(ref, rows, lead=(), first=0):
    return jnp.concatenate(
        [ref[lead + (pl.ds(first * ROW_TILE + g, rows, stride=ROW_TILE), slice(None))]
         for g in range(ROW_TILE)], axis=1)


def _inproj_kernel(x_ref, g_ref, w_ref, wvt_ref, o_ref, vt_ref):
    u = _rms(x_ref[...], g_ref[...]).astype(BF16)
    ch = 512
    for c in range(MAIN_COLS // ch):
        src = c * ch if c < 2 * ATTN_WIDTH // ch else c * ch + ATTN_WIDTH
        r = jnp.dot(u, w_ref[:, src:src + ch], preferred_element_type=F32)
        if c == 0:
            r = r * (HEAD_DIM ** -0.5 * LOG2_E)
        o_ref[:, c * ch:(c + 1) * ch] = r.astype(BF16)
    vt_ref[...] = lax.dot_general(wvt_ref[...], u, _NT,
                                  preferred_element_type=F32).astype(BF16)


def _inproj(xt, g, w, wvt):
    n = xt.shape[0]
    return pl.pallas_call(
        _inproj_kernel,
        grid=(n // TM_PROJ,),
        in_specs=[
            pl.BlockSpec((TM_PROJ, D_MODEL), lambda i: (i, 0)),
            pl.BlockSpec((1, D_MODEL), lambda i: (0, 0)),
            pl.BlockSpec((D_MODEL, IN_COLS), lambda i: (0, 0)),
            pl.BlockSpec((ATTN_WIDTH, D_MODEL), lambda i: (0, 0)),
        ],
        out_specs=[
            pl.BlockSpec((TM_PROJ, MAIN_COLS), lambda i: (i, 0)),
            pl.BlockSpec((ATTN_WIDTH, TM_PROJ), lambda i: (0, i)),
        ],
        out_shape=[
            jax.ShapeDtypeStruct((n, MAIN_COLS), BF16),
            jax.ShapeDtypeStruct((ATTN_WIDTH, n), BF16),
        ],
        compiler_params=_params(("parallel",)),
        name="inproj",
    )(xt, g, w, wvt)


def _attn_kernel(q_ref, k_ref, vt_ref, lq1_ref, lk1_ref, lq2_ref, lk2_ref, g_ref, o_ref):
    i = pl.program_id(1)
    s_len = k_ref.shape[0]
    lane = lax.broadcasted_iota(I32, (TQ, HEAD_V), 1)
    key = lax.broadcasted_iota(I32, (TQ, 2 * TQ), 0)
    qry = lax.broadcasted_iota(I32, (TQ, 2 * TQ), 1)
    causal = key <= jnp.where(qry >= TQ, qry - TQ, qry)
    lam = (jnp.exp(jnp.sum(lq1_ref[...] * lk1_ref[...]))
           - jnp.exp(jnp.sum(lq2_ref[...] * lk2_ref[...])) + LAMBDA_INIT)

    half_blocks = s_len // TQ // 2

    def block(c):
        chains = [(half, h) for half in range(2) for h in range(DIFF_HEADS)]
        cols = [slice(h * HEAD_V, (h + 1) * HEAD_V) for _, h in chains]
        n = [(c + half * half_blocks) * TQ for half, _ in chains]
        ids = range(len(chains))

        def stacked_q(j):
            q = q_ref[chains[j][0], :, cols[j]]
            zero = jnp.zeros_like(q)
            return jnp.concatenate([jnp.where(lane < HEAD_DIM, q, zero),
                                    jnp.where(lane >= HEAD_DIM, q, zero)], axis=0)

        s = [lax.dot_general(k_ref[0:n[j] + TQ, cols[j]], stacked_q(j), _NT,
                             preferred_element_type=F32) for j in ids]
        s_d = [jnp.where(causal, s[j][n[j]:], -1e30) for j in ids]
        m = [jnp.max(s_d[j], axis=0, keepdims=True) for j in ids]
        m = [jnp.maximum(m[j], jnp.max(s[j][:n[j]], axis=0, keepdims=True)) if n[j] else m[j]
             for j in ids]
        p = [jnp.exp2(s_d[j] - m[j]) for j in ids]
        p = [jnp.concatenate([jnp.exp2(s[j][:n[j]] - m[j]), p[j]], axis=0) if n[j] else p[j]
             for j in ids]
        l = [jnp.sum(p[j], axis=0, keepdims=True) for j in ids]
        acc = [jnp.dot(vt_ref[cols[j], 0:n[j] + TQ], p[j].astype(BF16),
                       preferred_element_type=F32) for j in ids]
        for j in ids:
            o12 = acc[j] / l[j]
            o = o12[:, :TQ] - lam * o12[:, TQ:]
            o = o * lax.rsqrt(jnp.mean(o * o, axis=0, keepdims=True) + EPS) * g_ref[...]
            o_ref[chains[j][0], :, cols[j]] = (o * (1.0 - LAMBDA_INIT)).T.astype(o_ref.dtype)

    for c in range(half_blocks):
        pl.when(i == c)(lambda c=c: block(c))


def _attention(proj3, vt, lq1, lk1, lq2, lk2, g_subln):
    b, s, width = proj3.shape
    vec = pl.BlockSpec((1, HEAD_DIM), lambda bi, i: (0, 0))
    halves = pl.BlockSpec((None, 2, TQ, ATTN_WIDTH), lambda bi, i: (bi, 0, i, 0))
    out = pl.pallas_call(
        _attn_kernel,
        grid=(b, s // TQ // 2),
        in_specs=[
            halves,
            pl.BlockSpec((None, s, ATTN_WIDTH), lambda bi, i: (bi, 0, 1)),
            pl.BlockSpec((ATTN_WIDTH, s), lambda bi, i: (0, bi)),
            vec, vec, vec, vec,
            pl.BlockSpec((HEAD_V, 1), lambda bi, i: (0, 0)),
        ],
        out_specs=halves,
        out_shape=jax.ShapeDtypeStruct((b, 2, s // 2, ATTN_WIDTH), BF16),
        compiler_params=_params(("parallel", "parallel")),
        name="attn",
    )(proj3.reshape(b, 2, s // 2, width), proj3, vt, lq1, lk1, lq2, lk2, g_subln)
    return out.reshape(b, s, ATTN_WIDTH)


def _conv_kernel(ca_ref, cb_ref, w_ref, b_ref, g_ref, beta_ref, o_ref, z_ref, stage_ref):
    s = ca_ref.shape[0]
    glu_rows = 256
    n_lg = CONV_WIDTH // LANES
    z_ref[:, 0:CONV_PAD, :] = jnp.zeros((n_lg, CONV_PAD, LANES), F32)

    def glu(c, carry):
        r0 = pl.multiple_of(c * glu_rows, glu_rows)
        a = ca_ref[pl.ds(r0, glu_rows), :].astype(F32)
        g = cb_ref[pl.ds(r0, glu_rows), :].astype(F32)
        z = a * jax.nn.sigmoid(g)
        for lg in range(n_lg):
            z_ref[lg, pl.ds(CONV_PAD + r0, glu_rows), :] = z[:, lg * LANES:(lg + 1) * LANES]
        return carry

    lax.fori_loop(0, s // glu_rows, glu, 0)

    def conv(c, carry):
        r0 = pl.multiple_of(c * CONV_ROWS, CONV_ROWS)
        stride = CONV_ROWS // SUBLANES
        for lg in range(n_lg):
            lanes = slice(lg * LANES, (lg + 1) * LANES)
            accs = [jnp.zeros((SUBLANES, LANES), F32) + b_ref[:, lanes] for _ in range(stride)]
            for j in range(CONV_KERNEL):
                off = CONV_PAD - (CONV_KERNEL - 1) + j
                wj = w_ref[j:j + 1, lanes]
                for g in range(stride):
                    accs[g] = accs[g] + wj * z_ref[lg, pl.ds(r0 + off + g, SUBLANES, stride=stride), :]
            for g in range(stride):
                stage_ref[lg, pl.ds(g, SUBLANES, stride=stride), :] = accs[g]
        acc = jnp.concatenate([stage_ref[lg] for lg in range(n_lg)], axis=1)
        mu = jnp.mean(acc, axis=-1, keepdims=True)
        xc = acc - mu
        y = xc * lax.rsqrt(jnp.mean(xc * xc, axis=-1, keepdims=True) + EPS)
        y = y * g_ref[...] + beta_ref[...]
        o_ref[pl.ds(r0, CONV_ROWS), :] = (y * jax.nn.sigmoid(y)).astype(o_ref.dtype)
        return carry

    lax.fori_loop(0, s // CONV_ROWS, conv, 0, unroll=2)


def _conv(proj3, w_dw, b_dw, g_ln, b_ln):
    b, s, _ = proj3.shape
    vec = pl.BlockSpec((1, CONV_WIDTH), lambda bi: (0, 0))
    return pl.pallas_call(
        _conv_kernel,
        grid=(b,),
        in_specs=[
            pl.BlockSpec((None, s, CONV_WIDTH), lambda bi: (bi, 0, COL_CONV_A)),
            pl.BlockSpec((None, s, CONV_WIDTH), lambda bi: (bi, 0, COL_CONV_B)),
            pl.BlockSpec((CONV_KERNEL, CONV_WIDTH), lambda bi: (0, 0)),
            vec, vec, vec,
        ],
        out_specs=pl.BlockSpec((None, s, CONV_WIDTH), lambda bi: (bi, 0, 0)),
        out_shape=jax.ShapeDtypeStruct((b, s, CONV_WIDTH), BF16),
        scratch_shapes=[
            pltpu.VMEM((CONV_WIDTH // LANES, s + CONV_PAD, LANES), F32),
            pltpu.VMEM((CONV_WIDTH // LANES, CONV_ROWS, LANES), F32),
        ],
        compiler_params=_params(("parallel",)),
        name="conv",
    )(proj3, proj3, w_dw, b_dw, g_ln, b_ln)


def _split_bf16(a):
    hi = a.astype(BF16)
    lo = (a - hi.astype(F32)).astype(BF16)
    return hi, lo


def _mix_kernel(x_ref, o_ref, c_ref, ga_ref, gb_ref,
                wa_ref, wc_ref, wo_ref, gffn_ref, wr_ref, br_ref, before_ref,
                h_ref, u_ref, gate_ref, slot_ref, cnt_ref):
    a = jnp.dot(o_ref[...], wa_ref[...], preferred_element_type=F32)
    b = jnp.dot(c_ref[...], wc_ref[...], preferred_element_type=F32)
    m = (jax.nn.sigmoid(ga_ref[...].astype(F32)) * a
         + jax.nn.sigmoid(gb_ref[...].astype(F32)) * b)
    acc = x_ref[...] + jnp.dot(m.astype(BF16), wo_ref[...], preferred_element_type=F32)
    h_ref[...] = acc
    u = _rms(acc, gffn_ref[...])
    _store_row_tiles(u_ref, u)

    u_hi, u_lo = _split_bf16(u)
    w_hi, w_lo = _split_bf16(wr_ref[...])
    logits = (lax.dot_general(w_hi, u_hi, _NT, preferred_element_type=F32)
              + lax.dot_general(w_hi, u_lo, _NT, preferred_element_type=F32)
              + lax.dot_general(w_lo, u_hi, _NT, preferred_element_type=F32)
              + br_ref[...])
    eidx = lax.broadcasted_iota(I32, logits.shape, 0)
    vals, idxs = [], []
    for _ in range(TOP_K):
        mx = jnp.max(logits, axis=0, keepdims=True)
        sel = jnp.min(jnp.where(logits == mx, eidx, N_EXPERTS), axis=0, keepdims=True)
        vals.append(mx)
        idxs.append(sel)
        logits = jnp.where(eidx == sel, -jnp.inf, logits)
    ex = [jnp.exp(v - vals[0]) for v in vals]
    den = sum(ex[1:], ex[0])
    gate_ref[...] = jnp.concatenate([e / den for e in ex], axis=0)

    tm = logits.shape[1]
    onehot = [eidx == idxs[k] for k in range(TOP_K)]
    routed = onehot[0]
    for k in range(1, TOP_K):
        routed = routed | onehot[k]
    member = jnp.where(routed, 1.0, 0.0).astype(BF16)
    rank = jnp.dot(member, before_ref[...], preferred_element_type=F32)
    cnt_ref[...] = jnp.sum(member.astype(F32), axis=1, keepdims=True).astype(I32)
    cnt_row = lax.dot_general(jnp.ones((SUBLANES, tm), BF16), member, _NT,
                              preferred_element_type=F32)[0:1]
    e_row = lax.broadcasted_iota(I32, (N_EXPERTS, N_EXPERTS), 0)
    e_col = lax.broadcasted_iota(I32, (N_EXPERTS, N_EXPERTS), 1)
    first = jnp.sum(jnp.where(e_col < e_row, cnt_row, 0.0), axis=1, keepdims=True)
    slot_ref[...] = jnp.concatenate(
        [jnp.sum(jnp.where(onehot[k], rank + first, 0.0), axis=0, keepdims=True)
         for k in range(TOP_K)], axis=0).astype(I32) * ROW_TILE


def _mix(xt, o, c, proj, wa, wc, wo, g_ffn, wr_t, br):
    n = xt.shape[0]
    tm = T_ROUTE
    pos = jnp.arange(tm, dtype=I32)
    before = (pos[:, None] < pos[None, :]).astype(BF16)
    row = lambda w, j: pl.BlockSpec((tm, w), lambda i, j=j: (i, j))
    full = lambda a: pl.BlockSpec(a.shape, lambda i: (0,) * a.ndim)
    return pl.pallas_call(
        _mix_kernel,
        grid=(n // tm,),
        in_specs=[
            row(D_MODEL, 0), row(ATTN_WIDTH, 0), row(CONV_WIDTH, 0),
            row(D_MODEL, COL_GATE_A // 2), row(D_MODEL, COL_GATE_B // 2),
            full(wa), full(wc), full(wo), full(g_ffn), full(wr_t), full(br), full(before),
        ],
        out_specs=[
            pl.BlockSpec((tm, D_MODEL), lambda i: (i, 0)),
            pl.BlockSpec((tm * ROW_TILE, LANES), lambda i: (i, 0)),
            pl.BlockSpec((None, TOP_K, tm), lambda i: (i, 0, 0)),
            pl.BlockSpec((None, TOP_K, tm), lambda i: (i, 0, 0)),
            pl.BlockSpec((None, N_EXPERTS, 1), lambda i: (i, 0, 0)),
        ],
        out_shape=[
            jax.ShapeDtypeStruct((n, D_MODEL), F32),
            jax.ShapeDtypeStruct((n * ROW_TILE, LANES), F32),
            jax.ShapeDtypeStruct((n // tm, TOP_K, tm), F32),
            jax.ShapeDtypeStruct((n // tm, TOP_K, tm), I32),
            jax.ShapeDtypeStruct((n // tm, N_EXPERTS, 1), I32),
        ],
        compiler_params=_params(("parallel",)),
        name="mix",
    )(xt, o, c, proj, proj, wa, wc, wo, g_ffn, wr_t, br, before)


def _run_copies(cnt, src_ref, src_row, dst_ref, dst_row, sem):
    def pieces(bits, src_row, dst_row):
        for bit in bits:
            size = 1 << bit
            piece = cnt & size

            @pl.when(piece != 0)
            def _(size=size, src_row=src_row, dst_row=dst_row):
                rows = size * ROW_TILE
                pltpu.make_async_copy(
                    src_ref.at[pl.ds(pl.multiple_of(src_row * ROW_TILE, ROW_TILE), rows), :],
                    dst_ref.at[pl.ds(pl.multiple_of(dst_row * ROW_TILE, ROW_TILE), rows), :],
                    sem).start()

            src_row = src_row + piece
            dst_row = dst_row + piece

    long_bits = RUN_LONG.bit_length() - 1

    @pl.when(cnt >= RUN_LONG)
    def _():
        pieces(reversed(range(long_bits, T_ROUTE.bit_length())), src_row, dst_row)

    head = cnt & -RUN_LONG
    pieces(reversed(range(long_bits)), src_row + head, dst_row + head)


def _dispatch_kernel(pstart_ref, pend_ref, cnt_ref, run_ref, slot_ref, u_ref, xs_ref,
                     grouped, zero_ref, sem, zsem):
    t = T_ROUTE
    i = pl.program_id(0)
    cur = i % 2
    blk = EXPERT_BLOCK * ROW_TILE
    buf = grouped.at[cur]

    def drain(slot):
        pltpu.make_async_copy(grouped.at[slot], grouped.at[slot], sem.at[slot]).wait()

    @pl.when(i == 0)
    def _():
        zero_ref[...] = jnp.zeros_like(zero_ref)

        def zero_block(first_row):
            return pltpu.make_async_copy(
                zero_ref, xs_ref.at[pl.ds(pl.multiple_of(first_row * ROW_TILE, blk), blk), :], zsem)

        for act in (lambda cp: cp.start(), lambda cp: cp.wait()):
            def tails(e, carry, act=act):
                @pl.when(pend_ref[e] > pstart_ref[e])
                def _():
                    act(zero_block(pend_ref[e] - EXPERT_BLOCK))
                return carry

            lax.fori_loop(0, N_EXPERTS, tails, 0)

            def unused(b, carry, act=act):
                act(zero_block(b * EXPERT_BLOCK))
                return carry

            lax.fori_loop(pend_ref[N_EXPERTS - 1] // EXPERT_BLOCK,
                          xs_ref.shape[0] // blk, unused, 0)

    @pl.when(i >= 2)
    def _():
        drain(cur)

    def group(tok, carry):
        tile = u_ref[pl.ds(pl.multiple_of(tok * ROW_TILE, ROW_TILE), ROW_TILE), :]
        for k in range(TOP_K):
            slot = pl.multiple_of(slot_ref[k * t + tok], ROW_TILE)
            buf[pl.ds(slot, ROW_TILE), :] = tile
        return carry

    lax.fori_loop(0, t, group, 0, unroll=DMA_UNROLL)

    def runs(e, off):
        cnt = cnt_ref[i * N_EXPERTS + e]
        _run_copies(cnt, buf, off, xs_ref, run_ref[i * N_EXPERTS + e], sem.at[cur])
        return off + cnt

    lax.fori_loop(0, N_EXPERTS, runs, 0)

    @pl.when(i == pl.num_programs(0) - 1)
    def _():
        drain(cur)

        @pl.when(i >= 1)
        def _():
            drain(1 - cur)


def _dispatch(pstart, pend, block_cnt, run_start, slot_t, u, n_rows):
    n = u.shape[0] // ROW_TILE
    t = T_ROUTE
    grid_spec = pltpu.PrefetchScalarGridSpec(
        num_scalar_prefetch=4,
        grid=(n // t,),
        in_specs=[
            pl.BlockSpec((TOP_K * t,), lambda i, *_: (i,), memory_space=pltpu.SMEM),
            pl.BlockSpec((t * ROW_TILE, LANES), lambda i, *_: (i, 0)),
        ],
        out_specs=pl.BlockSpec(memory_space=pl.ANY),
        scratch_shapes=[
            pltpu.VMEM((2, TOP_K * t * ROW_TILE, LANES), F32),
            pltpu.VMEM((EXPERT_BLOCK * ROW_TILE, LANES), F32),
            pltpu.SemaphoreType.DMA((2,)),
            pltpu.SemaphoreType.DMA(()),
        ],
    )
    return pl.pallas_call(
        _dispatch_kernel,
        grid_spec=grid_spec,
        out_shape=jax.ShapeDtypeStruct((n_rows * ROW_TILE, LANES), F32),
        compiler_params=_params(("arbitrary",)),
        name="dispatch",
    )(pstart, pend, block_cnt, run_start, slot_t, u)


def _expert_kernel(be_ref, nb_ref, x_ref, wgu_ref, bgu_ref, wd_ref, bd_ref, y_ref,
                   wgu_s, wd32_s, wd_s):
    i = pl.program_id(0)
    nb = nb_ref[0]
    blk = EXPERT_BLOCK
    half = EXPERT_FF // 2
    first = jnp.logical_or(i == 0, be_ref[i] != be_ref[jnp.maximum(i - 1, 0)])

    @pl.when(jnp.logical_and(first, i < nb))
    def _():
        rows = 256
        for r in range(0, D_MODEL, rows):
            wgu_s[r:r + rows, :] = wgu_ref[r:r + rows, :].astype(BF16)
        for cs in range(D_MODEL // LANES):
            lanes = slice(cs * LANES, (cs + 1) * LANES)
            wd32_s[cs, pl.ds(0, half, stride=2), :] = wd_ref[0:half, lanes]
            wd32_s[cs, pl.ds(1, half, stride=2), :] = wd_ref[half:EXPERT_FF, lanes]
            wd_s[:, lanes] = wd32_s[cs].astype(BF16)

    @pl.when(i < nb)
    def _():
        rows = blk // EXPERT_SPLIT
        parts = range(EXPERT_SPLIT)
        even = (lax.broadcasted_iota(I32, (rows, LANES), 1) % 2) == 0
        xb = [_load_row_tiles(x_ref, rows, first=s * rows).astype(BF16) for s in parts]
        gu1 = [jnp.dot(xb[s], wgu_s[:, :EXPERT_FF], preferred_element_type=F32)
               + bgu_ref[:, :EXPERT_FF] for s in parts]
        gu2 = [jnp.dot(xb[s], wgu_s[:, EXPERT_FF:], preferred_element_type=F32)
               + bgu_ref[:, EXPERT_FF:] for s in parts]
        act = []
        for s in parts:
            cols = []
            for v in range(EXPERT_FF // LANES):
                a = gu1[s][:, v * LANES:(v + 1) * LANES]
                b = gu2[s][:, v * LANES:(v + 1) * LANES]
                g = jnp.where(even, a, pltpu.roll(b, 1, axis=1))
                l = jnp.where(even, pltpu.roll(a, LANES - 1, axis=1), b)
                g = jnp.minimum(g, SWIGLU_LIMIT)
                l = jnp.clip(l, -SWIGLU_LIMIT, SWIGLU_LIMIT)
                cols.append(((l + 1.0) * (g * jax.nn.sigmoid(g * SWIGLU_ALPHA))).astype(BF16))
            act.append(jnp.concatenate(cols, axis=1))
        y = [jnp.dot(act[s], wd_s[...], preferred_element_type=F32) + bd_ref[...] for s in parts]
        for s in parts:
            _store_row_tiles(y_ref, y[s], first=s * rows)

    @pl.when(i >= nb)
    def _():
        y_ref[...] = jnp.zeros_like(y_ref)


def _experts(block_expert, n_used, xs, wgu, bgu, wd, bd):
    n_rows = xs.shape[0] // ROW_TILE
    blk = EXPERT_BLOCK
    wspec = lambda r, c: pl.BlockSpec((None, r, c), lambda i, be, nb: (be[i], 0, 0))
    grid_spec = pltpu.PrefetchScalarGridSpec(
        num_scalar_prefetch=2,
        grid=(n_rows // blk,),
        in_specs=[
            pl.BlockSpec((blk * ROW_TILE, LANES), lambda i, be, nb: (jnp.minimum(i, nb[0] - 1), 0)),
            wspec(D_MODEL, 2 * EXPERT_FF), wspec(1, 2 * EXPERT_FF),
            wspec(EXPERT_FF, D_MODEL), wspec(1, D_MODEL),
        ],
        out_specs=pl.BlockSpec((blk * ROW_TILE, LANES), lambda i, be, nb: (i, 0)),
        scratch_shapes=[
            pltpu.VMEM((D_MODEL, 2 * EXPERT_FF), BF16),
            pltpu.VMEM((D_MODEL // LANES, EXPERT_FF, LANES), F32),
            pltpu.VMEM((EXPERT_FF, D_MODEL), BF16),
        ],
    )
    return pl.pallas_call(
        _expert_kernel,
        grid_spec=grid_spec,
        out_shape=jax.ShapeDtypeStruct((n_rows * ROW_TILE, LANES), F32),
        compiler_params=_params(("arbitrary",)),
        name="expert",
    )(block_expert, n_used, xs, wgu, bgu, wd, bd)


def _combine_kernel(cnt_ref, run_ref, slot0_ref, gate0_ref, slot_ref, gate_ref, h_ref, p_ref, y_ref,
                    wple_ref, gple_ref, wpg_ref, gfin_ref, o_ref, staged, moe_s, sem):
    t = T_ROUTE
    i = pl.program_id(0)
    n_steps = pl.num_programs(0)
    cur = i % 2

    def fetch(block, slot):
        def runs(e, off):
            cnt = cnt_ref[block * N_EXPERTS + e]
            _run_copies(cnt, y_ref, run_ref[block * N_EXPERTS + e], staged.at[slot], off,
                        sem.at[slot])
            return off + cnt

        lax.fori_loop(0, N_EXPERTS, runs, 0)

    def arrived(slot):
        pltpu.make_async_copy(staged.at[slot], staged.at[slot], sem.at[slot]).wait()

    def pick(slots, gates, slot, tok):
        acc = None
        for k in range(TOP_K):
            at = pl.multiple_of(slots[k * t + tok], ROW_TILE)
            term = gates[k * t + tok] * staged[slot, pl.ds(at, ROW_TILE), :]
            acc = term if acc is None else acc + term
        moe_s[slot, pl.ds(pl.multiple_of(tok * ROW_TILE, ROW_TILE), ROW_TILE), :] = acc

    def compute(slot):
        rows = t // 2
        parts = [pl.ds(s * rows, rows) for s in range(2)]
        emb = [jnp.dot(p_ref[rs, :].astype(BF16), wple_ref[...], preferred_element_type=F32)
               for rs in parts]
        h = [h_ref[rs, :] + _load_row_tiles(moe_s, rows, lead=(slot,), first=s * rows)
             for s, rs in enumerate(parts)]
        r = [_rms(hs, gple_ref[...]).astype(BF16) for hs in h]
        sig = [jax.nn.sigmoid(jnp.dot(rs, wpg_ref[...], preferred_element_type=F32)) for rs in r]
        for s, rs in enumerate(parts):
            o_ref[rs, :] = _rms(h[s] + emb[s] * sig[s], gfin_ref[...])

    @pl.when(i == 0)
    def _():
        fetch(0, 0)

        @pl.when(n_steps > 1)
        def _():
            fetch(1, 1)

        arrived(0)

        def first_block(tok, carry):
            pick(slot0_ref, gate0_ref, 0, tok)
            return carry

        lax.fori_loop(0, t, first_block, 0, unroll=DMA_UNROLL)

    @pl.when(i + 2 < n_steps)
    def _():
        fetch(i + 2, cur)

    for parity in range(2):
        @pl.when(jnp.logical_and(cur == parity, i + 1 < n_steps))
        def _(parity=parity):
            arrived(1 - parity)
            for tok in range(t):
                pick(slot_ref, gate_ref, 1 - parity, tok)
            compute(parity)

        @pl.when(jnp.logical_and(cur == parity, i + 1 == n_steps))
        def _(parity=parity):
            compute(parity)


def _combine(block_cnt, run_start, slot_t, gates_t, h, p2, y, wple, gple, wpg, gfin):
    n = h.shape[0]
    t = T_ROUTE
    last = n // t - 1
    full = lambda a: pl.BlockSpec(a.shape, lambda i, *_: (0,) * a.ndim)
    table = lambda ahead: pl.BlockSpec((TOP_K * t,), lambda i, *_: (jnp.minimum(i + ahead, last),),
                                       memory_space=pltpu.SMEM)
    grid_spec = pltpu.PrefetchScalarGridSpec(
        num_scalar_prefetch=2,
        grid=(n // t,),
        in_specs=[
            table(0), table(0), table(1), table(1),
            pl.BlockSpec((t, D_MODEL), lambda i, *_: (i, 0)),
            pl.BlockSpec((t, PLE_DIM), lambda i, *_: (i, 0)),
            pl.BlockSpec(memory_space=pl.ANY),
            full(wple), full(gple), full(wpg), full(gfin),
        ],
        out_specs=pl.BlockSpec((t, D_MODEL), lambda i, *_: (i, 0)),
        scratch_shapes=[
            pltpu.VMEM((2, TOP_K * t * ROW_TILE, LANES), F32),
            pltpu.VMEM((2, t * ROW_TILE, LANES), F32),
            pltpu.SemaphoreType.DMA((2,)),
        ],
    )
    return pl.pallas_call(
        _combine_kernel,
        grid_spec=grid_spec,
        out_shape=jax.ShapeDtypeStruct((n, D_MODEL), F32),
        compiler_params=_params(("arbitrary",)),
        name="combine",
    )(block_cnt, run_start, slot_t, gates_t, slot_t, gates_t, h, p2, y, wple, gple, wpg, gfin)


def kernel(x, p, g_mix, w_in, lambda_q1, lambda_k1, lambda_q2, lambda_k2, g_subln, w_attn_out,
           w_dw, b_dw, g_conv_ln, b_conv_ln, w_conv_out, w_o, g_ffn, w_router, b_router,
           w_gate_up, b_gate_up, w_down, b_down, w_ple, g_ple, w_ple_gate, g_final):
    b, s, d = x.shape
    n = b * s
    xt = x.reshape(n, d)
    vec = lambda a: a.reshape(1, -1)

    w_in0 = w_in[0].astype(BF16)
    w_vt = w_in0[:, 2 * ATTN_WIDTH:3 * ATTN_WIDTH].T
    proj, vt = _inproj(xt, vec(g_mix[0]), w_in0, w_vt)
    proj3 = proj.reshape(b, s, MAIN_COLS)
    attn = _attention(proj3, vt, vec(lambda_q1[0]), vec(lambda_k1[0]), vec(lambda_q2[0]),
                      vec(lambda_k2[0]), g_subln[0].reshape(HEAD_V, 1))
    conv = _conv(proj3, w_dw[0], vec(b_dw[0]), vec(g_conv_ln[0]), vec(b_conv_ln[0]))
    h1, u2, gates_t, slot_t, block_cnt = _mix(
        xt, attn.reshape(n, ATTN_WIDTH), conv.reshape(n, CONV_WIDTH), proj,
        w_attn_out[0].astype(BF16), w_conv_out[0].astype(BF16), w_o[0].astype(BF16),
        vec(g_ffn[0]), w_router[0].T, b_router[0].reshape(N_EXPERTS, 1))

    block_cnt = block_cnt[:, :, 0]
    counts = jnp.sum(block_cnt, axis=0)
    blk = EXPERT_BLOCK
    n_blocks = n * TOP_K // blk + N_EXPERTS
    padded = (counts + blk - 1) // blk * blk
    pend = jnp.cumsum(padded).astype(I32)
    pstart = pend - padded
    n_used = (pend[-1:] // blk).astype(I32)
    block_start = jnp.arange(n_blocks, dtype=I32) * blk
    block_expert = jnp.minimum(
        jnp.sum((pend[None, :] <= block_start[:, None]).astype(I32), axis=1), N_EXPERTS - 1)
    run_start = (pstart[None, :] + jnp.cumsum(block_cnt, axis=0) - block_cnt).astype(I32)
    cnt_flat = block_cnt.reshape(-1).astype(I32)
    run_flat = run_start.reshape(-1)

    slot_flat = slot_t.reshape(-1)
    gate_flat = gates_t.reshape(-1)

    xs = _dispatch(pstart, pend, cnt_flat, run_flat, slot_flat, u2, n_blocks * blk)
    y = _experts(block_expert, n_used, xs, w_gate_up[0], b_gate_up[0][:, None, :],
                 w_down[0], b_down[0][:, None, :])

    out = _combine(cnt_flat, run_flat, slot_flat, gate_flat, h1, p[0].reshape(n, PLE_DIM), y,
                   w_ple[0].astype(BF16), vec(g_ple[0]), w_ple_gate[0].astype(BF16), vec(g_final))
    return out.reshape(b, s, d)
```

```python
import jax
import jax.numpy as jnp
from jax import lax
from jax.experimental import pallas as pl
from jax.experimental.pallas import tpu as pltpu

F32 = jnp.float32
BF16 = jnp.bfloat16
I32 = jnp.int32

D_MODEL = 1024
ATTN_WIDTH = 512
DIFF_HEADS = 4
HEAD_DIM = 64
HEAD_V = 2 * HEAD_DIM
CONV_WIDTH = 512
CONV_KERNEL = 31
N_EXPERTS = 32
TOP_K = 4
EXPERT_FF = 1024
PLE_DIM = 256
SWIGLU_ALPHA = 1.702
SWIGLU_LIMIT = 7.0
EPS = 1e-5
IN_COLS = 3 * ATTN_WIDTH + 2 * CONV_WIDTH + 2 * D_MODEL
LAMBDA_INIT = 0.2
LOG2_E = 1.4426950408889634

MAIN_COLS = IN_COLS - ATTN_WIDTH
COL_CONV_A = 2
COL_CONV_B = 3
COL_GATE_A = 4
COL_GATE_B = 6

LANES = 128
SUBLANES = 8
TM_PROJ = 512
TQ = 256
T_ROUTE = 512
EXPERT_BLOCK = 512
EXPERT_SPLIT = 2
CONV_ROWS = 64
CONV_PAD = 32
RUN_LONG = 128
DMA_UNROLL = 16
VMEM_V7X = 64 * 1024 * 1024
VMEM_LIMIT = VMEM_V7X - 8 * 1024 * 1024

_NT = (((1,), (1,)), ((), ()))


def _params(sem, vmem=VMEM_LIMIT):
    return pltpu.CompilerParams(dimension_semantics=sem, vmem_limit_bytes=vmem)


def _rms(x, g):
    return x * lax.rsqrt(jnp.mean(x * x, axis=-1, keepdims=True) + EPS) * g


ROW_TILE = D_MODEL // LANES
assert ROW_TILE == SUBLANES


def _store_row_tiles(ref, x, lead=(), first=0):
    rows = x.shape[0]
    for g in range(ROW_TILE):
        ref[lead + (pl.ds(first * ROW_TILE + g, rows, stride=ROW_TILE), slice(None))] = (
            x[:, g * LANES:(g + 1) * LANES])


def _load_row_tiles(ref, rows, lead=(), first=0):
    return jnp.concatenate(
        [ref[lead + (pl.ds(first * ROW_TILE + g, rows, stride=ROW_TILE), slice(None))]
         for g in range(ROW_TILE)], axis=1)


def _inproj_kernel(x_ref, g_ref, w_ref, o_ref, vt_ref):
    u = _rms(x_ref[...], g_ref[...]).astype(BF16)
    ch = 512
    for c in range(MAIN_COLS // ch):
        src = c * ch if c < 2 * ATTN_WIDTH // ch else c * ch + ATTN_WIDTH
        r = jnp.dot(u, w_ref[:, src:src + ch], preferred_element_type=F32)
        if c == 0:
            r = r * (HEAD_DIM ** -0.5 * LOG2_E)
        o_ref[:, c * ch:(c + 1) * ch] = r.astype(BF16)
    w_v = w_ref[:, 2 * ATTN_WIDTH:3 * ATTN_WIDTH]
    vt_ref[...] = lax.dot_general(w_v, u, (((0,), (1,)), ((), ())),
                                  preferred_element_type=F32).astype(BF16)


def _inproj(xt, g, w):
    n = xt.shape[0]
    return pl.pallas_call(
        _inproj_kernel,
        grid=(n // TM_PROJ,),
        in_specs=[
            pl.BlockSpec((TM_PROJ, D_MODEL), lambda i: (i, 0)),
            pl.BlockSpec((1, D_MODEL), lambda i: (0, 0)),
            pl.BlockSpec((D_MODEL, IN_COLS), lambda i: (0, 0)),
        ],
        out_specs=[
            pl.BlockSpec((TM_PROJ, MAIN_COLS), lambda i: (i, 0)),
            pl.BlockSpec((ATTN_WIDTH, TM_PROJ), lambda i: (0, i)),
        ],
        out_shape=[
            jax.ShapeDtypeStruct((n, MAIN_COLS), BF16),
            jax.ShapeDtypeStruct((ATTN_WIDTH, n), BF16),
        ],
        compiler_params=_params(("parallel",)),
        name="inproj",
    )(xt, g, w)


def _attn_kernel(q_ref, k_ref, vt_ref, lq1_ref, lk1_ref, lq2_ref, lk2_ref, g_ref, o_ref):
    i = pl.program_id(1)
    s_len = k_ref.shape[0]
    lane = lax.broadcasted_iota(I32, (TQ, HEAD_V), 1)
    key = lax.broadcasted_iota(I32, (TQ, 2 * TQ), 0)
    qry = lax.broadcasted_iota(I32, (TQ, 2 * TQ), 1)
    causal = key <= jnp.where(qry >= TQ, qry - TQ, qry)
    lam = (jnp.exp(jnp.sum(lq1_ref[...] * lk1_ref[...]))
           - jnp.exp(jnp.sum(lq2_ref[...] * lk2_ref[...])) + LAMBDA_INIT)

    half_blocks = s_len // TQ // 2

    def block(c):
        chains = [(half, h) for half in range(2) for h in range(DIFF_HEADS)]
        cols = [slice(h * HEAD_V, (h + 1) * HEAD_V) for _, h in chains]
        n = [(c + half * half_blocks) * TQ for half, _ in chains]
        ids = range(len(chains))

        def stacked_q(j):
            q = q_ref[chains[j][0], :, cols[j]]
            zero = jnp.zeros_like(q)
            return jnp.concatenate([jnp.where(lane < HEAD_DIM, q, zero),
                                    jnp.where(lane >= HEAD_DIM, q, zero)], axis=0)

        s = [lax.dot_general(k_ref[0:n[j] + TQ, cols[j]], stacked_q(j), _NT,
                             preferred_element_type=F32) for j in ids]
        s_d = [jnp.where(causal, s[j][n[j]:], -1e30) for j in ids]
        m = [jnp.max(s_d[j], axis=0, keepdims=True) for j in ids]
        m = [jnp.maximum(m[j], jnp.max(s[j][:n[j]], axis=0, keepdims=True)) if n[j] else m[j]
             for j in ids]
        p = [jnp.exp2(s_d[j] - m[j]) for j in ids]
        p = [jnp.concatenate([jnp.exp2(s[j][:n[j]] - m[j]), p[j]], axis=0) if n[j] else p[j]
             for j in ids]
        l = [jnp.sum(p[j], axis=0, keepdims=True) for j in ids]
        acc = [jnp.dot(vt_ref[cols[j], 0:n[j] + TQ], p[j].astype(BF16),
                       preferred_element_type=F32) for j in ids]
        for j in ids:
            o12 = acc[j] / l[j]
            o = o12[:, :TQ] - lam * o12[:, TQ:]
            o = o * lax.rsqrt(jnp.mean(o * o, axis=0, keepdims=True) + EPS) * g_ref[...]
            o_ref[chains[j][0], :, cols[j]] = (o * (1.0 - LAMBDA_INIT)).T.astype(o_ref.dtype)

    for c in range(half_blocks):
        pl.when(i == c)(lambda c=c: block(c))


def _attention(proj3, vt, lq1, lk1, lq2, lk2, g_subln):
    b, s, width = proj3.shape
    vec = pl.BlockSpec((1, HEAD_DIM), lambda bi, i: (0, 0))
    halves = pl.BlockSpec((None, 2, TQ, ATTN_WIDTH), lambda bi, i: (bi, 0, i, 0))
    out = pl.pallas_call(
        _attn_kernel,
        grid=(b, s // TQ // 2),
        in_specs=[
            halves,
            pl.BlockSpec((None, s, ATTN_WIDTH), lambda bi, i: (bi, 0, 1)),
            pl.BlockSpec((ATTN_WIDTH, s), lambda bi, i: (0, bi)),
            vec, vec, vec, vec,
            pl.BlockSpec((HEAD_V, 1), lambda bi, i: (0, 0)),
        ],
        out_specs=halves,
        out_shape=jax.ShapeDtypeStruct((b, 2, s // 2, ATTN_WIDTH), BF16),
        compiler_params=_params(("parallel", "parallel")),
        name="attn",
    )(proj3.reshape(b, 2, s // 2, width), proj3, vt, lq1, lk1, lq2, lk2, g_subln)
    return out.reshape(b, s, ATTN_WIDTH)


def _conv_kernel(ca_ref, cb_ref, w_ref, b_ref, g_ref, beta_ref, o_ref, z_ref, stage_ref):
    s = ca_ref.shape[0]
    glu_rows = 256
    n_lg = CONV_WIDTH // LANES
    z_ref[:, 0:CONV_PAD, :] = jnp.zeros((n_lg, CONV_PAD, LANES), F32)

    def glu(c, carry):
        r0 = pl.multiple_of(c * glu_rows, glu_rows)
        a = ca_ref[pl.ds(r0, glu_rows), :].astype(F32)
        g = cb_ref[pl.ds(r0, glu_rows), :].astype(F32)
        z = a * jax.nn.sigmoid(g)
        for lg in range(n_lg):
            z_ref[lg, pl.ds(CONV_PAD + r0, glu_rows), :] = z[:, lg * LANES:(lg + 1) * LANES]
        return carry

    lax.fori_loop(0, s // glu_rows, glu, 0)

    def conv(c, carry):
        r0 = pl.multiple_of(c * CONV_ROWS, CONV_ROWS)
        stride = CONV_ROWS // SUBLANES
        for lg in range(n_lg):
            lanes = slice(lg * LANES, (lg + 1) * LANES)
            accs = [jnp.zeros((SUBLANES, LANES), F32) + b_ref[:, lanes] for _ in range(stride)]
            for j in range(CONV_KERNEL):
                off = CONV_PAD - (CONV_KERNEL - 1) + j
                wj = w_ref[j:j + 1, lanes]
                for g in range(stride):
                    accs[g] = accs[g] + wj * z_ref[lg, pl.ds(r0 + off + g, SUBLANES, stride=stride), :]
            for g in range(stride):
                stage_ref[lg, pl.ds(g, SUBLANES, stride=stride), :] = accs[g]
        acc = jnp.concatenate([stage_ref[lg] for lg in range(n_lg)], axis=1)
        mu = jnp.mean(acc, axis=-1, keepdims=True)
        xc = acc - mu
        y = xc * lax.rsqrt(jnp.mean(xc * xc, axis=-1, keepdims=True) + EPS)
        y = y * g_ref[...] + beta_ref[...]
        o_ref[pl.ds(r0, CONV_ROWS), :] = (y * jax.nn.sigmoid(y)).astype(o_ref.dtype)
        return carry

    lax.fori_loop(0, s // CONV_ROWS, conv, 0, unroll=2)


def _conv(proj3, w_dw, b_dw, g_ln, b_ln):
    b, s, _ = proj3.shape
    vec = pl.BlockSpec((1, CONV_WIDTH), lambda bi: (0, 0))
    return pl.pallas_call(
        _conv_kernel,
        grid=(b,),
        in_specs=[
            pl.BlockSpec((None, s, CONV_WIDTH), lambda bi: (bi, 0, COL_CONV_A)),
            pl.BlockSpec((None, s, CONV_WIDTH), lambda bi: (bi, 0, COL_CONV_B)),
            pl.BlockSpec((CONV_KERNEL, CONV_WIDTH), lambda bi: (0, 0)),
            vec, vec, vec,
        ],
        out_specs=pl.BlockSpec((None, s, CONV_WIDTH), lambda bi: (bi, 0, 0)),
        out_shape=jax.ShapeDtypeStruct((b, s, CONV_WIDTH), BF16),
        scratch_shapes=[
            pltpu.VMEM((CONV_WIDTH // LANES, s + CONV_PAD, LANES), F32),
            pltpu.VMEM((CONV_WIDTH // LANES, CONV_ROWS, LANES), F32),
        ],
        compiler_params=_params(("parallel",)),
        name="conv",
    )(proj3, proj3, w_dw, b_dw, g_ln, b_ln)


def _split_bf16(a):
    hi = a.astype(BF16)
    lo = (a - hi.astype(F32)).astype(BF16)
    return hi, lo


def _mix_kernel(x_ref, o_ref, c_ref, ga_ref, gb_ref,
                wa_ref, wc_ref, wo_ref, gffn_ref, wr_ref, br_ref, before_ref,
                h_ref, u_ref, gate_ref, slot_ref, cnt_ref):
    a = jnp.dot(o_ref[...], wa_ref[...], preferred_element_type=F32)
    b = jnp.dot(c_ref[...], wc_ref[...], preferred_element_type=F32)
    m = (jax.nn.sigmoid(ga_ref[...].astype(F32)) * a
         + jax.nn.sigmoid(gb_ref[...].astype(F32)) * b)
    acc = x_ref[...] + jnp.dot(m.astype(BF16), wo_ref[...], preferred_element_type=F32)
    h_ref[...] = acc
    u = _rms(acc, gffn_ref[...])
    _store_row_tiles(u_ref, u)

    u_hi, u_lo = _split_bf16(u)
    w_hi, w_lo = _split_bf16(wr_ref[...])
    logits = (lax.dot_general(w_hi, u_hi, _NT, preferred_element_type=F32)
              + lax.dot_general(w_hi, u_lo, _NT, preferred_element_type=F32)
              + lax.dot_general(w_lo, u_hi, _NT, preferred_element_type=F32)
              + br_ref[...])
    eidx = lax.broadcasted_iota(I32, logits.shape, 0)
    vals, idxs = [], []
    for _ in range(TOP_K):
        mx = jnp.max(logits, axis=0, keepdims=True)
        sel = jnp.min(jnp.where(logits == mx, eidx, N_EXPERTS), axis=0, keepdims=True)
        vals.append(mx)
        idxs.append(sel)
        logits = jnp.where(eidx == sel, -jnp.inf, logits)
    ex = [jnp.exp(v - vals[0]) for v in vals]
    den = sum(ex[1:], ex[0])
    gate_ref[...] = jnp.concatenate([e / den for e in ex], axis=0)

    tm = logits.shape[1]
    onehot = [eidx == idxs[k] for k in range(TOP_K)]
    routed = onehot[0]
    for k in range(1, TOP_K):
        routed = routed | onehot[k]
    member = jnp.where(routed, 1.0, 0.0).astype(BF16)
    rank = jnp.dot(member, before_ref[...], preferred_element_type=F32)
    cnt_ref[...] = jnp.sum(member.astype(F32), axis=1, keepdims=True).astype(I32)
    cnt_row = lax.dot_general(jnp.ones((SUBLANES, tm), BF16), member, _NT,
                              preferred_element_type=F32)[0:1]
    e_row = lax.broadcasted_iota(I32, (N_EXPERTS, N_EXPERTS), 0)
    e_col = lax.broadcasted_iota(I32, (N_EXPERTS, N_EXPERTS), 1)
    first = jnp.sum(jnp.where(e_col < e_row, cnt_row, 0.0), axis=1, keepdims=True)
    slot_ref[...] = jnp.concatenate(
        [jnp.sum(jnp.where(onehot[k], rank + first, 0.0), axis=0, keepdims=True)
         for k in range(TOP_K)], axis=0).astype(I32) * ROW_TILE


def _mix(xt, o, c, proj, wa, wc, wo, g_ffn, wr_t, br):
    n = xt.shape[0]
    tm = T_ROUTE
    pos = jnp.arange(tm, dtype=I32)
    before = (pos[:, None] < pos[None, :]).astype(BF16)
    row = lambda w, j: pl.BlockSpec((tm, w), lambda i, j=j: (i, j))
    full = lambda a: pl.BlockSpec(a.shape, lambda i: (0,) * a.ndim)
    return pl.pallas_call(
        _mix_kernel,
        grid=(n // tm,),
        in_specs=[
            row(D_MODEL, 0), row(ATTN_WIDTH, 0), row(CONV_WIDTH, 0),
            row(D_MODEL, COL_GATE_A // 2), row(D_MODEL, COL_GATE_B // 2),
            full(wa), full(wc), full(wo), full(g_ffn), full(wr_t), full(br), full(before),
        ],
        out_specs=[
            pl.BlockSpec((tm, D_MODEL), lambda i: (i, 0)),
            pl.BlockSpec((tm * ROW_TILE, LANES), lambda i: (i, 0)),
            pl.BlockSpec((None, TOP_K, tm), lambda i: (i, 0, 0)),
            pl.BlockSpec((None, TOP_K, tm), lambda i: (i, 0, 0)),
            pl.BlockSpec((None, N_EXPERTS, 1), lambda i: (i, 0, 0)),
        ],
        out_shape=[
            jax.ShapeDtypeStruct((n, D_MODEL), F32),
            jax.ShapeDtypeStruct((n * ROW_TILE, LANES), F32),
            jax.ShapeDtypeStruct((n // tm, TOP_K, tm), F32),
            jax.ShapeDtypeStruct((n // tm, TOP_K, tm), I32),
            jax.ShapeDtypeStruct((n // tm, N_EXPERTS, 1), I32),
        ],
        compiler_params=_params(("parallel",)),
        name="mix",
    )(xt, o, c, proj, proj, wa, wc, wo, g_ffn, wr_t, br, before)


def _run_copies(cnt, src_ref, src_row, dst_ref, dst_row, sem):
    def pieces(bits, src_row, dst_row):
        for bit in bits:
            size = 1 << bit
            piece = cnt & size

            @pl.when(piece != 0)
            def _(size=size, src_row=src_row, dst_row=dst_row):
                rows = size * ROW_TILE
                pltpu.make_async_copy(
                    src_ref.at[pl.ds(pl.multiple_of(src_row * ROW_TILE, ROW_TILE), rows), :],
                    dst_ref.at[pl.ds(pl.multiple_of(dst_row * ROW_TILE, ROW_TILE), rows), :],
                    sem).start()

            src_row = src_row + piece
            dst_row = dst_row + piece

    long_bits = RUN_LONG.bit_length() - 1

    @pl.when(cnt >= RUN_LONG)
    def _():
        pieces(reversed(range(long_bits, T_ROUTE.bit_length())), src_row, dst_row)

    head = cnt & -RUN_LONG
    pieces(reversed(range(long_bits)), src_row + head, dst_row + head)


def _dispatch_kernel(pstart_ref, pend_ref, cnt_ref, run_ref, slot_ref, u_ref, xs_ref,
                     grouped, zero_ref, sem, zsem):
    t = T_ROUTE
    i = pl.program_id(0)
    cur = i % 2
    blk = EXPERT_BLOCK * ROW_TILE
    buf = grouped.at[cur]

    def drain(slot):
        pltpu.make_async_copy(grouped.at[slot], grouped.at[slot], sem.at[slot]).wait()

    @pl.when(i == 0)
    def _():
        zero_ref[...] = jnp.zeros_like(zero_ref)

        def zero_block(first_row):
            return pltpu.make_async_copy(
                zero_ref, xs_ref.at[pl.ds(pl.multiple_of(first_row * ROW_TILE, blk), blk), :], zsem)

        for act in (lambda cp: cp.start(), lambda cp: cp.wait()):
            def tails(e, carry, act=act):
                @pl.when(pend_ref[e] > pstart_ref[e])
                def _():
                    act(zero_block(pend_ref[e] - EXPERT_BLOCK))
                return carry

            lax.fori_loop(0, N_EXPERTS, tails, 0)

            def unused(b, carry, act=act):
                act(zero_block(b * EXPERT_BLOCK))
                return carry

            lax.fori_loop(pend_ref[N_EXPERTS - 1] // EXPERT_BLOCK,
                          xs_ref.shape[0] // blk, unused, 0)

    @pl.when(i >= 2)
    def _():
        drain(cur)

    def group(tok, carry):
        tile = u_ref[pl.ds(pl.multiple_of(tok * ROW_TILE, ROW_TILE), ROW_TILE), :]
        for k in range(TOP_K):
            slot = pl.multiple_of(slot_ref[k * t + tok], ROW_TILE)
            buf[pl.ds(slot, ROW_TILE), :] = tile
        return carry

    lax.fori_loop(0, t, group, 0, unroll=DMA_UNROLL)

    def runs(e, off):
        cnt = cnt_ref[i * N_EXPERTS + e]
        _run_copies(cnt, buf, off, xs_ref, run_ref[i * N_EXPERTS + e], sem.at[cur])
        return off + cnt

    lax.fori_loop(0, N_EXPERTS, runs, 0)

    @pl.when(i == pl.num_programs(0) - 1)
    def _():
        drain(cur)

        @pl.when(i >= 1)
        def _():
            drain(1 - cur)


def _dispatch(pstart, pend, block_cnt, run_start, slot_t, u, n_rows):
    n = u.shape[0] // ROW_TILE
    t = T_ROUTE
    grid_spec = pltpu.PrefetchScalarGridSpec(
        num_scalar_prefetch=4,
        grid=(n // t,),
        in_specs=[
            pl.BlockSpec((TOP_K * t,), lambda i, *_: (i,), memory_space=pltpu.SMEM),
            pl.BlockSpec((t * ROW_TILE, LANES), lambda i, *_: (i, 0)),
        ],
        out_specs=pl.BlockSpec(memory_space=pl.ANY),
        scratch_shapes=[
            pltpu.VMEM((2, TOP_K * t * ROW_TILE, LANES), F32),
            pltpu.VMEM((EXPERT_BLOCK * ROW_TILE, LANES), F32),
            pltpu.SemaphoreType.DMA((2,)),
            pltpu.SemaphoreType.DMA(()),
        ],
    )
    return pl.pallas_call(
        _dispatch_kernel,
        grid_spec=grid_spec,
        out_shape=jax.ShapeDtypeStruct((n_rows * ROW_TILE, LANES), F32),
        compiler_params=_params(("arbitrary",)),
        name="dispatch",
    )(pstart, pend, block_cnt, run_start, slot_t, u)


def _expert_kernel(be_ref, nb_ref, x_ref, wgu_ref, bgu_ref, wd_ref, bd_ref, y_ref,
                   wgu_s, wd32_s, wd_s):
    i = pl.program_id(0)
    nb = nb_ref[0]
    blk = EXPERT_BLOCK
    half = EXPERT_FF // 2
    first = jnp.logical_or(i == 0, be_ref[i] != be_ref[jnp.maximum(i - 1, 0)])

    @pl.when(jnp.logical_and(first, i < nb))
    def _():
        rows = 256
        for r in range(0, D_MODEL, rows):
            wgu_s[r:r + rows, :] = wgu_ref[r:r + rows, :].astype(BF16)
        for cs in range(D_MODEL // LANES):
            lanes = slice(cs * LANES, (cs + 1) * LANES)
            wd32_s[cs, pl.ds(0, half, stride=2), :] = wd_ref[0:half, lanes]
            wd32_s[cs, pl.ds(1, half, stride=2), :] = wd_ref[half:EXPERT_FF, lanes]
            wd_s[:, lanes] = wd32_s[cs].astype(BF16)

    @pl.when(i < nb)
    def _():
        rows = blk // EXPERT_SPLIT
        parts = range(EXPERT_SPLIT)
        even = (lax.broadcasted_iota(I32, (rows, LANES), 1) % 2) == 0
        xb = [_load_row_tiles(x_ref, rows, first=s * rows).astype(BF16) for s in parts]
        gu1 = [jnp.dot(xb[s], wgu_s[:, :EXPERT_FF], preferred_element_type=F32)
               + bgu_ref[:, :EXPERT_FF] for s in parts]
        gu2 = [jnp.dot(xb[s], wgu_s[:, EXPERT_FF:], preferred_element_type=F32)
               + bgu_ref[:, EXPERT_FF:] for s in parts]
        act = []
        for s in parts:
            cols = []
            for v in range(EXPERT_FF // LANES):
                a = gu1[s][:, v * LANES:(v + 1) * LANES]
                b = gu2[s][:, v * LANES:(v + 1) * LANES]
                g = jnp.where(even, a, pltpu.roll(b, 1, axis=1))
                l = jnp.where(even, pltpu.roll(a, LANES - 1, axis=1), b)
                g = jnp.minimum(g, SWIGLU_LIMIT)
                l = jnp.clip(l, -SWIGLU_LIMIT, SWIGLU_LIMIT)
                cols.append(((l + 1.0) * (g * jax.nn.sigmoid(g * SWIGLU_ALPHA))).astype(BF16))
            act.append(jnp.concatenate(cols, axis=1))
        y = [jnp.dot(act[s], wd_s[...], preferred_element_type=F32) + bd_ref[...] for s in parts]
        for s in parts:
            _store_row_tiles(y_ref, y[s], first=s * rows)

    @pl.when(i >= nb)
    def _():
        y_ref[...] = jnp.zeros_like(y_ref)


def _experts(block_expert, n_used, xs, wgu, bgu, wd, bd):
    n_rows = xs.shape[0] // ROW_TILE
    blk = EXPERT_BLOCK
    wspec = lambda r, c: pl.BlockSpec((None, r, c), lambda i, be, nb: (be[i], 0, 0))
    grid_spec = pltpu.PrefetchScalarGridSpec(
        num_scalar_prefetch=2,
        grid=(n_rows // blk,),
        in_specs=[
            pl.BlockSpec((blk * ROW_TILE, LANES), lambda i, be, nb: (jnp.minimum(i, nb[0] - 1), 0)),
            wspec(D_MODEL, 2 * EXPERT_FF), wspec(1, 2 * EXPERT_FF),
            wspec(EXPERT_FF, D_MODEL), wspec(1, D_MODEL),
        ],
        out_specs=pl.BlockSpec((blk * ROW_TILE, LANES), lambda i, be, nb: (i, 0)),
        scratch_shapes=[
            pltpu.VMEM((D_MODEL, 2 * EXPERT_FF), BF16),
            pltpu.VMEM((D_MODEL // LANES, EXPERT_FF, LANES), F32),
            pltpu.VMEM((EXPERT_FF, D_MODEL), BF16),
        ],
    )
    return pl.pallas_call(
        _expert_kernel,
        grid_spec=grid_spec,
        out_shape=jax.ShapeDtypeStruct((n_rows * ROW_TILE, LANES), F32),
        compiler_params=_params(("arbitrary",)),
        name="expert",
    )(block_expert, n_used, xs, wgu, bgu, wd, bd)


def _combine_kernel(cnt_ref, run_ref, slot0_ref, gate0_ref, slot_ref, gate_ref, h_ref, p_ref, y_ref,
                    wple_ref, gple_ref, wpg_ref, gfin_ref, o_ref, staged, moe_s, sem):
    t = T_ROUTE
    i = pl.program_id(0)
    n_steps = pl.num_programs(0)
    cur = i % 2

    def fetch(block, slot):
        def runs(e, off):
            cnt = cnt_ref[block * N_EXPERTS + e]
            _run_copies(cnt, y_ref, run_ref[block * N_EXPERTS + e], staged.at[slot], off,
                        sem.at[slot])
            return off + cnt

        lax.fori_loop(0, N_EXPERTS, runs, 0)

    def arrived(slot):
        pltpu.make_async_copy(staged.at[slot], staged.at[slot], sem.at[slot]).wait()

    def pick(slots, gates, slot, tok):
        acc = None
        for k in range(TOP_K):
            at = pl.multiple_of(slots[k * t + tok], ROW_TILE)
            term = gates[k * t + tok] * staged[slot, pl.ds(at, ROW_TILE), :]
            acc = term if acc is None else acc + term
        moe_s[slot, pl.ds(pl.multiple_of(tok * ROW_TILE, ROW_TILE), ROW_TILE), :] = acc

    def compute(slot):
        rows = t // 2
        parts = [pl.ds(s * rows, rows) for s in range(2)]
        emb = [jnp.dot(p_ref[rs, :].astype(BF16), wple_ref[...], preferred_element_type=F32)
               for rs in parts]
        h = [h_ref[rs, :] + _load_row_tiles(moe_s, rows, lead=(slot,), first=s * rows)
             for s, rs in enumerate(parts)]
        r = [_rms(hs, gple_ref[...]).astype(BF16) for hs in h]
        sig = [jax.nn.sigmoid(jnp.dot(rs, wpg_ref[...], preferred_element_type=F32)) for rs in r]
        for s, rs in enumerate(parts):
            o_ref[rs, :] = _rms(h[s] + emb[s] * sig[s], gfin_ref[...])

    @pl.when(i == 0)
    def _():
        fetch(0, 0)

        @pl.when(n_steps > 1)
        def _():
            fetch(1, 1)

        arrived(0)

        def first_block(tok, carry):
            pick(slot0_ref, gate0_ref, 0, tok)
            return carry

        lax.fori_loop(0, t, first_block, 0, unroll=DMA_UNROLL)

    @pl.when(i + 2 < n_steps)
    def _():
        fetch(i + 2, cur)

    for parity in range(2):
        @pl.when(jnp.logical_and(cur == parity, i + 1 < n_steps))
        def _(parity=parity):
            arrived(1 - parity)
            for tok in range(t):
                pick(slot_ref, gate_ref, 1 - parity, tok)
            compute(parity)

        @pl.when(jnp.logical_and(cur == parity, i + 1 == n_steps))
        def _(parity=parity):
            compute(parity)


def _combine(block_cnt, run_start, slot_t, gates_t, h, p2, y, wple, gple, wpg, gfin):
    n = h.shape[0]
    t = T_ROUTE
    last = n // t - 1
    full = lambda a: pl.BlockSpec(a.shape, lambda i, *_: (0,) * a.ndim)
    table = lambda ahead: pl.BlockSpec((TOP_K * t,), lambda i, *_: (jnp.minimum(i + ahead, last),),
                                       memory_space=pltpu.SMEM)
    grid_spec = pltpu.PrefetchScalarGridSpec(
        num_scalar_prefetch=2,
        grid=(n // t,),
        in_specs=[
            table(0), table(0), table(1), table(1),
            pl.BlockSpec((t, D_MODEL), lambda i, *_: (i, 0)),
            pl.BlockSpec((t, PLE_DIM), lambda i, *_: (i, 0)),
            pl.BlockSpec(memory_space=pl.ANY),
            full(wple), full(gple), full(wpg), full(gfin),
        ],
        out_specs=pl.BlockSpec((t, D_MODEL), lambda i, *_: (i, 0)),
        scratch_shapes=[
            pltpu.VMEM((2, TOP_K * t * ROW_TILE, LANES), F32),
            pltpu.VMEM((2, t * ROW_TILE, LANES), F32),
            pltpu.SemaphoreType.DMA((2,)),
        ],
    )
    return pl.pallas_call(
        _combine_kernel,
        grid_spec=grid_spec,
        out_shape=jax.ShapeDtypeStruct((n, D_MODEL), F32),
        compiler_params=_params(("arbitrary",)),
        name="combine",
    )(block_cnt, run_start, slot_t, gates_t, slot_t, gates_t, h, p2, y, wple, gple, wpg, gfin)


def kernel(x, p, g_mix, w_in, lambda_q1, lambda_k1, lambda_q2, lambda_k2, g_subln, w_attn_out,
           w_dw, b_dw, g_conv_ln, b_conv_ln, w_conv_out, w_o, g_ffn, w_router, b_router,
           w_gate_up, b_gate_up, w_down, b_down, w_ple, g_ple, w_ple_gate, g_final):
    b, s, d = x.shape
    n = b * s
    xt = x.reshape(n, d)
    vec = lambda a: a.reshape(1, -1)

    proj, vt = _inproj(xt, vec(g_mix[0]), w_in[0].astype(BF16))
    proj3 = proj.reshape(b, s, MAIN_COLS)
    attn = _attention(proj3, vt, vec(lambda_q1[0]), vec(lambda_k1[0]), vec(lambda_q2[0]),
                      vec(lambda_k2[0]), g_subln[0].reshape(HEAD_V, 1))
    conv = _conv(proj3, w_dw[0], vec(b_dw[0]), vec(g_conv_ln[0]), vec(b_conv_ln[0]))
    h1, u2, gates_t, slot_t, block_cnt = _mix(
        xt, attn.reshape(n, ATTN_WIDTH), conv.reshape(n, CONV_WIDTH), proj,
        w_attn_out[0].astype(BF16), w_conv_out[0].astype(BF16), w_o[0].astype(BF16),
        vec(g_ffn[0]), w_router[0].T, b_router[0].reshape(N_EXPERTS, 1))

    block_cnt = block_cnt[:, :, 0]
    counts = jnp.sum(block_cnt, axis=0)
    blk = EXPERT_BLOCK
    n_blocks = n * TOP_K // blk + N_EXPERTS
    padded = (counts + blk - 1) // blk * blk
    pend = jnp.cumsum(padded).astype(I32)
    pstart = pend - padded
    n_used = (pend[-1:] // blk).astype(I32)
    block_start = jnp.arange(n_blocks, dtype=I32) * blk
    block_expert = jnp.minimum(
        jnp.sum((pend[None, :] <= block_start[:, None]).astype(I32), axis=1), N_EXPERTS - 1)
    run_start = (pstart[None, :] + jnp.cumsum(block_cnt, axis=0) - block_cnt).astype(I32)
    cnt_flat = block_cnt.reshape(-1).astype(I32)
    run_flat = run_start.reshape(-1)

    slot_flat = slot_t.reshape(-1)
    gate_flat = gates_t.reshape(-1)

    xs = _dispatch(pstart, pend, cnt_flat, run_flat, slot_flat, u2, n_blocks * blk)
    y = _experts(block_expert, n_used, xs, w_gate_up[0], b_gate_up[0][:, None, :],
                 w_down[0], b_down[0][:, None, :])

    out = _combine(cnt_flat, run_flat, slot_flat, gate_flat, h1, p[0].reshape(n, PLE_DIM), y,
                   w_ple[0].astype(BF16), vec(g_ple[0]), w_ple_gate[0].astype(BF16), vec(g_final))
    return out.reshape(b, s, d)
```

```python
import jax
import jax.numpy as jnp
from jax import lax
from jax.experimental import pallas as pl
from jax.experimental.pallas import tpu as pltpu

F32 = jnp.float32
BF16 = jnp.bfloat16
I32 = jnp.int32

D_MODEL = 1024
ATTN_WIDTH = 512
DIFF_HEADS = 4
HEAD_DIM = 64
HEAD_V = 2 * HEAD_DIM
CONV_WIDTH = 512
CONV_KERNEL = 31
N_EXPERTS = 32
TOP_K = 4
EXPERT_FF = 1024
PLE_DIM = 256
SWIGLU_ALPHA = 1.702
SWIGLU_LIMIT = 7.0
EPS = 1e-5
IN_COLS = 3 * ATTN_WIDTH + 2 * CONV_WIDTH + 2 * D_MODEL
LAMBDA_INIT = 0.2
LOG2_E = 1.4426950408889634

MAIN_COLS = IN_COLS - ATTN_WIDTH
COL_CONV_A = 2
COL_CONV_B = 3
COL_GATE_A = 4
COL_GATE_B = 6

LANES = 128
SUBLANES = 8
TM_PROJ = 512
TQ = 256
T_ROUTE = 512
EXPERT_BLOCK = 512
EXPERT_SPLIT = 2
CONV_ROWS = 64
CONV_PAD = 32
RUN_LONG = 128
DMA_UNROLL = 16
VMEM_V7X = 64 * 1024 * 1024
VMEM_LIMIT = VMEM_V7X - 8 * 1024 * 1024

_NT = (((1,), (1,)), ((), ()))


def _params(sem, vmem=VMEM_LIMIT):
    return pltpu.CompilerParams(dimension_semantics=sem, vmem_limit_bytes=vmem)


def _rms(x, g):
    return x * lax.rsqrt(jnp.mean(x * x, axis=-1, keepdims=True) + EPS) * g


ROW_TILE = D_MODEL // LANES
assert ROW_TILE == SUBLANES


def _store_row_tiles(ref, x, lead=(), first=0):
    rows = x.shape[0]
    for g in range(ROW_TILE):
        ref[lead + (pl.ds(first * ROW_TILE + g, rows, stride=ROW_TILE), slice(None))] = (
            x[:, g * LANES:(g + 1) * LANES])


def _load_row_tiles(ref, rows, lead=(), first=0):
    return jnp.concatenate(
        [ref[lead + (pl.ds(first * ROW_TILE + g, rows, stride=ROW_TILE), slice(None))]
         for g in range(ROW_TILE)], axis=1)


def _inproj_kernel(x_ref, g_ref, w_ref, o_ref, vt_ref):
    u = _rms(x_ref[...], g_ref[...]).astype(BF16)
    ch = 512
    for c in range(MAIN_COLS // ch):
        src = c * ch if c < 2 * ATTN_WIDTH // ch else c * ch + ATTN_WIDTH
        r = jnp.dot(u, w_ref[:, src:src + ch], preferred_element_type=F32)
        if c == 0:
            r = r * (HEAD_DIM ** -0.5 * LOG2_E)
        o_ref[:, c * ch:(c + 1) * ch] = r.astype(BF16)
    w_v = w_ref[:, 2 * ATTN_WIDTH:3 * ATTN_WIDTH]
    vt_ref[...] = lax.dot_general(w_v, u, (((0,), (1,)), ((), ())),
                                  preferred_element_type=F32).astype(BF16)


def _inproj(xt, g, w):
    n = xt.shape[0]
    return pl.pallas_call(
        _inproj_kernel,
        grid=(n // TM_PROJ,),
        in_specs=[
            pl.BlockSpec((TM_PROJ, D_MODEL), lambda i: (i, 0)),
            pl.BlockSpec((1, D_MODEL), lambda i: (0, 0)),
            pl.BlockSpec((D_MODEL, IN_COLS), lambda i: (0, 0)),
        ],
        out_specs=[
            pl.BlockSpec((TM_PROJ, MAIN_COLS), lambda i: (i, 0)),
            pl.BlockSpec((ATTN_WIDTH, TM_PROJ), lambda i: (0, i)),
        ],
        out_shape=[
            jax.ShapeDtypeStruct((n, MAIN_COLS), BF16),
            jax.ShapeDtypeStruct((ATTN_WIDTH, n), BF16),
        ],
        compiler_params=_params(("parallel",)),
        name="inproj",
    )(xt, g, w)


def _attn_kernel(q_ref, k_ref, vt_ref, lq1_ref, lk1_ref, lq2_ref, lk2_ref, g_ref, o_ref):
    i = pl.program_id(1)
    s_len = k_ref.shape[0]
    lane = lax.broadcasted_iota(I32, (TQ, HEAD_V), 1)
    key = lax.broadcasted_iota(I32, (TQ, 2 * TQ), 0)
    qry = lax.broadcasted_iota(I32, (TQ, 2 * TQ), 1)
    causal = key <= jnp.where(qry >= TQ, qry - TQ, qry)
    lam = (jnp.exp(jnp.sum(lq1_ref[...] * lk1_ref[...]))
           - jnp.exp(jnp.sum(lq2_ref[...] * lk2_ref[...])) + LAMBDA_INIT)

    half_blocks = s_len // TQ // 2

    def block(c):
        chains = [(half, h) for half in range(2) for h in range(DIFF_HEADS)]
        cols = [slice(h * HEAD_V, (h + 1) * HEAD_V) for _, h in chains]
        n = [(c + half * half_blocks) * TQ for half, _ in chains]
        ids = range(len(chains))

        def stacked_q(j):
            q = q_ref[chains[j][0], :, cols[j]]
            zero = jnp.zeros_like(q)
            return jnp.concatenate([jnp.where(lane < HEAD_DIM, q, zero),
                                    jnp.where(lane >= HEAD_DIM, q, zero)], axis=0)

        s = [lax.dot_general(k_ref[0:n[j] + TQ, cols[j]], stacked_q(j), _NT,
                             preferred_element_type=F32) for j in ids]
        s_d = [jnp.where(causal, s[j][n[j]:], -1e30) for j in ids]
        m = [jnp.max(s_d[j], axis=0, keepdims=True) for j in ids]
        m = [jnp.maximum(m[j], jnp.max(s[j][:n[j]], axis=0, keepdims=True)) if n[j] else m[j]
             for j in ids]
        p = [jnp.exp2(s_d[j] - m[j]) for j in ids]
        p = [jnp.concatenate([jnp.exp2(s[j][:n[j]] - m[j]), p[j]], axis=0) if n[j] else p[j]
             for j in ids]
        l = [jnp.sum(p[j], axis=0, keepdims=True) for j in ids]
        acc = [jnp.dot(vt_ref[cols[j], 0:n[j] + TQ], p[j].astype(BF16),
                       preferred_element_type=F32) for j in ids]
        for j in ids:
            o12 = acc[j] / l[j]
            o = o12[:, :TQ] - lam * o12[:, TQ:]
            o = o * lax.rsqrt(jnp.mean(o * o, axis=0, keepdims=True) + EPS) * g_ref[...]
            o_ref[chains[j][0], :, cols[j]] = (o * (1.0 - LAMBDA_INIT)).T.astype(o_ref.dtype)

    for c in range(half_blocks):
        pl.when(i == c)(lambda c=c: block(c))


def _attention(proj3, vt, lq1, lk1, lq2, lk2, g_subln):
    b, s, width = proj3.shape
    vec = pl.BlockSpec((1, HEAD_DIM), lambda bi, i: (0, 0))
    halves = pl.BlockSpec((None, 2, TQ, ATTN_WIDTH), lambda bi, i: (bi, 0, i, 0))
    out = pl.pallas_call(
        _attn_kernel,
        grid=(b, s // TQ // 2),
        in_specs=[
            halves,
            pl.BlockSpec((None, s, ATTN_WIDTH), lambda bi, i: (bi, 0, 1)),
            pl.BlockSpec((ATTN_WIDTH, s), lambda bi, i: (0, bi)),
            vec, vec, vec, vec,
            pl.BlockSpec((HEAD_V, 1), lambda bi, i: (0, 0)),
        ],
        out_specs=halves,
        out_shape=jax.ShapeDtypeStruct((b, 2, s // 2, ATTN_WIDTH), BF16),
        compiler_params=_params(("parallel", "parallel")),
        name="attn",
    )(proj3.reshape(b, 2, s // 2, width), proj3, vt, lq1, lk1, lq2, lk2, g_subln)
    return out.reshape(b, s, ATTN_WIDTH)


def _conv_kernel(ca_ref, cb_ref, w_ref, b_ref, g_ref, beta_ref, o_ref, z_ref, stage_ref):
    s = ca_ref.shape[0]
    glu_rows = 256
    n_lg = CONV_WIDTH // LANES
    z_ref[:, 0:CONV_PAD, :] = jnp.zeros((n_lg, CONV_PAD, LANES), F32)

    def glu(c, carry):
        r0 = pl.multiple_of(c * glu_rows, glu_rows)
        a = ca_ref[pl.ds(r0, glu_rows), :].astype(F32)
        g = cb_ref[pl.ds(r0, glu_rows), :].astype(F32)
        z = a * jax.nn.sigmoid(g)
        for lg in range(n_lg):
            z_ref[lg, pl.ds(CONV_PAD + r0, glu_rows), :] = z[:, lg * LANES:(lg + 1) * LANES]
        return carry

    lax.fori_loop(0, s // glu_rows, glu, 0)

    def conv(c, carry):
        r0 = pl.multiple_of(c * CONV_ROWS, CONV_ROWS)
        stride = CONV_ROWS // SUBLANES
        for lg in range(n_lg):
            lanes = slice(lg * LANES, (lg + 1) * LANES)
            accs = [jnp.zeros((SUBLANES, LANES), F32) + b_ref[:, lanes] for _ in range(stride)]
            for j in range(CONV_KERNEL):
                off = CONV_PAD - (CONV_KERNEL - 1) + j
                wj = w_ref[j:j + 1, lanes]
                for g in range(stride):
                    accs[g] = accs[g] + wj * z_ref[lg, pl.ds(r0 + off + g, SUBLANES, stride=stride), :]
            for g in range(stride):
                stage_ref[lg, pl.ds(g, SUBLANES, stride=stride), :] = accs[g]
        acc = jnp.concatenate([stage_ref[lg] for lg in range(n_lg)], axis=1)
        mu = jnp.mean(acc, axis=-1, keepdims=True)
        xc = acc - mu
        y = xc * lax.rsqrt(jnp.mean(xc * xc, axis=-1, keepdims=True) + EPS)
        y = y * g_ref[...] + beta_ref[...]
        o_ref[pl.ds(r0, CONV_ROWS), :] = (y * jax.nn.sigmoid(y)).astype(o_ref.dtype)
        return carry

    lax.fori_loop(0, s // CONV_ROWS, conv, 0, unroll=2)


def _conv(proj3, w_dw, b_dw, g_ln, b_ln):
    b, s, _ = proj3.shape
    vec = pl.BlockSpec((1, CONV_WIDTH), lambda bi: (0, 0))
    return pl.pallas_call(
        _conv_kernel,
        grid=(b,),
        in_specs=[
            pl.BlockSpec((None, s, CONV_WIDTH), lambda bi: (bi, 0, COL_CONV_A)),
            pl.BlockSpec((None, s, CONV_WIDTH), lambda bi: (bi, 0, COL_CONV_B)),
            pl.BlockSpec((CONV_KERNEL, CONV_WIDTH), lambda bi: (0, 0)),
            vec, vec, vec,
        ],
        out_specs=pl.BlockSpec((None, s, CONV_WIDTH), lambda bi: (bi, 0, 0)),
        out_shape=jax.ShapeDtypeStruct((b, s, CONV_WIDTH), BF16),
        scratch_shapes=[
            pltpu.VMEM((CONV_WIDTH // LANES, s + CONV_PAD, LANES), F32),
            pltpu.VMEM((CONV_WIDTH // LANES, CONV_ROWS, LANES), F32),
        ],
        compiler_params=_params(("parallel",)),
        name="conv",
    )(proj3, proj3, w_dw, b_dw, g_ln, b_ln)


def _split_bf16(a):
    hi = a.astype(BF16)
    lo = (a - hi.astype(F32)).astype(BF16)
    return hi, lo


def _mix_kernel(x_ref, o_ref, c_ref, ga_ref, gb_ref,
                wa_ref, wc_ref, wo_ref, gffn_ref, wr_ref, br_ref, before_ref,
                h_ref, u_ref, gate_ref, slot_ref, cnt_ref):
    a = jnp.dot(o_ref[...], wa_ref[...], preferred_element_type=F32)
    b = jnp.dot(c_ref[...], wc_ref[...], preferred_element_type=F32)
    m = (jax.nn.sigmoid(ga_ref[...].astype(F32)) * a
         + jax.nn.sigmoid(gb_ref[...].astype(F32)) * b)
    acc = x_ref[...] + jnp.dot(m.astype(BF16), wo_ref[...], preferred_element_type=F32)
    h_ref[...] = acc
    u = _rms(acc, gffn_ref[...])
    _store_row_tiles(u_ref, u)

    u_hi, u_lo = _split_bf16(u)
    w_hi, w_lo = _split_bf16(wr_ref[...])
    logits = (lax.dot_general(w_hi, u_hi, _NT, preferred_element_type=F32)
              + lax.dot_general(w_hi, u_lo, _NT, preferred_element_type=F32)
              + lax.dot_general(w_lo, u_hi, _NT, preferred_element_type=F32)
              + br_ref[...])
    eidx = lax.broadcasted_iota(I32, logits.shape, 0)
    vals, idxs = [], []
    for _ in range(TOP_K):
        mx = jnp.max(logits, axis=0, keepdims=True)
        sel = jnp.min(jnp.where(logits == mx, eidx, N_EXPERTS), axis=0, keepdims=True)
        vals.append(mx)
        idxs.append(sel)
        logits = jnp.where(eidx == sel, -jnp.inf, logits)
    ex = [jnp.exp(v - vals[0]) for v in vals]
    den = sum(ex[1:], ex[0])
    gate_ref[...] = jnp.concatenate([e / den for e in ex], axis=0)

    tm = logits.shape[1]
    onehot = [eidx == idxs[k] for k in range(TOP_K)]
    routed = onehot[0]
    for k in range(1, TOP_K):
        routed = routed | onehot[k]
    member = jnp.where(routed, 1.0, 0.0).astype(BF16)
    rank = jnp.dot(member, before_ref[...], preferred_element_type=F32)
    cnt_ref[...] = jnp.sum(member.astype(F32), axis=1, keepdims=True).astype(I32)
    cnt_row = lax.dot_general(jnp.ones((SUBLANES, tm), BF16), member, _NT,
                              preferred_element_type=F32)[0:1]
    e_row = lax.broadcasted_iota(I32, (N_EXPERTS, N_EXPERTS), 0)
    e_col = lax.broadcasted_iota(I32, (N_EXPERTS, N_EXPERTS), 1)
    first = jnp.sum(jnp.where(e_col < e_row, cnt_row, 0.0), axis=1, keepdims=True)
    slot_ref[...] = jnp.concatenate(
        [jnp.sum(jnp.where(onehot[k], rank + first, 0.0), axis=0, keepdims=True)
         for k in range(TOP_K)], axis=0).astype(I32) * ROW_TILE


def _mix(xt, o, c, proj, wa, wc, wo, g_ffn, wr_t, br):
    n = xt.shape[0]
    tm = T_ROUTE
    pos = jnp.arange(tm, dtype=I32)
    before = (pos[:, None] < pos[None, :]).astype(BF16)
    row = lambda w, j: pl.BlockSpec((tm, w), lambda i, j=j: (i, j))
    full = lambda a: pl.BlockSpec(a.shape, lambda i: (0,) * a.ndim)
    return pl.pallas_call(
        _mix_kernel,
        grid=(n // tm,),
        in_specs=[
            row(D_MODEL, 0), row(ATTN_WIDTH, 0), row(CONV_WIDTH, 0),
            row(D_MODEL, COL_GATE_A // 2), row(D_MODEL, COL_GATE_B // 2),
            full(wa), full(wc), full(wo), full(g_ffn), full(wr_t), full(br), full(before),
        ],
        out_specs=[
            pl.BlockSpec((tm, D_MODEL), lambda i: (i, 0)),
            pl.BlockSpec((tm * ROW_TILE, LANES), lambda i: (i, 0)),
            pl.BlockSpec((None, TOP_K, tm), lambda i: (i, 0, 0)),
            pl.BlockSpec((None, TOP_K, tm), lambda i: (i, 0, 0)),
            pl.BlockSpec((None, N_EXPERTS, 1), lambda i: (i, 0, 0)),
        ],
        out_shape=[
            jax.ShapeDtypeStruct((n, D_MODEL), F32),
            jax.ShapeDtypeStruct((n * ROW_TILE, LANES), F32),
            jax.ShapeDtypeStruct((n // tm, TOP_K, tm), F32),
            jax.ShapeDtypeStruct((n // tm, TOP_K, tm), I32),
            jax.ShapeDtypeStruct((n // tm, N_EXPERTS, 1), I32),
        ],
        compiler_params=_params(("parallel",)),
        name="mix",
    )(xt, o, c, proj, proj, wa, wc, wo, g_ffn, wr_t, br, before)


def _run_copies(cnt, src_ref, src_row, dst_ref, dst_row, sem):
    def pieces(bits, src_row, dst_row):
        for bit in bits:
            size = 1 << bit
            piece = cnt & size

            @pl.when(piece != 0)
            def _(size=size, src_row=src_row, dst_row=dst_row):
                rows = size * ROW_TILE
                pltpu.make_async_copy(
                    src_ref.at[pl.ds(pl.multiple_of(src_row * ROW_TILE, ROW_TILE), rows), :],
                    dst_ref.at[pl.ds(pl.multiple_of(dst_row * ROW_TILE, ROW_TILE), rows), :],
                    sem).start()

            src_row = src_row + piece
            dst_row = dst_row + piece

    long_bits = RUN_LONG.bit_length() - 1

    @pl.when(cnt >= RUN_LONG)
    def _():
        pieces(reversed(range(long_bits, T_ROUTE.bit_length())), src_row, dst_row)

    head = cnt & -RUN_LONG
    pieces(reversed(range(long_bits)), src_row + head, dst_row + head)


def _dispatch_kernel(pstart_ref, pend_ref, cnt_ref, run_ref, slot_ref, u_ref, xs_ref,
                     grouped, zero_ref, sem, zsem):
    t = T_ROUTE
    i = pl.program_id(0)
    cur = i % 2
    blk = EXPERT_BLOCK * ROW_TILE
    buf = grouped.at[cur]

    def drain(slot):
        pltpu.make_async_copy(grouped.at[slot], grouped.at[slot], sem.at[slot]).wait()

    @pl.when(i == 0)
    def _():
        zero_ref[...] = jnp.zeros_like(zero_ref)

        def zero_block(first_row):
            return pltpu.make_async_copy(
                zero_ref, xs_ref.at[pl.ds(pl.multiple_of(first_row * ROW_TILE, blk), blk), :], zsem)

        for act in (lambda cp: cp.start(), lambda cp: cp.wait()):
            def tails(e, carry, act=act):
                @pl.when(pend_ref[e] > pstart_ref[e])
                def _():
                    act(zero_block(pend_ref[e] - EXPERT_BLOCK))
                return carry

            lax.fori_loop(0, N_EXPERTS, tails, 0)

            def unused(b, carry, act=act):
                act(zero_block(b * EXPERT_BLOCK))
                return carry

            lax.fori_loop(pend_ref[N_EXPERTS - 1] // EXPERT_BLOCK,
                          xs_ref.shape[0] // blk, unused, 0)

    @pl.when(i >= 2)
    def _():
        drain(cur)

    def group(tok, carry):
        tile = u_ref[pl.ds(pl.multiple_of(tok * ROW_TILE, ROW_TILE), ROW_TILE), :]
        for k in range(TOP_K):
            slot = pl.multiple_of(slot_ref[k * t + tok], ROW_TILE)
            buf[pl.ds(slot, ROW_TILE), :] = tile
        return carry

    lax.fori_loop(0, t, group, 0, unroll=DMA_UNROLL)

    def runs(e, off):
        cnt = cnt_ref[i * N_EXPERTS + e]
        _run_copies(cnt, buf, off, xs_ref, run_ref[i * N_EXPERTS + e], sem.at[cur])
        return off + cnt

    lax.fori_loop(0, N_EXPERTS, runs, 0)

    @pl.when(i == pl.num_programs(0) - 1)
    def _():
        drain(cur)

        @pl.when(i >= 1)
        def _():
            drain(1 - cur)


def _dispatch(pstart, pend, block_cnt, run_start, slot_t, u, n_rows):
    n = u.shape[0] // ROW_TILE
    t = T_ROUTE
    grid_spec = pltpu.PrefetchScalarGridSpec(
        num_scalar_prefetch=4,
        grid=(n // t,),
        in_specs=[
            pl.BlockSpec((TOP_K * t,), lambda i, *_: (i,), memory_space=pltpu.SMEM),
            pl.BlockSpec((t * ROW_TILE, LANES), lambda i, *_: (i, 0)),
        ],
        out_specs=pl.BlockSpec(memory_space=pl.ANY),
        scratch_shapes=[
            pltpu.VMEM((2, TOP_K * t * ROW_TILE, LANES), F32),
            pltpu.VMEM((EXPERT_BLOCK * ROW_TILE, LANES), F32),
            pltpu.SemaphoreType.DMA((2,)),
            pltpu.SemaphoreType.DMA(()),
        ],
    )
    return pl.pallas_call(
        _dispatch_kernel,
        grid_spec=grid_spec,
        out_shape=jax.ShapeDtypeStruct((n_rows * ROW_TILE, LANES), F32),
        compiler_params=_params(("arbitrary",)),
        name="dispatch",
    )(pstart, pend, block_cnt, run_start, slot_t, u)


def _expert_kernel(be_ref, nb_ref, len_ref, par_ref, x_ref, wgu_hbm, bgu_ref, wd_hbm, bd_ref, y_ref,
                   wgu_s, wd32_s, wd_s, wgu_ref2, wd_ref2, wsem):
    i = pl.program_id(0)
    nb = nb_ref[0]
    blk = EXPERT_BLOCK
    half = EXPERT_FF // 2
    first = jnp.logical_or(i == 0, be_ref[i] != be_ref[jnp.maximum(i - 1, 0)])
    cur = par_ref[i]

    def weight_copies(expert, slot):
        return (pltpu.make_async_copy(wgu_hbm.at[expert], wgu_ref2.at[slot], wsem.at[0, slot]),
                pltpu.make_async_copy(wd_hbm.at[expert], wd_ref2.at[slot], wsem.at[1, slot]))

    @pl.when(i == 0)
    def _():
        for cp in weight_copies(be_ref[0], 0):
            cp.start()

    @pl.when(jnp.logical_and(first, i < nb))
    def _():
        for cp in weight_copies(be_ref[i], cur):
            cp.wait()
        nxt = i + len_ref[i]

        @pl.when(nxt < nb)
        def _():
            for cp in weight_copies(be_ref[jnp.minimum(nxt, be_ref.shape[0] - 1)], 1 - cur):
                cp.start()

        wgu_ref = wgu_ref2.at[cur]
        wd_ref = wd_ref2.at[cur]
        rows = 256
        for r in range(0, D_MODEL, rows):
            wgu_s[r:r + rows, :] = wgu_ref[r:r + rows, :].astype(BF16)
        for cs in range(D_MODEL // LANES):
            lanes = slice(cs * LANES, (cs + 1) * LANES)
            wd32_s[cs, pl.ds(0, half, stride=2), :] = wd_ref[0:half, lanes]
            wd32_s[cs, pl.ds(1, half, stride=2), :] = wd_ref[half:EXPERT_FF, lanes]
            wd_s[:, lanes] = wd32_s[cs].astype(BF16)

    @pl.when(i < nb)
    def _():
        rows = blk // EXPERT_SPLIT
        parts = range(EXPERT_SPLIT)
        even = (lax.broadcasted_iota(I32, (rows, LANES), 1) % 2) == 0
        xb = [_load_row_tiles(x_ref, rows, first=s * rows).astype(BF16) for s in parts]
        gu1 = [jnp.dot(xb[s], wgu_s[:, :EXPERT_FF], preferred_element_type=F32)
               + bgu_ref[:, :EXPERT_FF] for s in parts]
        gu2 = [jnp.dot(xb[s], wgu_s[:, EXPERT_FF:], preferred_element_type=F32)
               + bgu_ref[:, EXPERT_FF:] for s in parts]
        act = []
        for s in parts:
            cols = []
            for v in range(EXPERT_FF // LANES):
                a = gu1[s][:, v * LANES:(v + 1) * LANES]
                b = gu2[s][:, v * LANES:(v + 1) * LANES]
                g = jnp.where(even, a, pltpu.roll(b, 1, axis=1))
                l = jnp.where(even, pltpu.roll(a, LANES - 1, axis=1), b)
                g = jnp.minimum(g, SWIGLU_LIMIT)
                l = jnp.clip(l, -SWIGLU_LIMIT, SWIGLU_LIMIT)
                cols.append(((l + 1.0) * (g * jax.nn.sigmoid(g * SWIGLU_ALPHA))).astype(BF16))
            act.append(jnp.concatenate(cols, axis=1))
        y = [jnp.dot(act[s], wd_s[...], preferred_element_type=F32) + bd_ref[...] for s in parts]
        for s in parts:
            _store_row_tiles(y_ref, y[s], first=s * rows)

    @pl.when(i >= nb)
    def _():
        y_ref[...] = jnp.zeros_like(y_ref)


def _experts(block_expert, n_used, run_len, run_parity, xs, wgu, bgu, wd, bd):
    n_rows = xs.shape[0] // ROW_TILE
    blk = EXPERT_BLOCK
    bspec = lambda c: pl.BlockSpec((None, 1, c), lambda i, be, *_: (be[i], 0, 0))
    grid_spec = pltpu.PrefetchScalarGridSpec(
        num_scalar_prefetch=4,
        grid=(n_rows // blk,),
        in_specs=[
            pl.BlockSpec((blk * ROW_TILE, LANES),
                         lambda i, be, nb, *_: (jnp.minimum(i, nb[0] - 1), 0)),
            pl.BlockSpec(memory_space=pl.ANY), bspec(2 * EXPERT_FF),
            pl.BlockSpec(memory_space=pl.ANY), bspec(D_MODEL),
        ],
        out_specs=pl.BlockSpec((blk * ROW_TILE, LANES), lambda i, *_: (i, 0)),
        scratch_shapes=[
            pltpu.VMEM((D_MODEL, 2 * EXPERT_FF), BF16),
            pltpu.VMEM((D_MODEL // LANES, EXPERT_FF, LANES), F32),
            pltpu.VMEM((EXPERT_FF, D_MODEL), BF16),
            pltpu.VMEM((2, D_MODEL, 2 * EXPERT_FF), F32),
            pltpu.VMEM((2, EXPERT_FF, D_MODEL), F32),
            pltpu.SemaphoreType.DMA((2, 2)),
        ],
    )
    return pl.pallas_call(
        _expert_kernel,
        grid_spec=grid_spec,
        out_shape=jax.ShapeDtypeStruct((n_rows * ROW_TILE, LANES), F32),
        compiler_params=_params(("arbitrary",)),
        name="expert",
    )(block_expert, n_used, run_len, run_parity, xs, wgu, bgu, wd, bd)


def _combine_kernel(cnt_ref, run_ref, slot0_ref, gate0_ref, slot_ref, gate_ref, h_ref, p_ref, y_ref,
                    wple_ref, gple_ref, wpg_ref, gfin_ref, o_ref, staged, moe_s, sem):
    t = T_ROUTE
    i = pl.program_id(0)
    n_steps = pl.num_programs(0)
    cur = i % 2

    def fetch(block, slot):
        def runs(e, off):
            cnt = cnt_ref[block * N_EXPERTS + e]
            _run_copies(cnt, y_ref, run_ref[block * N_EXPERTS + e], staged.at[slot], off,
                        sem.at[slot])
            return off + cnt

        lax.fori_loop(0, N_EXPERTS, runs, 0)

    def arrived(slot):
        pltpu.make_async_copy(staged.at[slot], staged.at[slot], sem.at[slot]).wait()

    def pick(slots, gates, slot, tok):
        acc = None
        for k in range(TOP_K):
            at = pl.multiple_of(slots[k * t + tok], ROW_TILE)
            term = gates[k * t + tok] * staged[slot, pl.ds(at, ROW_TILE), :]
            acc = term if acc is None else acc + term
        moe_s[slot, pl.ds(pl.multiple_of(tok * ROW_TILE, ROW_TILE), ROW_TILE), :] = acc

    def compute(slot):
        rows = t // 2
        parts = [pl.ds(s * rows, rows) for s in range(2)]
        emb = [jnp.dot(p_ref[rs, :].astype(BF16), wple_ref[...], preferred_element_type=F32)
               for rs in parts]
        h = [h_ref[rs, :] + _load_row_tiles(moe_s, rows, lead=(slot,), first=s * rows)
             for s, rs in enumerate(parts)]
        r = [_rms(hs, gple_ref[...]).astype(BF16) for hs in h]
        sig = [jax.nn.sigmoid(jnp.dot(rs, wpg_ref[...], preferred_element_type=F32)) for rs in r]
        for s, rs in enumerate(parts):
            o_ref[rs, :] = _rms(h[s] + emb[s] * sig[s], gfin_ref[...])

    @pl.when(i == 0)
    def _():
        fetch(0, 0)

        @pl.when(n_steps > 1)
        def _():
            fetch(1, 1)

        arrived(0)

        def first_block(tok, carry):
            pick(slot0_ref, gate0_ref, 0, tok)
            return carry

        lax.fori_loop(0, t, first_block, 0, unroll=DMA_UNROLL)

    @pl.when(i + 2 < n_steps)
    def _():
        fetch(i + 2, cur)

    for parity in range(2):
        @pl.when(jnp.logical_and(cur == parity, i + 1 < n_steps))
        def _(parity=parity):
            arrived(1 - parity)
            for tok in range(t):
                pick(slot_ref, gate_ref, 1 - parity, tok)
            compute(parity)

        @pl.when(jnp.logical_and(cur == parity, i + 1 == n_steps))
        def _(parity=parity):
            compute(parity)


def _combine(block_cnt, run_start, slot_t, gates_t, h, p2, y, wple, gple, wpg, gfin):
    n = h.shape[0]
    t = T_ROUTE
    last = n // t - 1
    full = lambda a: pl.BlockSpec(a.shape, lambda i, *_: (0,) * a.ndim)
    table = lambda ahead: pl.BlockSpec((TOP_K * t,), lambda i, *_: (jnp.minimum(i + ahead, last),),
                                       memory_space=pltpu.SMEM)
    grid_spec = pltpu.PrefetchScalarGridSpec(
        num_scalar_prefetch=2,
        grid=(n // t,),
        in_specs=[
            table(0), table(0), table(1), table(1),
            pl.BlockSpec((t, D_MODEL), lambda i, *_: (i, 0)),
            pl.BlockSpec((t, PLE_DIM), lambda i, *_: (i, 0)),
            pl.BlockSpec(memory_space=pl.ANY),
            full(wple), full(gple), full(wpg), full(gfin),
        ],
        out_specs=pl.BlockSpec((t, D_MODEL), lambda i, *_: (i, 0)),
        scratch_shapes=[
            pltpu.VMEM((2, TOP_K * t * ROW_TILE, LANES), F32),
            pltpu.VMEM((2, t * ROW_TILE, LANES), F32),
            pltpu.SemaphoreType.DMA((2,)),
        ],
    )
    return pl.pallas_call(
        _combine_kernel,
        grid_spec=grid_spec,
        out_shape=jax.ShapeDtypeStruct((n, D_MODEL), F32),
        compiler_params=_params(("arbitrary",)),
        name="combine",
    )(block_cnt, run_start, slot_t, gates_t, slot_t, gates_t, h, p2, y, wple, gple, wpg, gfin)


def kernel(x, p, g_mix, w_in, lambda_q1, lambda_k1, lambda_q2, lambda_k2, g_subln, w_attn_out,
           w_dw, b_dw, g_conv_ln, b_conv_ln, w_conv_out, w_o, g_ffn, w_router, b_router,
           w_gate_up, b_gate_up, w_down, b_down, w_ple, g_ple, w_ple_gate, g_final):
    b, s, d = x.shape
    n = b * s
    xt = x.reshape(n, d)
    vec = lambda a: a.reshape(1, -1)

    proj, vt = _inproj(xt, vec(g_mix[0]), w_in[0].astype(BF16))
    proj3 = proj.reshape(b, s, MAIN_COLS)
    attn = _attention(proj3, vt, vec(lambda_q1[0]), vec(lambda_k1[0]), vec(lambda_q2[0]),
                      vec(lambda_k2[0]), g_subln[0].reshape(HEAD_V, 1))
    conv = _conv(proj3, w_dw[0], vec(b_dw[0]), vec(g_conv_ln[0]), vec(b_conv_ln[0]))
    h1, u2, gates_t, slot_t, block_cnt = _mix(
        xt, attn.reshape(n, ATTN_WIDTH), conv.reshape(n, CONV_WIDTH), proj,
        w_attn_out[0].astype(BF16), w_conv_out[0].astype(BF16), w_o[0].astype(BF16),
        vec(g_ffn[0]), w_router[0].T, b_router[0].reshape(N_EXPERTS, 1))

    block_cnt = block_cnt[:, :, 0]
    counts = jnp.sum(block_cnt, axis=0)
    blk = EXPERT_BLOCK
    n_blocks = n * TOP_K // blk + N_EXPERTS
    padded = (counts + blk - 1) // blk * blk
    pend = jnp.cumsum(padded).astype(I32)
    pstart = pend - padded
    n_used = (pend[-1:] // blk).astype(I32)
    block_start = jnp.arange(n_blocks, dtype=I32) * blk
    block_expert = jnp.minimum(
        jnp.sum((pend[None, :] <= block_start[:, None]).astype(I32), axis=1), N_EXPERTS - 1)
    expert_blocks = (padded // blk).astype(I32)
    run_len = expert_blocks[block_expert]
    run_parity = ((jnp.cumsum(expert_blocks > 0) - 1) % 2).astype(I32)[block_expert]
    run_start = (pstart[None, :] + jnp.cumsum(block_cnt, axis=0) - block_cnt).astype(I32)
    cnt_flat = block_cnt.reshape(-1).astype(I32)
    run_flat = run_start.reshape(-1)

    slot_flat = slot_t.reshape(-1)
    gate_flat = gates_t.reshape(-1)

    xs = _dispatch(pstart, pend, cnt_flat, run_flat, slot_flat, u2, n_blocks * blk)
    y = _experts(block_expert, n_used, run_len, run_parity, xs, w_gate_up[0], b_gate_up[0][:, None, :],
                 w_down[0], b_down[0][:, None, :])

    out = _combine(cnt_flat, run_flat, slot_flat, gate_flat, h1, p[0].reshape(n, PLE_DIM), y,
                   w_ple[0].astype(BF16), vec(g_ple[0]), w_ple_gate[0].astype(BF16), vec(g_final))
    return out.reshape(b, s, d)
```
